```python
import jax, jax.numpy as jnp
from jax import lax
import numpy as np

D_MODEL = 1024
BATCH = 8
SEQ = 4096
DEPTH = 1

C_CONV = D_MODEL
CONV_WIDTH = 31
POOL_WINDOWS = (2, 4, 8, 16)
N_POOL_GROUPS = len(POOL_WINDOWS)
C_POOL = D_MODEL
POOL_GROUP = C_POOL // N_POOL_GROUPS
N_BRANCHES = 2
D_FF = 4 * D_MODEL
IN_COLS = 2 * C_CONV + C_POOL + N_BRANCHES * D_MODEL
RMS_EPS = 1e-6
LN_EPS = 1e-5

kernel_name = "hybrid_conv_pool_gated_block"


def rms_norm(x, g):
    xf = x.astype(jnp.float32)
    y = xf * lax.rsqrt(jnp.mean(xf * xf, axis=-1, keepdims=True) + RMS_EPS)
    return (y * g.astype(jnp.float32)).astype(x.dtype)


def layer_norm(x, g, b):
    xf = x.astype(jnp.float32)
    mu = jnp.mean(xf, axis=-1, keepdims=True)
    var = jnp.mean(jnp.square(xf - mu), axis=-1, keepdims=True)
    y = (xf - mu) * lax.rsqrt(var + LN_EPS)
    return (y * g.astype(jnp.float32) + b.astype(jnp.float32)).astype(x.dtype)


def depthwise_causal_conv(u, k, b):
    out = lax.conv_general_dilated(
        u, k[:, None, :].astype(u.dtype), window_strides=(1,),
        padding=[(CONV_WIDTH - 1, 0)],
        dimension_numbers=("NWC", "WIO", "NWC"),
        feature_group_count=u.shape[-1])
    return out + b.astype(u.dtype)


def conformer_conv_branch(u_glu, dw_kernel, dw_bias, ln_g, ln_b, w_out):
    a, gate = jnp.split(u_glu, 2, axis=-1)
    u = a * jax.nn.sigmoid(gate)
    u = depthwise_causal_conv(u, dw_kernel, dw_bias)
    u = layer_norm(u, ln_g, ln_b)
    u = jax.nn.swish(u)
    return u @ w_out


def causal_multiscale_pool(p):
    B, S, _ = p.shape
    pg = p.reshape(B, S, N_POOL_GROUPS, POOL_GROUP)
    cs = jnp.cumsum(pg.astype(jnp.float32), axis=1)
    pos = jnp.arange(1, S + 1, dtype=jnp.int32)
    outs = []
    for g, w in enumerate(POOL_WINDOWS):
        c = cs[:, :, g]
        prev = jnp.pad(c[:, :-w], ((0, 0), (w, 0), (0, 0)))
        cnt = jnp.minimum(pos, w).astype(jnp.float32)[None, :, None]
        outs.append((c - prev) / cnt)
    pooled = jnp.stack(outs, axis=2) - pg.astype(jnp.float32)
    return pooled.astype(p.dtype)


def pooling_branch(p, pool_w, pool_scale, w_out):
    B, S, _ = p.shape
    z = causal_multiscale_pool(p)
    z = jnp.einsum("bsgc,gcd->bsgd", z, pool_w).reshape(B, S, C_POOL)
    z = z * pool_scale
    return z @ w_out


def _fwd_setup_inputs(seed: int = 0) -> dict:
    key = jax.random.key(seed)
    ks = jax.random.split(key, 20)
    f32 = jnp.float32
    nrm = lambda k, shape, s: jax.random.normal(k, shape, f32) * s
    return {
        "x": jax.random.normal(ks[0], (BATCH, SEQ, D_MODEL), f32),
        "mix_pre_g": 1.0 + nrm(ks[1], (D_MODEL,), 0.05),
        "w_in": nrm(ks[2], (D_MODEL, IN_COLS), D_MODEL ** -0.5),
        "dw_kernel": nrm(ks[3], (CONV_WIDTH, C_CONV), CONV_WIDTH ** -0.5),
        "dw_bias": nrm(ks[4], (C_CONV,), 0.02),
        "conv_ln_g": 1.0 + nrm(ks[5], (C_CONV,), 0.05),
        "conv_ln_b": nrm(ks[6], (C_CONV,), 0.02),
        "w_conv_out": nrm(ks[7], (C_CONV, D_MODEL), C_CONV ** -0.5),
        "pool_w": nrm(ks[8], (N_POOL_GROUPS, POOL_GROUP, POOL_GROUP), POOL_GROUP ** -0.5),
        "pool_scale": 1.0 + nrm(ks[9], (C_POOL,), 0.1),
        "w_pool_out": nrm(ks[10], (C_POOL, D_MODEL), C_POOL ** -0.5),
        "w_o": nrm(ks[11], (D_MODEL, D_MODEL), D_MODEL ** -0.5),
        "mix_post_g": 1.0 + nrm(ks[12], (D_MODEL,), 0.05),
        "mlp_pre_g": 1.0 + nrm(ks[13], (D_MODEL,), 0.05),
        "w_ff1": nrm(ks[14], (D_MODEL, D_FF), D_MODEL ** -0.5),
        "w_ff2": nrm(ks[15], (D_FF, D_MODEL), D_FF ** -0.5),
        "mlp_post_g": 1.0 + nrm(ks[16], (D_MODEL,), 0.05),
    }


def _fwd_reference(x, mix_pre_g, w_in, dw_kernel, dw_bias, conv_ln_g, conv_ln_b,
              w_conv_out, pool_w, pool_scale, w_pool_out, w_o, mix_post_g,
              mlp_pre_g, w_ff1, w_ff2, mlp_post_g):
    h = x
    for _ in range(DEPTH):
        u = rms_norm(h, mix_pre_g)
        proj = u @ w_in
        u_glu = proj[..., :2 * C_CONV]
        p = proj[..., 2 * C_CONV:2 * C_CONV + C_POOL]
        gates = jax.nn.sigmoid(proj[..., 2 * C_CONV + C_POOL:])
        g_conv, g_pool = jnp.split(gates, N_BRANCHES, axis=-1)
        y_conv = conformer_conv_branch(u_glu, dw_kernel, dw_bias, conv_ln_g, conv_ln_b, w_conv_out)
        y_pool = pooling_branch(p, pool_w, pool_scale, w_pool_out)
        merged = g_conv * y_conv + g_pool * y_pool
        h = h + rms_norm(merged @ w_o, mix_post_g)
        v = rms_norm(h, mlp_pre_g)
        v = jnp.square(jax.nn.relu(v @ w_ff1)) @ w_ff2
        h = h + rms_norm(v, mlp_post_g)
    return h


import jax as _jax
import jax.numpy as _jnp

TWIN_FORMAT = 'train_step'
FWD_PARAMS = ['x', 'mix_pre_g', 'w_in', 'dw_kernel', 'dw_bias', 'conv_ln_g', 'conv_ln_b', 'w_conv_out', 'pool_w', 'pool_scale', 'w_pool_out', 'w_o', 'mix_post_g', 'mlp_pre_g', 'w_ff1', 'w_ff2', 'mlp_post_g']
TWIN_WEIGHTS = ['mix_pre_g', 'w_in', 'dw_kernel', 'dw_bias', 'conv_ln_g', 'conv_ln_b', 'w_conv_out', 'pool_w', 'pool_scale', 'w_pool_out', 'w_o', 'mix_post_g', 'mlp_pre_g', 'w_ff1', 'w_ff2', 'mlp_post_g']
TWIN_DIFF_INPUT = 'x'
TWIN_INPUTS = ['x', 'mix_pre_g', 'w_in', 'dw_kernel', 'dw_bias', 'conv_ln_g', 'conv_ln_b', 'w_conv_out', 'pool_w', 'pool_scale', 'w_pool_out', 'w_o', 'mix_post_g', 'mlp_pre_g', 'w_ff1', 'w_ff2', 'mlp_post_g', 'loss_target', 'm_mix_pre_g', 'm_w_in', 'm_dw_kernel', 'm_dw_bias', 'm_conv_ln_g', 'm_conv_ln_b', 'm_w_conv_out', 'm_pool_w', 'm_pool_scale', 'm_w_pool_out', 'm_w_o', 'm_mix_post_g', 'm_mlp_pre_g', 'm_w_ff1', 'm_w_ff2', 'm_mlp_post_g', 'v_mix_pre_g', 'v_w_in', 'v_dw_kernel', 'v_dw_bias', 'v_conv_ln_g', 'v_conv_ln_b', 'v_w_conv_out', 'v_pool_w', 'v_pool_scale', 'v_w_pool_out', 'v_w_o', 'v_mix_post_g', 'v_mlp_pre_g', 'v_w_ff1', 'v_w_ff2', 'v_mlp_post_g']
TWIN_OUTPUTS = ['loss', 'grad_x', 'grad_mix_pre_g', 'grad_w_in', 'grad_dw_kernel', 'grad_dw_bias', 'grad_conv_ln_g', 'grad_conv_ln_b', 'grad_w_conv_out', 'grad_pool_w', 'grad_pool_scale', 'grad_w_pool_out', 'grad_w_o', 'grad_mix_post_g', 'grad_mlp_pre_g', 'grad_w_ff1', 'grad_w_ff2', 'grad_mlp_post_g', 'delta_mix_pre_g', 'delta_w_in', 'delta_dw_kernel', 'delta_dw_bias', 'delta_conv_ln_g', 'delta_conv_ln_b', 'delta_w_conv_out', 'delta_pool_w', 'delta_pool_scale', 'delta_w_pool_out', 'delta_w_o', 'delta_mix_post_g', 'delta_mlp_pre_g', 'delta_w_ff1', 'delta_w_ff2', 'delta_mlp_post_g', 'new_m_mix_pre_g', 'new_m_w_in', 'new_m_dw_kernel', 'new_m_dw_bias', 'new_m_conv_ln_g', 'new_m_conv_ln_b', 'new_m_w_conv_out', 'new_m_pool_w', 'new_m_pool_scale', 'new_m_w_pool_out', 'new_m_w_o', 'new_m_mix_post_g', 'new_m_mlp_pre_g', 'new_m_w_ff1', 'new_m_w_ff2', 'new_m_mlp_post_g', 'new_v_mix_pre_g', 'new_v_w_in', 'new_v_dw_kernel', 'new_v_dw_bias', 'new_v_conv_ln_g', 'new_v_conv_ln_b', 'new_v_w_conv_out', 'new_v_pool_w', 'new_v_pool_scale', 'new_v_w_pool_out', 'new_v_w_o', 'new_v_mix_post_g', 'new_v_mlp_pre_g', 'new_v_w_ff1', 'new_v_w_ff2', 'new_v_mlp_post_g']
TWIN_LEAF_KINDS = {'loss': 'loss', 'grad_x': 'grad_x', 'grad_mix_pre_g': 'grad_w', 'grad_w_in': 'grad_w', 'grad_dw_kernel': 'grad_w', 'grad_dw_bias': 'grad_w', 'grad_conv_ln_g': 'grad_w', 'grad_conv_ln_b': 'grad_w', 'grad_w_conv_out': 'grad_w', 'grad_pool_w': 'grad_w', 'grad_pool_scale': 'grad_w', 'grad_w_pool_out': 'grad_w', 'grad_w_o': 'grad_w', 'grad_mix_post_g': 'grad_w', 'grad_mlp_pre_g': 'grad_w', 'grad_w_ff1': 'grad_w', 'grad_w_ff2': 'grad_w', 'grad_mlp_post_g': 'grad_w', 'delta_mix_pre_g': 'delta_w', 'delta_w_in': 'delta_w', 'delta_dw_kernel': 'delta_w', 'delta_dw_bias': 'delta_w', 'delta_conv_ln_g': 'delta_w', 'delta_conv_ln_b': 'delta_w', 'delta_w_conv_out': 'delta_w', 'delta_pool_w': 'delta_w', 'delta_pool_scale': 'delta_w', 'delta_w_pool_out': 'delta_w', 'delta_w_o': 'delta_w', 'delta_mix_post_g': 'delta_w', 'delta_mlp_pre_g': 'delta_w', 'delta_w_ff1': 'delta_w', 'delta_w_ff2': 'delta_w', 'delta_mlp_post_g': 'delta_w', 'new_m_mix_pre_g': 'new_m', 'new_m_w_in': 'new_m', 'new_m_dw_kernel': 'new_m', 'new_m_dw_bias': 'new_m', 'new_m_conv_ln_g': 'new_m', 'new_m_conv_ln_b': 'new_m', 'new_m_w_conv_out': 'new_m', 'new_m_pool_w': 'new_m', 'new_m_pool_scale': 'new_m', 'new_m_w_pool_out': 'new_m', 'new_m_w_o': 'new_m', 'new_m_mix_post_g': 'new_m', 'new_m_mlp_pre_g': 'new_m', 'new_m_w_ff1': 'new_m', 'new_m_w_ff2': 'new_m', 'new_m_mlp_post_g': 'new_m', 'new_v_mix_pre_g': 'new_v', 'new_v_w_in': 'new_v', 'new_v_dw_kernel': 'new_v', 'new_v_dw_bias': 'new_v', 'new_v_conv_ln_g': 'new_v', 'new_v_conv_ln_b': 'new_v', 'new_v_w_conv_out': 'new_v', 'new_v_pool_w': 'new_v', 'new_v_pool_scale': 'new_v', 'new_v_w_pool_out': 'new_v', 'new_v_w_o': 'new_v', 'new_v_mix_post_g': 'new_v', 'new_v_mlp_pre_g': 'new_v', 'new_v_w_ff1': 'new_v', 'new_v_w_ff2': 'new_v', 'new_v_mlp_post_g': 'new_v'}


def _forward(args):
    return _fwd_reference(*[args[k] for k in FWD_PARAMS])


def _output_shape():
    def fwd():
        inp = _fwd_setup_inputs(0)
        return _fwd_reference(*[inp[k] for k in FWD_PARAMS])
    out = _jax.eval_shape(fwd)
    return out.shape, out.dtype

N_MICROBATCH = 1
ADAM_LR = 0.001
ADAM_B1 = 0.9
ADAM_B2 = 0.999
ADAM_EPS = 1e-08
ADAM_WD = 0.01
ADAM_STEP = 10
PER_EXAMPLE_BATCH_AXIS = {'x': 0, 'loss_target': 0}
SHARED_INPUTS = []
_WEIGHT_DTYPES = {'mix_pre_g': _jnp.float32, 'w_in': _jnp.float32, 'dw_kernel': _jnp.float32, 'dw_bias': _jnp.float32, 'conv_ln_g': _jnp.float32, 'conv_ln_b': _jnp.float32, 'w_conv_out': _jnp.float32, 'pool_w': _jnp.float32, 'pool_scale': _jnp.float32, 'w_pool_out': _jnp.float32, 'w_o': _jnp.float32, 'mix_post_g': _jnp.float32, 'mlp_pre_g': _jnp.float32, 'w_ff1': _jnp.float32, 'w_ff2': _jnp.float32, 'mlp_post_g': _jnp.float32}
MOMENT_SCALE = {'mix_pre_g': 8.565605e-01, 'w_in': 3.708979e-01, 'dw_kernel': 6.790922e-01, 'dw_bias': 1.101635e+01, 'conv_ln_g': 4.287112e+00, 'conv_ln_b': 6.231755e+00, 'w_conv_out': 2.514686e+00, 'pool_w': 8.589098e-01, 'pool_scale': 9.721936e-01, 'w_pool_out': 9.291635e-01, 'w_o': 2.763486e+00, 'mix_post_g': 3.231103e+01, 'mlp_pre_g': 1.291600e+00, 'w_ff1': 6.072062e-01, 'w_ff2': 2.733634e+00, 'mlp_post_g': 3.312035e+01}


def _to_microbatches(a, axis):
    t = _jnp.moveaxis(a, axis, 0)
    t = t.reshape((N_MICROBATCH, t.shape[0] // N_MICROBATCH) + t.shape[1:])
    return _jnp.moveaxis(t, 1, axis + 1)


def setup_inputs(seed: int = 0) -> dict:
    inp = _fwd_setup_inputs(seed)
    key = _jax.random.fold_in(_jax.random.key(seed), 7919)
    shape, _ = _output_shape()
    out = dict(inp)
    out["loss_target"] = _jax.random.normal(_jax.random.fold_in(key, 0), shape, _jnp.float32)
    for i, name in enumerate(TWIN_WEIGHTS):
        w = inp[name].astype(_jnp.float32)
        if MOMENT_SCALE is None:
            s = _jnp.sqrt(_jnp.mean(_jnp.square(w)) + 1e-30)
        else:
            s = MOMENT_SCALE[name]
        km, kv = _jax.random.split(_jax.random.fold_in(key, i + 1))
        out[name] = w
        out["m_" + name] = s * _jax.random.normal(km, w.shape, _jnp.float32)
        out["v_" + name] = (s * s) * _jax.random.uniform(kv, w.shape, _jnp.float32, 0.5, 1.5)
    if N_MICROBATCH > 1:
        for name, axis in PER_EXAMPLE_BATCH_AXIS.items():
            out[name] = _to_microbatches(out[name], axis)
    return {'x': out['x'], 'mix_pre_g': out['mix_pre_g'], 'w_in': out['w_in'], 'dw_kernel': out['dw_kernel'], 'dw_bias': out['dw_bias'], 'conv_ln_g': out['conv_ln_g'], 'conv_ln_b': out['conv_ln_b'], 'w_conv_out': out['w_conv_out'], 'pool_w': out['pool_w'], 'pool_scale': out['pool_scale'], 'w_pool_out': out['w_pool_out'], 'w_o': out['w_o'], 'mix_post_g': out['mix_post_g'], 'mlp_pre_g': out['mlp_pre_g'], 'w_ff1': out['w_ff1'], 'w_ff2': out['w_ff2'], 'mlp_post_g': out['mlp_post_g'], 'loss_target': out['loss_target'], 'm_mix_pre_g': out['m_mix_pre_g'], 'm_w_in': out['m_w_in'], 'm_dw_kernel': out['m_dw_kernel'], 'm_dw_bias': out['m_dw_bias'], 'm_conv_ln_g': out['m_conv_ln_g'], 'm_conv_ln_b': out['m_conv_ln_b'], 'm_w_conv_out': out['m_w_conv_out'], 'm_pool_w': out['m_pool_w'], 'm_pool_scale': out['m_pool_scale'], 'm_w_pool_out': out['m_w_pool_out'], 'm_w_o': out['m_w_o'], 'm_mix_post_g': out['m_mix_post_g'], 'm_mlp_pre_g': out['m_mlp_pre_g'], 'm_w_ff1': out['m_w_ff1'], 'm_w_ff2': out['m_w_ff2'], 'm_mlp_post_g': out['m_mlp_post_g'], 'v_mix_pre_g': out['v_mix_pre_g'], 'v_w_in': out['v_w_in'], 'v_dw_kernel': out['v_dw_kernel'], 'v_dw_bias': out['v_dw_bias'], 'v_conv_ln_g': out['v_conv_ln_g'], 'v_conv_ln_b': out['v_conv_ln_b'], 'v_w_conv_out': out['v_w_conv_out'], 'v_pool_w': out['v_pool_w'], 'v_pool_scale': out['v_pool_scale'], 'v_w_pool_out': out['v_w_pool_out'], 'v_w_o': out['v_w_o'], 'v_mix_post_g': out['v_mix_post_g'], 'v_mlp_pre_g': out['v_mlp_pre_g'], 'v_w_ff1': out['v_w_ff1'], 'v_w_ff2': out['v_w_ff2'], 'v_mlp_post_g': out['v_mlp_post_g']}


def _loss(weights, diff, rest, loss_target):
    with _jax.named_scope("forward"):
        args = {**rest, TWIN_DIFF_INPUT: diff, **{k: w.astype(_WEIGHT_DTYPES[k]) for k, w in weights.items()}}
        y = _forward(args)
    with _jax.named_scope("loss_head"):
        err = _jnp.square(y.astype(_jnp.float32) - loss_target)
        return 0.5 * _jnp.sum(_jnp.mean(err, axis=-1)) if err.ndim else 0.5 * err


def _adamw(w, g, m, v):
    m = ADAM_B1 * m + (1.0 - ADAM_B1) * g
    v = ADAM_B2 * v + (1.0 - ADAM_B2) * _jnp.square(g)
    m_hat = m / (1.0 - ADAM_B1 ** ADAM_STEP)
    v_hat = v / (1.0 - ADAM_B2 ** ADAM_STEP)
    delta = -ADAM_LR * (m_hat / (_jnp.sqrt(v_hat) + ADAM_EPS) + ADAM_WD * w)
    return delta, m, v


def reference(x, mix_pre_g, w_in, dw_kernel, dw_bias, conv_ln_g, conv_ln_b, w_conv_out, pool_w, pool_scale, w_pool_out, w_o, mix_post_g, mlp_pre_g, w_ff1, w_ff2, mlp_post_g, loss_target, m_mix_pre_g, m_w_in, m_dw_kernel, m_dw_bias, m_conv_ln_g, m_conv_ln_b, m_w_conv_out, m_pool_w, m_pool_scale, m_w_pool_out, m_w_o, m_mix_post_g, m_mlp_pre_g, m_w_ff1, m_w_ff2, m_mlp_post_g, v_mix_pre_g, v_w_in, v_dw_kernel, v_dw_bias, v_conv_ln_g, v_conv_ln_b, v_w_conv_out, v_pool_w, v_pool_scale, v_w_pool_out, v_w_o, v_mix_post_g, v_mlp_pre_g, v_w_ff1, v_w_ff2, v_mlp_post_g):
    given = dict(x=x, mix_pre_g=mix_pre_g, w_in=w_in, dw_kernel=dw_kernel, dw_bias=dw_bias, conv_ln_g=conv_ln_g, conv_ln_b=conv_ln_b, w_conv_out=w_conv_out, pool_w=pool_w, pool_scale=pool_scale, w_pool_out=w_pool_out, w_o=w_o, mix_post_g=mix_post_g, mlp_pre_g=mlp_pre_g, w_ff1=w_ff1, w_ff2=w_ff2, mlp_post_g=mlp_post_g, loss_target=loss_target, m_mix_pre_g=m_mix_pre_g, m_w_in=m_w_in, m_dw_kernel=m_dw_kernel, m_dw_bias=m_dw_bias, m_conv_ln_g=m_conv_ln_g, m_conv_ln_b=m_conv_ln_b, m_w_conv_out=m_w_conv_out, m_pool_w=m_pool_w, m_pool_scale=m_pool_scale, m_w_pool_out=m_w_pool_out, m_w_o=m_w_o, m_mix_post_g=m_mix_post_g, m_mlp_pre_g=m_mlp_pre_g, m_w_ff1=m_w_ff1, m_w_ff2=m_w_ff2, m_mlp_post_g=m_mlp_post_g, v_mix_pre_g=v_mix_pre_g, v_w_in=v_w_in, v_dw_kernel=v_dw_kernel, v_dw_bias=v_dw_bias, v_conv_ln_g=v_conv_ln_g, v_conv_ln_b=v_conv_ln_b, v_w_conv_out=v_w_conv_out, v_pool_w=v_pool_w, v_pool_scale=v_pool_scale, v_w_pool_out=v_w_pool_out, v_w_o=v_w_o, v_mix_post_g=v_mix_post_g, v_mlp_pre_g=v_mlp_pre_g, v_w_ff1=v_w_ff1, v_w_ff2=v_w_ff2, v_mlp_post_g=v_mlp_post_g)
    weights = {n: given[n] for n in TWIN_WEIGHTS}
    shared = {n: given[n] for n in SHARED_INPUTS}
    per_example = {n: given[n] for n in ['x']}
    grad_fn = _jax.value_and_grad(_loss, argnums=(0, 1))

    def one_microbatch(ex, loss_target):
        ex = dict(ex)
        diff = ex.pop(TWIN_DIFF_INPUT)
        return grad_fn(weights, diff, {**shared, **ex}, loss_target)

    if N_MICROBATCH == 1:
        loss, (grad_w, grad_x) = one_microbatch(per_example, given["loss_target"])
    else:
        def body(carry, xs):
            loss_sum, grad_sum = carry
            l_k, (gw_k, gx_k) = one_microbatch(xs[0], xs[1])
            with _jax.named_scope("update"):
                return (loss_sum + l_k, _jax.tree.map(_jnp.add, grad_sum, gw_k)), gx_k

        init = (_jnp.zeros((), _jnp.float32), _jax.tree.map(_jnp.zeros_like, weights))
        (loss, grad_w), grad_x = _jax.lax.scan(body, init, (per_example, given["loss_target"]))
    with _jax.named_scope("update"):
        delta_w, new_m, new_v = {}, {}, {}
        for n in TWIN_WEIGHTS:
            delta_w[n], new_m[n], new_v[n] = _adamw(weights[n], grad_w[n], given["m_" + n], given["v_" + n])
    return (loss, grad_x, *[grad_w[n] for n in TWIN_WEIGHTS], *[delta_w[n] for n in TWIN_WEIGHTS],
            *[new_m[n] for n in TWIN_WEIGHTS], *[new_v[n] for n in TWIN_WEIGHTS])
```

```python
import functools

import jax
import jax.numpy as jnp
from jax import lax
from jax.experimental import pallas as pl
from jax.experimental.pallas import tpu as pltpu

F32 = jnp.float32
BF16 = jnp.bfloat16

D = 1024
FF = 4096
NPROJ = 5
KW = 31
KW_PAD = 32
HALO = 32
POOL_WINDOWS = (2, 4, 8, 16)
NG = 4
GW = D // NG
RMS_EPS = 1e-6
LN_EPS = 1e-5
LR, B1, B2, ADAM_EPS, WD, STEP = 0.001, 0.9, 0.999, 1e-08, 0.01, 10
NCHIP = 4
VMEM_LIMIT = 60 * 1024 * 1024
MESH = pl.DeviceIdType.MESH


def _cp(**kw):
    return pltpu.CompilerParams(vmem_limit_bytes=VMEM_LIMIT, **kw)


def _mm(a, b):
    return jnp.dot(a, b, preferred_element_type=F32)


def _mm_nt(a, b):
    return lax.dot_general(a, b, (((1,), (1,)), ((), ())), preferred_element_type=F32)


def _mm_tn(a, b):
    return lax.dot_general(a, b, (((0,), (0,)), ((), ())), preferred_element_type=F32)


def _sigmoid(x):
    return 1.0 / (1.0 + jnp.exp(-x))


def _rowsum(x):
    return jnp.sum(x, axis=0, keepdims=True)


def _full(shape):
    return pl.BlockSpec(shape, lambda i: (0,) * len(shape))


def _tile(tm, cols):
    return pl.BlockSpec((tm, cols), lambda i: (i, 0))


def _prev_halo(tm):
    return pl.BlockSpec((HALO, D), lambda i: (jnp.maximum(i * (tm // HALO) - 1, 0), 0))


def _next_halo(tm, nt):
    return pl.BlockSpec((HALO, D), lambda i: (jnp.minimum((i + 1) * (tm // HALO), nt * (tm // HALO) - 1), 0))


def _taps(src_ref, k_ref, dst_ref, tm, off0, reverse):
    rc, cw = 64, 128

    def col_chunk(cc, carry):
        cols = pl.ds(pl.multiple_of(cc * cw, cw), cw)
        for r in range(tm // rc):
            acc = jnp.zeros((rc, cw), F32)
            for j in range(KW):
                kj = KW - 1 - j if reverse else j
                acc = acc + k_ref[kj:kj + 1, cols] * src_ref[pl.ds(r * rc + off0 + j, rc), cols]
            dst_ref[pl.ds(r * rc, rc), cols] = acc
        return carry

    lax.fori_loop(0, D // cw, col_chunk, 0)


def fwd_in(x, g1, w_in, tm):
    t = x.shape[0]

    def body(x_ref, g_ref, w_ref, u_ref, glu_ref, ag_ref, p_ref, gt_ref):
        xf = x_ref[...]
        r = lax.rsqrt(jnp.mean(xf * xf, axis=-1, keepdims=True) + RMS_EPS)
        u = (xf * r * g_ref[...]).astype(BF16)
        u_ref[...] = u
        a = _mm(u, w_ref[:, 0:D])
        gate = _mm(u, w_ref[:, D:2 * D])
        glu_ref[...] = a * _sigmoid(gate)
        ag_ref[:, 0:D] = a.astype(BF16)
        ag_ref[:, D:2 * D] = gate.astype(BF16)
        p_ref[...] = _mm(u, w_ref[:, 2 * D:3 * D])
        gt_ref[:, 0:D] = _mm(u, w_ref[:, 3 * D:4 * D]).astype(BF16)
        gt_ref[:, D:2 * D] = _mm(u, w_ref[:, 4 * D:5 * D]).astype(BF16)

    return pl.pallas_call(
        body, name="fwd_in", grid=(t // tm,),
        in_specs=[_tile(tm, D), _full((1, D)), _full((D, NPROJ * D))],
        out_specs=[_tile(tm, D), _tile(tm, D), _tile(tm, 2 * D), _tile(tm, D), _tile(tm, 2 * D)],
        out_shape=[jax.ShapeDtypeStruct((t, D), BF16), jax.ShapeDtypeStruct((t, D), F32),
                   jax.ShapeDtypeStruct((t, 2 * D), BF16), jax.ShapeDtypeStruct((t, D), F32),
                   jax.ShapeDtypeStruct((t, 2 * D), BF16)],
        compiler_params=_cp(dimension_semantics=("arbitrary",)),
    )(x, g1, w_in)


def _pool_inv_count(i, tm, w):
    pos = i * tm + lax.broadcasted_iota(jnp.int32, (tm, 1), 0) + 1
    return 1.0 / jnp.minimum(pos, w).astype(F32)


def fwd_mix(x, glu, p, gt, dwk, dwb, lng, lnb, w_co, pool_w, ps, w_po, w_o, g2, tm):
    t = x.shape[0]

    def body(x_ref, glu_ref, gluh_ref, p_ref, ph_ref, gt_ref, k_ref, b_ref, lg_ref, lb_ref, wco_ref, pw_ref,
             ps_ref, wpo_ref, wo_ref, g2_ref,
             cv_ref, sw_ref, z_ref, zl_ref, zs_ref, yc_ref, yp_ref, mg_ref, mo_ref, h1_ref, ext_ref):
        i = pl.program_id(0)
        keep = (i > 0).astype(F32)
        ext_ref[0:HALO, :] = gluh_ref[...] * keep
        ext_ref[HALO:HALO + tm, :] = glu_ref[...]
        _taps(ext_ref, k_ref, cv_ref, tm, HALO - (KW - 1), False)
        cv = cv_ref[...] + b_ref[...]
        cv_ref[...] = cv
        mu = jnp.mean(cv, axis=-1, keepdims=True)
        cen = cv - mu
        rstd = lax.rsqrt(jnp.mean(cen * cen, axis=-1, keepdims=True) + LN_EPS)
        ln = cen * rstd * lg_ref[...] + lb_ref[...]
        sw = (ln * _sigmoid(ln)).astype(BF16)
        sw_ref[...] = sw
        yc = _mm(sw, wco_ref[...])
        yc_ref[...] = yc.astype(BF16)
        ext_ref[0:HALO, :] = ph_ref[...] * keep
        ext_ref[HALO:HALO + tm, :] = p_ref[...]
        for g, w in enumerate(POOL_WINDOWS):
            cols = pl.ds(g * GW, GW)
            acc = ext_ref[pl.ds(HALO, tm), cols]
            for s in range(1, w):
                acc = acc + ext_ref[pl.ds(HALO - s, tm), cols]
            zg = (acc * _pool_inv_count(i, tm, w) - p_ref[:, cols]).astype(BF16)
            z_ref[:, cols] = zg
            zl_ref[:, cols] = _mm(zg, pw_ref[g])
        zl = zl_ref[...]
        zs = (zl * ps_ref[...]).astype(BF16)
        zs_ref[...] = zs
        yp = _mm(zs, wpo_ref[...])
        yp_ref[...] = yp.astype(BF16)
        gc = _sigmoid(gt_ref[:, 0:D].astype(F32))
        gp = _sigmoid(gt_ref[:, D:2 * D].astype(F32))
        mg = (gc * yc + gp * yp).astype(BF16)
        mg_ref[...] = mg
        mo = _mm(mg, wo_ref[...])
        mo_ref[...] = mo
        r2 = lax.rsqrt(jnp.mean(mo * mo, axis=-1, keepdims=True) + RMS_EPS)
        h1_ref[...] = x_ref[...] + mo * r2 * g2_ref[...]

    vec = _full((1, D))
    act = lambda dt: jax.ShapeDtypeStruct((t, D), dt)
    return pl.pallas_call(
        body, name="fwd_mix", grid=(t // tm,),
        in_specs=[_tile(tm, D), _tile(tm, D), _prev_halo(tm), _tile(tm, D), _prev_halo(tm), _tile(tm, 2 * D),
                  _full((KW_PAD, D)), vec, vec, vec, _full((D, D)), _full((NG, GW, GW)), vec, _full((D, D)),
                  _full((D, D)), vec],
        out_specs=[_tile(tm, D)] * 10,
        out_shape=[act(F32), act(BF16), act(BF16), act(F32), act(BF16), act(BF16), act(BF16), act(BF16), act(F32),
                   act(F32)],
        scratch_shapes=[pltpu.VMEM((tm + HALO, D), F32)],
        compiler_params=_cp(dimension_semantics=("arbitrary",)),
    )(x, glu, glu, p, p, gt, dwk, dwb, lng, lnb, w_co, pool_w, ps, w_po, w_o, g2)


def mlp_fwd_bwd(h1, tgt, g3, g4, w1, w2, tm):
    t = h1.shape[0]
    fc = 1024

    def body(h1_ref, tgt_ref, g3_ref, g4_ref, w1_ref, w2_ref,
             v_ref, a2_ref, df2_ref, df1_ref, dh1_ref, vec_ref, f1_ref):
        i = pl.program_id(0)

        @pl.when(i == 0)
        def _():
            vec_ref[...] = jnp.zeros_like(vec_ref)

        h1v = h1_ref[...]
        r3 = lax.rsqrt(jnp.mean(h1v * h1v, axis=-1, keepdims=True) + RMS_EPS)
        n3 = h1v * r3
        v = (n3 * g3_ref[...]).astype(BF16)
        v_ref[...] = v
        f2 = jnp.zeros((tm, D), F32)
        for c in range(FF // fc):
            cols = pl.ds(c * fc, fc)
            f1 = jnp.maximum(_mm(v, w1_ref[:, cols]), 0.0)
            f1_ref[:, cols] = f1
            a2 = (f1 * f1).astype(BF16)
            a2_ref[:, cols] = a2
            f2 = f2 + _mm(a2, w2_ref[cols, :])
        r4 = lax.rsqrt(jnp.mean(f2 * f2, axis=-1, keepdims=True) + RMS_EPS)
        n4 = f2 * r4
        err = h1v + n4 * g4_ref[...] - tgt_ref[...]
        vec_ref[2:3, :] += _rowsum(err * err) * (0.5 / D)
        dh2 = err * (1.0 / D)
        vec_ref[1:2, :] += _rowsum(dh2 * n4)
        dn4 = dh2 * g4_ref[...]
        df2 = (r4 * (dn4 - n4 * jnp.mean(dn4 * n4, axis=-1, keepdims=True))).astype(BF16)
        df2_ref[...] = df2
        dv = jnp.zeros((tm, D), F32)
        for c in range(FF // fc):
            cols = pl.ds(c * fc, fc)
            da2 = _mm_nt(df2, w2_ref[cols, :])
            df1 = (da2 * (2.0 * f1_ref[:, cols])).astype(BF16)
            df1_ref[:, cols] = df1
            dv = dv + _mm_nt(df1, w1_ref[:, cols])
        vec_ref[0:1, :] += _rowsum(dv * n3)
        dn3 = dv * g3_ref[...]
        dh1_ref[...] = dh2 + r3 * (dn3 - n3 * jnp.mean(dn3 * n3, axis=-1, keepdims=True))

    vec = _full((1, D))
    return pl.pallas_call(
        body, name="mlp_fwd_bwd", grid=(t // tm,),
        in_specs=[_tile(tm, D), _tile(tm, D), vec, vec, _full((D, FF)), _full((FF, D))],
        out_specs=[_tile(tm, D), _tile(tm, FF), _tile(tm, D), _tile(tm, FF), _tile(tm, D), _full((8, D))],
        out_shape=[jax.ShapeDtypeStruct((t, D), BF16), jax.ShapeDtypeStruct((t, FF), BF16),
                   jax.ShapeDtypeStruct((t, D), BF16), jax.ShapeDtypeStruct((t, FF), BF16),
                   jax.ShapeDtypeStruct((t, D), F32), jax.ShapeDtypeStruct((8, D), F32)],
        scratch_shapes=[pltpu.VMEM((tm, FF), F32)],
        compiler_params=_cp(dimension_semantics=("arbitrary",)),
    )(h1, tgt, g3, g4, w1, w2)


def bwd_mix(dh1, mo, cv, zl, yc, yp, gt, lng, lnb, ps, g2, w_co, pool_w, w_po, w_o, tm):
    t = dh1.shape[0]

    def body(dh1_ref, mo_ref, cv_ref, zl_ref, yc_ref, yp_ref, gt_ref, lg_ref, lb_ref, ps_ref, g2_ref,
             wco_ref, pw_ref, wpo_ref, wo_ref,
             dmo_ref, dgt_ref, dyc_ref, dyp_ref, dcv_ref, dzl_ref, dz_ref, vec_ref):
        i = pl.program_id(0)

        @pl.when(i == 0)
        def _():
            vec_ref[...] = jnp.zeros_like(vec_ref)

        dh1v = dh1_ref[...]
        mo = mo_ref[...]
        r2 = lax.rsqrt(jnp.mean(mo * mo, axis=-1, keepdims=True) + RMS_EPS)
        n2 = mo * r2
        vec_ref[0:1, :] += _rowsum(dh1v * n2)
        dn2 = dh1v * g2_ref[...]
        dmo = (r2 * (dn2 - n2 * jnp.mean(dn2 * n2, axis=-1, keepdims=True))).astype(BF16)
        dmo_ref[...] = dmo
        dmg = _mm_nt(dmo, wo_ref[...])
        gc = _sigmoid(gt_ref[:, 0:D].astype(F32))
        gp = _sigmoid(gt_ref[:, D:2 * D].astype(F32))
        dgt_ref[:, 0:D] = (dmg * yc_ref[...].astype(F32) * gc * (1.0 - gc)).astype(BF16)
        dgt_ref[:, D:2 * D] = (dmg * yp_ref[...].astype(F32) * gp * (1.0 - gp)).astype(BF16)
        dyc = (dmg * gc).astype(BF16)
        dyp = (dmg * gp).astype(BF16)
        dyc_ref[...] = dyc
        dyp_ref[...] = dyp
        dsw = _mm_nt(dyc, wco_ref[...])
        cv = cv_ref[...]
        mu = jnp.mean(cv, axis=-1, keepdims=True)
        cen = cv - mu
        rstd = lax.rsqrt(jnp.mean(cen * cen, axis=-1, keepdims=True) + LN_EPS)
        y = cen * rstd
        ln = y * lg_ref[...] + lb_ref[...]
        sg = _sigmoid(ln)
        dln = dsw * (sg * (1.0 + ln * (1.0 - sg)))
        vec_ref[1:2, :] += _rowsum(dln * y)
        vec_ref[2:3, :] += _rowsum(dln)
        dy = dln * lg_ref[...]
        dcv = rstd * (dy - jnp.mean(dy, axis=-1, keepdims=True) - y * jnp.mean(dy * y, axis=-1, keepdims=True))
        dcv_ref[...] = dcv
        vec_ref[3:4, :] += _rowsum(dcv)
        dzs = _mm_nt(dyp, wpo_ref[...])
        vec_ref[4:5, :] += _rowsum(dzs * zl_ref[...])
        dzl = (dzs * ps_ref[...]).astype(BF16)
        dzl_ref[...] = dzl
        for g in range(NG):
            cols = pl.ds(g * GW, GW)
            dz_ref[:, cols] = _mm_nt(dzl_ref[:, cols], pw_ref[g])

    vec = _full((1, D))
    act = lambda dt: jax.ShapeDtypeStruct((t, D), dt)
    return pl.pallas_call(
        body, name="bwd_mix", grid=(t // tm,),
        in_specs=[_tile(tm, D)] * 6 + [_tile(tm, 2 * D), vec, vec, vec, vec, _full((D, D)), _full((NG, GW, GW)),
                                       _full((D, D)), _full((D, D))],
        out_specs=[_tile(tm, D), _tile(tm, 2 * D)] + [_tile(tm, D)] * 5 + [_full((8, D))],
        out_shape=[act(BF16), jax.ShapeDtypeStruct((t, 2 * D), BF16), act(BF16), act(BF16), act(F32), act(BF16),
                   act(F32), jax.ShapeDtypeStruct((8, D), F32)],
        compiler_params=_cp(dimension_semantics=("arbitrary",)),
    )(dh1, mo, cv, zl, yc, yp, gt, lng, lnb, ps, g2, w_co, pool_w, w_po, w_o)


def bwd_in(x, dh1, dcv, dz, glu, ag, dgt, g1, dwk, w_in, tm):
    t = x.shape[0]
    nt = t // tm

    def body(x_ref, dh1_ref, dcv_ref, dcvh_ref, dz_ref, dzh_ref, glu_ref, gluh_ref, ag_ref, dgt_ref, g1_ref,
             k_ref, w_ref, dx_ref, dproj_ref, vec_ref, dk_ref, ext_ref, tmp_ref):
        i = pl.program_id(0)

        @pl.when(i == 0)
        def _():
            vec_ref[...] = jnp.zeros_like(vec_ref)
            dk_ref[...] = jnp.zeros_like(dk_ref)

        first = (i > 0).astype(F32)
        last = (i < nt - 1).astype(F32)
        ext_ref[0:HALO, :] = gluh_ref[...] * first
        ext_ref[HALO:HALO + tm, :] = glu_ref[...]
        rc, cw = 64, 128

        def dk_chunk(cc, carry):
            cols = pl.ds(pl.multiple_of(cc * cw, cw), cw)
            for j in range(KW):
                acc = jnp.zeros((rc, cw), F32)
                for r in range(tm // rc):
                    acc = acc + (dcv_ref[pl.ds(r * rc, rc), cols]
                                 * ext_ref[pl.ds(r * rc + HALO - (KW - 1) + j, rc), cols])
                dk_ref[j:j + 1, cols] += _rowsum(acc)
            return carry

        lax.fori_loop(0, D // cw, dk_chunk, 0)
        ext_ref[0:tm, :] = dcv_ref[...]
        ext_ref[tm:tm + HALO, :] = dcvh_ref[...] * last
        _taps(ext_ref, k_ref, tmp_ref, tm, 0, True)
        dglu = tmp_ref[...]
        a = ag_ref[:, 0:D].astype(F32)
        sg = _sigmoid(ag_ref[:, D:2 * D].astype(F32))
        dproj_ref[:, 0:D] = (dglu * sg).astype(BF16)
        dproj_ref[:, D:2 * D] = (dglu * a * sg * (1.0 - sg)).astype(BF16)
        for g, w in enumerate(POOL_WINDOWS):
            cols = pl.ds(g * GW, GW)
            pos = i * tm + lax.broadcasted_iota(jnp.int32, (tm + HALO, 1), 0) + 1
            inv = 1.0 / jnp.minimum(pos, w).astype(F32)
            ext_ref[0:tm, cols] = dz_ref[:, cols] * inv[0:tm]
            ext_ref[tm:tm + HALO, cols] = dzh_ref[:, cols] * inv[tm:tm + HALO] * last
            acc = ext_ref[pl.ds(0, tm), cols]
            for s in range(1, w):
                acc = acc + ext_ref[pl.ds(s, tm), cols]
            dproj_ref[:, pl.ds(2 * D + g * GW, GW)] = (acc - dz_ref[:, cols]).astype(BF16)
        dproj_ref[:, 3 * D:5 * D] = dgt_ref[...]
        du = jnp.zeros((tm, D), F32)
        for c in range(NPROJ):
            cols = pl.ds(c * D, D)
            du = du + _mm_nt(dproj_ref[:, cols], w_ref[:, cols])
        xf = x_ref[...]
        r1 = lax.rsqrt(jnp.mean(xf * xf, axis=-1, keepdims=True) + RMS_EPS)
        n1 = xf * r1
        vec_ref[0:1, :] += _rowsum(du * n1)
        dn1 = du * g1_ref[...]
        dx_ref[...] = dh1_ref[...] + r1 * (dn1 - n1 * jnp.mean(dn1 * n1, axis=-1, keepdims=True))

    return pl.pallas_call(
        body, name="bwd_in", grid=(nt,),
        in_specs=[_tile(tm, D), _tile(tm, D), _tile(tm, D), _next_halo(tm, nt), _tile(tm, D), _next_halo(tm, nt),
                  _tile(tm, D), _prev_halo(tm), _tile(tm, 2 * D), _tile(tm, 2 * D), _full((1, D)),
                  _full((KW_PAD, D)), _full((D, NPROJ * D))],
        out_specs=[_tile(tm, D), _tile(tm, NPROJ * D), _full((8, D)), _full((KW_PAD, D))],
        out_shape=[jax.ShapeDtypeStruct((t, D), F32), jax.ShapeDtypeStruct((t, NPROJ * D), BF16),
                   jax.ShapeDtypeStruct((8, D), F32), jax.ShapeDtypeStruct((KW_PAD, D), F32)],
        scratch_shapes=[pltpu.VMEM((tm + HALO, D), F32), pltpu.VMEM((tm, D), F32)],
        compiler_params=_cp(dimension_semantics=("arbitrary",)),
    )(x, dh1, dcv, dcv, dz, dz, glu, glu, ag, dgt, g1, dwk, w_in)


def wgrad(a, b, name, bm=512, bn=512, bt=1024):
    t, m = a.shape
    n = b.shape[1]
    bm, bn, bt = min(bm, m), min(bn, n), min(bt, t)
    nk = t // bt

    def body(a_ref, b_ref, o_ref):
        k = pl.program_id(2)

        @pl.when(k == 0)
        def _():
            o_ref[...] = jnp.zeros_like(o_ref)

        o_ref[...] += _mm_tn(a_ref[...], b_ref[...])

    return pl.pallas_call(
        body, name=name, grid=(m // bm, n // bn, nk),
        in_specs=[pl.BlockSpec((bt, bm), lambda i, j, k: (k, i)), pl.BlockSpec((bt, bn), lambda i, j, k: (k, j))],
        out_specs=pl.BlockSpec((bm, bn), lambda i, j, k: (i, j)),
        out_shape=jax.ShapeDtypeStruct((m, n), F32),
        compiler_params=_cp(dimension_semantics=("arbitrary", "arbitrary", "arbitrary")),
    )(a, b)


def wgrad_pool(z, dzl, bt=1024):
    t = z.shape[0]
    bt = min(bt, t)

    def body(a_ref, b_ref, o_ref):
        k = pl.program_id(1)

        @pl.when(k == 0)
        def _():
            o_ref[...] = jnp.zeros_like(o_ref)

        o_ref[0] += _mm_tn(a_ref[...], b_ref[...])

    return pl.pallas_call(
        body, name="wgrad_pool", grid=(NG, t // bt),
        in_specs=[pl.BlockSpec((bt, GW), lambda g, k: (k, g)), pl.BlockSpec((bt, GW), lambda g, k: (k, g))],
        out_specs=pl.BlockSpec((1, GW, GW), lambda g, k: (g, 0, 0)),
        out_shape=jax.ShapeDtypeStruct((NG, GW, GW), F32),
        compiler_params=_cp(dimension_semantics=("arbitrary", "arbitrary")),
    )(z, dzl)


VEC_ROWS = 24
ROW = dict(mlp_pre_g=0, mlp_post_g=1, loss=2, mix_post_g=8, conv_ln_g=9, conv_ln_b=10, dw_bias=11, pool_scale=12,
           mix_pre_g=16)


def local_step(x, tgt, vecs, dwk, w_in, w_co, pool_w, w_po, w_o, w_ff1, w_ff2, tm=256, tm_in=512):
    u, glu, ag, p, gt = fwd_in(x, vecs["mix_pre_g"], w_in, tm_in)
    cv, sw, z, zl, zs, yc, yp, mg, mo, h1 = fwd_mix(
        x, glu, p, gt, dwk, vecs["dw_bias"], vecs["conv_ln_g"], vecs["conv_ln_b"], w_co, pool_w,
        vecs["pool_scale"], w_po, w_o, vecs["mix_post_g"], tm)
    v, a2, df2, df1, dh1, vec_mlp = mlp_fwd_bwd(h1, tgt, vecs["mlp_pre_g"], vecs["mlp_post_g"], w_ff1, w_ff2, tm)
    dmo, dgt, dyc, dyp, dcv, dzl, dz, vec_mix = bwd_mix(
        dh1, mo, cv, zl, yc, yp, gt, vecs["conv_ln_g"], vecs["conv_ln_b"], vecs["pool_scale"], vecs["mix_post_g"],
        w_co, pool_w, w_po, w_o, tm)
    dx, dproj, vec_in, dk = bwd_in(x, dh1, dcv, dz, glu, ag, dgt, vecs["mix_pre_g"], dwk, w_in, tm)
    grads = dict(
        w_in=wgrad(u, dproj, "wgrad_in"),
        w_conv_out=wgrad(sw, dyc, "wgrad_conv_out"),
        pool_w=wgrad_pool(z, dzl),
        w_pool_out=wgrad(zs, dyp, "wgrad_pool_out"),
        w_o=wgrad(mg, dmo, "wgrad_o"),
        w_ff1=wgrad(v, df1, "wgrad_ff1"),
        w_ff2=wgrad(a2, df2, "wgrad_ff2"),
    )
    return dx, grads, dk, jnp.concatenate([vec_mlp, vec_mix, vec_in], axis=0)


MATS = ("w_in", "w_conv_out", "pool_w", "w_pool_out", "w_o", "w_ff1", "w_ff2")
_ROWS = lambda k, h: (2 * k + h, 0)
_COLS = lambda k, h: (h, k)
GEOM = dict(
    w_in=((D, NPROJ * D), (D // 2, NPROJ * D // NCHIP), _COLS),
    w_conv_out=((D, D), (D // (2 * NCHIP), D), _ROWS),
    pool_w=((NG, GW, GW), (NG // 2, GW // NCHIP, GW), lambda k, h: (h, k, 0)),
    w_pool_out=((D, D), (D // (2 * NCHIP), D), _ROWS),
    w_o=((D, D), (D // (2 * NCHIP), D), _ROWS),
    w_ff1=((D, FF), (D // 2, FF // NCHIP), _COLS),
    w_ff2=((FF, D), (FF // (2 * NCHIP), D), _ROWS),
)


def _window(name, k, h):
    _, blk, idx = GEOM[name]
    return tuple(pl.ds(i * b, b) for i, b in zip(idx(k, h), blk))


def _shard_half(name, h):
    _, blk, idx = GEOM[name]
    return tuple(pl.ds(i * b, b) for i, b in zip(idx(0, h), blk))


def _shard_shape(name):
    _, blk, idx = GEOM[name]
    return tuple(b * (i + 1) for i, b in zip(idx(0, 1), blk))


def _place():
    x, y, c = lax.axis_index("x"), lax.axis_index("y"), lax.axis_index("c")
    chips = [(1 - x, y), (x, 1 - y), (1 - x, 1 - y)]
    return x, y, c, 2 * x + y, chips, [2 * px + py for px, py in chips]


ANY = pl.BlockSpec(memory_space=pl.ANY)
VMEM_SPEC = pl.BlockSpec(memory_space=pltpu.VMEM)


def cast_shards(shards):
    n = len(shards)

    def body(*refs):
        for src, dst in zip(refs[:n], refs[n:]):
            dst[...] = src[...].astype(BF16)

    return pl.pallas_call(
        body, name="cast_shards", in_specs=[VMEM_SPEC] * n, out_specs=[VMEM_SPEC] * n,
        out_shape=[jax.ShapeDtypeStruct(s.shape, BF16) for s in shards],
        compiler_params=_cp(),
    )(*shards)


def gather_weights(shards, dwk_shard):
    nm = len(MATS)

    def body(*refs):
        srcs, dwk_src = refs[:nm], refs[nm]
        dsts, dwk_dst = refs[nm + 1:2 * nm + 1], refs[2 * nm + 1]
        send_sems, recv_sems, local_sems = refs[2 * nm + 2:]
        x, y, c, me, chips, chip_ids = _place()
        sibling = (x, y, 1 - c)
        sends, locals_ = [], []
        for w, name in enumerate(MATS):
            for h in range(2):
                cp = pltpu.make_async_copy(srcs[w].at[_shard_half(name, h)], dsts[w].at[_window(name, me, h)],
                                           local_sems.at[w, h])
                cp.start()
                locals_.append(cp)
        cp = pltpu.make_async_copy(dwk_src, dwk_dst.at[:, pl.ds(me * (D // NCHIP), D // NCHIP)], local_sems.at[nm, 0])
        cp.start()
        locals_.append(cp)

        def remote(w, slot, src, dst, to):
            return pltpu.make_async_remote_copy(src_ref=src, dst_ref=dst, send_sem=send_sems.at[w, slot],
                                                recv_sem=recv_sems.at[w, slot], device_id=to, device_id_type=MESH)

        for w, name in enumerate(MATS):
            for j, chip in enumerate(chips):
                cp = remote(w, j, srcs[w].at[_shard_half(name, c)], dsts[w].at[_window(name, me, c)], (*chip, c))
                cp.start()
                sends.append(cp)
        for j, chip in enumerate(chips):
            cp = remote(nm, j, dwk_src, dwk_dst.at[:, pl.ds(me * (D // NCHIP), D // NCHIP)], (*chip, c))
            cp.start()
            sends.append(cp)
        for w, name in enumerate(MATS):
            for j in range(3):
                got = dsts[w].at[_window(name, chip_ids[j], c)]
                remote(w, j, got, got, sibling).wait_recv()
                cp = remote(w, 3 + j, got, got, sibling)
                cp.start()
                sends.append(cp)
        for w, name in enumerate(MATS):
            for j in range(3):
                got = dsts[w].at[_window(name, chip_ids[j], 1 - c)]
                remote(w, 3 + j, got, got, sibling).wait_recv()
        for j in range(3):
            got = dwk_dst.at[:, pl.ds(chip_ids[j] * (D // NCHIP), D // NCHIP)]
            remote(nm, j, got, got, sibling).wait_recv()
        for cp in sends:
            cp.wait_send()
        for cp in locals_:
            cp.wait()

    out_shape = [jax.ShapeDtypeStruct(GEOM[name][0], BF16) for name in MATS] + [jax.ShapeDtypeStruct((KW_PAD, D), F32)]
    return pl.pallas_call(
        body, name="gather_weights", in_specs=[ANY] * (nm + 1), out_specs=[ANY] * (nm + 1), out_shape=out_shape,
        scratch_shapes=[pltpu.SemaphoreType.DMA((nm + 1, 6)), pltpu.SemaphoreType.DMA((nm + 1, 6)),
                        pltpu.SemaphoreType.DMA((nm + 1, 2))],
        compiler_params=_cp(has_side_effects=True),
    )(*shards, dwk_shard)


def allreduce_small(part):
    rows = part.shape[0]
    ndev = 8

    def body(part_ref, sum_ref, loss_ref, buf_ref, send_sems, recv_sems):
        x, y, c = lax.axis_index("x"), lax.axis_index("y"), lax.axis_index("c")
        me = 4 * x + 2 * y + c
        buf_ref[me] = part_ref[...]
        copies = []
        for m in range(1, ndev):
            to = (x ^ (m >> 2), y ^ ((m >> 1) & 1), c ^ (m & 1))
            cp = pltpu.make_async_remote_copy(src_ref=part_ref, dst_ref=buf_ref.at[me], send_sem=send_sems.at[m - 1],
                                              recv_sem=recv_sems.at[m - 1], device_id=to, device_id_type=MESH)
            cp.start()
            copies.append(cp)
        for m in range(1, ndev):
            got = buf_ref.at[me ^ m]
            pltpu.make_async_remote_copy(src_ref=got, dst_ref=got, send_sem=send_sems.at[m - 1],
                                         recv_sem=recv_sems.at[m - 1], device_id=(x, y, c),
                                         device_id_type=MESH).wait_recv()
        for cp in copies:
            cp.wait_send()
        total = buf_ref[0]
        for d in range(1, ndev):
            total = total + buf_ref[d]
        sum_ref[...] = total
        r = ROW["loss"]
        loss_ref[...] = jnp.zeros_like(loss_ref) + jnp.sum(total[r:r + 1, :])

    return pl.pallas_call(
        body, name="allreduce_small", in_specs=[VMEM_SPEC], out_specs=[VMEM_SPEC, VMEM_SPEC],
        out_shape=[jax.ShapeDtypeStruct((rows, D), F32), jax.ShapeDtypeStruct((8, 128), F32)],
        scratch_shapes=[pltpu.VMEM((ndev, rows, D), F32), pltpu.SemaphoreType.DMA((ndev - 1,)),
                        pltpu.SemaphoreType.DMA((ndev - 1,))],
        compiler_params=_cp(has_side_effects=True),
    )(part)


def exchange_pair(grads):
    nm = len(MATS)

    def body(*refs):
        srcs, dsts = refs[:nm], refs[nm:2 * nm]
        send_sems, recv_sems = refs[2 * nm:]
        x, y, c, me, chips, chip_ids = _place()
        sibling = (x, y, 1 - c)
        for w, name in enumerate(MATS):
            for k in range(NCHIP):
                pltpu.make_async_remote_copy(src_ref=srcs[w].at[_window(name, k, 1 - c)], dst_ref=dsts[w].at[k],
                                             send_sem=send_sems.at[w], recv_sem=recv_sems.at[w], device_id=sibling,
                                             device_id_type=MESH).start()
        for w in range(nm):
            pltpu.make_async_remote_copy(src_ref=dsts[w], dst_ref=dsts[w], send_sem=send_sems.at[w],
                                         recv_sem=recv_sems.at[w], device_id=sibling, device_id_type=MESH).wait()

    return pl.pallas_call(
        body, name="exchange_pair", in_specs=[ANY] * nm, out_specs=[ANY] * nm,
        out_shape=[jax.ShapeDtypeStruct((NCHIP, *GEOM[name][1]), F32) for name in MATS],
        scratch_shapes=[pltpu.SemaphoreType.DMA((nm,)), pltpu.SemaphoreType.DMA((nm,))],
        compiler_params=_cp(has_side_effects=True),
    )(*grads)


def pair_add(name, place, grad, from_sibling):
    _, blk, idx = GEOM[name]
    zeros = (0,) * len(blk)

    def body(place_ref, g_ref, s_ref, wire_ref, own_ref):
        k = pl.program_id(0)
        total = g_ref[...] + s_ref[0]
        wire_ref[0] = total.astype(BF16)

        @pl.when(k == place_ref[1])
        def _():
            own_ref[...] = total

    return pl.pallas_call(
        body, name="pair_add_" + name,
        grid_spec=pltpu.PrefetchScalarGridSpec(
            num_scalar_prefetch=1, grid=(NCHIP,),
            in_specs=[pl.BlockSpec(blk, lambda k, pr: idx(k, pr[0])), pl.BlockSpec((1, *blk), lambda k, pr: (k, *zeros))],
            out_specs=[pl.BlockSpec((1, *blk), lambda k, pr: (k, *zeros)), pl.BlockSpec(blk, lambda k, pr: zeros)]),
        out_shape=[jax.ShapeDtypeStruct((NCHIP, *blk), BF16), jax.ShapeDtypeStruct(blk, F32)],
        compiler_params=_cp(dimension_semantics=("arbitrary",)),
    )(place, grad, from_sibling)


def exchange_chips(wires):
    nm = len(MATS)

    def body(*refs):
        srcs, dsts = refs[:nm], refs[nm:2 * nm]
        send_sems, recv_sems = refs[2 * nm:]
        x, y, c, me, chips, chip_ids = _place()
        copies = []
        for w in range(nm):
            for j, chip in enumerate(chips):
                cp = pltpu.make_async_remote_copy(src_ref=srcs[w].at[chip_ids[j]], dst_ref=dsts[w].at[j],
                                                  send_sem=send_sems.at[w, j], recv_sem=recv_sems.at[w, j],
                                                  device_id=(*chip, c), device_id_type=MESH)
                cp.start()
                copies.append(cp)
        for cp in copies:
            cp.wait()

    return pl.pallas_call(
        body, name="exchange_chips", in_specs=[ANY] * nm, out_specs=[ANY] * nm,
        out_shape=[jax.ShapeDtypeStruct((3, *GEOM[name][1]), BF16) for name in MATS],
        scratch_shapes=[pltpu.SemaphoreType.DMA((nm, 3)), pltpu.SemaphoreType.DMA((nm, 3))],
        compiler_params=_cp(has_side_effects=True),
    )(*wires)


def sum_partials(name, place, own, from_chips):
    _, blk, idx = GEOM[name]
    zeros = (0,) * len(blk)

    def body(place_ref, own_ref, r_ref, out_ref):
        total = own_ref[...]
        for j in range(3):
            total = total + r_ref[j].astype(F32)
        out_ref[0] = total

    return pl.pallas_call(
        body, name="sum_partials_" + name,
        grid_spec=pltpu.PrefetchScalarGridSpec(
            num_scalar_prefetch=1, grid=(1,),
            in_specs=[pl.BlockSpec(blk, lambda i, pr: zeros), pl.BlockSpec((3, *blk), lambda i, pr: (0, *zeros))],
            out_specs=[pl.BlockSpec((1, *blk), lambda i, pr: (pr[0], *zeros))]),
        out_shape=[jax.ShapeDtypeStruct((2, *blk), F32)],
        compiler_params=_cp(dimension_semantics=("arbitrary",)),
    )(place, own, from_chips)[0]


def swap_halves(halves):
    nm = len(MATS)

    def body(*refs):
        srcs, dsts = refs[:nm], refs[nm:2 * nm]
        send_sems, recv_sems = refs[2 * nm:]
        x, y, c, me, chips, chip_ids = _place()
        copies = []
        for w in range(nm):
            cp = pltpu.make_async_remote_copy(src_ref=srcs[w].at[c], dst_ref=dsts[w].at[c], send_sem=send_sems.at[w],
                                              recv_sem=recv_sems.at[w], device_id=(x, y, 1 - c), device_id_type=MESH)
            cp.start()
            copies.append(cp)
        for w, cp in enumerate(copies):
            cp.wait_send()
            pltpu.make_async_remote_copy(src_ref=srcs[w].at[1 - c], dst_ref=dsts[w].at[1 - c], send_sem=send_sems.at[w],
                                         recv_sem=recv_sems.at[w], device_id=(x, y, 1 - c),
                                         device_id_type=MESH).wait_recv()

    return pl.pallas_call(
        body, name="swap_halves", in_specs=[ANY] * nm, out_specs=[ANY] * nm,
        out_shape=[jax.ShapeDtypeStruct(h.shape, F32) for h in halves],
        input_output_aliases={w: w for w in range(nm)},
        scratch_shapes=[pltpu.SemaphoreType.DMA((nm,)), pltpu.SemaphoreType.DMA((nm,))],
        compiler_params=_cp(has_side_effects=True),
    )(*halves)


def adamw(name, w, g, m, v, rows_per_block=256):
    r, cdim = w.shape
    br = min(rows_per_block, r)

    def body(w_ref, g_ref, m_ref, v_ref, d_ref, nm_ref, nv_ref):
        gv = g_ref[...]
        mn = B1 * m_ref[...] + (1.0 - B1) * gv
        vn = B2 * v_ref[...] + (1.0 - B2) * (gv * gv)
        m_hat = mn / (1.0 - B1 ** STEP)
        v_hat = vn / (1.0 - B2 ** STEP)
        d_ref[...] = -LR * (m_hat / (jnp.sqrt(v_hat) + ADAM_EPS) + WD * w_ref[...])
        nm_ref[...] = mn
        nv_ref[...] = vn

    spec = pl.BlockSpec((br, cdim), lambda i: (i, 0))
    return pl.pallas_call(
        body, name="adamw_" + name, grid=(r // br,), in_specs=[spec] * 4, out_specs=[spec] * 3,
        out_shape=[jax.ShapeDtypeStruct((r, cdim), F32)] * 3,
        compiler_params=_cp(dimension_semantics=("arbitrary",)),
    )(w, g, m, v)


VECS = ("mix_pre_g", "dw_bias", "conv_ln_g", "conv_ln_b", "pool_scale", "mix_post_g", "mlp_pre_g", "mlp_post_g")
WEIGHTS = ("mix_pre_g", "w_in", "dw_kernel", "dw_bias", "conv_ln_g", "conv_ln_b", "w_conv_out", "pool_w", "pool_scale",
           "w_pool_out", "w_o", "mix_post_g", "mlp_pre_g", "w_ff1", "w_ff2", "mlp_post_g")


def kernel(x, mix_pre_g, w_in, dw_kernel, dw_bias, conv_ln_g, conv_ln_b, w_conv_out, pool_w, pool_scale, w_pool_out, w_o, mix_post_g, mlp_pre_g, w_ff1, w_ff2, mlp_post_g, loss_target, m_mix_pre_g, m_w_in, m_dw_kernel, m_dw_bias, m_conv_ln_g, m_conv_ln_b, m_w_conv_out, m_pool_w, m_pool_scale, m_w_pool_out, m_w_o, m_mix_post_g, m_mlp_pre_g, m_w_ff1, m_w_ff2, m_mlp_post_g, v_mix_pre_g, v_w_in, v_dw_kernel, v_dw_bias, v_conv_ln_g, v_conv_ln_b, v_w_conv_out, v_pool_w, v_pool_scale, v_w_pool_out, v_w_o, v_mix_post_g, v_mlp_pre_g, v_w_ff1, v_w_ff2, v_mlp_post_g):
    w = dict(mix_pre_g=mix_pre_g, w_in=w_in, dw_kernel=dw_kernel, dw_bias=dw_bias, conv_ln_g=conv_ln_g,
             conv_ln_b=conv_ln_b, w_conv_out=w_conv_out, pool_w=pool_w, pool_scale=pool_scale, w_pool_out=w_pool_out,
             w_o=w_o, mix_post_g=mix_post_g, mlp_pre_g=mlp_pre_g, w_ff1=w_ff1, w_ff2=w_ff2, mlp_post_g=mlp_post_g)
    m = dict(mix_pre_g=m_mix_pre_g, w_in=m_w_in, dw_kernel=m_dw_kernel, dw_bias=m_dw_bias, conv_ln_g=m_conv_ln_g,
             conv_ln_b=m_conv_ln_b, w_conv_out=m_w_conv_out, pool_w=m_pool_w, pool_scale=m_pool_scale,
             w_pool_out=m_w_pool_out, w_o=m_w_o, mix_post_g=m_mix_post_g, mlp_pre_g=m_mlp_pre_g, w_ff1=m_w_ff1,
             w_ff2=m_w_ff2, mlp_post_g=m_mlp_post_g)
    v = dict(mix_pre_g=v_mix_pre_g, w_in=v_w_in, dw_kernel=v_dw_kernel, dw_bias=v_dw_bias, conv_ln_g=v_conv_ln_g,
             conv_ln_b=v_conv_ln_b, w_conv_out=v_w_conv_out, pool_w=v_pool_w, pool_scale=v_pool_scale,
             w_pool_out=v_w_pool_out, w_o=v_w_o, mix_post_g=v_mix_post_g, mlp_pre_g=v_mlp_pre_g, w_ff1=v_w_ff1,
             w_ff2=v_w_ff2, mlp_post_g=v_mlp_post_g)
    cx, cy, cc = lax.axis_index("x"), lax.axis_index("y"), lax.axis_index("c")
    chip = 2 * cx + cy
    place = jnp.stack([cc, chip]).astype(jnp.int32)
    dsh = D // NCHIP

    shards = cast_shards([w[name] for name in MATS])
    dwk_shard = jnp.pad(dw_kernel, ((0, KW_PAD - KW), (0, 0)))
    *full, dwk = gather_weights(shards, dwk_shard)
    full = dict(zip(MATS, full))
    vecs = {name: w[name].reshape(1, D) for name in VECS}

    dx, grads, dk, vec = local_step(x[0], loss_target[0], vecs, dwk, *[full[name] for name in MATS])

    small, loss8 = allreduce_small(jnp.concatenate([vec, dk], axis=0))
    loss = loss8[0, 0]
    g = {name: small[ROW[name]] for name in VECS}
    g["dw_kernel"] = lax.dynamic_slice(small[VEC_ROWS:VEC_ROWS + KW_PAD], (0, chip * dsh), (KW_PAD, dsh))

    glist = [grads[name] for name in MATS]
    from_sibling = exchange_pair(glist)
    wires, owns = zip(*[pair_add(name, place, gr, fs) for name, gr, fs in zip(MATS, glist, from_sibling)])
    from_chips = exchange_chips(wires)
    halves = [sum_partials(name, place, own, fc) for name, own, fc in zip(MATS, owns, from_chips)]
    reduced = swap_halves(halves)
    for name, red in zip(MATS, reduced):
        g[name] = red.reshape(_shard_shape(name))

    delta, new_m, new_v = {}, {}, {}
    for name in MATS:
        shp = w[name].shape
        two_d = (-1, shp[-1])
        res = adamw(name, w[name].reshape(two_d), g[name].reshape(two_d), m[name].reshape(two_d),
                    v[name].reshape(two_d))
        delta[name], new_m[name], new_v[name] = [r.reshape(shp) for r in res]
    stack = lambda d: jnp.concatenate([d[name].reshape(1, D) for name in VECS], axis=0)
    res = adamw("vectors", stack(w), stack(g), stack(m), stack(v))
    for i, name in enumerate(VECS):
        delta[name], new_m[name], new_v[name] = [r[i] for r in res]
    padk = lambda a: jnp.pad(a, ((0, KW_PAD - KW), (0, 0)))
    res = adamw("dw_kernel", padk(w["dw_kernel"]), g["dw_kernel"], padk(m["dw_kernel"]), padk(v["dw_kernel"]))
    delta["dw_kernel"], new_m["dw_kernel"], new_v["dw_kernel"] = [r[:KW] for r in res]
    g["dw_kernel"] = g["dw_kernel"][:KW]

    return (loss, dx[None], *[g[n] for n in WEIGHTS], *[delta[n] for n in WEIGHTS], *[new_m[n] for n in WEIGHTS],
            *[new_v[n] for n in WEIGHTS])
```

```python
import functools

import jax
import jax.numpy as jnp
from jax import lax
from jax.experimental import pallas as pl
from jax.experimental.pallas import tpu as pltpu

F32 = jnp.float32
BF16 = jnp.bfloat16

D = 1024
FF = 4096
NPROJ = 5
KW = 31
KW_PAD = 32
SUBLANES = 8
LANES = 128
HALO = 32
POOL_WINDOWS = (2, 4, 8, 16)
NG = 4
GW = D // NG
RMS_EPS = 1e-6
LN_EPS = 1e-5
LR, B1, B2, ADAM_EPS, WD, STEP = 0.001, 0.9, 0.999, 1e-08, 0.01, 10
NCHIP = 4
VMEM_LIMIT = 60 * 1024 * 1024
MESH = pl.DeviceIdType.MESH


def _cp(**kw):
    return pltpu.CompilerParams(vmem_limit_bytes=VMEM_LIMIT, **kw)


def _mm(a, b):
    return jnp.dot(a, b, preferred_element_type=F32)


def _mm_nt(a, b):
    return lax.dot_general(a, b, (((1,), (1,)), ((), ())), preferred_element_type=F32)


def _mm_tn(a, b):
    return lax.dot_general(a, b, (((0,), (0,)), ((), ())), preferred_element_type=F32)


def _sigmoid(x):
    return 1.0 / (1.0 + jnp.exp(-x))


def _rowsum(x):
    return jnp.sum(x, axis=0, keepdims=True)


def _full(shape):
    return pl.BlockSpec(shape, lambda i: (0,) * len(shape))


def _tile(tm, cols):
    return pl.BlockSpec((tm, cols), lambda i: (i, 0))


def _prev_halo(tm):
    return pl.BlockSpec((HALO, D), lambda i: (jnp.maximum(i * (tm // HALO) - 1, 0), 0))


def _next_halo(tm, nt):
    return pl.BlockSpec((HALO, D), lambda i: (jnp.minimum((i + 1) * (tm // HALO), nt * (tm // HALO) - 1), 0))


def _taps_scratch(tm):
    return pltpu.VMEM((SUBLANES, tm + HALO, LANES), F32)


def _taps(src_ref, k_ref, dst_ref, sh_ref, tm, off0, reverse):
    rc, cw = 64, LANES

    def col_chunk(cc, carry):
        cols = pl.ds(pl.multiple_of(cc * cw, cw), cw)
        for q in range(SUBLANES):
            n = tm + SUBLANES * (len(range(q, KW, SUBLANES)) - 1)
            sh_ref[q, 0:n, :] = src_ref[pl.ds(off0 + q, n), cols]
        for r in range(tm // rc):
            acc = jnp.zeros((rc, cw), F32)
            for q in range(SUBLANES):
                for a, j in enumerate(range(q, KW, SUBLANES)):
                    kj = KW - 1 - j if reverse else j
                    acc = acc + k_ref[kj:kj + 1, cols] * sh_ref[q, pl.ds(r * rc + SUBLANES * a, rc), :]
            dst_ref[pl.ds(r * rc, rc), cols] = acc
        return carry

    lax.fori_loop(0, D // cw, col_chunk, 0)


def fwd_in(x, g1, w_in, tm):
    t = x.shape[0]

    def body(x_ref, g_ref, w_ref, u_ref, glu_ref, ag_ref, p_ref, gt_ref):
        xf = x_ref[...]
        r = lax.rsqrt(jnp.mean(xf * xf, axis=-1, keepdims=True) + RMS_EPS)
        u = (xf * r * g_ref[...]).astype(BF16)
        u_ref[...] = u
        a = _mm(u, w_ref[:, 0:D])
        gate = _mm(u, w_ref[:, D:2 * D])
        glu_ref[...] = a * _sigmoid(gate)
        ag_ref[:, 0:D] = a.astype(BF16)
        ag_ref[:, D:2 * D] = gate.astype(BF16)
        p_ref[...] = _mm(u, w_ref[:, 2 * D:3 * D])
        gt_ref[:, 0:D] = _mm(u, w_ref[:, 3 * D:4 * D]).astype(BF16)
        gt_ref[:, D:2 * D] = _mm(u, w_ref[:, 4 * D:5 * D]).astype(BF16)

    return pl.pallas_call(
        body, name="fwd_in", grid=(t // tm,),
        in_specs=[_tile(tm, D), _full((1, D)), _full((D, NPROJ * D))],
        out_specs=[_tile(tm, D), _tile(tm, D), _tile(tm, 2 * D), _tile(tm, D), _tile(tm, 2 * D)],
        out_shape=[jax.ShapeDtypeStruct((t, D), BF16), jax.ShapeDtypeStruct((t, D), F32),
                   jax.ShapeDtypeStruct((t, 2 * D), BF16), jax.ShapeDtypeStruct((t, D), F32),
                   jax.ShapeDtypeStruct((t, 2 * D), BF16)],
        compiler_params=_cp(dimension_semantics=("arbitrary",)),
    )(x, g1, w_in)


def _pool_inv_count(i, tm, w):
    pos = i * tm + lax.broadcasted_iota(jnp.int32, (tm, 1), 0) + 1
    return 1.0 / jnp.minimum(pos, w).astype(F32)


def _window_sum(src_ref, tmp_ref, cols, tm, w, causal):
    lo, hi = 0, tm + HALO
    cur, span = None, 1
    while span < w:
        new_lo, new_hi = (lo + SUBLANES, hi) if causal else (lo, hi - SUBLANES)
        far = new_lo - span if causal else new_lo + span
        n = new_hi - new_lo
        if cur is None:
            near_v, far_v = src_ref[pl.ds(new_lo, n), cols], src_ref[pl.ds(far, n), cols]
        else:
            near_v = cur[new_lo - lo:new_lo - lo + n]
            if span % SUBLANES == 0:
                far_v = cur[far - lo:far - lo + n]
            else:
                tmp_ref[pl.ds(lo, hi - lo), :] = cur
                far_v = tmp_ref[pl.ds(far, n), :]
        cur, lo, hi, span = near_v + far_v, new_lo, new_hi, 2 * span
    off = HALO if causal else 0
    return cur[off - lo:off - lo + tm]


def fwd_mix(x, glu, p, gt, dwk, dwb, lng, lnb, w_co, pool_w, ps, w_po, w_o, g2, tm):
    t = x.shape[0]

    def body(x_ref, glu_ref, gluh_ref, p_ref, ph_ref, gt_ref, k_ref, b_ref, lg_ref, lb_ref, wco_ref, pw_ref,
             ps_ref, wpo_ref, wo_ref, g2_ref,
             cv_ref, sw_ref, z_ref, zl_ref, zs_ref, yc_ref, yp_ref, mg_ref, mo_ref, h1_ref, ext_ref, win_ref, sh_ref):
        i = pl.program_id(0)
        keep = (i > 0).astype(F32)
        ext_ref[0:HALO, :] = gluh_ref[...] * keep
        ext_ref[HALO:HALO + tm, :] = glu_ref[...]
        _taps(ext_ref, k_ref, cv_ref, sh_ref, tm, HALO - (KW - 1), False)
        cv = cv_ref[...] + b_ref[...]
        cv_ref[...] = cv
        mu = jnp.mean(cv, axis=-1, keepdims=True)
        cen = cv - mu
        rstd = lax.rsqrt(jnp.mean(cen * cen, axis=-1, keepdims=True) + LN_EPS)
        ln = cen * rstd * lg_ref[...] + lb_ref[...]
        sw = (ln * _sigmoid(ln)).astype(BF16)
        sw_ref[...] = sw
        yc = _mm(sw, wco_ref[...])
        yc_ref[...] = yc.astype(BF16)
        ext_ref[0:HALO, :] = ph_ref[...] * keep
        ext_ref[HALO:HALO + tm, :] = p_ref[...]
        for g, w in enumerate(POOL_WINDOWS):
            cols = pl.ds(g * GW, GW)
            acc = _window_sum(ext_ref, win_ref, cols, tm, w, True)
            zg = (acc * _pool_inv_count(i, tm, w) - p_ref[:, cols]).astype(BF16)
            z_ref[:, cols] = zg
            zl_ref[:, cols] = _mm(zg, pw_ref[g])
        zl = zl_ref[...]
        zs = (zl * ps_ref[...]).astype(BF16)
        zs_ref[...] = zs
        yp = _mm(zs, wpo_ref[...])
        yp_ref[...] = yp.astype(BF16)
        gc = _sigmoid(gt_ref[:, 0:D].astype(F32))
        gp = _sigmoid(gt_ref[:, D:2 * D].astype(F32))
        mg = (gc * yc + gp * yp).astype(BF16)
        mg_ref[...] = mg
        mo = _mm(mg, wo_ref[...])
        mo_ref[...] = mo
        r2 = lax.rsqrt(jnp.mean(mo * mo, axis=-1, keepdims=True) + RMS_EPS)
        h1_ref[...] = x_ref[...] + mo * r2 * g2_ref[...]

    vec = _full((1, D))
    act = lambda dt: jax.ShapeDtypeStruct((t, D), dt)
    return pl.pallas_call(
        body, name="fwd_mix", grid=(t // tm,),
        in_specs=[_tile(tm, D), _tile(tm, D), _prev_halo(tm), _tile(tm, D), _prev_halo(tm), _tile(tm, 2 * D),
                  _full((KW_PAD, D)), vec, vec, vec, _full((D, D)), _full((NG, GW, GW)), vec, _full((D, D)),
                  _full((D, D)), vec],
        out_specs=[_tile(tm, D)] * 10,
        out_shape=[act(F32), act(BF16), act(BF16), act(F32), act(BF16), act(BF16), act(BF16), act(BF16), act(F32),
                   act(F32)],
        scratch_shapes=[pltpu.VMEM((tm + HALO, D), F32), pltpu.VMEM((tm + HALO, GW), F32), _taps_scratch(tm)],
        compiler_params=_cp(dimension_semantics=("arbitrary",)),
    )(x, glu, glu, p, p, gt, dwk, dwb, lng, lnb, w_co, pool_w, ps, w_po, w_o, g2)


def mlp_fwd_bwd(h1, tgt, g3, g4, w1, w2, tm):
    t = h1.shape[0]
    fc = 1024

    def body(h1_ref, tgt_ref, g3_ref, g4_ref, w1_ref, w2_ref,
             v_ref, a2_ref, df2_ref, df1_ref, dh1_ref, vec_ref, f1_ref):
        i = pl.program_id(0)

        @pl.when(i == 0)
        def _():
            vec_ref[...] = jnp.zeros_like(vec_ref)

        h1v = h1_ref[...]
        r3 = lax.rsqrt(jnp.mean(h1v * h1v, axis=-1, keepdims=True) + RMS_EPS)
        n3 = h1v * r3
        v = (n3 * g3_ref[...]).astype(BF16)
        v_ref[...] = v
        f2 = jnp.zeros((tm, D), F32)
        for c in range(FF // fc):
            cols = pl.ds(c * fc, fc)
            f1 = jnp.maximum(_mm(v, w1_ref[:, cols]), 0.0)
            f1_ref[:, cols] = f1
            a2 = (f1 * f1).astype(BF16)
            a2_ref[:, cols] = a2
            f2 = f2 + _mm(a2, w2_ref[cols, :])
        r4 = lax.rsqrt(jnp.mean(f2 * f2, axis=-1, keepdims=True) + RMS_EPS)
        n4 = f2 * r4
        err = h1v + n4 * g4_ref[...] - tgt_ref[...]
        vec_ref[2:3, :] += _rowsum(err * err) * (0.5 / D)
        dh2 = err * (1.0 / D)
        vec_ref[1:2, :] += _rowsum(dh2 * n4)
        dn4 = dh2 * g4_ref[...]
        df2 = (r4 * (dn4 - n4 * jnp.mean(dn4 * n4, axis=-1, keepdims=True))).astype(BF16)
        df2_ref[...] = df2
        dv = jnp.zeros((tm, D), F32)
        for c in range(FF // fc):
            cols = pl.ds(c * fc, fc)
            da2 = _mm_nt(df2, w2_ref[cols, :])
            df1 = (da2 * (2.0 * f1_ref[:, cols])).astype(BF16)
            df1_ref[:, cols] = df1
            dv = dv + _mm_nt(df1, w1_ref[:, cols])
        vec_ref[0:1, :] += _rowsum(dv * n3)
        dn3 = dv * g3_ref[...]
        dh1_ref[...] = dh2 + r3 * (dn3 - n3 * jnp.mean(dn3 * n3, axis=-1, keepdims=True))

    vec = _full((1, D))
    return pl.pallas_call(
        body, name="mlp_fwd_bwd", grid=(t // tm,),
        in_specs=[_tile(tm, D), _tile(tm, D), vec, vec, _full((D, FF)), _full((FF, D))],
        out_specs=[_tile(tm, D), _tile(tm, FF), _tile(tm, D), _tile(tm, FF), _tile(tm, D), _full((8, D))],
        out_shape=[jax.ShapeDtypeStruct((t, D), BF16), jax.ShapeDtypeStruct((t, FF), BF16),
                   jax.ShapeDtypeStruct((t, D), BF16), jax.ShapeDtypeStruct((t, FF), BF16),
                   jax.ShapeDtypeStruct((t, D), F32), jax.ShapeDtypeStruct((8, D), F32)],
        scratch_shapes=[pltpu.VMEM((tm, FF), F32)],
        compiler_params=_cp(dimension_semantics=("arbitrary",)),
    )(h1, tgt, g3, g4, w1, w2)


def bwd_mix(dh1, mo, cv, zl, yc, yp, gt, lng, lnb, ps, g2, w_co, pool_w, w_po, w_o, tm):
    t = dh1.shape[0]

    def body(dh1_ref, mo_ref, cv_ref, zl_ref, yc_ref, yp_ref, gt_ref, lg_ref, lb_ref, ps_ref, g2_ref,
             wco_ref, pw_ref, wpo_ref, wo_ref,
             dmo_ref, dgt_ref, dyc_ref, dyp_ref, dcv_ref, dzl_ref, dz_ref, vec_ref):
        i = pl.program_id(0)

        @pl.when(i == 0)
        def _():
            vec_ref[...] = jnp.zeros_like(vec_ref)

        dh1v = dh1_ref[...]
        mo = mo_ref[...]
        r2 = lax.rsqrt(jnp.mean(mo * mo, axis=-1, keepdims=True) + RMS_EPS)
        n2 = mo * r2
        vec_ref[0:1, :] += _rowsum(dh1v * n2)
        dn2 = dh1v * g2_ref[...]
        dmo = (r2 * (dn2 - n2 * jnp.mean(dn2 * n2, axis=-1, keepdims=True))).astype(BF16)
        dmo_ref[...] = dmo
        dmg = _mm_nt(dmo, wo_ref[...])
        gc = _sigmoid(gt_ref[:, 0:D].astype(F32))
        gp = _sigmoid(gt_ref[:, D:2 * D].astype(F32))
        dgt_ref[:, 0:D] = (dmg * yc_ref[...].astype(F32) * gc * (1.0 - gc)).astype(BF16)
        dgt_ref[:, D:2 * D] = (dmg * yp_ref[...].astype(F32) * gp * (1.0 - gp)).astype(BF16)
        dyc = (dmg * gc).astype(BF16)
        dyp = (dmg * gp).astype(BF16)
        dyc_ref[...] = dyc
        dyp_ref[...] = dyp
        dsw = _mm_nt(dyc, wco_ref[...])
        cv = cv_ref[...]
        mu = jnp.mean(cv, axis=-1, keepdims=True)
        cen = cv - mu
        rstd = lax.rsqrt(jnp.mean(cen * cen, axis=-1, keepdims=True) + LN_EPS)
        y = cen * rstd
        ln = y * lg_ref[...] + lb_ref[...]
        sg = _sigmoid(ln)
        dln = dsw * (sg * (1.0 + ln * (1.0 - sg)))
        vec_ref[1:2, :] += _rowsum(dln * y)
        vec_ref[2:3, :] += _rowsum(dln)
        dy = dln * lg_ref[...]
        dcv = rstd * (dy - jnp.mean(dy, axis=-1, keepdims=True) - y * jnp.mean(dy * y, axis=-1, keepdims=True))
        dcv_ref[...] = dcv
        vec_ref[3:4, :] += _rowsum(dcv)
        dzs = _mm_nt(dyp, wpo_ref[...])
        vec_ref[4:5, :] += _rowsum(dzs * zl_ref[...])
        dzl = (dzs * ps_ref[...]).astype(BF16)
        dzl_ref[...] = dzl
        for g in range(NG):
            cols = pl.ds(g * GW, GW)
            dz_ref[:, cols] = _mm_nt(dzl_ref[:, cols], pw_ref[g])

    vec = _full((1, D))
    act = lambda dt: jax.ShapeDtypeStruct((t, D), dt)
    return pl.pallas_call(
        body, name="bwd_mix", grid=(t // tm,),
        in_specs=[_tile(tm, D)] * 6 + [_tile(tm, 2 * D), vec, vec, vec, vec, _full((D, D)), _full((NG, GW, GW)),
                                       _full((D, D)), _full((D, D))],
        out_specs=[_tile(tm, D), _tile(tm, 2 * D)] + [_tile(tm, D)] * 5 + [_full((8, D))],
        out_shape=[act(BF16), jax.ShapeDtypeStruct((t, 2 * D), BF16), act(BF16), act(BF16), act(F32), act(BF16),
                   act(F32), jax.ShapeDtypeStruct((8, D), F32)],
        compiler_params=_cp(dimension_semantics=("arbitrary",)),
    )(dh1, mo, cv, zl, yc, yp, gt, lng, lnb, ps, g2, w_co, pool_w, w_po, w_o)


def bwd_in(x, dh1, dcv, dz, glu, ag, dgt, g1, dwk, w_in, tm):
    t = x.shape[0]
    nt = t // tm

    def body(x_ref, dh1_ref, dcv_ref, dcvh_ref, dz_ref, dzh_ref, glu_ref, gluh_ref, ag_ref, dgt_ref, g1_ref,
             k_ref, w_ref, dx_ref, dproj_ref, vec_ref, dk_ref, ext_ref, tmp_ref, win_ref, sh_ref):
        i = pl.program_id(0)

        @pl.when(i == 0)
        def _():
            vec_ref[...] = jnp.zeros_like(vec_ref)
            dk_ref[...] = jnp.zeros_like(dk_ref)

        first = (i > 0).astype(F32)
        last = (i < nt - 1).astype(F32)
        ext_ref[0:HALO, :] = gluh_ref[...] * first
        ext_ref[HALO:HALO + tm, :] = glu_ref[...]
        rc, cw = 64, 128

        def dk_chunk(cc, carry):
            cols = pl.ds(pl.multiple_of(cc * cw, cw), cw)
            for q in range(SUBLANES):
                taps = range(q, KW, SUBLANES)
                n = tm + SUBLANES * (len(taps) - 1)
                sh_ref[q, 0:n, :] = ext_ref[pl.ds(HALO - (KW - 1) + q, n), cols]
                accs = [jnp.zeros((SUBLANES, cw), F32) for _ in taps]
                for r in range(tm // rc):
                    dchunk = dcv_ref[pl.ds(r * rc, rc), cols]
                    for a in range(len(taps)):
                        prod = dchunk * sh_ref[q, pl.ds(r * rc + SUBLANES * a, rc), :]
                        accs[a] = accs[a] + jnp.sum(prod.reshape(rc // SUBLANES, SUBLANES, cw), axis=0)
                for a, j in enumerate(taps):
                    dk_ref[j:j + 1, cols] += _rowsum(accs[a])
            return carry

        lax.fori_loop(0, D // cw, dk_chunk, 0)
        ext_ref[0:tm, :] = dcv_ref[...]
        ext_ref[tm:tm + HALO, :] = dcvh_ref[...] * last
        _taps(ext_ref, k_ref, tmp_ref, sh_ref, tm, 0, True)
        dglu = tmp_ref[...]
        a = ag_ref[:, 0:D].astype(F32)
        sg = _sigmoid(ag_ref[:, D:2 * D].astype(F32))
        dproj_ref[:, 0:D] = (dglu * sg).astype(BF16)
        dproj_ref[:, D:2 * D] = (dglu * a * sg * (1.0 - sg)).astype(BF16)
        for g, w in enumerate(POOL_WINDOWS):
            cols = pl.ds(g * GW, GW)
            pos = i * tm + lax.broadcasted_iota(jnp.int32, (tm + HALO, 1), 0) + 1
            inv = 1.0 / jnp.minimum(pos, w).astype(F32)
            ext_ref[0:tm, cols] = dz_ref[:, cols] * inv[0:tm]
            ext_ref[tm:tm + HALO, cols] = dzh_ref[:, cols] * inv[tm:tm + HALO] * last
            acc = _window_sum(ext_ref, win_ref, cols, tm, w, False)
            dproj_ref[:, pl.ds(2 * D + g * GW, GW)] = (acc - dz_ref[:, cols]).astype(BF16)
        dproj_ref[:, 3 * D:5 * D] = dgt_ref[...]
        du = jnp.zeros((tm, D), F32)
        for c in range(NPROJ):
            cols = pl.ds(c * D, D)
            du = du + _mm_nt(dproj_ref[:, cols], w_ref[:, cols])
        xf = x_ref[...]
        r1 = lax.rsqrt(jnp.mean(xf * xf, axis=-1, keepdims=True) + RMS_EPS)
        n1 = xf * r1
        vec_ref[0:1, :] += _rowsum(du * n1)
        dn1 = du * g1_ref[...]
        dx_ref[...] = dh1_ref[...] + r1 * (dn1 - n1 * jnp.mean(dn1 * n1, axis=-1, keepdims=True))

    return pl.pallas_call(
        body, name="bwd_in", grid=(nt,),
        in_specs=[_tile(tm, D), _tile(tm, D), _tile(tm, D), _next_halo(tm, nt), _tile(tm, D), _next_halo(tm, nt),
                  _tile(tm, D), _prev_halo(tm), _tile(tm, 2 * D), _tile(tm, 2 * D), _full((1, D)),
                  _full((KW_PAD, D)), _full((D, NPROJ * D))],
        out_specs=[_tile(tm, D), _tile(tm, NPROJ * D), _full((8, D)), _full((KW_PAD, D))],
        out_shape=[jax.ShapeDtypeStruct((t, D), F32), jax.ShapeDtypeStruct((t, NPROJ * D), BF16),
                   jax.ShapeDtypeStruct((8, D), F32), jax.ShapeDtypeStruct((KW_PAD, D), F32)],
        scratch_shapes=[pltpu.VMEM((tm + HALO, D), F32), pltpu.VMEM((tm, D), F32), pltpu.VMEM((tm + HALO, GW), F32),
                        _taps_scratch(tm)],
        compiler_params=_cp(dimension_semantics=("arbitrary",)),
    )(x, dh1, dcv, dcv, dz, dz, glu, glu, ag, dgt, g1, dwk, w_in)


def wgrad(a, b, name, bm=1024, bn=1024, bt=512):
    t, m = a.shape
    n = b.shape[1]
    bm, bn, bt = min(bm, m), min(bn, n), min(bt, t)
    assert m % bm == 0 and n % bn == 0 and t % bt == 0, (a.shape, b.shape, bm, bn, bt)
    nk = t // bt

    def body(a_ref, b_ref, o_ref):
        k = pl.program_id(2)

        @pl.when(k == 0)
        def _():
            o_ref[...] = jnp.zeros_like(o_ref)

        o_ref[...] += _mm_tn(a_ref[...], b_ref[...])

    return pl.pallas_call(
        body, name=name, grid=(m // bm, n // bn, nk),
        in_specs=[pl.BlockSpec((bt, bm), lambda i, j, k: (k, i)), pl.BlockSpec((bt, bn), lambda i, j, k: (k, j))],
        out_specs=pl.BlockSpec((bm, bn), lambda i, j, k: (i, j)),
        out_shape=jax.ShapeDtypeStruct((m, n), F32),
        compiler_params=_cp(dimension_semantics=("arbitrary", "arbitrary", "arbitrary")),
    )(a, b)


def wgrad_pool(z, dzl, bt=1024):
    t = z.shape[0]
    bt = min(bt, t)

    def body(a_ref, b_ref, o_ref):
        k = pl.program_id(1)

        @pl.when(k == 0)
        def _():
            o_ref[...] = jnp.zeros_like(o_ref)

        o_ref[0] += _mm_tn(a_ref[...], b_ref[...])

    return pl.pallas_call(
        body, name="wgrad_pool", grid=(NG, t // bt),
        in_specs=[pl.BlockSpec((bt, GW), lambda g, k: (k, g)), pl.BlockSpec((bt, GW), lambda g, k: (k, g))],
        out_specs=pl.BlockSpec((1, GW, GW), lambda g, k: (g, 0, 0)),
        out_shape=jax.ShapeDtypeStruct((NG, GW, GW), F32),
        compiler_params=_cp(dimension_semantics=("arbitrary", "arbitrary")),
    )(z, dzl)


VEC_ROWS = 24
ROW = dict(mlp_pre_g=0, mlp_post_g=1, loss=2, mix_post_g=8, conv_ln_g=9, conv_ln_b=10, dw_bias=11, pool_scale=12,
           mix_pre_g=16)


def local_step(x, tgt, vecs, dwk, w_in, w_co, pool_w, w_po, w_o, w_ff1, w_ff2, tm=256, tm_in=512):
    u, glu, ag, p, gt = fwd_in(x, vecs["mix_pre_g"], w_in, tm_in)
    cv, sw, z, zl, zs, yc, yp, mg, mo, h1 = fwd_mix(
        x, glu, p, gt, dwk, vecs["dw_bias"], vecs["conv_ln_g"], vecs["conv_ln_b"], w_co, pool_w,
        vecs["pool_scale"], w_po, w_o, vecs["mix_post_g"], tm)
    v, a2, df2, df1, dh1, vec_mlp = mlp_fwd_bwd(h1, tgt, vecs["mlp_pre_g"], vecs["mlp_post_g"], w_ff1, w_ff2, tm)
    dmo, dgt, dyc, dyp, dcv, dzl, dz, vec_mix = bwd_mix(
        dh1, mo, cv, zl, yc, yp, gt, vecs["conv_ln_g"], vecs["conv_ln_b"], vecs["pool_scale"], vecs["mix_post_g"],
        w_co, pool_w, w_po, w_o, tm)
    dx, dproj, vec_in, dk = bwd_in(x, dh1, dcv, dz, glu, ag, dgt, vecs["mix_pre_g"], dwk, w_in, tm)
    grads = dict(
        w_in=wgrad(u, dproj, "wgrad_in", bn=NPROJ * D // NCHIP),
        w_conv_out=wgrad(sw, dyc, "wgrad_conv_out"),
        pool_w=wgrad_pool(z, dzl),
        w_pool_out=wgrad(zs, dyp, "wgrad_pool_out"),
        w_o=wgrad(mg, dmo, "wgrad_o"),
        w_ff1=wgrad(v, df1, "wgrad_ff1"),
        w_ff2=wgrad(a2, df2, "wgrad_ff2"),
    )
    return dx, grads, dk, jnp.concatenate([vec_mlp, vec_mix, vec_in], axis=0)


MATS = ("w_in", "w_conv_out", "pool_w", "w_pool_out", "w_o", "w_ff1", "w_ff2")
_ROWS = lambda k, h: (2 * k + h, 0)
_COLS = lambda k, h: (h, k)
GEOM = dict(
    w_in=((D, NPROJ * D), (D // 2, NPROJ * D // NCHIP), _COLS),
    w_conv_out=((D, D), (D // (2 * NCHIP), D), _ROWS),
    pool_w=((NG, GW, GW), (NG // 2, GW // NCHIP, GW), lambda k, h: (h, k, 0)),
    w_pool_out=((D, D), (D // (2 * NCHIP), D), _ROWS),
    w_o=((D, D), (D // (2 * NCHIP), D), _ROWS),
    w_ff1=((D, FF), (D // 2, FF // NCHIP), _COLS),
    w_ff2=((FF, D), (FF // (2 * NCHIP), D), _ROWS),
)


def _window(name, k, h):
    _, blk, idx = GEOM[name]
    return tuple(pl.ds(i * b, b) for i, b in zip(idx(k, h), blk))


def _shard_half(name, h):
    _, blk, idx = GEOM[name]
    return tuple(pl.ds(i * b, b) for i, b in zip(idx(0, h), blk))


def _shard_shape(name):
    _, blk, idx = GEOM[name]
    return tuple(b * (i + 1) for i, b in zip(idx(0, 1), blk))


def _place():
    x, y, c = lax.axis_index("x"), lax.axis_index("y"), lax.axis_index("c")
    chips = [(1 - x, y), (x, 1 - y), (1 - x, 1 - y)]
    return x, y, c, 2 * x + y, chips, [2 * px + py for px, py in chips]


ANY = pl.BlockSpec(memory_space=pl.ANY)
VMEM_SPEC = pl.BlockSpec(memory_space=pltpu.VMEM)


def cast_shards(shards):
    n = len(shards)

    def body(*refs):
        for src, dst in zip(refs[:n], refs[n:]):
            dst[...] = src[...].astype(BF16)

    return pl.pallas_call(
        body, name="cast_shards", in_specs=[VMEM_SPEC] * n, out_specs=[VMEM_SPEC] * n,
        out_shape=[jax.ShapeDtypeStruct(s.shape, BF16) for s in shards],
        compiler_params=_cp(),
    )(*shards)


def gather_weights(shards, dwk_shard):
    nm = len(MATS)

    def body(*refs):
        srcs, dwk_src = refs[:nm], refs[nm]
        dsts, dwk_dst = refs[nm + 1:2 * nm + 1], refs[2 * nm + 1]
        send_sems, recv_sems, local_sems = refs[2 * nm + 2:]
        x, y, c, me, chips, chip_ids = _place()
        sibling = (x, y, 1 - c)
        sends, locals_ = [], []
        for w, name in enumerate(MATS):
            for h in range(2):
                cp = pltpu.make_async_copy(srcs[w].at[_shard_half(name, h)], dsts[w].at[_window(name, me, h)],
                                           local_sems.at[w, h])
                cp.start()
                locals_.append(cp)
        cp = pltpu.make_async_copy(dwk_src, dwk_dst.at[:, pl.ds(me * (D // NCHIP), D // NCHIP)], local_sems.at[nm, 0])
        cp.start()
        locals_.append(cp)

        def remote(w, slot, src, dst, to):
            return pltpu.make_async_remote_copy(src_ref=src, dst_ref=dst, send_sem=send_sems.at[w, slot],
                                                recv_sem=recv_sems.at[w, slot], device_id=to, device_id_type=MESH)

        for w, name in enumerate(MATS):
            for j, chip in enumerate(chips):
                cp = remote(w, j, srcs[w].at[_shard_half(name, c)], dsts[w].at[_window(name, me, c)], (*chip, c))
                cp.start()
                sends.append(cp)
        for j, chip in enumerate(chips):
            cp = remote(nm, j, dwk_src, dwk_dst.at[:, pl.ds(me * (D // NCHIP), D // NCHIP)], (*chip, c))
            cp.start()
            sends.append(cp)
        for w, name in enumerate(MATS):
            for j in range(3):
                got = dsts[w].at[_window(name, chip_ids[j], c)]
                remote(w, j, got, got, sibling).wait_recv()
                cp = remote(w, 3 + j, got, got, sibling)
                cp.start()
                sends.append(cp)
        for w, name in enumerate(MATS):
            for j in range(3):
                got = dsts[w].at[_window(name, chip_ids[j], 1 - c)]
                remote(w, 3 + j, got, got, sibling).wait_recv()
        for j in range(3):
            got = dwk_dst.at[:, pl.ds(chip_ids[j] * (D // NCHIP), D // NCHIP)]
            remote(nm, j, got, got, sibling).wait_recv()
        for cp in sends:
            cp.wait_send()
        for cp in locals_:
            cp.wait()

    out_shape = [jax.ShapeDtypeStruct(GEOM[name][0], BF16) for name in MATS] + [jax.ShapeDtypeStruct((KW_PAD, D), F32)]
    return pl.pallas_call(
        body, name="gather_weights", in_specs=[ANY] * (nm + 1), out_specs=[ANY] * (nm + 1), out_shape=out_shape,
        scratch_shapes=[pltpu.SemaphoreType.DMA((nm + 1, 6)), pltpu.SemaphoreType.DMA((nm + 1, 6)),
                        pltpu.SemaphoreType.DMA((nm + 1, 2))],
        compiler_params=_cp(has_side_effects=True),
    )(*shards, dwk_shard)


def allreduce_small(part):
    rows = part.shape[0]
    ndev = 8

    def body(part_ref, sum_ref, loss_ref, buf_ref, send_sems, recv_sems):
        x, y, c = lax.axis_index("x"), lax.axis_index("y"), lax.axis_index("c")
        me = 4 * x + 2 * y + c
        buf_ref[me] = part_ref[...]
        copies = []
        for m in range(1, ndev):
            to = (x ^ (m >> 2), y ^ ((m >> 1) & 1), c ^ (m & 1))
            cp = pltpu.make_async_remote_copy(src_ref=part_ref, dst_ref=buf_ref.at[me], send_sem=send_sems.at[m - 1],
                                              recv_sem=recv_sems.at[m - 1], device_id=to, device_id_type=MESH)
            cp.start()
            copies.append(cp)
        for m in range(1, ndev):
            got = buf_ref.at[me ^ m]
            pltpu.make_async_remote_copy(src_ref=got, dst_ref=got, send_sem=send_sems.at[m - 1],
                                         recv_sem=recv_sems.at[m - 1], device_id=(x, y, c),
                                         device_id_type=MESH).wait_recv()
        for cp in copies:
            cp.wait_send()
        total = buf_ref[0]
        for d in range(1, ndev):
            total = total + buf_ref[d]
        sum_ref[...] = total
        r = ROW["loss"]
        loss_ref[...] = jnp.zeros_like(loss_ref) + jnp.sum(total[r:r + 1, :])

    return pl.pallas_call(
        body, name="allreduce_small", in_specs=[VMEM_SPEC], out_specs=[VMEM_SPEC, VMEM_SPEC],
        out_shape=[jax.ShapeDtypeStruct((rows, D), F32), jax.ShapeDtypeStruct((8, 128), F32)],
        scratch_shapes=[pltpu.VMEM((ndev, rows, D), F32), pltpu.SemaphoreType.DMA((ndev - 1,)),
                        pltpu.SemaphoreType.DMA((ndev - 1,))],
        compiler_params=_cp(has_side_effects=True),
    )(part)


def exchange_pair(grads):
    nm = len(MATS)

    def body(*refs):
        srcs, dsts = refs[:nm], refs[nm:2 * nm]
        send_sems, recv_sems = refs[2 * nm:]
        x, y, c, me, chips, chip_ids = _place()
        sibling = (x, y, 1 - c)
        for w, name in enumerate(MATS):
            for k in range(NCHIP):
                pltpu.make_async_remote_copy(src_ref=srcs[w].at[_window(name, k, 1 - c)], dst_ref=dsts[w].at[k],
                                             send_sem=send_sems.at[w], recv_sem=recv_sems.at[w], device_id=sibling,
                                             device_id_type=MESH).start()
        for w in range(nm):
            pltpu.make_async_remote_copy(src_ref=dsts[w], dst_ref=dsts[w], send_sem=send_sems.at[w],
                                         recv_sem=recv_sems.at[w], device_id=sibling, device_id_type=MESH).wait()

    return pl.pallas_call(
        body, name="exchange_pair", in_specs=[ANY] * nm, out_specs=[ANY] * nm,
        out_shape=[jax.ShapeDtypeStruct((NCHIP, *GEOM[name][1]), F32) for name in MATS],
        scratch_shapes=[pltpu.SemaphoreType.DMA((nm,)), pltpu.SemaphoreType.DMA((nm,))],
        compiler_params=_cp(has_side_effects=True),
    )(*grads)


def pair_add(name, place, grad, from_sibling):
    _, blk, idx = GEOM[name]
    zeros = (0,) * len(blk)

    def body(place_ref, g_ref, s_ref, wire_ref, own_ref):
        k = pl.program_id(0)
        total = g_ref[...] + s_ref[0]
        wire_ref[0] = total.astype(BF16)

        @pl.when(k == place_ref[1])
        def _():
            own_ref[...] = total

    return pl.pallas_call(
        body, name="pair_add_" + name,
        grid_spec=pltpu.PrefetchScalarGridSpec(
            num_scalar_prefetch=1, grid=(NCHIP,),
            in_specs=[pl.BlockSpec(blk, lambda k, pr: idx(k, pr[0])), pl.BlockSpec((1, *blk), lambda k, pr: (k, *zeros))],
            out_specs=[pl.BlockSpec((1, *blk), lambda k, pr: (k, *zeros)), pl.BlockSpec(blk, lambda k, pr: zeros)]),
        out_shape=[jax.ShapeDtypeStruct((NCHIP, *blk), BF16), jax.ShapeDtypeStruct(blk, F32)],
        compiler_params=_cp(dimension_semantics=("arbitrary",)),
    )(place, grad, from_sibling)


def exchange_chips(wires):
    nm = len(MATS)

    def body(*refs):
        srcs, dsts = refs[:nm], refs[nm:2 * nm]
        send_sems, recv_sems = refs[2 * nm:]
        x, y, c, me, chips, chip_ids = _place()
        copies = []
        for w in range(nm):
            for j, chip in enumerate(chips):
                cp = pltpu.make_async_remote_copy(src_ref=srcs[w].at[chip_ids[j]], dst_ref=dsts[w].at[j],
                                                  send_sem=send_sems.at[w, j], recv_sem=recv_sems.at[w, j],
                                                  device_id=(*chip, c), device_id_type=MESH)
                cp.start()
                copies.append(cp)
        for cp in copies:
            cp.wait()

    return pl.pallas_call(
        body, name="exchange_chips", in_specs=[ANY] * nm, out_specs=[ANY] * nm,
        out_shape=[jax.ShapeDtypeStruct((3, *GEOM[name][1]), BF16) for name in MATS],
        scratch_shapes=[pltpu.SemaphoreType.DMA((nm, 3)), pltpu.SemaphoreType.DMA((nm, 3))],
        compiler_params=_cp(has_side_effects=True),
    )(*wires)


def sum_partials(name, place, own, from_chips):
    _, blk, idx = GEOM[name]
    zeros = (0,) * len(blk)

    def body(place_ref, own_ref, r_ref, out_ref):
        total = own_ref[...]
        for j in range(3):
            total = total + r_ref[j].astype(F32)
        out_ref[0] = total

    return pl.pallas_call(
        body, name="sum_partials_" + name,
        grid_spec=pltpu.PrefetchScalarGridSpec(
            num_scalar_prefetch=1, grid=(1,),
            in_specs=[pl.BlockSpec(blk, lambda i, pr: zeros), pl.BlockSpec((3, *blk), lambda i, pr: (0, *zeros))],
            out_specs=[pl.BlockSpec((1, *blk), lambda i, pr: (pr[0], *zeros))]),
        out_shape=[jax.ShapeDtypeStruct((2, *blk), F32)],
        compiler_params=_cp(dimension_semantics=("arbitrary",)),
    )(place, own, from_chips)[0]


def swap_halves(halves):
    nm = len(MATS)

    def body(*refs):
        srcs, dsts = refs[:nm], refs[nm:2 * nm]
        send_sems, recv_sems = refs[2 * nm:]
        x, y, c, me, chips, chip_ids = _place()
        copies = []
        for w in range(nm):
            cp = pltpu.make_async_remote_copy(src_ref=srcs[w].at[c], dst_ref=dsts[w].at[c], send_sem=send_sems.at[w],
                                              recv_sem=recv_sems.at[w], device_id=(x, y, 1 - c), device_id_type=MESH)
            cp.start()
            copies.append(cp)
        for w, cp in enumerate(copies):
            cp.wait_send()
            pltpu.make_async_remote_copy(src_ref=srcs[w].at[1 - c], dst_ref=dsts[w].at[1 - c], send_sem=send_sems.at[w],
                                         recv_sem=recv_sems.at[w], device_id=(x, y, 1 - c),
                                         device_id_type=MESH).wait_recv()

    return pl.pallas_call(
        body, name="swap_halves", in_specs=[ANY] * nm, out_specs=[ANY] * nm,
        out_shape=[jax.ShapeDtypeStruct(h.shape, F32) for h in halves],
        input_output_aliases={w: w for w in range(nm)},
        scratch_shapes=[pltpu.SemaphoreType.DMA((nm,)), pltpu.SemaphoreType.DMA((nm,))],
        compiler_params=_cp(has_side_effects=True),
    )(*halves)


def adamw(name, w, g, m, v, rows_per_block=256):
    r, cdim = w.shape
    br = min(rows_per_block, r)

    def body(w_ref, g_ref, m_ref, v_ref, d_ref, nm_ref, nv_ref):
        gv = g_ref[...]
        mn = B1 * m_ref[...] + (1.0 - B1) * gv
        vn = B2 * v_ref[...] + (1.0 - B2) * (gv * gv)
        m_hat = mn / (1.0 - B1 ** STEP)
        v_hat = vn / (1.0 - B2 ** STEP)
        d_ref[...] = -LR * (m_hat / (jnp.sqrt(v_hat) + ADAM_EPS) + WD * w_ref[...])
        nm_ref[...] = mn
        nv_ref[...] = vn

    spec = pl.BlockSpec((br, cdim), lambda i: (i, 0))
    return pl.pallas_call(
        body, name="adamw_" + name, grid=(r // br,), in_specs=[spec] * 4, out_specs=[spec] * 3,
        out_shape=[jax.ShapeDtypeStruct((r, cdim), F32)] * 3,
        compiler_params=_cp(dimension_semantics=("arbitrary",)),
    )(w, g, m, v)


VECS = ("mix_pre_g", "dw_bias", "conv_ln_g", "conv_ln_b", "pool_scale", "mix_post_g", "mlp_pre_g", "mlp_post_g")
WEIGHTS = ("mix_pre_g", "w_in", "dw_kernel", "dw_bias", "conv_ln_g", "conv_ln_b", "w_conv_out", "pool_w", "pool_scale",
           "w_pool_out", "w_o", "mix_post_g", "mlp_pre_g", "w_ff1", "w_ff2", "mlp_post_g")


def kernel(x, mix_pre_g, w_in, dw_kernel, dw_bias, conv_ln_g, conv_ln_b, w_conv_out, pool_w, pool_scale, w_pool_out, w_o, mix_post_g, mlp_pre_g, w_ff1, w_ff2, mlp_post_g, loss_target, m_mix_pre_g, m_w_in, m_dw_kernel, m_dw_bias, m_conv_ln_g, m_conv_ln_b, m_w_conv_out, m_pool_w, m_pool_scale, m_w_pool_out, m_w_o, m_mix_post_g, m_mlp_pre_g, m_w_ff1, m_w_ff2, m_mlp_post_g, v_mix_pre_g, v_w_in, v_dw_kernel, v_dw_bias, v_conv_ln_g, v_conv_ln_b, v_w_conv_out, v_pool_w, v_pool_scale, v_w_pool_out, v_w_o, v_mix_post_g, v_mlp_pre_g, v_w_ff1, v_w_ff2, v_mlp_post_g):
    w = dict(mix_pre_g=mix_pre_g, w_in=w_in, dw_kernel=dw_kernel, dw_bias=dw_bias, conv_ln_g=conv_ln_g,
             conv_ln_b=conv_ln_b, w_conv_out=w_conv_out, pool_w=pool_w, pool_scale=pool_scale, w_pool_out=w_pool_out,
             w_o=w_o, mix_post_g=mix_post_g, mlp_pre_g=mlp_pre_g, w_ff1=w_ff1, w_ff2=w_ff2, mlp_post_g=mlp_post_g)
    m = dict(mix_pre_g=m_mix_pre_g, w_in=m_w_in, dw_kernel=m_dw_kernel, dw_bias=m_dw_bias, conv_ln_g=m_conv_ln_g,
             conv_ln_b=m_conv_ln_b, w_conv_out=m_w_conv_out, pool_w=m_pool_w, pool_scale=m_pool_scale,
             w_pool_out=m_w_pool_out, w_o=m_w_o, mix_post_g=m_mix_post_g, mlp_pre_g=m_mlp_pre_g, w_ff1=m_w_ff1,
             w_ff2=m_w_ff2, mlp_post_g=m_mlp_post_g)
    v = dict(mix_pre_g=v_mix_pre_g, w_in=v_w_in, dw_kernel=v_dw_kernel, dw_bias=v_dw_bias, conv_ln_g=v_conv_ln_g,
             conv_ln_b=v_conv_ln_b, w_conv_out=v_w_conv_out, pool_w=v_pool_w, pool_scale=v_pool_scale,
             w_pool_out=v_w_pool_out, w_o=v_w_o, mix_post_g=v_mix_post_g, mlp_pre_g=v_mlp_pre_g, w_ff1=v_w_ff1,
             w_ff2=v_w_ff2, mlp_post_g=v_mlp_post_g)
    cx, cy, cc = lax.axis_index("x"), lax.axis_index("y"), lax.axis_index("c")
    chip = 2 * cx + cy
    place = jnp.stack([cc, chip]).astype(jnp.int32)
    dsh = D // NCHIP

    shards = cast_shards([w[name] for name in MATS])
    dwk_shard = jnp.pad(dw_kernel, ((0, KW_PAD - KW), (0, 0)))
    *full, dwk = gather_weights(shards, dwk_shard)
    full = dict(zip(MATS, full))
    vecs = {name: w[name].reshape(1, D) for name in VECS}

    dx, grads, dk, vec = local_step(x[0], loss_target[0], vecs, dwk, *[full[name] for name in MATS])

    small, loss8 = allreduce_small(jnp.concatenate([vec, dk], axis=0))
    loss = loss8[0, 0]
    g = {name: small[ROW[name]] for name in VECS}
    g["dw_kernel"] = lax.dynamic_slice(small[VEC_ROWS:VEC_ROWS + KW_PAD], (0, chip * dsh), (KW_PAD, dsh))

    glist = [grads[name] for name in MATS]
    from_sibling = exchange_pair(glist)
    wires, owns = zip(*[pair_add(name, place, gr, fs) for name, gr, fs in zip(MATS, glist, from_sibling)])
    from_chips = exchange_chips(wires)
    halves = [sum_partials(name, place, own, fc) for name, own, fc in zip(MATS, owns, from_chips)]
    reduced = swap_halves(halves)
    for name, red in zip(MATS, reduced):
        g[name] = red.reshape(_shard_shape(name))

    delta, new_m, new_v = {}, {}, {}
    for name in MATS:
        shp = w[name].shape
        two_d = (-1, shp[-1])
        res = adamw(name, w[name].reshape(two_d), g[name].reshape(two_d), m[name].reshape(two_d),
                    v[name].reshape(two_d))
        delta[name], new_m[name], new_v[name] = [r.reshape(shp) for r in res]
    stack = lambda d: jnp.concatenate([d[name].reshape(1, D) for name in VECS], axis=0)
    res = adamw("vectors", stack(w), stack(g), stack(m), stack(v))
    for i, name in enumerate(VECS):
        delta[name], new_m[name], new_v[name] = [r[i] for r in res]
    padk = lambda a: jnp.pad(a, ((0, KW_PAD - KW), (0, 0)))
    res = adamw("dw_kernel", padk(w["dw_kernel"]), g["dw_kernel"], padk(m["dw_kernel"]), padk(v["dw_kernel"]))
    delta["dw_kernel"], new_m["dw_kernel"], new_v["dw_kernel"] = [r[:KW] for r in res]
    g["dw_kernel"] = g["dw_kernel"][:KW]

    return (loss, dx[None], *[g[n] for n in WEIGHTS], *[delta[n] for n in WEIGHTS], *[new_m[n] for n in WEIGHTS],
            *[new_v[n] for n in WEIGHTS])
```

```python
import math

import jax
import jax.numpy as jnp
from jax import lax
from jax.experimental import pallas as pl
from jax.experimental.pallas import tpu as pltpu

F32 = jnp.float32
BF16 = jnp.bfloat16

D = 1024
FF = 4096
NPROJ = 5
KW = 31
KW_PAD = 32
SUBLANES = 8
LANES = 128
HALO = 32
POOL_WINDOWS = (2, 4, 8, 16)
NG = 4
GW = D // NG
RMS_EPS = 1e-6
LN_EPS = 1e-5
LR, B1, B2, ADAM_EPS, WD, STEP = 0.001, 0.9, 0.999, 1e-08, 0.01, 10
NCHIP = 4
VMEM_LIMIT = 60 * 1024 * 1024
MESH = pl.DeviceIdType.MESH
TM = 256
TM_IN = 512

ANY = pl.BlockSpec(memory_space=pl.ANY)
VMEM_SPEC = pl.BlockSpec(memory_space=pltpu.VMEM)
SDS = jax.ShapeDtypeStruct


def _cp(**kw):
    return pltpu.CompilerParams(vmem_limit_bytes=VMEM_LIMIT, **kw)


def _mm(a, b):
    return jnp.dot(a, b, preferred_element_type=F32)


def _mm_nt(a, b):
    return lax.dot_general(a, b, (((1,), (1,)), ((), ())), preferred_element_type=F32)


def _mm_tn(a, b):
    return lax.dot_general(a, b, (((0,), (0,)), ((), ())), preferred_element_type=F32)


def _sigmoid(x):
    return 1.0 / (1.0 + jnp.exp(-x))


def _rowsum(x):
    return jnp.sum(x, axis=0, keepdims=True)


def _full(shape):
    return pl.BlockSpec(shape, lambda i: (0,) * len(shape))


def _tile(tm, cols):
    return pl.BlockSpec((tm, cols), lambda i: (i, 0))


def _prev_halo(tm):
    return pl.BlockSpec((HALO, D), lambda i: (jnp.maximum(i * (tm // HALO) - 1, 0), 0))


def _next_halo(tm, nt):
    return pl.BlockSpec((HALO, D), lambda i: (jnp.minimum((i + 1) * (tm // HALO), nt * (tm // HALO) - 1), 0))


MATS = ("w_in", "w_conv_out", "pool_w", "w_pool_out", "w_o", "w_ff1", "w_ff2")
_ROWS = lambda k, h: (2 * k + h, 0)
_COLS = lambda k, h: (h, k)
GEOM = dict(
    w_in=((D, NPROJ * D), (D // 2, NPROJ * D // NCHIP), _COLS),
    w_conv_out=((D, D), (D // (2 * NCHIP), D), _ROWS),
    pool_w=((NG, GW, GW), (NG // 2, GW // NCHIP, GW), lambda k, h: (h, k, 0)),
    w_pool_out=((D, D), (D // (2 * NCHIP), D), _ROWS),
    w_o=((D, D), (D // (2 * NCHIP), D), _ROWS),
    w_ff1=((D, FF), (D // 2, FF // NCHIP), _COLS),
    w_ff2=((FF, D), (FF // (2 * NCHIP), D), _ROWS),
)
DSH = D // NCHIP


def _window(name, k, h):
    _, blk, idx = GEOM[name]
    return tuple(pl.ds(i * b, b) for i, b in zip(idx(k, h), blk))


def _shard_half(name, h):
    _, blk, idx = GEOM[name]
    return tuple(pl.ds(i * b, b) for i, b in zip(idx(0, h), blk))


def _shard_shape(name):
    _, blk, idx = GEOM[name]
    return tuple(b * (i + 1) for i, b in zip(idx(0, 1), blk))


def _place():
    x, y, c = lax.axis_index("x"), lax.axis_index("y"), lax.axis_index("c")
    chips = [(1 - x, y), (x, 1 - y), (1 - x, 1 - y)]
    return x, y, c, 2 * x + y, chips, [2 * px + py for px, py in chips]


def _remote(src, dst, send_sem, recv_sem, to):
    return pltpu.make_async_remote_copy(src_ref=src, dst_ref=dst, send_sem=send_sem, recv_sem=recv_sem,
                                        device_id=to, device_id_type=MESH)


class GatherWeights:
    has_mid = True

    def __init__(self, names, shards, taps=None):
        self.names = names
        self.ins = list(shards) + ([taps] if taps is not None else [])
        self.has_taps = taps is not None
        self.out_shapes = [SDS(GEOM[n][0], BF16) for n in names] + ([SDS((KW_PAD, D), F32)] if self.has_taps else [])
        n = len(self.ins)
        self.sems = [pltpu.SemaphoreType.DMA((n, 6)), pltpu.SemaphoreType.DMA((n, 6)), pltpu.SemaphoreType.DMA((n, 2))]

    def _copies(self, ins, outs, sems):
        send_sems, recv_sems, local_sems = sems
        x, y, c, me, chips, chip_ids = _place()
        sibling = (x, y, 1 - c)
        local, ici, ici_recv, d2d, d2d_recv = [], [], [], [], []
        for w, name in enumerate(self.names):
            for h in range(2):
                local.append(pltpu.make_async_copy(ins[w].at[_shard_half(name, h)], outs[w].at[_window(name, me, h)],
                                                   local_sems.at[w, h]))
            for j, chip in enumerate(chips):
                ici.append(_remote(ins[w].at[_shard_half(name, c)], outs[w].at[_window(name, me, c)],
                                   send_sems.at[w, j], recv_sems.at[w, j], (*chip, c)))
                got = outs[w].at[_window(name, chip_ids[j], c)]
                ici_recv.append(_remote(got, got, send_sems.at[w, j], recv_sems.at[w, j], sibling))
                d2d.append(_remote(got, got, send_sems.at[w, 3 + j], recv_sems.at[w, 3 + j], sibling))
                got = outs[w].at[_window(name, chip_ids[j], 1 - c)]
                d2d_recv.append(_remote(got, got, send_sems.at[w, 3 + j], recv_sems.at[w, 3 + j], sibling))
        if self.has_taps:
            w = len(self.names)
            mine = outs[w].at[:, pl.ds(me * DSH, DSH)]
            local.append(pltpu.make_async_copy(ins[w], mine, local_sems.at[w, 0]))
            for j, chip in enumerate(chips):
                ici.append(_remote(ins[w], mine, send_sems.at[w, j], recv_sems.at[w, j], (*chip, c)))
                got = outs[w].at[:, pl.ds(chip_ids[j] * DSH, DSH)]
                d2d_recv.append(_remote(got, got, send_sems.at[w, j], recv_sems.at[w, j], sibling))
        return local, ici, ici_recv, d2d, d2d_recv

    def start(self, ins, outs, sems):
        local, ici, _, _, _ = self._copies(ins, outs, sems)
        for cp in local + ici:
            cp.start()

    def mid(self, ins, outs, sems):
        _, _, ici_recv, d2d, _ = self._copies(ins, outs, sems)
        for got, fwd in zip(ici_recv, d2d):
            got.wait_recv()
            fwd.start()

    def finish(self, ins, outs, sems):
        local, ici, _, d2d, d2d_recv = self._copies(ins, outs, sems)
        for cp in d2d_recv:
            cp.wait_recv()
        for cp in ici + d2d:
            cp.wait_send()
        for cp in local:
            cp.wait()


class ExchangePair:
    has_mid = False

    def __init__(self, names, grads):
        self.names, self.ins = names, list(grads)
        self.out_shapes = [SDS((NCHIP, *GEOM[n][1]), F32) for n in names]
        self.sems = [pltpu.SemaphoreType.DMA((len(names),)), pltpu.SemaphoreType.DMA((len(names),))]

    def start(self, ins, outs, sems):
        send_sems, recv_sems = sems
        x, y, c, me, chips, chip_ids = _place()
        for w, name in enumerate(self.names):
            for k in range(NCHIP):
                _remote(ins[w].at[_window(name, k, 1 - c)], outs[w].at[k], send_sems.at[w], recv_sems.at[w],
                        (x, y, 1 - c)).start()

    def finish(self, ins, outs, sems):
        send_sems, recv_sems = sems
        x, y, c, me, chips, chip_ids = _place()
        for w in range(len(self.names)):
            _remote(outs[w], outs[w], send_sems.at[w], recv_sems.at[w], (x, y, 1 - c)).wait()


class ExchangeChips:
    has_mid = False

    def __init__(self, names, wires):
        self.names, self.ins = names, list(wires)
        self.out_shapes = [SDS((NCHIP - 1, *GEOM[n][1]), BF16) for n in names]
        self.sems = [pltpu.SemaphoreType.DMA((len(names), NCHIP - 1)), pltpu.SemaphoreType.DMA((len(names), NCHIP - 1))]

    def _copies(self, ins, outs, sems):
        send_sems, recv_sems = sems
        x, y, c, me, chips, chip_ids = _place()
        return [_remote(ins[w].at[chip_ids[j]], outs[w].at[j], send_sems.at[w, j], recv_sems.at[w, j], (*chip, c))
                for w in range(len(self.names)) for j, chip in enumerate(chips)]

    def start(self, ins, outs, sems):
        for cp in self._copies(ins, outs, sems):
            cp.start()

    def finish(self, ins, outs, sems):
        for cp in self._copies(ins, outs, sems):
            cp.wait()


def _call(body, name, grid, in_specs, out_specs, out_shape, scratch, args, exchanges=()):
    n_in, n_out, n_scr = len(in_specs), len(out_specs), len(scratch)
    x_in = [a for e in exchanges for a in e.ins]
    x_out = [s for e in exchanges for s in e.out_shapes]
    x_sem = [s for e in exchanges for s in e.sems]
    nsteps = math.prod(grid)

    def wrapped(*refs):
        ins, rest = refs[:n_in], refs[n_in:]
        xin, rest = rest[:len(x_in)], rest[len(x_in):]
        outs, rest = rest[:n_out], rest[n_out:]
        xout, rest = rest[:len(x_out)], rest[len(x_out):]
        scr, xsem = rest[:n_scr], rest[n_scr:]
        parts = []
        for e in exchanges:
            parts.append((xin[:len(e.ins)], xout[:len(e.out_shapes)], xsem[:len(e.sems)]))
            xin, xout, xsem = xin[len(e.ins):], xout[len(e.out_shapes):], xsem[len(e.sems):]
        if not grid:
            for e, p in zip(exchanges, parts):
                e.start(*p)
            body(*ins, *outs, *scr)
            for e, p in zip(exchanges, parts):
                if e.has_mid:
                    e.mid(*p)
            for e, p in zip(exchanges, parts):
                e.finish(*p)
            return
        step = 0
        for axis, extent in enumerate(grid):
            step = step * extent + pl.program_id(axis)
        if exchanges:
            @pl.when(step == 0)
            def _():
                for e, p in zip(exchanges, parts):
                    e.start(*p)

        body(*ins, *outs, *scr)
        if any(e.has_mid for e in exchanges):
            @pl.when(step == max(nsteps - 2, 0))
            def _():
                for e, p in zip(exchanges, parts):
                    if e.has_mid:
                        e.mid(*p)

        if exchanges:
            @pl.when(step == nsteps - 1)
            def _():
                for e, p in zip(exchanges, parts):
                    e.finish(*p)

    kw = dict(grid=grid, compiler_params=_cp(dimension_semantics=("arbitrary",) * len(grid))) if grid else dict(
        compiler_params=_cp())
    res = pl.pallas_call(
        wrapped, name=name, in_specs=list(in_specs) + [ANY] * len(x_in), out_specs=list(out_specs) + [ANY] * len(x_out),
        out_shape=list(out_shape) + x_out, scratch_shapes=list(scratch) + x_sem, **kw,
    )(*args, *x_in)
    outs, rest = res[:n_out], res[n_out:]
    xouts = []
    for e in exchanges:
        xouts.append(rest[:len(e.out_shapes)])
        rest = rest[len(e.out_shapes):]
    return outs, xouts


def exchange(name, ex):
    return _call(lambda: None, name, (), [], [], [], [], [], [ex])[1][0]


def _taps_scratch(tm):
    return pltpu.VMEM((SUBLANES, tm + HALO, LANES), F32)


def _taps(src_ref, k_ref, dst_ref, sh_ref, tm, off0, reverse):
    rc, cw = 64, LANES

    def col_chunk(cc, carry):
        cols = pl.ds(pl.multiple_of(cc * cw, cw), cw)
        for q in range(SUBLANES):
            n = tm + SUBLANES * (len(range(q, KW, SUBLANES)) - 1)
            sh_ref[q, 0:n, :] = src_ref[pl.ds(off0 + q, n), cols]
        for r in range(tm // rc):
            acc = jnp.zeros((rc, cw), F32)
            for q in range(SUBLANES):
                for a, j in enumerate(range(q, KW, SUBLANES)):
                    kj = KW - 1 - j if reverse else j
                    acc = acc + k_ref[kj:kj + 1, cols] * sh_ref[q, pl.ds(r * rc + SUBLANES * a, rc), :]
            dst_ref[pl.ds(r * rc, rc), cols] = acc
        return carry

    lax.fori_loop(0, D // cw, col_chunk, 0)


def fwd_in(x, g1, w_in, tm, exchanges=()):
    t = x.shape[0]

    def body(x_ref, g_ref, w_ref, u_ref, glu_ref, ag_ref, p_ref, gt_ref):
        xf = x_ref[...]
        r = lax.rsqrt(jnp.mean(xf * xf, axis=-1, keepdims=True) + RMS_EPS)
        u = (xf * r * g_ref[...]).astype(BF16)
        u_ref[...] = u
        a = _mm(u, w_ref[:, 0:D])
        gate = _mm(u, w_ref[:, D:2 * D])
        glu_ref[...] = a * _sigmoid(gate)
        ag_ref[:, 0:D] = a.astype(BF16)
        ag_ref[:, D:2 * D] = gate.astype(BF16)
        p_ref[...] = _mm(u, w_ref[:, 2 * D:3 * D])
        gt_ref[:, 0:D] = _mm(u, w_ref[:, 3 * D:4 * D]).astype(BF16)
        gt_ref[:, D:2 * D] = _mm(u, w_ref[:, 4 * D:5 * D]).astype(BF16)

    return _call(
        body, "fwd_in", (t // tm,),
        [_tile(tm, D), _full((1, D)), _full((D, NPROJ * D))],
        [_tile(tm, D), _tile(tm, D), _tile(tm, 2 * D), _tile(tm, D), _tile(tm, 2 * D)],
        [SDS((t, D), BF16), SDS((t, D), F32), SDS((t, 2 * D), BF16), SDS((t, D), F32), SDS((t, 2 * D), BF16)],
        [], [x, g1, w_in], exchanges)


def _pool_inv_count(i, tm, w):
    pos = i * tm + lax.broadcasted_iota(jnp.int32, (tm, 1), 0) + 1
    return 1.0 / jnp.minimum(pos, w).astype(F32)


def _window_sum(src_ref, tmp_ref, cols, tm, w, causal):
    lo, hi = 0, tm + HALO
    cur, span = None, 1
    while span < w:
        new_lo, new_hi = (lo + SUBLANES, hi) if causal else (lo, hi - SUBLANES)
        far = new_lo - span if causal else new_lo + span
        n = new_hi - new_lo
        if cur is None:
            near_v, far_v = src_ref[pl.ds(new_lo, n), cols], src_ref[pl.ds(far, n), cols]
        else:
            near_v = cur[new_lo - lo:new_lo - lo + n]
            if span % SUBLANES == 0:
                far_v = cur[far - lo:far - lo + n]
            else:
                tmp_ref[pl.ds(lo, hi - lo), :] = cur
                far_v = tmp_ref[pl.ds(far, n), :]
        cur, lo, hi, span = near_v + far_v, new_lo, new_hi, 2 * span
    off = HALO if causal else 0
    return cur[off - lo:off - lo + tm]


def fwd_mix(x, glu, p, gt, dwk, dwb, lng, lnb, w_co, pool_w, ps, w_po, w_o, g2, tm, exchanges=()):
    t = x.shape[0]

    def body(x_ref, glu_ref, gluh_ref, p_ref, ph_ref, gt_ref, k_ref, b_ref, lg_ref, lb_ref, wco_ref, pw_ref,
             ps_ref, wpo_ref, wo_ref, g2_ref,
             cv_ref, sw_ref, z_ref, zl_ref, zs_ref, yc_ref, yp_ref, mg_ref, mo_ref, h1_ref, ext_ref, win_ref, sh_ref):
        i = pl.program_id(0)
        keep = (i > 0).astype(F32)
        ext_ref[0:HALO, :] = gluh_ref[...] * keep
        ext_ref[HALO:HALO + tm, :] = glu_ref[...]
        _taps(ext_ref, k_ref, cv_ref, sh_ref, tm, HALO - (KW - 1), False)
        cv = cv_ref[...] + b_ref[...]
        cv_ref[...] = cv
        mu = jnp.mean(cv, axis=-1, keepdims=True)
        cen = cv - mu
        rstd = lax.rsqrt(jnp.mean(cen * cen, axis=-1, keepdims=True) + LN_EPS)
        ln = cen * rstd * lg_ref[...] + lb_ref[...]
        sw = (ln * _sigmoid(ln)).astype(BF16)
        sw_ref[...] = sw
        yc = _mm(sw, wco_ref[...])
        yc_ref[...] = yc.astype(BF16)
        ext_ref[0:HALO, :] = ph_ref[...] * keep
        ext_ref[HALO:HALO + tm, :] = p_ref[...]
        for g, w in enumerate(POOL_WINDOWS):
            cols = pl.ds(g * GW, GW)
            acc = _window_sum(ext_ref, win_ref, cols, tm, w, True)
            zg = (acc * _pool_inv_count(i, tm, w) - p_ref[:, cols]).astype(BF16)
            z_ref[:, cols] = zg
            zl_ref[:, cols] = _mm(zg, pw_ref[g])
        zl = zl_ref[...]
        zs = (zl * ps_ref[...]).astype(BF16)
        zs_ref[...] = zs
        yp = _mm(zs, wpo_ref[...])
        yp_ref[...] = yp.astype(BF16)
        gc = _sigmoid(gt_ref[:, 0:D].astype(F32))
        gp = _sigmoid(gt_ref[:, D:2 * D].astype(F32))
        mg = (gc * yc + gp * yp).astype(BF16)
        mg_ref[...] = mg
        mo = _mm(mg, wo_ref[...])
        mo_ref[...] = mo
        r2 = lax.rsqrt(jnp.mean(mo * mo, axis=-1, keepdims=True) + RMS_EPS)
        h1_ref[...] = x_ref[...] + mo * r2 * g2_ref[...]

    vec = _full((1, D))
    act = lambda dt: SDS((t, D), dt)
    return _call(
        body, "fwd_mix", (t // tm,),
        [_tile(tm, D), _tile(tm, D), _prev_halo(tm), _tile(tm, D), _prev_halo(tm), _tile(tm, 2 * D),
         _full((KW_PAD, D)), vec, vec, vec, _full((D, D)), _full((NG, GW, GW)), vec, _full((D, D)), _full((D, D)), vec],
        [_tile(tm, D)] * 10,
        [act(F32), act(BF16), act(BF16), act(F32), act(BF16), act(BF16), act(BF16), act(BF16), act(F32), act(F32)],
        [pltpu.VMEM((tm + HALO, D), F32), pltpu.VMEM((tm + HALO, GW), F32), _taps_scratch(tm)],
        [x, glu, glu, p, p, gt, dwk, dwb, lng, lnb, w_co, pool_w, ps, w_po, w_o, g2], exchanges)


def mlp_fwd_bwd(h1, tgt, g3, g4, w1, w2, tm, exchanges=()):
    t = h1.shape[0]
    fc = 1024

    def body(h1_ref, tgt_ref, g3_ref, g4_ref, w1_ref, w2_ref,
             v_ref, a2_ref, df2_ref, df1_ref, dh1_ref, vec_ref, f1_ref):
        i = pl.program_id(0)

        @pl.when(i == 0)
        def _():
            vec_ref[...] = jnp.zeros_like(vec_ref)

        h1v = h1_ref[...]
        r3 = lax.rsqrt(jnp.mean(h1v * h1v, axis=-1, keepdims=True) + RMS_EPS)
        n3 = h1v * r3
        v = (n3 * g3_ref[...]).astype(BF16)
        v_ref[...] = v
        f2 = jnp.zeros((tm, D), F32)
        for c in range(FF // fc):
            cols = pl.ds(c * fc, fc)
            f1 = jnp.maximum(_mm(v, w1_ref[:, cols]), 0.0)
            f1_ref[:, cols] = f1
            a2 = (f1 * f1).astype(BF16)
            a2_ref[:, cols] = a2
            f2 = f2 + _mm(a2, w2_ref[cols, :])
        r4 = lax.rsqrt(jnp.mean(f2 * f2, axis=-1, keepdims=True) + RMS_EPS)
        n4 = f2 * r4
        err = h1v + n4 * g4_ref[...] - tgt_ref[...]
        vec_ref[2:3, :] += _rowsum(err * err) * (0.5 / D)
        dh2 = err * (1.0 / D)
        vec_ref[1:2, :] += _rowsum(dh2 * n4)
        dn4 = dh2 * g4_ref[...]
        df2 = (r4 * (dn4 - n4 * jnp.mean(dn4 * n4, axis=-1, keepdims=True))).astype(BF16)
        df2_ref[...] = df2
        dv = jnp.zeros((tm, D), F32)
        for c in range(FF // fc):
            cols = pl.ds(c * fc, fc)
            da2 = _mm_nt(df2, w2_ref[cols, :])
            df1 = (da2 * (2.0 * f1_ref[:, cols])).astype(BF16)
            df1_ref[:, cols] = df1
            dv = dv + _mm_nt(df1, w1_ref[:, cols])
        vec_ref[0:1, :] += _rowsum(dv * n3)
        dn3 = dv * g3_ref[...]
        dh1_ref[...] = dh2 + r3 * (dn3 - n3 * jnp.mean(dn3 * n3, axis=-1, keepdims=True))

    vec = _full((1, D))
    return _call(
        body, "mlp_fwd_bwd", (t // tm,),
        [_tile(tm, D), _tile(tm, D), vec, vec, _full((D, FF)), _full((FF, D))],
        [_tile(tm, D), _tile(tm, FF), _tile(tm, D), _tile(tm, FF), _tile(tm, D), _full((8, D))],
        [SDS((t, D), BF16), SDS((t, FF), BF16), SDS((t, D), BF16), SDS((t, FF), BF16), SDS((t, D), F32),
         SDS((8, D), F32)],
        [pltpu.VMEM((tm, FF), F32)], [h1, tgt, g3, g4, w1, w2], exchanges)


def bwd_mix(dh1, mo, cv, zl, yc, yp, gt, lng, lnb, ps, g2, w_co, pool_w, w_po, w_o, tm, exchanges=()):
    t = dh1.shape[0]

    def body(dh1_ref, mo_ref, cv_ref, zl_ref, yc_ref, yp_ref, gt_ref, lg_ref, lb_ref, ps_ref, g2_ref,
             wco_ref, pw_ref, wpo_ref, wo_ref,
             dmo_ref, dgt_ref, dyc_ref, dyp_ref, dcv_ref, dzl_ref, dz_ref, vec_ref):
        i = pl.program_id(0)

        @pl.when(i == 0)
        def _():
            vec_ref[...] = jnp.zeros_like(vec_ref)

        dh1v = dh1_ref[...]
        mo = mo_ref[...]
        r2 = lax.rsqrt(jnp.mean(mo * mo, axis=-1, keepdims=True) + RMS_EPS)
        n2 = mo * r2
        vec_ref[0:1, :] += _rowsum(dh1v * n2)
        dn2 = dh1v * g2_ref[...]
        dmo = (r2 * (dn2 - n2 * jnp.mean(dn2 * n2, axis=-1, keepdims=True))).astype(BF16)
        dmo_ref[...] = dmo
        dmg = _mm_nt(dmo, wo_ref[...])
        gc = _sigmoid(gt_ref[:, 0:D].astype(F32))
        gp = _sigmoid(gt_ref[:, D:2 * D].astype(F32))
        dgt_ref[:, 0:D] = (dmg * yc_ref[...].astype(F32) * gc * (1.0 - gc)).astype(BF16)
        dgt_ref[:, D:2 * D] = (dmg * yp_ref[...].astype(F32) * gp * (1.0 - gp)).astype(BF16)
        dyc = (dmg * gc).astype(BF16)
        dyp = (dmg * gp).astype(BF16)
        dyc_ref[...] = dyc
        dyp_ref[...] = dyp
        dsw = _mm_nt(dyc, wco_ref[...])
        cv = cv_ref[...]
        mu = jnp.mean(cv, axis=-1, keepdims=True)
        cen = cv - mu
        rstd = lax.rsqrt(jnp.mean(cen * cen, axis=-1, keepdims=True) + LN_EPS)
        y = cen * rstd
        ln = y * lg_ref[...] + lb_ref[...]
        sg = _sigmoid(ln)
        dln = dsw * (sg * (1.0 + ln * (1.0 - sg)))
        vec_ref[1:2, :] += _rowsum(dln * y)
        vec_ref[2:3, :] += _rowsum(dln)
        dy = dln * lg_ref[...]
        dcv = rstd * (dy - jnp.mean(dy, axis=-1, keepdims=True) - y * jnp.mean(dy * y, axis=-1, keepdims=True))
        dcv_ref[...] = dcv
        vec_ref[3:4, :] += _rowsum(dcv)
        dzs = _mm_nt(dyp, wpo_ref[...])
        vec_ref[4:5, :] += _rowsum(dzs * zl_ref[...])
        dzl = (dzs * ps_ref[...]).astype(BF16)
        dzl_ref[...] = dzl
        for g in range(NG):
            cols = pl.ds(g * GW, GW)
            dz_ref[:, cols] = _mm_nt(dzl_ref[:, cols], pw_ref[g])

    vec = _full((1, D))
    act = lambda dt: SDS((t, D), dt)
    return _call(
        body, "bwd_mix", (t // tm,),
        [_tile(tm, D)] * 6 + [_tile(tm, 2 * D), vec, vec, vec, vec, _full((D, D)), _full((NG, GW, GW)), _full((D, D)),
                              _full((D, D))],
        [_tile(tm, D), _tile(tm, 2 * D)] + [_tile(tm, D)] * 5 + [_full((8, D))],
        [act(BF16), SDS((t, 2 * D), BF16), act(BF16), act(BF16), act(F32), act(BF16), act(F32), SDS((8, D), F32)],
        [], [dh1, mo, cv, zl, yc, yp, gt, lng, lnb, ps, g2, w_co, pool_w, w_po, w_o], exchanges)


def bwd_in(x, dh1, dcv, dz, glu, ag, dgt, g1, dwk, w_in, tm, exchanges=()):
    t = x.shape[0]
    nt = t // tm

    def body(x_ref, dh1_ref, dcv_ref, dcvh_ref, dz_ref, dzh_ref, glu_ref, gluh_ref, ag_ref, dgt_ref, g1_ref,
             k_ref, w_ref, dx_ref, dproj_ref, vec_ref, dk_ref, ext_ref, tmp_ref, win_ref, sh_ref):
        i = pl.program_id(0)

        @pl.when(i == 0)
        def _():
            vec_ref[...] = jnp.zeros_like(vec_ref)
            dk_ref[...] = jnp.zeros_like(dk_ref)

        first = (i > 0).astype(F32)
        last = (i < nt - 1).astype(F32)
        ext_ref[0:HALO, :] = gluh_ref[...] * first
        ext_ref[HALO:HALO + tm, :] = glu_ref[...]
        rc, cw = 64, LANES

        def dk_chunk(cc, carry):
            cols = pl.ds(pl.multiple_of(cc * cw, cw), cw)
            for q in range(SUBLANES):
                taps = range(q, KW, SUBLANES)
                n = tm + SUBLANES * (len(taps) - 1)
                sh_ref[q, 0:n, :] = ext_ref[pl.ds(HALO - (KW - 1) + q, n), cols]
                accs = [jnp.zeros((SUBLANES, cw), F32) for _ in taps]
                for r in range(tm // rc):
                    dchunk = dcv_ref[pl.ds(r * rc, rc), cols]
                    for a in range(len(taps)):
                        prod = dchunk * sh_ref[q, pl.ds(r * rc + SUBLANES * a, rc), :]
                        accs[a] = accs[a] + jnp.sum(prod.reshape(rc // SUBLANES, SUBLANES, cw), axis=0)
                for a, j in enumerate(taps):
                    dk_ref[j:j + 1, cols] += _rowsum(accs[a])
            return carry

        lax.fori_loop(0, D // cw, dk_chunk, 0)
        ext_ref[0:tm, :] = dcv_ref[...]
        ext_ref[tm:tm + HALO, :] = dcvh_ref[...] * last
        _taps(ext_ref, k_ref, tmp_ref, sh_ref, tm, 0, True)
        dglu = tmp_ref[...]
        a = ag_ref[:, 0:D].astype(F32)
        sg = _sigmoid(ag_ref[:, D:2 * D].astype(F32))
        dproj_ref[:, 0:D] = (dglu * sg).astype(BF16)
        dproj_ref[:, D:2 * D] = (dglu * a * sg * (1.0 - sg)).astype(BF16)
        for g, w in enumerate(POOL_WINDOWS):
            cols = pl.ds(g * GW, GW)
            pos = i * tm + lax.broadcasted_iota(jnp.int32, (tm + HALO, 1), 0) + 1
            inv = 1.0 / jnp.minimum(pos, w).astype(F32)
            ext_ref[0:tm, cols] = dz_ref[:, cols] * inv[0:tm]
            ext_ref[tm:tm + HALO, cols] = dzh_ref[:, cols] * inv[tm:tm + HALO] * last
            acc = _window_sum(ext_ref, win_ref, cols, tm, w, False)
            dproj_ref[:, pl.ds(2 * D + g * GW, GW)] = (acc - dz_ref[:, cols]).astype(BF16)
        dproj_ref[:, 3 * D:5 * D] = dgt_ref[...]
        du = jnp.zeros((tm, D), F32)
        for c in range(NPROJ):
            cols = pl.ds(c * D, D)
            du = du + _mm_nt(dproj_ref[:, cols], w_ref[:, cols])
        xf = x_ref[...]
        r1 = lax.rsqrt(jnp.mean(xf * xf, axis=-1, keepdims=True) + RMS_EPS)
        n1 = xf * r1
        vec_ref[0:1, :] += _rowsum(du * n1)
        dn1 = du * g1_ref[...]
        dx_ref[...] = dh1_ref[...] + r1 * (dn1 - n1 * jnp.mean(dn1 * n1, axis=-1, keepdims=True))

    return _call(
        body, "bwd_in", (nt,),
        [_tile(tm, D), _tile(tm, D), _tile(tm, D), _next_halo(tm, nt), _tile(tm, D), _next_halo(tm, nt),
         _tile(tm, D), _prev_halo(tm), _tile(tm, 2 * D), _tile(tm, 2 * D), _full((1, D)),
         _full((KW_PAD, D)), _full((D, NPROJ * D))],
        [_tile(tm, D), _tile(tm, NPROJ * D), _full((8, D)), _full((KW_PAD, D))],
        [SDS((t, D), F32), SDS((t, NPROJ * D), BF16), SDS((8, D), F32), SDS((KW_PAD, D), F32)],
        [pltpu.VMEM((tm + HALO, D), F32), pltpu.VMEM((tm, D), F32), pltpu.VMEM((tm + HALO, GW), F32),
         _taps_scratch(tm)],
        [x, dh1, dcv, dcv, dz, dz, glu, glu, ag, dgt, g1, dwk, w_in], exchanges)


def wgrad(a, b, name, bm=1024, bn=1024, bt=512, exchanges=()):
    t, m = a.shape
    n = b.shape[1]
    bm, bn, bt = min(bm, m), min(bn, n), min(bt, t)
    assert m % bm == 0 and n % bn == 0 and t % bt == 0, (a.shape, b.shape, bm, bn, bt)

    def body(a_ref, b_ref, o_ref):
        k = pl.program_id(2)

        @pl.when(k == 0)
        def _():
            o_ref[...] = jnp.zeros_like(o_ref)

        o_ref[...] += _mm_tn(a_ref[...], b_ref[...])

    outs, xouts = _call(
        body, name, (m // bm, n // bn, t // bt),
        [pl.BlockSpec((bt, bm), lambda i, j, k: (k, i)), pl.BlockSpec((bt, bn), lambda i, j, k: (k, j))],
        [pl.BlockSpec((bm, bn), lambda i, j, k: (i, j))], [SDS((m, n), F32)], [], [a, b], exchanges)
    return outs[0], xouts


def wgrad_pool(z, dzl, bt=1024):
    t = z.shape[0]
    bt = min(bt, t)
    assert t % bt == 0

    def body(a_ref, b_ref, o_ref):
        k = pl.program_id(1)

        @pl.when(k == 0)
        def _():
            o_ref[...] = jnp.zeros_like(o_ref)

        o_ref[0] += _mm_tn(a_ref[...], b_ref[...])

    return _call(
        body, "wgrad_pool", (NG, t // bt),
        [pl.BlockSpec((bt, GW), lambda g, k: (k, g)), pl.BlockSpec((bt, GW), lambda g, k: (k, g))],
        [pl.BlockSpec((1, GW, GW), lambda g, k: (g, 0, 0))], [SDS((NG, GW, GW), F32)], [], [z, dzl])[0][0]


def cast_shards(shards):
    n = len(shards)

    def body(*refs):
        for src, dst in zip(refs[:n], refs[n:]):
            dst[...] = src[...].astype(BF16)

    return pl.pallas_call(
        body, name="cast_shards", in_specs=[VMEM_SPEC] * n, out_specs=[VMEM_SPEC] * n,
        out_shape=[SDS(s.shape, BF16) for s in shards], compiler_params=_cp(),
    )(*shards)


VEC_ROWS = 24
ROW = dict(mlp_pre_g=0, mlp_post_g=1, loss=2, mix_post_g=8, conv_ln_g=9, conv_ln_b=10, dw_bias=11, pool_scale=12,
           mix_pre_g=16)


def allreduce_small(part):
    rows = part.shape[0]
    ndev = 8

    def body(part_ref, sum_ref, loss_ref, buf_ref, send_sems, recv_sems):
        x, y, c = lax.axis_index("x"), lax.axis_index("y"), lax.axis_index("c")
        me = 4 * x + 2 * y + c
        buf_ref[me] = part_ref[...]
        copies = []
        for m in range(1, ndev):
            to = (x ^ (m >> 2), y ^ ((m >> 1) & 1), c ^ (m & 1))
            cp = _remote(part_ref, buf_ref.at[me], send_sems.at[m - 1], recv_sems.at[m - 1], to)
            cp.start()
            copies.append(cp)
        for m in range(1, ndev):
            got = buf_ref.at[me ^ m]
            _remote(got, got, send_sems.at[m - 1], recv_sems.at[m - 1], (x, y, c)).wait_recv()
        for cp in copies:
            cp.wait_send()
        total = buf_ref[0]
        for d in range(1, ndev):
            total = total + buf_ref[d]
        sum_ref[...] = total
        r = ROW["loss"]
        loss_ref[...] = jnp.zeros_like(loss_ref) + jnp.sum(total[r:r + 1, :])

    return pl.pallas_call(
        body, name="allreduce_small", in_specs=[VMEM_SPEC], out_specs=[VMEM_SPEC, VMEM_SPEC],
        out_shape=[SDS((rows, D), F32), SDS((8, 128), F32)],
        scratch_shapes=[pltpu.VMEM((ndev, rows, D), F32), pltpu.SemaphoreType.DMA((ndev - 1,)),
                        pltpu.SemaphoreType.DMA((ndev - 1,))],
        compiler_params=_cp(),
    )(part)


def pair_add(name, place, grad, from_sibling):
    _, blk, idx = GEOM[name]
    zeros = (0,) * len(blk)

    def body(place_ref, g_ref, s_ref, wire_ref, own_ref):
        k = pl.program_id(0)
        total = g_ref[...] + s_ref[0]
        wire_ref[0] = total.astype(BF16)

        @pl.when(k == place_ref[1])
        def _():
            own_ref[...] = total

    return pl.pallas_call(
        body, name="pair_add_" + name,
        grid_spec=pltpu.PrefetchScalarGridSpec(
            num_scalar_prefetch=1, grid=(NCHIP,),
            in_specs=[pl.BlockSpec(blk, lambda k, pr: idx(k, pr[0])), pl.BlockSpec((1, *blk), lambda k, pr: (k, *zeros))],
            out_specs=[pl.BlockSpec((1, *blk), lambda k, pr: (k, *zeros)), pl.BlockSpec(blk, lambda k, pr: zeros)]),
        out_shape=[SDS((NCHIP, *blk), BF16), SDS(blk, F32)],
        compiler_params=_cp(dimension_semantics=("arbitrary",)),
    )(place, grad, from_sibling)


def sum_partials(name, place, own, from_chips):
    _, blk, idx = GEOM[name]
    zeros = (0,) * len(blk)

    def body(place_ref, own_ref, r_ref, out_ref):
        total = own_ref[...]
        for j in range(NCHIP - 1):
            total = total + r_ref[j].astype(F32)
        out_ref[0] = total

    return pl.pallas_call(
        body, name="sum_partials_" + name,
        grid_spec=pltpu.PrefetchScalarGridSpec(
            num_scalar_prefetch=1, grid=(1,),
            in_specs=[pl.BlockSpec(blk, lambda i, pr: zeros), pl.BlockSpec((NCHIP - 1, *blk), lambda i, pr: (0, *zeros))],
            out_specs=[pl.BlockSpec((1, *blk), lambda i, pr: (pr[0], *zeros))]),
        out_shape=[SDS((2, *blk), F32)],
        compiler_params=_cp(dimension_semantics=("arbitrary",)),
    )(place, own, from_chips)[0]


def swap_halves(halves):
    nm = len(halves)

    def body(*refs):
        srcs, dsts = refs[:nm], refs[nm:2 * nm]
        send_sems, recv_sems = refs[2 * nm:]
        x, y, c, me, chips, chip_ids = _place()
        copies = []
        for w in range(nm):
            cp = _remote(srcs[w].at[c], dsts[w].at[c], send_sems.at[w], recv_sems.at[w], (x, y, 1 - c))
            cp.start()
            copies.append(cp)
        for w, cp in enumerate(copies):
            cp.wait_send()
            _remote(srcs[w].at[1 - c], dsts[w].at[1 - c], send_sems.at[w], recv_sems.at[w], (x, y, 1 - c)).wait_recv()

    return pl.pallas_call(
        body, name="swap_halves", in_specs=[ANY] * nm, out_specs=[ANY] * nm,
        out_shape=[SDS(h.shape, F32) for h in halves],
        input_output_aliases={w: w for w in range(nm)},
        scratch_shapes=[pltpu.SemaphoreType.DMA((nm,)), pltpu.SemaphoreType.DMA((nm,))],
        compiler_params=_cp(),
    )(*halves)


def adamw(name, w, g, m, v, rows_per_block=256):
    r, cdim = w.shape
    br = min(rows_per_block, r)

    def body(w_ref, g_ref, m_ref, v_ref, d_ref, nm_ref, nv_ref):
        gv = g_ref[...]
        mn = B1 * m_ref[...] + (1.0 - B1) * gv
        vn = B2 * v_ref[...] + (1.0 - B2) * (gv * gv)
        m_hat = mn / (1.0 - B1 ** STEP)
        v_hat = vn / (1.0 - B2 ** STEP)
        d_ref[...] = -LR * (m_hat / (jnp.sqrt(v_hat) + ADAM_EPS) + WD * w_ref[...])
        nm_ref[...] = mn
        nv_ref[...] = vn

    spec = pl.BlockSpec((br, cdim), lambda i: (i, 0))
    return pl.pallas_call(
        body, name="adamw_" + name, grid=(r // br,), in_specs=[spec] * 4, out_specs=[spec] * 3,
        out_shape=[SDS((r, cdim), F32)] * 3,
        compiler_params=_cp(dimension_semantics=("arbitrary",)),
    )(w, g, m, v)


VECS = ("mix_pre_g", "dw_bias", "conv_ln_g", "conv_ln_b", "pool_scale", "mix_post_g", "mlp_pre_g", "mlp_post_g")
WEIGHTS = ("mix_pre_g", "w_in", "dw_kernel", "dw_bias", "conv_ln_g", "conv_ln_b", "w_conv_out", "pool_w", "pool_scale",
           "w_pool_out", "w_o", "mix_post_g", "mlp_pre_g", "w_ff1", "w_ff2", "mlp_post_g")
MIX_MATS = ("w_conv_out", "pool_w", "w_pool_out", "w_o")
FF_MATS = ("w_ff1", "w_ff2")


def kernel(x, mix_pre_g, w_in, dw_kernel, dw_bias, conv_ln_g, conv_ln_b, w_conv_out, pool_w, pool_scale, w_pool_out, w_o, mix_post_g, mlp_pre_g, w_ff1, w_ff2, mlp_post_g, loss_target, m_mix_pre_g, m_w_in, m_dw_kernel, m_dw_bias, m_conv_ln_g, m_conv_ln_b, m_w_conv_out, m_pool_w, m_pool_scale, m_w_pool_out, m_w_o, m_mix_post_g, m_mlp_pre_g, m_w_ff1, m_w_ff2, m_mlp_post_g, v_mix_pre_g, v_w_in, v_dw_kernel, v_dw_bias, v_conv_ln_g, v_conv_ln_b, v_w_conv_out, v_pool_w, v_pool_scale, v_w_pool_out, v_w_o, v_mix_post_g, v_mlp_pre_g, v_w_ff1, v_w_ff2, v_mlp_post_g):
    w = dict(mix_pre_g=mix_pre_g, w_in=w_in, dw_kernel=dw_kernel, dw_bias=dw_bias, conv_ln_g=conv_ln_g,
             conv_ln_b=conv_ln_b, w_conv_out=w_conv_out, pool_w=pool_w, pool_scale=pool_scale, w_pool_out=w_pool_out,
             w_o=w_o, mix_post_g=mix_post_g, mlp_pre_g=mlp_pre_g, w_ff1=w_ff1, w_ff2=w_ff2, mlp_post_g=mlp_post_g)
    m = dict(mix_pre_g=m_mix_pre_g, w_in=m_w_in, dw_kernel=m_dw_kernel, dw_bias=m_dw_bias, conv_ln_g=m_conv_ln_g,
             conv_ln_b=m_conv_ln_b, w_conv_out=m_w_conv_out, pool_w=m_pool_w, pool_scale=m_pool_scale,
             w_pool_out=m_w_pool_out, w_o=m_w_o, mix_post_g=m_mix_post_g, mlp_pre_g=m_mlp_pre_g, w_ff1=m_w_ff1,
             w_ff2=m_w_ff2, mlp_post_g=m_mlp_post_g)
    v = dict(mix_pre_g=v_mix_pre_g, w_in=v_w_in, dw_kernel=v_dw_kernel, dw_bias=v_dw_bias, conv_ln_g=v_conv_ln_g,
             conv_ln_b=v_conv_ln_b, w_conv_out=v_w_conv_out, pool_w=v_pool_w, pool_scale=v_pool_scale,
             w_pool_out=v_w_pool_out, w_o=v_w_o, mix_post_g=v_mix_post_g, mlp_pre_g=v_mlp_pre_g, w_ff1=v_w_ff1,
             w_ff2=v_w_ff2, mlp_post_g=v_mlp_post_g)
    cx, cy, cc = lax.axis_index("x"), lax.axis_index("y"), lax.axis_index("c")
    chip = 2 * cx + cy
    place = jnp.stack([cc, chip]).astype(jnp.int32)
    xs, tgt = x[0], loss_target[0]
    vecs = {name: w[name].reshape(1, D) for name in VECS}

    shards = dict(zip(MATS, cast_shards([w[name] for name in MATS])))
    taps_shard = jnp.pad(dw_kernel, ((0, KW_PAD - KW), (0, 0)))
    full = {}
    full["w_in"], dwk = exchange("gather_w_in", GatherWeights(("w_in",), [shards["w_in"]], taps_shard))
    (u, glu, ag, p, gt), (got,) = fwd_in(
        xs, vecs["mix_pre_g"], full["w_in"], TM_IN, [GatherWeights(MIX_MATS, [shards[n] for n in MIX_MATS])])
    full.update(zip(MIX_MATS, got))
    (cv, sw, z, zl, zs, yc, yp, mg, mo, h1), (got,) = fwd_mix(
        xs, glu, p, gt, dwk, vecs["dw_bias"], vecs["conv_ln_g"], vecs["conv_ln_b"], full["w_conv_out"], full["pool_w"],
        vecs["pool_scale"], full["w_pool_out"], full["w_o"], vecs["mix_post_g"], TM,
        [GatherWeights(FF_MATS, [shards[n] for n in FF_MATS])])
    full.update(zip(FF_MATS, got))

    (v_, a2, df2, df1, dh1, vec_mlp), _ = mlp_fwd_bwd(
        h1, tgt, vecs["mlp_pre_g"], vecs["mlp_post_g"], full["w_ff1"], full["w_ff2"], TM)
    grads, own, wire, halves = {}, {}, {}, {}
    grads["w_ff1"], _ = wgrad(v_, df1, "wgrad_ff1")
    grads["w_ff2"], _ = wgrad(a2, df2, "wgrad_ff2")
    (dmo, dgt, dyc, dyp, dcv, dzl, dz, vec_mix), (from_sibling,) = bwd_mix(
        dh1, mo, cv, zl, yc, yp, gt, vecs["conv_ln_g"], vecs["conv_ln_b"], vecs["pool_scale"], vecs["mix_post_g"],
        full["w_conv_out"], full["pool_w"], full["w_pool_out"], full["w_o"], TM,
        [ExchangePair(FF_MATS, [grads[n] for n in FF_MATS])])
    for n, fs in zip(FF_MATS, from_sibling):
        wire[n], own[n] = pair_add(n, place, grads[n], fs)
    grads["w_conv_out"], _ = wgrad(sw, dyc, "wgrad_conv_out")
    grads["pool_w"] = wgrad_pool(z, dzl)
    grads["w_pool_out"], _ = wgrad(zs, dyp, "wgrad_pool_out")
    grads["w_o"], _ = wgrad(mg, dmo, "wgrad_o")
    (dx, dproj, vec_in, dk), (from_chips, from_sibling) = bwd_in(
        xs, dh1, dcv, dz, glu, ag, dgt, vecs["mix_pre_g"], dwk, full["w_in"], TM,
        [ExchangeChips(FF_MATS, [wire[n] for n in FF_MATS]), ExchangePair(MIX_MATS, [grads[n] for n in MIX_MATS])])
    for n, fc in zip(FF_MATS, from_chips):
        halves[n] = sum_partials(n, place, own[n], fc)
    for n, fs in zip(MIX_MATS, from_sibling):
        wire[n], own[n] = pair_add(n, place, grads[n], fs)
    grads["w_in"], (from_chips,) = wgrad(
        u, dproj, "wgrad_in", bn=NPROJ * D // NCHIP, exchanges=[ExchangeChips(MIX_MATS, [wire[n] for n in MIX_MATS])])
    for n, fc in zip(MIX_MATS, from_chips):
        halves[n] = sum_partials(n, place, own[n], fc)
    (from_sibling,) = exchange("pair_w_in", ExchangePair(("w_in",), [grads["w_in"]]))
    wire["w_in"], own["w_in"] = pair_add("w_in", place, grads["w_in"], from_sibling)
    (from_chips,) = exchange("chips_w_in", ExchangeChips(("w_in",), [wire["w_in"]]))
    halves["w_in"] = sum_partials("w_in", place, own["w_in"], from_chips)
    reduced = swap_halves([halves[n] for n in MATS])
    g = {n: red.reshape(_shard_shape(n)) for n, red in zip(MATS, reduced)}

    small, loss8 = allreduce_small(jnp.concatenate([vec_mlp, vec_mix, vec_in, dk], axis=0))
    loss = loss8[0, 0]
    for name in VECS:
        g[name] = small[ROW[name]]
    g["dw_kernel"] = lax.dynamic_slice(small[VEC_ROWS:VEC_ROWS + KW_PAD], (0, chip * DSH), (KW_PAD, DSH))

    delta, new_m, new_v = {}, {}, {}
    for name in MATS:
        shp = w[name].shape
        two_d = (-1, shp[-1])
        res = adamw(name, w[name].reshape(two_d), g[name].reshape(two_d), m[name].reshape(two_d),
                    v[name].reshape(two_d))
        delta[name], new_m[name], new_v[name] = [r.reshape(shp) for r in res]
    stack = lambda d: jnp.concatenate([d[name].reshape(1, D) for name in VECS], axis=0)
    res = adamw("vectors", stack(w), stack(g), stack(m), stack(v))
    for i, name in enumerate(VECS):
        delta[name], new_m[name], new_v[name] = [r[i] for r in res]
    padk = lambda a: jnp.pad(a, ((0, KW_PAD - KW), (0, 0)))
    res = adamw("dw_kernel", padk(w["dw_kernel"]), g["dw_kernel"], padk(m["dw_kernel"]), padk(v["dw_kernel"]))
    delta["dw_kernel"], new_m["dw_kernel"], new_v["dw_kernel"] = [r[:KW] for r in res]
    g["dw_kernel"] = g["dw_kernel"][:KW]

    return (loss, dx[None], *[g[n] for n in WEIGHTS], *[delta[n] for n in WEIGHTS], *[new_m[n] for n in WEIGHTS],
            *[new_v[n] for n in WEIGHTS])
```

```python
import math

import jax
import jax.numpy as jnp
from jax import lax
from jax.experimental import pallas as pl
from jax.experimental.pallas import tpu as pltpu

F32 = jnp.float32
BF16 = jnp.bfloat16

D = 1024
FF = 4096
NPROJ = 5
KW = 31
KW_PAD = 32
SUBLANES = 8
LANES = 128
HALO = 32
POOL_WINDOWS = (2, 4, 8, 16)
NG = 4
GW = D // NG
RMS_EPS = 1e-6
LN_EPS = 1e-5
LR, B1, B2, ADAM_EPS, WD, STEP = 0.001, 0.9, 0.999, 1e-08, 0.01, 10
NCHIP = 4
VMEM_LIMIT = 60 * 1024 * 1024
MESH = pl.DeviceIdType.MESH
TM = 256
TM_IN = 512

ANY = pl.BlockSpec(memory_space=pl.ANY)
VMEM_SPEC = pl.BlockSpec(memory_space=pltpu.VMEM)
SDS = jax.ShapeDtypeStruct


def _cp(**kw):
    return pltpu.CompilerParams(vmem_limit_bytes=VMEM_LIMIT, **kw)


def _mm(a, b):
    return jnp.dot(a, b, preferred_element_type=F32)


def _mm_nt(a, b):
    return lax.dot_general(a, b, (((1,), (1,)), ((), ())), preferred_element_type=F32)


def _mm_tn(a, b):
    return lax.dot_general(a, b, (((0,), (0,)), ((), ())), preferred_element_type=F32)


def _sigmoid(x):
    return 1.0 / (1.0 + jnp.exp(-x))


def _rowsum(x):
    return jnp.sum(x, axis=0, keepdims=True)


def _full(shape):
    return pl.BlockSpec(shape, lambda i: (0,) * len(shape))


def _tile(tm, cols):
    return pl.BlockSpec((tm, cols), lambda i: (i, 0))


def _prev_halo(tm):
    return pl.BlockSpec((HALO, D), lambda i: (jnp.maximum(i * (tm // HALO) - 1, 0), 0))


def _next_halo(tm, nt):
    return pl.BlockSpec((HALO, D), lambda i: (jnp.minimum((i + 1) * (tm // HALO), nt * (tm // HALO) - 1), 0))


MATS = ("w_in", "w_conv_out", "pool_w", "w_pool_out", "w_o", "w_ff1", "w_ff2")
CIN = NPROJ * D // NCHIP
CFF = FF // NCHIP
_ROWS = lambda k, h: (2 * k + h, 0)
_CHIP_MAJOR = lambda k, h: (k, h, 0)
GEOM = dict(
    w_in=((NCHIP, D, CIN), (None, D // 2, CIN), _CHIP_MAJOR),
    w_conv_out=((D, D), (D // (2 * NCHIP), D), _ROWS),
    pool_w=((NG, GW, GW), (NG // 2, GW // NCHIP, GW), lambda k, h: (h, k, 0)),
    w_pool_out=((D, D), (D // (2 * NCHIP), D), _ROWS),
    w_o=((D, D), (D // (2 * NCHIP), D), _ROWS),
    w_ff1=((NCHIP, D, CFF), (None, D // 2, CFF), _CHIP_MAJOR),
    w_ff2=((FF, D), (FF // (2 * NCHIP), D), _ROWS),
)
DSH = D // NCHIP


def _half_shape(name):
    return tuple(b for b in GEOM[name][1] if b is not None)


def _window(name, k, h):
    _, blk, idx = GEOM[name]
    return tuple(i if b is None else pl.ds(i * b, b) for i, b in zip(idx(k, h), blk))


def _shard_half(name, h):
    n0 = _half_shape(name)[0]
    return (pl.ds(h * n0, n0),) + (slice(None),) * (len(_half_shape(name)) - 1)


def _place():
    x, y, c = lax.axis_index("x"), lax.axis_index("y"), lax.axis_index("c")
    chips = [(1 - x, y), (x, 1 - y), (1 - x, 1 - y)]
    return x, y, c, 2 * x + y, chips, [2 * px + py for px, py in chips]


def _remote(src, dst, send_sem, recv_sem, to):
    return pltpu.make_async_remote_copy(src_ref=src, dst_ref=dst, send_sem=send_sem, recv_sem=recv_sem,
                                        device_id=to, device_id_type=MESH)


class GatherWeights:
    has_mid = True

    def __init__(self, names, fulls, taps=None):
        self.names = names
        self.ins = list(fulls) + ([taps] if taps is not None else [])
        self.has_taps = taps is not None
        self.out_shapes = [SDS(a.shape, a.dtype) for a in self.ins]
        self.aliases = [(i, i) for i in range(len(self.ins))]
        n = len(self.ins)
        self.sems = [pltpu.SemaphoreType.DMA((n, 6)), pltpu.SemaphoreType.DMA((n, 6))]

    def _copies(self, ins, outs, sems):
        send_sems, recv_sems = sems
        x, y, c, me, chips, chip_ids = _place()
        sibling = (x, y, 1 - c)
        ici, ici_recv, d2d, d2d_recv = [], [], [], []
        for w, name in enumerate(self.names):
            for j, chip in enumerate(chips):
                ici.append(_remote(ins[w].at[_window(name, me, c)], outs[w].at[_window(name, me, c)],
                                   send_sems.at[w, j], recv_sems.at[w, j], (*chip, c)))
                got = outs[w].at[_window(name, chip_ids[j], c)]
                ici_recv.append(_remote(got, got, send_sems.at[w, j], recv_sems.at[w, j], sibling))
                d2d.append(_remote(got, got, send_sems.at[w, 3 + j], recv_sems.at[w, 3 + j], sibling))
                got = outs[w].at[_window(name, chip_ids[j], 1 - c)]
                d2d_recv.append(_remote(got, got, send_sems.at[w, 3 + j], recv_sems.at[w, 3 + j], sibling))
        if self.has_taps:
            w = len(self.names)
            for j, chip in enumerate(chips):
                ici.append(_remote(ins[w].at[:, pl.ds(me * DSH, DSH)], outs[w].at[:, pl.ds(me * DSH, DSH)],
                                   send_sems.at[w, j], recv_sems.at[w, j], (*chip, c)))
                got = outs[w].at[:, pl.ds(chip_ids[j] * DSH, DSH)]
                d2d_recv.append(_remote(got, got, send_sems.at[w, j], recv_sems.at[w, j], sibling))
        return ici, ici_recv, d2d, d2d_recv

    def start(self, ins, outs, sems):
        for cp in self._copies(ins, outs, sems)[0]:
            cp.start()

    def mid(self, ins, outs, sems):
        _, ici_recv, d2d, _ = self._copies(ins, outs, sems)
        for got, fwd in zip(ici_recv, d2d):
            got.wait_recv()
            fwd.start()

    def finish(self, ins, outs, sems):
        ici, _, d2d, d2d_recv = self._copies(ins, outs, sems)
        for cp in d2d_recv:
            cp.wait_recv()
        for cp in ici + d2d:
            cp.wait_send()


class ExchangePair:
    has_mid = False
    aliases = ()

    def __init__(self, names, grads):
        self.names, self.ins = names, list(grads)
        self.out_shapes = [SDS((NCHIP, *_half_shape(n)), F32) for n in names]
        self.sems = [pltpu.SemaphoreType.DMA((len(names),)), pltpu.SemaphoreType.DMA((len(names),))]

    def start(self, ins, outs, sems):
        send_sems, recv_sems = sems
        x, y, c, me, chips, chip_ids = _place()
        for w, name in enumerate(self.names):
            for k in range(NCHIP):
                _remote(ins[w].at[_window(name, k, 1 - c)], outs[w].at[k], send_sems.at[w], recv_sems.at[w],
                        (x, y, 1 - c)).start()

    def finish(self, ins, outs, sems):
        send_sems, recv_sems = sems
        x, y, c, me, chips, chip_ids = _place()
        for w in range(len(self.names)):
            _remote(outs[w], outs[w], send_sems.at[w], recv_sems.at[w], (x, y, 1 - c)).wait()


class ExchangeChips:
    has_mid = False
    aliases = ()

    def __init__(self, names, wires):
        self.names, self.ins = names, list(wires)
        self.out_shapes = [SDS((NCHIP - 1, *_half_shape(n)), BF16) for n in names]
        self.sems = [pltpu.SemaphoreType.DMA((len(names), NCHIP - 1)), pltpu.SemaphoreType.DMA((len(names), NCHIP - 1))]

    def _copies(self, ins, outs, sems):
        send_sems, recv_sems = sems
        x, y, c, me, chips, chip_ids = _place()
        return [_remote(ins[w].at[chip_ids[j]], outs[w].at[j], send_sems.at[w, j], recv_sems.at[w, j], (*chip, c))
                for w in range(len(self.names)) for j, chip in enumerate(chips)]

    def start(self, ins, outs, sems):
        for cp in self._copies(ins, outs, sems):
            cp.start()

    def finish(self, ins, outs, sems):
        for cp in self._copies(ins, outs, sems):
            cp.wait()


def _call(body, name, grid, in_specs, out_specs, out_shape, scratch, args, exchanges=()):
    n_in, n_out, n_scr = len(in_specs), len(out_specs), len(scratch)
    x_in = [a for e in exchanges for a in e.ins]
    x_out = [s for e in exchanges for s in e.out_shapes]
    x_sem = [s for e in exchanges for s in e.sems]
    nsteps = math.prod(grid)

    def wrapped(*refs):
        ins, rest = refs[:n_in], refs[n_in:]
        xin, rest = rest[:len(x_in)], rest[len(x_in):]
        outs, rest = rest[:n_out], rest[n_out:]
        xout, rest = rest[:len(x_out)], rest[len(x_out):]
        scr, xsem = rest[:n_scr], rest[n_scr:]
        parts = []
        for e in exchanges:
            parts.append((xin[:len(e.ins)], xout[:len(e.out_shapes)], xsem[:len(e.sems)]))
            xin, xout, xsem = xin[len(e.ins):], xout[len(e.out_shapes):], xsem[len(e.sems):]
        if not grid:
            for e, p in zip(exchanges, parts):
                e.start(*p)
            body(*ins, *outs, *scr)
            for e, p in zip(exchanges, parts):
                if e.has_mid:
                    e.mid(*p)
            for e, p in zip(exchanges, parts):
                e.finish(*p)
            return
        step = 0
        for axis, extent in enumerate(grid):
            step = step * extent + pl.program_id(axis)
        if exchanges:
            @pl.when(step == 0)
            def _():
                for e, p in zip(exchanges, parts):
                    e.start(*p)

        body(*ins, *outs, *scr)
        if any(e.has_mid for e in exchanges):
            @pl.when(step == max(nsteps - 2, 0))
            def _():
                for e, p in zip(exchanges, parts):
                    if e.has_mid:
                        e.mid(*p)

        if exchanges:
            @pl.when(step == nsteps - 1)
            def _():
                for e, p in zip(exchanges, parts):
                    e.finish(*p)

    kw = dict(grid=grid, compiler_params=_cp(dimension_semantics=("arbitrary",) * len(grid))) if grid else dict(
        compiler_params=_cp())
    aliases, i0, o0 = {}, n_in, n_out
    for e in exchanges:
        aliases.update({i0 + i: o0 + o for i, o in e.aliases})
        i0, o0 = i0 + len(e.ins), o0 + len(e.out_shapes)
    res = pl.pallas_call(
        wrapped, name=name, in_specs=list(in_specs) + [ANY] * len(x_in), out_specs=list(out_specs) + [ANY] * len(x_out),
        out_shape=list(out_shape) + x_out, scratch_shapes=list(scratch) + x_sem, input_output_aliases=aliases, **kw,
    )(*args, *x_in)
    outs, rest = res[:n_out], res[n_out:]
    xouts = []
    for e in exchanges:
        xouts.append(rest[:len(e.out_shapes)])
        rest = rest[len(e.out_shapes):]
    return outs, xouts


def exchange(name, ex):
    return _call(lambda: None, name, (), [], [], [], [], [], [ex])[1][0]


def _taps_scratch(tm):
    return pltpu.VMEM((SUBLANES, tm + HALO, LANES), F32)


def _taps(src_ref, k_ref, dst_ref, sh_ref, tm, off0, reverse):
    rc, cw = 64, LANES

    def col_chunk(cc, carry):
        cols = pl.ds(pl.multiple_of(cc * cw, cw), cw)
        for q in range(SUBLANES):
            n = tm + SUBLANES * (len(range(q, KW, SUBLANES)) - 1)
            sh_ref[q, 0:n, :] = src_ref[pl.ds(off0 + q, n), cols]
        for r in range(tm // rc):
            acc = jnp.zeros((rc, cw), F32)
            for q in range(SUBLANES):
                for a, j in enumerate(range(q, KW, SUBLANES)):
                    kj = KW - 1 - j if reverse else j
                    acc = acc + k_ref[kj:kj + 1, cols] * sh_ref[q, pl.ds(r * rc + SUBLANES * a, rc), :]
            dst_ref[pl.ds(r * rc, rc), cols] = acc
        return carry

    lax.fori_loop(0, D // cw, col_chunk, 0)


def _proj_pieces(g):
    lo, hi, pieces = g * D, (g + 1) * D, []
    while lo < hi:
        k = lo // CIN
        b = min(hi - k * CIN, CIN)
        pieces.append((k, lo - k * CIN, b))
        lo = k * CIN + b
    return pieces


def fwd_in(x, g1, w_in, tm, exchanges=()):
    t = x.shape[0]

    def body(x_ref, g_ref, w_ref, u_ref, glu_ref, ag_ref, p_ref, gt_ref):
        xf = x_ref[...]
        r = lax.rsqrt(jnp.mean(xf * xf, axis=-1, keepdims=True) + RMS_EPS)
        u = (xf * r * g_ref[...]).astype(BF16)
        u_ref[...] = u
        proj = lambda g: jnp.concatenate([_mm(u, w_ref[k, :, lo:hi]) for k, lo, hi in _proj_pieces(g)], axis=1)
        a = proj(0)
        gate = proj(1)
        glu_ref[...] = a * _sigmoid(gate)
        ag_ref[:, 0:D] = a.astype(BF16)
        ag_ref[:, D:2 * D] = gate.astype(BF16)
        p_ref[...] = proj(2)
        gt_ref[:, 0:D] = proj(3).astype(BF16)
        gt_ref[:, D:2 * D] = proj(4).astype(BF16)

    return _call(
        body, "fwd_in", (t // tm,),
        [_tile(tm, D), _full((1, D)), _full((NCHIP, D, CIN))],
        [_tile(tm, D), _tile(tm, D), _tile(tm, 2 * D), _tile(tm, D), _tile(tm, 2 * D)],
        [SDS((t, D), BF16), SDS((t, D), F32), SDS((t, 2 * D), BF16), SDS((t, D), F32), SDS((t, 2 * D), BF16)],
        [], [x, g1, w_in], exchanges)


def _pool_inv_count(i, tm, w):
    pos = i * tm + lax.broadcasted_iota(jnp.int32, (tm, 1), 0) + 1
    return 1.0 / jnp.minimum(pos, w).astype(F32)


def _window_sum(src_ref, tmp_ref, cols, tm, w, causal):
    lo, hi = 0, tm + HALO
    cur, span = None, 1
    while span < w:
        new_lo, new_hi = (lo + SUBLANES, hi) if causal else (lo, hi - SUBLANES)
        far = new_lo - span if causal else new_lo + span
        n = new_hi - new_lo
        if cur is None:
            near_v, far_v = src_ref[pl.ds(new_lo, n), cols], src_ref[pl.ds(far, n), cols]
        else:
            near_v = cur[new_lo - lo:new_lo - lo + n]
            if span % SUBLANES == 0:
                far_v = cur[far - lo:far - lo + n]
            else:
                tmp_ref[pl.ds(lo, hi - lo), :] = cur
                far_v = tmp_ref[pl.ds(far, n), :]
        cur, lo, hi, span = near_v + far_v, new_lo, new_hi, 2 * span
    off = HALO if causal else 0
    return cur[off - lo:off - lo + tm]


def fwd_mix(x, glu, p, gt, dwk, dwb, lng, lnb, w_co, pool_w, ps, w_po, w_o, g2, tm, exchanges=()):
    t = x.shape[0]

    def body(x_ref, glu_ref, gluh_ref, p_ref, ph_ref, gt_ref, k_ref, b_ref, lg_ref, lb_ref, wco_ref, pw_ref,
             ps_ref, wpo_ref, wo_ref, g2_ref,
             cv_ref, sw_ref, z_ref, zl_ref, zs_ref, yc_ref, yp_ref, mg_ref, mo_ref, h1_ref, ext_ref, win_ref, sh_ref):
        i = pl.program_id(0)
        keep = (i > 0).astype(F32)
        ext_ref[0:HALO, :] = gluh_ref[...] * keep
        ext_ref[HALO:HALO + tm, :] = glu_ref[...]
        _taps(ext_ref, k_ref, cv_ref, sh_ref, tm, HALO - (KW - 1), False)
        cv = cv_ref[...] + b_ref[...]
        cv_ref[...] = cv
        mu = jnp.mean(cv, axis=-1, keepdims=True)
        cen = cv - mu
        rstd = lax.rsqrt(jnp.mean(cen * cen, axis=-1, keepdims=True) + LN_EPS)
        ln = cen * rstd * lg_ref[...] + lb_ref[...]
        sw = (ln * _sigmoid(ln)).astype(BF16)
        sw_ref[...] = sw
        yc = _mm(sw, wco_ref[...])
        yc_ref[...] = yc.astype(BF16)
        ext_ref[0:HALO, :] = ph_ref[...] * keep
        ext_ref[HALO:HALO + tm, :] = p_ref[...]
        for g, w in enumerate(POOL_WINDOWS):
            cols = pl.ds(g * GW, GW)
            acc = _window_sum(ext_ref, win_ref, cols, tm, w, True)
            zg = (acc * _pool_inv_count(i, tm, w) - p_ref[:, cols]).astype(BF16)
            z_ref[:, cols] = zg
            zl_ref[:, cols] = _mm(zg, pw_ref[g])
        zl = zl_ref[...]
        zs = (zl * ps_ref[...]).astype(BF16)
        zs_ref[...] = zs
        yp = _mm(zs, wpo_ref[...])
        yp_ref[...] = yp.astype(BF16)
        gc = _sigmoid(gt_ref[:, 0:D].astype(F32))
        gp = _sigmoid(gt_ref[:, D:2 * D].astype(F32))
        mg = (gc * yc + gp * yp).astype(BF16)
        mg_ref[...] = mg
        mo = _mm(mg, wo_ref[...])
        mo_ref[...] = mo
        r2 = lax.rsqrt(jnp.mean(mo * mo, axis=-1, keepdims=True) + RMS_EPS)
        h1_ref[...] = x_ref[...] + mo * r2 * g2_ref[...]

    vec = _full((1, D))
    act = lambda dt: SDS((t, D), dt)
    return _call(
        body, "fwd_mix", (t // tm,),
        [_tile(tm, D), _tile(tm, D), _prev_halo(tm), _tile(tm, D), _prev_halo(tm), _tile(tm, 2 * D),
         _full((KW_PAD, D)), vec, vec, vec, _full((D, D)), _full((NG, GW, GW)), vec, _full((D, D)), _full((D, D)), vec],
        [_tile(tm, D)] * 10,
        [act(F32), act(BF16), act(BF16), act(F32), act(BF16), act(BF16), act(BF16), act(BF16), act(F32), act(F32)],
        [pltpu.VMEM((tm + HALO, D), F32), pltpu.VMEM((tm + HALO, GW), F32), _taps_scratch(tm)],
        [x, glu, glu, p, p, gt, dwk, dwb, lng, lnb, w_co, pool_w, ps, w_po, w_o, g2], exchanges)


def mlp_fwd_bwd(h1, tgt, g3, g4, w1, w2, tm, exchanges=()):
    t = h1.shape[0]
    fc = CFF

    def body(h1_ref, tgt_ref, g3_ref, g4_ref, w1_ref, w2_ref,
             v_ref, a2_ref, df2_ref, df1_ref, dh1_ref, vec_ref, f1_ref):
        i = pl.program_id(0)

        @pl.when(i == 0)
        def _():
            vec_ref[...] = jnp.zeros_like(vec_ref)

        h1v = h1_ref[...]
        r3 = lax.rsqrt(jnp.mean(h1v * h1v, axis=-1, keepdims=True) + RMS_EPS)
        n3 = h1v * r3
        v = (n3 * g3_ref[...]).astype(BF16)
        v_ref[...] = v
        f2 = jnp.zeros((tm, D), F32)
        for c in range(FF // fc):
            cols = pl.ds(c * fc, fc)
            f1 = jnp.maximum(_mm(v, w1_ref[c]), 0.0)
            f1_ref[:, cols] = f1
            a2 = (f1 * f1).astype(BF16)
            a2_ref[:, cols] = a2
            f2 = f2 + _mm(a2, w2_ref[cols, :])
        r4 = lax.rsqrt(jnp.mean(f2 * f2, axis=-1, keepdims=True) + RMS_EPS)
        n4 = f2 * r4
        err = h1v + n4 * g4_ref[...] - tgt_ref[...]
        vec_ref[2:3, :] += _rowsum(err * err) * (0.5 / D)
        dh2 = err * (1.0 / D)
        vec_ref[1:2, :] += _rowsum(dh2 * n4)
        dn4 = dh2 * g4_ref[...]
        df2 = (r4 * (dn4 - n4 * jnp.mean(dn4 * n4, axis=-1, keepdims=True))).astype(BF16)
        df2_ref[...] = df2
        dv = jnp.zeros((tm, D), F32)
        for c in range(FF // fc):
            cols = pl.ds(c * fc, fc)
            da2 = _mm_nt(df2, w2_ref[cols, :])
            df1 = (da2 * (2.0 * f1_ref[:, cols])).astype(BF16)
            df1_ref[:, cols] = df1
            dv = dv + _mm_nt(df1, w1_ref[c])
        vec_ref[0:1, :] += _rowsum(dv * n3)
        dn3 = dv * g3_ref[...]
        dh1_ref[...] = dh2 + r3 * (dn3 - n3 * jnp.mean(dn3 * n3, axis=-1, keepdims=True))

    vec = _full((1, D))
    return _call(
        body, "mlp_fwd_bwd", (t // tm,),
        [_tile(tm, D), _tile(tm, D), vec, vec, _full((NCHIP, D, CFF)), _full((FF, D))],
        [_tile(tm, D), _tile(tm, FF), _tile(tm, D), _tile(tm, FF), _tile(tm, D), _full((8, D))],
        [SDS((t, D), BF16), SDS((t, FF), BF16), SDS((t, D), BF16), SDS((t, FF), BF16), SDS((t, D), F32),
         SDS((8, D), F32)],
        [pltpu.VMEM((tm, FF), F32)], [h1, tgt, g3, g4, w1, w2], exchanges)


def bwd_mix(dh1, mo, cv, zl, yc, yp, gt, lng, lnb, ps, g2, w_co, pool_w, w_po, w_o, tm, exchanges=()):
    t = dh1.shape[0]

    def body(dh1_ref, mo_ref, cv_ref, zl_ref, yc_ref, yp_ref, gt_ref, lg_ref, lb_ref, ps_ref, g2_ref,
             wco_ref, pw_ref, wpo_ref, wo_ref,
             dmo_ref, dgt_ref, dyc_ref, dyp_ref, dcv_ref, dzl_ref, dz_ref, vec_ref):
        i = pl.program_id(0)

        @pl.when(i == 0)
        def _():
            vec_ref[...] = jnp.zeros_like(vec_ref)

        dh1v = dh1_ref[...]
        mo = mo_ref[...]
        r2 = lax.rsqrt(jnp.mean(mo * mo, axis=-1, keepdims=True) + RMS_EPS)
        n2 = mo * r2
        vec_ref[0:1, :] += _rowsum(dh1v * n2)
        dn2 = dh1v * g2_ref[...]
        dmo = (r2 * (dn2 - n2 * jnp.mean(dn2 * n2, axis=-1, keepdims=True))).astype(BF16)
        dmo_ref[...] = dmo
        dmg = _mm_nt(dmo, wo_ref[...])
        gc = _sigmoid(gt_ref[:, 0:D].astype(F32))
        gp = _sigmoid(gt_ref[:, D:2 * D].astype(F32))
        dgt_ref[:, 0:D] = (dmg * yc_ref[...].astype(F32) * gc * (1.0 - gc)).astype(BF16)
        dgt_ref[:, D:2 * D] = (dmg * yp_ref[...].astype(F32) * gp * (1.0 - gp)).astype(BF16)
        dyc = (dmg * gc).astype(BF16)
        dyp = (dmg * gp).astype(BF16)
        dyc_ref[...] = dyc
        dyp_ref[...] = dyp
        dsw = _mm_nt(dyc, wco_ref[...])
        cv = cv_ref[...]
        mu = jnp.mean(cv, axis=-1, keepdims=True)
        cen = cv - mu
        rstd = lax.rsqrt(jnp.mean(cen * cen, axis=-1, keepdims=True) + LN_EPS)
        y = cen * rstd
        ln = y * lg_ref[...] + lb_ref[...]
        sg = _sigmoid(ln)
        dln = dsw * (sg * (1.0 + ln * (1.0 - sg)))
        vec_ref[1:2, :] += _rowsum(dln * y)
        vec_ref[2:3, :] += _rowsum(dln)
        dy = dln * lg_ref[...]
        dcv = rstd * (dy - jnp.mean(dy, axis=-1, keepdims=True) - y * jnp.mean(dy * y, axis=-1, keepdims=True))
        dcv_ref[...] = dcv
        vec_ref[3:4, :] += _rowsum(dcv)
        dzs = _mm_nt(dyp, wpo_ref[...])
        vec_ref[4:5, :] += _rowsum(dzs * zl_ref[...])
        dzl = (dzs * ps_ref[...]).astype(BF16)
        dzl_ref[...] = dzl
        for g in range(NG):
            cols = pl.ds(g * GW, GW)
            dz_ref[:, cols] = _mm_nt(dzl_ref[:, cols], pw_ref[g])

    vec = _full((1, D))
    act = lambda dt: SDS((t, D), dt)
    return _call(
        body, "bwd_mix", (t // tm,),
        [_tile(tm, D)] * 6 + [_tile(tm, 2 * D), vec, vec, vec, vec, _full((D, D)), _full((NG, GW, GW)), _full((D, D)),
                              _full((D, D))],
        [_tile(tm, D), _tile(tm, 2 * D)] + [_tile(tm, D)] * 5 + [_full((8, D))],
        [act(BF16), SDS((t, 2 * D), BF16), act(BF16), act(BF16), act(F32), act(BF16), act(F32), SDS((8, D), F32)],
        [], [dh1, mo, cv, zl, yc, yp, gt, lng, lnb, ps, g2, w_co, pool_w, w_po, w_o], exchanges)


def bwd_in(x, dh1, dcv, dz, glu, ag, dgt, g1, dwk, w_in, tm, exchanges=()):
    t = x.shape[0]
    nt = t // tm

    def body(x_ref, dh1_ref, dcv_ref, dcvh_ref, dz_ref, dzh_ref, glu_ref, gluh_ref, ag_ref, dgt_ref, g1_ref,
             k_ref, w_ref, dx_ref, dproj_ref, vec_ref, dk_ref, ext_ref, tmp_ref, win_ref, sh_ref):
        i = pl.program_id(0)

        @pl.when(i == 0)
        def _():
            vec_ref[...] = jnp.zeros_like(vec_ref)
            dk_ref[...] = jnp.zeros_like(dk_ref)

        first = (i > 0).astype(F32)
        last = (i < nt - 1).astype(F32)
        ext_ref[0:HALO, :] = gluh_ref[...] * first
        ext_ref[HALO:HALO + tm, :] = glu_ref[...]
        rc, cw = 64, LANES

        def dk_chunk(cc, carry):
            cols = pl.ds(pl.multiple_of(cc * cw, cw), cw)
            for q in range(SUBLANES):
                taps = range(q, KW, SUBLANES)
                n = tm + SUBLANES * (len(taps) - 1)
                sh_ref[q, 0:n, :] = ext_ref[pl.ds(HALO - (KW - 1) + q, n), cols]
                accs = [jnp.zeros((SUBLANES, cw), F32) for _ in taps]
                for r in range(tm // rc):
                    dchunk = dcv_ref[pl.ds(r * rc, rc), cols]
                    for a in range(len(taps)):
                        prod = dchunk * sh_ref[q, pl.ds(r * rc + SUBLANES * a, rc), :]
                        accs[a] = accs[a] + jnp.sum(prod.reshape(rc // SUBLANES, SUBLANES, cw), axis=0)
                for a, j in enumerate(taps):
                    dk_ref[j:j + 1, cols] += _rowsum(accs[a])
            return carry

        lax.fori_loop(0, D // cw, dk_chunk, 0)
        ext_ref[0:tm, :] = dcv_ref[...]
        ext_ref[tm:tm + HALO, :] = dcvh_ref[...] * last
        _taps(ext_ref, k_ref, tmp_ref, sh_ref, tm, 0, True)
        dglu = tmp_ref[...]
        a = ag_ref[:, 0:D].astype(F32)
        sg = _sigmoid(ag_ref[:, D:2 * D].astype(F32))
        dproj_ref[:, 0:D] = (dglu * sg).astype(BF16)
        dproj_ref[:, D:2 * D] = (dglu * a * sg * (1.0 - sg)).astype(BF16)
        for g, w in enumerate(POOL_WINDOWS):
            cols = pl.ds(g * GW, GW)
            pos = i * tm + lax.broadcasted_iota(jnp.int32, (tm + HALO, 1), 0) + 1
            inv = 1.0 / jnp.minimum(pos, w).astype(F32)
            ext_ref[0:tm, cols] = dz_ref[:, cols] * inv[0:tm]
            ext_ref[tm:tm + HALO, cols] = dzh_ref[:, cols] * inv[tm:tm + HALO] * last
            acc = _window_sum(ext_ref, win_ref, cols, tm, w, False)
            dproj_ref[:, pl.ds(2 * D + g * GW, GW)] = (acc - dz_ref[:, cols]).astype(BF16)
        dproj_ref[:, 3 * D:5 * D] = dgt_ref[...]
        du = jnp.zeros((tm, D), F32)
        for k in range(NCHIP):
            du = du + _mm_nt(dproj_ref[:, k * CIN:(k + 1) * CIN], w_ref[k])
        xf = x_ref[...]
        r1 = lax.rsqrt(jnp.mean(xf * xf, axis=-1, keepdims=True) + RMS_EPS)
        n1 = xf * r1
        vec_ref[0:1, :] += _rowsum(du * n1)
        dn1 = du * g1_ref[...]
        dx_ref[...] = dh1_ref[...] + r1 * (dn1 - n1 * jnp.mean(dn1 * n1, axis=-1, keepdims=True))

    return _call(
        body, "bwd_in", (nt,),
        [_tile(tm, D), _tile(tm, D), _tile(tm, D), _next_halo(tm, nt), _tile(tm, D), _next_halo(tm, nt),
         _tile(tm, D), _prev_halo(tm), _tile(tm, 2 * D), _tile(tm, 2 * D), _full((1, D)),
         _full((KW_PAD, D)), _full((NCHIP, D, CIN))],
        [_tile(tm, D), _tile(tm, NPROJ * D), _full((8, D)), _full((KW_PAD, D))],
        [SDS((t, D), F32), SDS((t, NPROJ * D), BF16), SDS((8, D), F32), SDS((KW_PAD, D), F32)],
        [pltpu.VMEM((tm + HALO, D), F32), pltpu.VMEM((tm, D), F32), pltpu.VMEM((tm + HALO, GW), F32),
         _taps_scratch(tm)],
        [x, dh1, dcv, dcv, dz, dz, glu, glu, ag, dgt, g1, dwk, w_in], exchanges)


def wgrad(a, b, name, bm=1024, bn=1024, bt=512, chip_major=False, exchanges=()):
    t, m = a.shape
    n = b.shape[1]
    bm, bn, bt = min(bm, m), min(bn, n), min(bt, t)
    assert m % bm == 0 and n % bn == 0 and t % bt == 0, (a.shape, b.shape, bm, bn, bt)

    def body(a_ref, b_ref, o_ref):
        k = pl.program_id(2)

        @pl.when(k == 0)
        def _():
            o_ref[...] = jnp.zeros_like(o_ref)

        o_ref[...] += _mm_tn(a_ref[...], b_ref[...])

    if chip_major:
        out_spec, out_shape = pl.BlockSpec((None, bm, bn), lambda i, j, k: (j, i, 0)), SDS((n // bn, m, bn), F32)
    else:
        out_spec, out_shape = pl.BlockSpec((bm, bn), lambda i, j, k: (i, j)), SDS((m, n), F32)
    outs, xouts = _call(
        body, name, (m // bm, n // bn, t // bt),
        [pl.BlockSpec((bt, bm), lambda i, j, k: (k, i)), pl.BlockSpec((bt, bn), lambda i, j, k: (k, j))],
        [out_spec], [out_shape], [], [a, b], exchanges)
    return outs[0], xouts


def wgrad_pool(z, dzl, bt=1024):
    t = z.shape[0]
    bt = min(bt, t)
    assert t % bt == 0

    def body(a_ref, b_ref, o_ref):
        k = pl.program_id(1)

        @pl.when(k == 0)
        def _():
            o_ref[...] = jnp.zeros_like(o_ref)

        o_ref[0] += _mm_tn(a_ref[...], b_ref[...])

    return _call(
        body, "wgrad_pool", (NG, t // bt),
        [pl.BlockSpec((bt, GW), lambda g, k: (k, g)), pl.BlockSpec((bt, GW), lambda g, k: (k, g))],
        [pl.BlockSpec((1, GW, GW), lambda g, k: (g, 0, 0))], [SDS((NG, GW, GW), F32)], [], [z, dzl])[0][0]


def cast_shards(name, names, shards, exchanges=()):
    n = len(shards)

    def body(*refs):
        srcs, dsts, bufs, sems = refs[:n], refs[n:2 * n], refs[2 * n:3 * n], refs[3 * n]
        me = 2 * lax.axis_index("x") + lax.axis_index("y")
        copies = []
        for w, mat in enumerate(names):
            bufs[w][...] = srcs[w][...].astype(BF16)
            for h in range(2):
                cp = pltpu.make_async_copy(bufs[w].at[_shard_half(mat, h)], dsts[w].at[_window(mat, me, h)], sems.at[w, h])
                cp.start()
                copies.append(cp)
        for cp in copies:
            cp.wait()

    return _call(body, name, (), [VMEM_SPEC] * n, [ANY] * n, [SDS(GEOM[mat][0], BF16) for mat in names],
                 [pltpu.VMEM(s.shape, BF16) for s in shards] + [pltpu.SemaphoreType.DMA((n, 2))], list(shards), exchanges)


VEC_ROWS = 24
ROW = dict(mlp_pre_g=0, mlp_post_g=1, loss=2, mix_post_g=8, conv_ln_g=9, conv_ln_b=10, dw_bias=11, pool_scale=12,
           mix_pre_g=16)


def allreduce_small(part):
    rows = part.shape[0]
    ndev = 8

    def body(part_ref, sum_ref, loss_ref, buf_ref, send_sems, recv_sems):
        x, y, c = lax.axis_index("x"), lax.axis_index("y"), lax.axis_index("c")
        me = 4 * x + 2 * y + c
        buf_ref[me] = part_ref[...]
        copies = []
        for m in range(1, ndev):
            to = (x ^ (m >> 2), y ^ ((m >> 1) & 1), c ^ (m & 1))
            cp = _remote(part_ref, buf_ref.at[me], send_sems.at[m - 1], recv_sems.at[m - 1], to)
            cp.start()
            copies.append(cp)
        for m in range(1, ndev):
            got = buf_ref.at[me ^ m]
            _remote(got, got, send_sems.at[m - 1], recv_sems.at[m - 1], (x, y, c)).wait_recv()
        for cp in copies:
            cp.wait_send()
        total = buf_ref[0]
        for d in range(1, ndev):
            total = total + buf_ref[d]
        sum_ref[...] = total
        r = ROW["loss"]
        loss_ref[...] = jnp.zeros_like(loss_ref) + jnp.sum(total[r:r + 1, :])

    return pl.pallas_call(
        body, name="allreduce_small", in_specs=[VMEM_SPEC], out_specs=[VMEM_SPEC, VMEM_SPEC],
        out_shape=[SDS((rows, D), F32), SDS((8, 128), F32)],
        scratch_shapes=[pltpu.VMEM((ndev, rows, D), F32), pltpu.SemaphoreType.DMA((ndev - 1,)),
                        pltpu.SemaphoreType.DMA((ndev - 1,))],
        compiler_params=_cp(),
    )(part)


def pair_add(name, place, grad, from_sibling):
    _, gblk, idx = GEOM[name]
    blk = _half_shape(name)
    zeros = (0,) * len(blk)

    def body(place_ref, g_ref, s_ref, wire_ref, own_ref):
        k = pl.program_id(0)
        total = g_ref[...] + s_ref[0]
        wire_ref[0] = total.astype(BF16)

        @pl.when(k == place_ref[1])
        def _():
            own_ref[...] = total

    return pl.pallas_call(
        body, name="pair_add_" + name,
        grid_spec=pltpu.PrefetchScalarGridSpec(
            num_scalar_prefetch=1, grid=(NCHIP,),
            in_specs=[pl.BlockSpec(gblk, lambda k, pr: idx(k, pr[0])), pl.BlockSpec((1, *blk), lambda k, pr: (k, *zeros))],
            out_specs=[pl.BlockSpec((1, *blk), lambda k, pr: (k, *zeros)), pl.BlockSpec(blk, lambda k, pr: zeros)]),
        out_shape=[SDS((NCHIP, *blk), BF16), SDS(blk, F32)],
        compiler_params=_cp(dimension_semantics=("arbitrary",)),
    )(place, grad, from_sibling)


def sum_partials(name, place, own, from_chips):
    blk = _half_shape(name)
    zeros = (0,) * len(blk)

    def body(place_ref, own_ref, r_ref, out_ref):
        total = own_ref[...]
        for j in range(NCHIP - 1):
            total = total + r_ref[j].astype(F32)
        out_ref[0] = total

    return pl.pallas_call(
        body, name="sum_partials_" + name,
        grid_spec=pltpu.PrefetchScalarGridSpec(
            num_scalar_prefetch=1, grid=(1,),
            in_specs=[pl.BlockSpec(blk, lambda i, pr: zeros), pl.BlockSpec((NCHIP - 1, *blk), lambda i, pr: (0, *zeros))],
            out_specs=[pl.BlockSpec((1, *blk), lambda i, pr: (pr[0], *zeros))]),
        out_shape=[SDS((2, *blk), F32)],
        compiler_params=_cp(dimension_semantics=("arbitrary",)),
    )(place, own, from_chips)[0]


def swap_halves(halves):
    nm = len(halves)

    def body(*refs):
        srcs, dsts = refs[:nm], refs[nm:2 * nm]
        send_sems, recv_sems = refs[2 * nm:]
        x, y, c, me, chips, chip_ids = _place()
        copies = []
        for w in range(nm):
            cp = _remote(srcs[w].at[c], dsts[w].at[c], send_sems.at[w], recv_sems.at[w], (x, y, 1 - c))
            cp.start()
            copies.append(cp)
        for w, cp in enumerate(copies):
            cp.wait_send()
            _remote(srcs[w].at[1 - c], dsts[w].at[1 - c], send_sems.at[w], recv_sems.at[w], (x, y, 1 - c)).wait_recv()

    return pl.pallas_call(
        body, name="swap_halves", in_specs=[ANY] * nm, out_specs=[ANY] * nm,
        out_shape=[SDS(h.shape, F32) for h in halves],
        input_output_aliases={w: w for w in range(nm)},
        scratch_shapes=[pltpu.SemaphoreType.DMA((nm,)), pltpu.SemaphoreType.DMA((nm,))],
        compiler_params=_cp(),
    )(*halves)


def adamw(name, w, g, m, v, rows_per_block=256):
    r, cdim = w.shape
    br = min(rows_per_block, r)

    def body(w_ref, g_ref, m_ref, v_ref, d_ref, nm_ref, nv_ref):
        gv = g_ref[...]
        mn = B1 * m_ref[...] + (1.0 - B1) * gv
        vn = B2 * v_ref[...] + (1.0 - B2) * (gv * gv)
        m_hat = mn / (1.0 - B1 ** STEP)
        v_hat = vn / (1.0 - B2 ** STEP)
        d_ref[...] = -LR * (m_hat / (jnp.sqrt(v_hat) + ADAM_EPS) + WD * w_ref[...])
        nm_ref[...] = mn
        nv_ref[...] = vn

    spec = pl.BlockSpec((br, cdim), lambda i: (i, 0))
    return pl.pallas_call(
        body, name="adamw_" + name, grid=(r // br,), in_specs=[spec] * 4, out_specs=[spec] * 3,
        out_shape=[SDS((r, cdim), F32)] * 3,
        compiler_params=_cp(dimension_semantics=("arbitrary",)),
    )(w, g, m, v)


VECS = ("mix_pre_g", "dw_bias", "conv_ln_g", "conv_ln_b", "pool_scale", "mix_post_g", "mlp_pre_g", "mlp_post_g")
WEIGHTS = ("mix_pre_g", "w_in", "dw_kernel", "dw_bias", "conv_ln_g", "conv_ln_b", "w_conv_out", "pool_w", "pool_scale",
           "w_pool_out", "w_o", "mix_post_g", "mlp_pre_g", "w_ff1", "w_ff2", "mlp_post_g")
MIX_MATS = ("w_conv_out", "pool_w", "w_pool_out", "w_o")
FF_MATS = ("w_ff1", "w_ff2")


def kernel(x, mix_pre_g, w_in, dw_kernel, dw_bias, conv_ln_g, conv_ln_b, w_conv_out, pool_w, pool_scale, w_pool_out, w_o, mix_post_g, mlp_pre_g, w_ff1, w_ff2, mlp_post_g, loss_target, m_mix_pre_g, m_w_in, m_dw_kernel, m_dw_bias, m_conv_ln_g, m_conv_ln_b, m_w_conv_out, m_pool_w, m_pool_scale, m_w_pool_out, m_w_o, m_mix_post_g, m_mlp_pre_g, m_w_ff1, m_w_ff2, m_mlp_post_g, v_mix_pre_g, v_w_in, v_dw_kernel, v_dw_bias, v_conv_ln_g, v_conv_ln_b, v_w_conv_out, v_pool_w, v_pool_scale, v_w_pool_out, v_w_o, v_mix_post_g, v_mlp_pre_g, v_w_ff1, v_w_ff2, v_mlp_post_g):
    w = dict(mix_pre_g=mix_pre_g, w_in=w_in, dw_kernel=dw_kernel, dw_bias=dw_bias, conv_ln_g=conv_ln_g,
             conv_ln_b=conv_ln_b, w_conv_out=w_conv_out, pool_w=pool_w, pool_scale=pool_scale, w_pool_out=w_pool_out,
             w_o=w_o, mix_post_g=mix_post_g, mlp_pre_g=mlp_pre_g, w_ff1=w_ff1, w_ff2=w_ff2, mlp_post_g=mlp_post_g)
    m = dict(mix_pre_g=m_mix_pre_g, w_in=m_w_in, dw_kernel=m_dw_kernel, dw_bias=m_dw_bias, conv_ln_g=m_conv_ln_g,
             conv_ln_b=m_conv_ln_b, w_conv_out=m_w_conv_out, pool_w=m_pool_w, pool_scale=m_pool_scale,
             w_pool_out=m_w_pool_out, w_o=m_w_o, mix_post_g=m_mix_post_g, mlp_pre_g=m_mlp_pre_g, w_ff1=m_w_ff1,
             w_ff2=m_w_ff2, mlp_post_g=m_mlp_post_g)
    v = dict(mix_pre_g=v_mix_pre_g, w_in=v_w_in, dw_kernel=v_dw_kernel, dw_bias=v_dw_bias, conv_ln_g=v_conv_ln_g,
             conv_ln_b=v_conv_ln_b, w_conv_out=v_w_conv_out, pool_w=v_pool_w, pool_scale=v_pool_scale,
             w_pool_out=v_w_pool_out, w_o=v_w_o, mix_post_g=v_mix_post_g, mlp_pre_g=v_mlp_pre_g, w_ff1=v_w_ff1,
             w_ff2=v_w_ff2, mlp_post_g=v_mlp_post_g)
    cx, cy, cc = lax.axis_index("x"), lax.axis_index("y"), lax.axis_index("c")
    chip = 2 * cx + cy
    place = jnp.stack([cc, chip]).astype(jnp.int32)
    xs, tgt = x[0], loss_target[0]
    vecs = {name: w[name].reshape(1, D) for name in VECS}

    taps = lax.dynamic_update_slice(jnp.zeros((KW_PAD, D), F32), dw_kernel, (0, chip * DSH))
    mine, full = {}, {}
    (mine["w_in"],), _ = cast_shards("cast_w_in", ("w_in",), [w["w_in"]])
    rest = MATS[1:]
    cast, ((full["w_in"], dwk),) = cast_shards(
        "cast_rest", rest, [w[name] for name in rest], [GatherWeights(("w_in",), [mine["w_in"]], taps)])
    mine.update(zip(rest, cast))
    (u, glu, ag, p, gt), (got,) = fwd_in(
        xs, vecs["mix_pre_g"], full["w_in"], TM_IN, [GatherWeights(MIX_MATS, [mine[n] for n in MIX_MATS])])
    full.update(zip(MIX_MATS, got))
    (cv, sw, z, zl, zs, yc, yp, mg, mo, h1), (got,) = fwd_mix(
        xs, glu, p, gt, dwk, vecs["dw_bias"], vecs["conv_ln_g"], vecs["conv_ln_b"], full["w_conv_out"], full["pool_w"],
        vecs["pool_scale"], full["w_pool_out"], full["w_o"], vecs["mix_post_g"], TM,
        [GatherWeights(FF_MATS, [mine[n] for n in FF_MATS])])
    full.update(zip(FF_MATS, got))

    (v_, a2, df2, df1, dh1, vec_mlp), _ = mlp_fwd_bwd(
        h1, tgt, vecs["mlp_pre_g"], vecs["mlp_post_g"], full["w_ff1"], full["w_ff2"], TM)
    grads, own, wire, halves = {}, {}, {}, {}
    grads["w_ff1"], _ = wgrad(v_, df1, "wgrad_ff1", bn=CFF, chip_major=True)
    grads["w_ff2"], _ = wgrad(a2, df2, "wgrad_ff2")
    (dmo, dgt, dyc, dyp, dcv, dzl, dz, vec_mix), (from_sibling,) = bwd_mix(
        dh1, mo, cv, zl, yc, yp, gt, vecs["conv_ln_g"], vecs["conv_ln_b"], vecs["pool_scale"], vecs["mix_post_g"],
        full["w_conv_out"], full["pool_w"], full["w_pool_out"], full["w_o"], TM,
        [ExchangePair(FF_MATS, [grads[n] for n in FF_MATS])])
    for n, fs in zip(FF_MATS, from_sibling):
        wire[n], own[n] = pair_add(n, place, grads[n], fs)
    grads["w_conv_out"], _ = wgrad(sw, dyc, "wgrad_conv_out")
    grads["pool_w"] = wgrad_pool(z, dzl)
    grads["w_pool_out"], _ = wgrad(zs, dyp, "wgrad_pool_out")
    grads["w_o"], _ = wgrad(mg, dmo, "wgrad_o")
    (dx, dproj, vec_in, dk), (from_chips, from_sibling) = bwd_in(
        xs, dh1, dcv, dz, glu, ag, dgt, vecs["mix_pre_g"], dwk, full["w_in"], TM,
        [ExchangeChips(FF_MATS, [wire[n] for n in FF_MATS]), ExchangePair(MIX_MATS, [grads[n] for n in MIX_MATS])])
    for n, fc in zip(FF_MATS, from_chips):
        halves[n] = sum_partials(n, place, own[n], fc)
    for n, fs in zip(MIX_MATS, from_sibling):
        wire[n], own[n] = pair_add(n, place, grads[n], fs)
    grads["w_in"], (from_chips,) = wgrad(
        u, dproj, "wgrad_in", bn=CIN, chip_major=True, exchanges=[ExchangeChips(MIX_MATS, [wire[n] for n in MIX_MATS])])
    for n, fc in zip(MIX_MATS, from_chips):
        halves[n] = sum_partials(n, place, own[n], fc)
    (from_sibling,) = exchange("pair_w_in", ExchangePair(("w_in",), [grads["w_in"]]))
    wire["w_in"], own["w_in"] = pair_add("w_in", place, grads["w_in"], from_sibling)
    (from_chips,) = exchange("chips_w_in", ExchangeChips(("w_in",), [wire["w_in"]]))
    halves["w_in"] = sum_partials("w_in", place, own["w_in"], from_chips)
    reduced = swap_halves([halves[n] for n in MATS])
    g = {n: red.reshape(w[n].shape) for n, red in zip(MATS, reduced)}

    small, loss8 = allreduce_small(jnp.concatenate([vec_mlp, vec_mix, vec_in, dk], axis=0))
    loss = loss8[0, 0]
    for name in VECS:
        g[name] = small[ROW[name]]
    g["dw_kernel"] = lax.dynamic_slice(small[VEC_ROWS:VEC_ROWS + KW_PAD], (0, chip * DSH), (KW_PAD, DSH))

    delta, new_m, new_v = {}, {}, {}
    for name in MATS:
        shp = w[name].shape
        two_d = (-1, shp[-1])
        res = adamw(name, w[name].reshape(two_d), g[name].reshape(two_d), m[name].reshape(two_d),
                    v[name].reshape(two_d))
        delta[name], new_m[name], new_v[name] = [r.reshape(shp) for r in res]
    stack = lambda d: jnp.concatenate([d[name].reshape(1, D) for name in VECS], axis=0)
    res = adamw("vectors", stack(w), stack(g), stack(m), stack(v))
    for i, name in enumerate(VECS):
        delta[name], new_m[name], new_v[name] = [r[i] for r in res]
    padk = lambda a: jnp.pad(a, ((0, KW_PAD - KW), (0, 0)))
    res = adamw("dw_kernel", padk(w["dw_kernel"]), g["dw_kernel"], padk(m["dw_kernel"]), padk(v["dw_kernel"]))
    delta["dw_kernel"], new_m["dw_kernel"], new_v["dw_kernel"] = [r[:KW] for r in res]
    g["dw_kernel"] = g["dw_kernel"][:KW]

    return (loss, dx[None], *[g[n] for n in WEIGHTS], *[delta[n] for n in WEIGHTS], *[new_m[n] for n in WEIGHTS],
            *[new_v[n] for n in WEIGHTS])
```

```python
import math

import jax
import jax.numpy as jnp
from jax import lax
from jax.experimental import pallas as pl
from jax.experimental.pallas import tpu as pltpu

F32 = jnp.float32
BF16 = jnp.bfloat16

D = 1024
FF = 4096
NPROJ = 5
KW = 31
KW_PAD = 32
SUBLANES = 8
LANES = 128
HALO = 32
POOL_WINDOWS = (2, 4, 8, 16)
NG = 4
GW = D // NG
RMS_EPS = 1e-6
LN_EPS = 1e-5
LR, B1, B2, ADAM_EPS, WD, STEP = 0.001, 0.9, 0.999, 1e-08, 0.01, 10
NCHIP = 4
VMEM_LIMIT = 60 * 1024 * 1024
MESH = pl.DeviceIdType.MESH
TM = 256
TM_IN = 512

ANY = pl.BlockSpec(memory_space=pl.ANY)
VMEM_SPEC = pl.BlockSpec(memory_space=pltpu.VMEM)
SDS = jax.ShapeDtypeStruct


def _cp(**kw):
    return pltpu.CompilerParams(vmem_limit_bytes=VMEM_LIMIT, **kw)


def _mm(a, b):
    return jnp.dot(a, b, preferred_element_type=F32)


def _mm_nt(a, b):
    return lax.dot_general(a, b, (((1,), (1,)), ((), ())), preferred_element_type=F32)


def _mm_tn(a, b):
    return lax.dot_general(a, b, (((0,), (0,)), ((), ())), preferred_element_type=F32)


def _sigmoid(x):
    return 1.0 / (1.0 + jnp.exp(-x))


def _rowsum(x):
    return jnp.sum(x, axis=0, keepdims=True)


def _full(shape):
    return pl.BlockSpec(shape, lambda i: (0,) * len(shape))


def _tile(tm, cols):
    return pl.BlockSpec((tm, cols), lambda i: (i, 0))


def _prev_halo(tm):
    return pl.BlockSpec((HALO, D), lambda i: (jnp.maximum(i * (tm // HALO) - 1, 0), 0))


def _next_halo(tm, nt):
    return pl.BlockSpec((HALO, D), lambda i: (jnp.minimum((i + 1) * (tm // HALO), nt * (tm // HALO) - 1), 0))


MATS = ("w_in", "w_conv_out", "pool_w", "w_pool_out", "w_o", "w_ff1", "w_ff2")
CIN = NPROJ * D // NCHIP
CFF = FF // NCHIP
_ROWS = lambda k, h: (2 * k + h, 0)
_CHIP_MAJOR = lambda k, h: (k, h, 0)
GEOM = dict(
    w_in=((NCHIP, D, CIN), (None, D // 2, CIN), _CHIP_MAJOR),
    w_conv_out=((D, D), (D // (2 * NCHIP), D), _ROWS),
    pool_w=((NG, GW, GW), (NG // 2, GW // NCHIP, GW), lambda k, h: (h, k, 0)),
    w_pool_out=((D, D), (D // (2 * NCHIP), D), _ROWS),
    w_o=((D, D), (D // (2 * NCHIP), D), _ROWS),
    w_ff1=((NCHIP, D, CFF), (None, D // 2, CFF), _CHIP_MAJOR),
    w_ff2=((FF, D), (FF // (2 * NCHIP), D), _ROWS),
)
DSH = D // NCHIP


def _half_shape(name):
    return tuple(b for b in GEOM[name][1] if b is not None)


def _window(name, k, h):
    _, blk, idx = GEOM[name]
    return tuple(i if b is None else pl.ds(i * b, b) for i, b in zip(idx(k, h), blk))


def _shard_half(name, h):
    n0 = _half_shape(name)[0]
    return (pl.ds(h * n0, n0),) + (slice(None),) * (len(_half_shape(name)) - 1)


def _place():
    x, y, c = lax.axis_index("x"), lax.axis_index("y"), lax.axis_index("c")
    chips = [(1 - x, y), (x, 1 - y), (1 - x, 1 - y)]
    return x, y, c, 2 * x + y, chips, [2 * px + py for px, py in chips]


def _remote(src, dst, send_sem, recv_sem, to):
    return pltpu.make_async_remote_copy(src_ref=src, dst_ref=dst, send_sem=send_sem, recv_sem=recv_sem,
                                        device_id=to, device_id_type=MESH)


class GatherWeights:
    has_mid = True

    def __init__(self, names, fulls, taps=None):
        self.names = names
        self.ins = list(fulls) + ([taps] if taps is not None else [])
        self.has_taps = taps is not None
        self.out_shapes = [SDS(a.shape, a.dtype) for a in self.ins]
        self.aliases = [(i, i) for i in range(len(self.ins))]
        n = len(self.ins)
        self.sems = [pltpu.SemaphoreType.DMA((n, 6)), pltpu.SemaphoreType.DMA((n, 6))]

    def _copies(self, ins, outs, sems):
        send_sems, recv_sems = sems
        x, y, c, me, chips, chip_ids = _place()
        sibling = (x, y, 1 - c)
        ici, ici_recv, d2d, d2d_recv = [], [], [], []
        for w, name in enumerate(self.names):
            for j, chip in enumerate(chips):
                ici.append(_remote(ins[w].at[_window(name, me, c)], outs[w].at[_window(name, me, c)],
                                   send_sems.at[w, j], recv_sems.at[w, j], (*chip, c)))
                got = outs[w].at[_window(name, chip_ids[j], c)]
                ici_recv.append(_remote(got, got, send_sems.at[w, j], recv_sems.at[w, j], sibling))
                d2d.append(_remote(got, got, send_sems.at[w, 3 + j], recv_sems.at[w, 3 + j], sibling))
                got = outs[w].at[_window(name, chip_ids[j], 1 - c)]
                d2d_recv.append(_remote(got, got, send_sems.at[w, 3 + j], recv_sems.at[w, 3 + j], sibling))
        if self.has_taps:
            w = len(self.names)
            for j, chip in enumerate(chips):
                ici.append(_remote(ins[w].at[:, pl.ds(me * DSH, DSH)], outs[w].at[:, pl.ds(me * DSH, DSH)],
                                   send_sems.at[w, j], recv_sems.at[w, j], (*chip, c)))
                got = outs[w].at[:, pl.ds(chip_ids[j] * DSH, DSH)]
                d2d_recv.append(_remote(got, got, send_sems.at[w, j], recv_sems.at[w, j], sibling))
        return ici, ici_recv, d2d, d2d_recv

    def start(self, ins, outs, sems):
        for cp in self._copies(ins, outs, sems)[0]:
            cp.start()

    def mid(self, ins, outs, sems):
        _, ici_recv, d2d, _ = self._copies(ins, outs, sems)
        for got, fwd in zip(ici_recv, d2d):
            got.wait_recv()
            fwd.start()

    def finish(self, ins, outs, sems):
        ici, _, d2d, d2d_recv = self._copies(ins, outs, sems)
        for cp in d2d_recv:
            cp.wait_recv()
        for cp in ici + d2d:
            cp.wait_send()


class ExchangePair:
    has_mid = False
    aliases = ()

    def __init__(self, names, grads):
        self.names, self.ins = names, list(grads)
        self.out_shapes = [SDS((NCHIP, *_half_shape(n)), F32) for n in names]
        self.sems = [pltpu.SemaphoreType.DMA((len(names),)), pltpu.SemaphoreType.DMA((len(names),))]

    def start(self, ins, outs, sems):
        send_sems, recv_sems = sems
        x, y, c, me, chips, chip_ids = _place()
        for w, name in enumerate(self.names):
            for k in range(NCHIP):
                _remote(ins[w].at[_window(name, k, 1 - c)], outs[w].at[k], send_sems.at[w], recv_sems.at[w],
                        (x, y, 1 - c)).start()

    def finish(self, ins, outs, sems):
        send_sems, recv_sems = sems
        x, y, c, me, chips, chip_ids = _place()
        for w in range(len(self.names)):
            _remote(outs[w], outs[w], send_sems.at[w], recv_sems.at[w], (x, y, 1 - c)).wait()


class ExchangeChips:
    has_mid = False
    aliases = ()

    def __init__(self, names, wires):
        self.names, self.ins = names, list(wires)
        self.out_shapes = [SDS((NCHIP - 1, *_half_shape(n)), BF16) for n in names]
        self.sems = [pltpu.SemaphoreType.DMA((len(names), NCHIP - 1)), pltpu.SemaphoreType.DMA((len(names), NCHIP - 1))]

    def _copies(self, ins, outs, sems):
        send_sems, recv_sems = sems
        x, y, c, me, chips, chip_ids = _place()
        return [_remote(ins[w].at[chip_ids[j]], outs[w].at[j], send_sems.at[w, j], recv_sems.at[w, j], (*chip, c))
                for w in range(len(self.names)) for j, chip in enumerate(chips)]

    def start(self, ins, outs, sems):
        for cp in self._copies(ins, outs, sems):
            cp.start()

    def finish(self, ins, outs, sems):
        for cp in self._copies(ins, outs, sems):
            cp.wait()


def _call(body, name, grid, in_specs, out_specs, out_shape, scratch, args, exchanges=()):
    n_in, n_out, n_scr = len(in_specs), len(out_specs), len(scratch)
    x_in = [a for e in exchanges for a in e.ins]
    x_out = [s for e in exchanges for s in e.out_shapes]
    x_sem = [s for e in exchanges for s in e.sems]
    nsteps = math.prod(grid)

    def wrapped(*refs):
        ins, rest = refs[:n_in], refs[n_in:]
        xin, rest = rest[:len(x_in)], rest[len(x_in):]
        outs, rest = rest[:n_out], rest[n_out:]
        xout, rest = rest[:len(x_out)], rest[len(x_out):]
        scr, xsem = rest[:n_scr], rest[n_scr:]
        parts = []
        for e in exchanges:
            parts.append((xin[:len(e.ins)], xout[:len(e.out_shapes)], xsem[:len(e.sems)]))
            xin, xout, xsem = xin[len(e.ins):], xout[len(e.out_shapes):], xsem[len(e.sems):]
        if not grid:
            for e, p in zip(exchanges, parts):
                e.start(*p)
            body(*ins, *outs, *scr)
            for e, p in zip(exchanges, parts):
                if e.has_mid:
                    e.mid(*p)
            for e, p in zip(exchanges, parts):
                e.finish(*p)
            return
        step = 0
        for axis, extent in enumerate(grid):
            step = step * extent + pl.program_id(axis)
        if exchanges:
            @pl.when(step == 0)
            def _():
                for e, p in zip(exchanges, parts):
                    e.start(*p)

        body(*ins, *outs, *scr)
        if any(e.has_mid for e in exchanges):
            @pl.when(step == max(nsteps - 2, 0))
            def _():
                for e, p in zip(exchanges, parts):
                    if e.has_mid:
                        e.mid(*p)

        if exchanges:
            @pl.when(step == nsteps - 1)
            def _():
                for e, p in zip(exchanges, parts):
                    e.finish(*p)

    kw = dict(grid=grid, compiler_params=_cp(dimension_semantics=("arbitrary",) * len(grid))) if grid else dict(
        compiler_params=_cp())
    aliases, i0, o0 = {}, n_in, n_out
    for e in exchanges:
        aliases.update({i0 + i: o0 + o for i, o in e.aliases})
        i0, o0 = i0 + len(e.ins), o0 + len(e.out_shapes)
    res = pl.pallas_call(
        wrapped, name=name, in_specs=list(in_specs) + [ANY] * len(x_in), out_specs=list(out_specs) + [ANY] * len(x_out),
        out_shape=list(out_shape) + x_out, scratch_shapes=list(scratch) + x_sem, input_output_aliases=aliases, **kw,
    )(*args, *x_in)
    outs, rest = res[:n_out], res[n_out:]
    xouts = []
    for e in exchanges:
        xouts.append(rest[:len(e.out_shapes)])
        rest = rest[len(e.out_shapes):]
    return outs, xouts


def exchange(name, ex):
    return _call(lambda: None, name, (), [], [], [], [], [], [ex])[1][0]


def _taps_scratch(tm):
    return pltpu.VMEM((SUBLANES, tm + HALO, LANES), F32)


def _taps(src_ref, k_ref, dst_ref, sh_ref, tm, off0, reverse):
    rc, cw = 64, LANES

    def col_chunk(cc, carry):
        cols = pl.ds(pl.multiple_of(cc * cw, cw), cw)
        for q in range(SUBLANES):
            n = tm + SUBLANES * (len(range(q, KW, SUBLANES)) - 1)
            sh_ref[q, 0:n, :] = src_ref[pl.ds(off0 + q, n), cols]
        for r in range(tm // rc):
            acc = jnp.zeros((rc, cw), F32)
            for q in range(SUBLANES):
                for a, j in enumerate(range(q, KW, SUBLANES)):
                    kj = KW - 1 - j if reverse else j
                    acc = acc + k_ref[kj:kj + 1, cols] * sh_ref[q, pl.ds(r * rc + SUBLANES * a, rc), :]
            dst_ref[pl.ds(r * rc, rc), cols] = acc
        return carry

    lax.fori_loop(0, D // cw, col_chunk, 0)


def _proj_pieces(g):
    lo, hi, pieces = g * D, (g + 1) * D, []
    while lo < hi:
        k = lo // CIN
        b = min(hi - k * CIN, CIN)
        pieces.append((k, lo - k * CIN, b))
        lo = k * CIN + b
    return pieces


def fwd_in(x, g1, w_in, tm, exchanges=()):
    t = x.shape[0]

    def body(x_ref, g_ref, w_ref, u_ref, glu_ref, ag_ref, p_ref, gt_ref):
        xf = x_ref[...]
        r = lax.rsqrt(jnp.mean(xf * xf, axis=-1, keepdims=True) + RMS_EPS)
        u = (xf * r * g_ref[...]).astype(BF16)
        u_ref[...] = u
        proj = lambda g: jnp.concatenate([_mm(u, w_ref[k, :, lo:hi]) for k, lo, hi in _proj_pieces(g)], axis=1)
        a = proj(0)
        gate = proj(1)
        glu_ref[...] = a * _sigmoid(gate)
        ag_ref[:, 0:D] = a.astype(BF16)
        ag_ref[:, D:2 * D] = gate.astype(BF16)
        p_ref[...] = proj(2)
        gt_ref[:, 0:D] = proj(3).astype(BF16)
        gt_ref[:, D:2 * D] = proj(4).astype(BF16)

    return _call(
        body, "fwd_in", (t // tm,),
        [_tile(tm, D), _full((1, D)), _full((NCHIP, D, CIN))],
        [_tile(tm, D), _tile(tm, D), _tile(tm, 2 * D), _tile(tm, D), _tile(tm, 2 * D)],
        [SDS((t, D), BF16), SDS((t, D), F32), SDS((t, 2 * D), BF16), SDS((t, D), F32), SDS((t, 2 * D), BF16)],
        [], [x, g1, w_in], exchanges)


def _pool_inv_count(i, tm, w):
    pos = i * tm + lax.broadcasted_iota(jnp.int32, (tm, 1), 0) + 1
    return 1.0 / jnp.minimum(pos, w).astype(F32)


def _window_sum(src_ref, tmp_ref, cols, tm, w, causal):
    lo, hi = 0, tm + HALO
    cur, span = None, 1
    while span < w:
        new_lo, new_hi = (lo + SUBLANES, hi) if causal else (lo, hi - SUBLANES)
        far = new_lo - span if causal else new_lo + span
        n = new_hi - new_lo
        if cur is None:
            near_v, far_v = src_ref[pl.ds(new_lo, n), cols], src_ref[pl.ds(far, n), cols]
        else:
            near_v = cur[new_lo - lo:new_lo - lo + n]
            if span % SUBLANES == 0:
                far_v = cur[far - lo:far - lo + n]
            else:
                tmp_ref[pl.ds(lo, hi - lo), :] = cur
                far_v = tmp_ref[pl.ds(far, n), :]
        cur, lo, hi, span = near_v + far_v, new_lo, new_hi, 2 * span
    off = HALO if causal else 0
    return cur[off - lo:off - lo + tm]


def fwd_mix(x, glu, p, gt, dwk, dwb, lng, lnb, w_co, pool_w, ps, w_po, w_o, g2, tm, exchanges=()):
    t = x.shape[0]

    def body(x_ref, glu_ref, gluh_ref, p_ref, ph_ref, gt_ref, k_ref, b_ref, lg_ref, lb_ref, wco_ref, pw_ref,
             ps_ref, wpo_ref, wo_ref, g2_ref,
             cv_ref, sw_ref, z_ref, zl_ref, zs_ref, yc_ref, yp_ref, mg_ref, mo_ref, h1_ref, ext_ref, win_ref, sh_ref):
        i = pl.program_id(0)
        keep = (i > 0).astype(F32)
        ext_ref[0:HALO, :] = gluh_ref[...] * keep
        ext_ref[HALO:HALO + tm, :] = glu_ref[...]
        _taps(ext_ref, k_ref, cv_ref, sh_ref, tm, HALO - (KW - 1), False)
        cv = cv_ref[...] + b_ref[...]
        cv_ref[...] = cv
        mu = jnp.mean(cv, axis=-1, keepdims=True)
        cen = cv - mu
        rstd = lax.rsqrt(jnp.mean(cen * cen, axis=-1, keepdims=True) + LN_EPS)
        ln = cen * rstd * lg_ref[...] + lb_ref[...]
        sw = (ln * _sigmoid(ln)).astype(BF16)
        sw_ref[...] = sw
        yc = _mm(sw, wco_ref[...])
        yc_ref[...] = yc.astype(BF16)
        ext_ref[0:HALO, :] = ph_ref[...] * keep
        ext_ref[HALO:HALO + tm, :] = p_ref[...]
        for g, w in enumerate(POOL_WINDOWS):
            cols = pl.ds(g * GW, GW)
            acc = _window_sum(ext_ref, win_ref, cols, tm, w, True)
            zg = (acc * _pool_inv_count(i, tm, w) - p_ref[:, cols]).astype(BF16)
            z_ref[:, cols] = zg
            zl_ref[:, cols] = _mm(zg, pw_ref[g])
        zl = zl_ref[...]
        zs = (zl * ps_ref[...]).astype(BF16)
        zs_ref[...] = zs
        yp = _mm(zs, wpo_ref[...])
        yp_ref[...] = yp.astype(BF16)
        gc = _sigmoid(gt_ref[:, 0:D].astype(F32))
        gp = _sigmoid(gt_ref[:, D:2 * D].astype(F32))
        mg = (gc * yc + gp * yp).astype(BF16)
        mg_ref[...] = mg
        mo = _mm(mg, wo_ref[...])
        mo_ref[...] = mo
        r2 = lax.rsqrt(jnp.mean(mo * mo, axis=-1, keepdims=True) + RMS_EPS)
        h1_ref[...] = x_ref[...] + mo * r2 * g2_ref[...]

    vec = _full((1, D))
    act = lambda dt: SDS((t, D), dt)
    return _call(
        body, "fwd_mix", (t // tm,),
        [_tile(tm, D), _tile(tm, D), _prev_halo(tm), _tile(tm, D), _prev_halo(tm), _tile(tm, 2 * D),
         _full((KW_PAD, D)), vec, vec, vec, _full((D, D)), _full((NG, GW, GW)), vec, _full((D, D)), _full((D, D)), vec],
        [_tile(tm, D)] * 10,
        [act(F32), act(BF16), act(BF16), act(F32), act(BF16), act(BF16), act(BF16), act(BF16), act(F32), act(F32)],
        [pltpu.VMEM((tm + HALO, D), F32), pltpu.VMEM((tm + HALO, GW), F32), _taps_scratch(tm)],
        [x, glu, glu, p, p, gt, dwk, dwb, lng, lnb, w_co, pool_w, ps, w_po, w_o, g2], exchanges)


def mlp_fwd_bwd(h1, tgt, g3, g4, w1, w2, tm, exchanges=()):
    t = h1.shape[0]
    fc = CFF

    def body(h1_ref, tgt_ref, g3_ref, g4_ref, w1_ref, w2_ref,
             v_ref, a2_ref, df2_ref, df1_ref, dh1_ref, vec_ref, f1_ref):
        i = pl.program_id(0)

        @pl.when(i == 0)
        def _():
            vec_ref[...] = jnp.zeros_like(vec_ref)

        h1v = h1_ref[...]
        r3 = lax.rsqrt(jnp.mean(h1v * h1v, axis=-1, keepdims=True) + RMS_EPS)
        n3 = h1v * r3
        v = (n3 * g3_ref[...]).astype(BF16)
        v_ref[...] = v
        f2 = jnp.zeros((tm, D), F32)
        for c in range(FF // fc):
            cols = pl.ds(c * fc, fc)
            f1 = jnp.maximum(_mm(v, w1_ref[c]), 0.0)
            f1_ref[:, cols] = f1
            a2 = (f1 * f1).astype(BF16)
            a2_ref[:, cols] = a2
            f2 = f2 + _mm(a2, w2_ref[cols, :])
        r4 = lax.rsqrt(jnp.mean(f2 * f2, axis=-1, keepdims=True) + RMS_EPS)
        n4 = f2 * r4
        err = h1v + n4 * g4_ref[...] - tgt_ref[...]
        vec_ref[2:3, :] += _rowsum(err * err) * (0.5 / D)
        dh2 = err * (1.0 / D)
        vec_ref[1:2, :] += _rowsum(dh2 * n4)
        dn4 = dh2 * g4_ref[...]
        df2 = (r4 * (dn4 - n4 * jnp.mean(dn4 * n4, axis=-1, keepdims=True))).astype(BF16)
        df2_ref[...] = df2
        dv = jnp.zeros((tm, D), F32)
        for c in range(FF // fc):
            cols = pl.ds(c * fc, fc)
            da2 = _mm_nt(df2, w2_ref[cols, :])
            df1 = (da2 * (2.0 * f1_ref[:, cols])).astype(BF16)
            df1_ref[:, cols] = df1
            dv = dv + _mm_nt(df1, w1_ref[c])
        vec_ref[0:1, :] += _rowsum(dv * n3)
        dn3 = dv * g3_ref[...]
        dh1_ref[...] = dh2 + r3 * (dn3 - n3 * jnp.mean(dn3 * n3, axis=-1, keepdims=True))

    vec = _full((1, D))
    return _call(
        body, "mlp_fwd_bwd", (t // tm,),
        [_tile(tm, D), _tile(tm, D), vec, vec, _full((NCHIP, D, CFF)), _full((FF, D))],
        [_tile(tm, D), _tile(tm, FF), _tile(tm, D), _tile(tm, FF), _tile(tm, D), _full((8, D))],
        [SDS((t, D), BF16), SDS((t, FF), BF16), SDS((t, D), BF16), SDS((t, FF), BF16), SDS((t, D), F32),
         SDS((8, D), F32)],
        [pltpu.VMEM((tm, FF), F32)], [h1, tgt, g3, g4, w1, w2], exchanges)


def bwd_mix(dh1, mo, cv, zl, yc, yp, gt, lng, lnb, ps, g2, w_co, pool_w, w_po, w_o, tm, exchanges=()):
    t = dh1.shape[0]

    def body(dh1_ref, mo_ref, cv_ref, zl_ref, yc_ref, yp_ref, gt_ref, lg_ref, lb_ref, ps_ref, g2_ref,
             wco_ref, pw_ref, wpo_ref, wo_ref,
             dmo_ref, dgt_ref, dyc_ref, dyp_ref, dcv_ref, dzl_ref, dz_ref, vec_ref):
        i = pl.program_id(0)

        @pl.when(i == 0)
        def _():
            vec_ref[...] = jnp.zeros_like(vec_ref)

        dh1v = dh1_ref[...]
        mo = mo_ref[...]
        r2 = lax.rsqrt(jnp.mean(mo * mo, axis=-1, keepdims=True) + RMS_EPS)
        n2 = mo * r2
        vec_ref[0:1, :] += _rowsum(dh1v * n2)
        dn2 = dh1v * g2_ref[...]
        dmo = (r2 * (dn2 - n2 * jnp.mean(dn2 * n2, axis=-1, keepdims=True))).astype(BF16)
        dmo_ref[...] = dmo
        dmg = _mm_nt(dmo, wo_ref[...])
        gc = _sigmoid(gt_ref[:, 0:D].astype(F32))
        gp = _sigmoid(gt_ref[:, D:2 * D].astype(F32))
        dgt_ref[:, 0:D] = (dmg * yc_ref[...].astype(F32) * gc * (1.0 - gc)).astype(BF16)
        dgt_ref[:, D:2 * D] = (dmg * yp_ref[...].astype(F32) * gp * (1.0 - gp)).astype(BF16)
        dyc = (dmg * gc).astype(BF16)
        dyp = (dmg * gp).astype(BF16)
        dyc_ref[...] = dyc
        dyp_ref[...] = dyp
        dsw = _mm_nt(dyc, wco_ref[...])
        cv = cv_ref[...]
        mu = jnp.mean(cv, axis=-1, keepdims=True)
        cen = cv - mu
        rstd = lax.rsqrt(jnp.mean(cen * cen, axis=-1, keepdims=True) + LN_EPS)
        y = cen * rstd
        ln = y * lg_ref[...] + lb_ref[...]
        sg = _sigmoid(ln)
        dln = dsw * (sg * (1.0 + ln * (1.0 - sg)))
        vec_ref[1:2, :] += _rowsum(dln * y)
        vec_ref[2:3, :] += _rowsum(dln)
        dy = dln * lg_ref[...]
        dcv = rstd * (dy - jnp.mean(dy, axis=-1, keepdims=True) - y * jnp.mean(dy * y, axis=-1, keepdims=True))
        dcv_ref[...] = dcv
        vec_ref[3:4, :] += _rowsum(dcv)
        dzs = _mm_nt(dyp, wpo_ref[...])
        vec_ref[4:5, :] += _rowsum(dzs * zl_ref[...])
        dzl = (dzs * ps_ref[...]).astype(BF16)
        dzl_ref[...] = dzl
        for g in range(NG):
            cols = pl.ds(g * GW, GW)
            dz_ref[:, cols] = _mm_nt(dzl_ref[:, cols], pw_ref[g])

    vec = _full((1, D))
    act = lambda dt: SDS((t, D), dt)
    return _call(
        body, "bwd_mix", (t // tm,),
        [_tile(tm, D)] * 6 + [_tile(tm, 2 * D), vec, vec, vec, vec, _full((D, D)), _full((NG, GW, GW)), _full((D, D)),
                              _full((D, D))],
        [_tile(tm, D), _tile(tm, 2 * D)] + [_tile(tm, D)] * 5 + [_full((8, D))],
        [act(BF16), SDS((t, 2 * D), BF16), act(BF16), act(BF16), act(F32), act(BF16), act(F32), SDS((8, D), F32)],
        [], [dh1, mo, cv, zl, yc, yp, gt, lng, lnb, ps, g2, w_co, pool_w, w_po, w_o], exchanges)


def bwd_in(x, dh1, dcv, dz, glu, ag, dgt, g1, dwk, w_in, tm, exchanges=()):
    t = x.shape[0]
    nt = t // tm

    def body(x_ref, dh1_ref, dcv_ref, dcvh_ref, dz_ref, dzh_ref, glu_ref, gluh_ref, ag_ref, dgt_ref, g1_ref,
             k_ref, w_ref, dx_ref, dproj_ref, vec_ref, dk_ref, ext_ref, tmp_ref, win_ref, sh_ref):
        i = pl.program_id(0)

        @pl.when(i == 0)
        def _():
            vec_ref[...] = jnp.zeros_like(vec_ref)
            dk_ref[...] = jnp.zeros_like(dk_ref)

        first = (i > 0).astype(F32)
        last = (i < nt - 1).astype(F32)
        ext_ref[0:HALO, :] = gluh_ref[...] * first
        ext_ref[HALO:HALO + tm, :] = glu_ref[...]
        rc, cw = 64, LANES

        def dk_chunk(cc, carry):
            cols = pl.ds(pl.multiple_of(cc * cw, cw), cw)
            for q in range(SUBLANES):
                taps = range(q, KW, SUBLANES)
                n = tm + SUBLANES * (len(taps) - 1)
                sh_ref[q, 0:n, :] = ext_ref[pl.ds(HALO - (KW - 1) + q, n), cols]
                accs = [jnp.zeros((SUBLANES, cw), F32) for _ in taps]
                for r in range(tm // rc):
                    dchunk = dcv_ref[pl.ds(r * rc, rc), cols]
                    for a in range(len(taps)):
                        prod = dchunk * sh_ref[q, pl.ds(r * rc + SUBLANES * a, rc), :]
                        accs[a] = accs[a] + jnp.sum(prod.reshape(rc // SUBLANES, SUBLANES, cw), axis=0)
                for a, j in enumerate(taps):
                    dk_ref[j:j + 1, cols] += _rowsum(accs[a])
            return carry

        lax.fori_loop(0, D // cw, dk_chunk, 0)
        ext_ref[0:tm, :] = dcv_ref[...]
        ext_ref[tm:tm + HALO, :] = dcvh_ref[...] * last
        _taps(ext_ref, k_ref, tmp_ref, sh_ref, tm, 0, True)
        dglu = tmp_ref[...]
        a = ag_ref[:, 0:D].astype(F32)
        sg = _sigmoid(ag_ref[:, D:2 * D].astype(F32))
        dproj_ref[:, 0:D] = (dglu * sg).astype(BF16)
        dproj_ref[:, D:2 * D] = (dglu * a * sg * (1.0 - sg)).astype(BF16)
        for g, w in enumerate(POOL_WINDOWS):
            cols = pl.ds(g * GW, GW)
            pos = i * tm + lax.broadcasted_iota(jnp.int32, (tm + HALO, 1), 0) + 1
            inv = 1.0 / jnp.minimum(pos, w).astype(F32)
            ext_ref[0:tm, cols] = dz_ref[:, cols] * inv[0:tm]
            ext_ref[tm:tm + HALO, cols] = dzh_ref[:, cols] * inv[tm:tm + HALO] * last
            acc = _window_sum(ext_ref, win_ref, cols, tm, w, False)
            dproj_ref[:, pl.ds(2 * D + g * GW, GW)] = (acc - dz_ref[:, cols]).astype(BF16)
        dproj_ref[:, 3 * D:5 * D] = dgt_ref[...]
        du = jnp.zeros((tm, D), F32)
        for k in range(NCHIP):
            du = du + _mm_nt(dproj_ref[:, k * CIN:(k + 1) * CIN], w_ref[k])
        xf = x_ref[...]
        r1 = lax.rsqrt(jnp.mean(xf * xf, axis=-1, keepdims=True) + RMS_EPS)
        n1 = xf * r1
        vec_ref[0:1, :] += _rowsum(du * n1)
        dn1 = du * g1_ref[...]
        dx_ref[...] = dh1_ref[...] + r1 * (dn1 - n1 * jnp.mean(dn1 * n1, axis=-1, keepdims=True))

    return _call(
        body, "bwd_in", (nt,),
        [_tile(tm, D), _tile(tm, D), _tile(tm, D), _next_halo(tm, nt), _tile(tm, D), _next_halo(tm, nt),
         _tile(tm, D), _prev_halo(tm), _tile(tm, 2 * D), _tile(tm, 2 * D), _full((1, D)),
         _full((KW_PAD, D)), _full((NCHIP, D, CIN))],
        [_tile(tm, D), _tile(tm, NPROJ * D), _full((8, D)), _full((KW_PAD, D))],
        [SDS((t, D), F32), SDS((t, NPROJ * D), BF16), SDS((8, D), F32), SDS((KW_PAD, D), F32)],
        [pltpu.VMEM((tm + HALO, D), F32), pltpu.VMEM((tm, D), F32), pltpu.VMEM((tm + HALO, GW), F32),
         _taps_scratch(tm)],
        [x, dh1, dcv, dcv, dz, dz, glu, glu, ag, dgt, g1, dwk, w_in], exchanges)


def wgrad(a, b, name, bm=1024, bn=1024, bt=2048, chip_major=False, exchanges=()):
    t, m = a.shape
    n = b.shape[1]
    bm, bn, bt = min(bm, m), min(bn, n), min(bt, t)
    assert m % bm == 0 and n % bn == 0 and t % bt == 0, (a.shape, b.shape, bm, bn, bt)

    def body(a_ref, b_ref, o_ref):
        k = pl.program_id(2)

        @pl.when(k == 0)
        def _():
            o_ref[...] = jnp.zeros_like(o_ref)

        o_ref[...] += _mm_tn(a_ref[...], b_ref[...])

    if chip_major:
        out_spec, out_shape = pl.BlockSpec((None, bm, bn), lambda i, j, k: (j, i, 0)), SDS((n // bn, m, bn), F32)
    else:
        out_spec, out_shape = pl.BlockSpec((bm, bn), lambda i, j, k: (i, j)), SDS((m, n), F32)
    outs, xouts = _call(
        body, name, (m // bm, n // bn, t // bt),
        [pl.BlockSpec((bt, bm), lambda i, j, k: (k, i)), pl.BlockSpec((bt, bn), lambda i, j, k: (k, j))],
        [out_spec], [out_shape], [], [a, b], exchanges)
    return outs[0], xouts


def wgrad_pool(z, dzl, bt=1024):
    t = z.shape[0]
    bt = min(bt, t)
    assert t % bt == 0

    def body(a_ref, b_ref, o_ref):
        k = pl.program_id(1)

        @pl.when(k == 0)
        def _():
            o_ref[...] = jnp.zeros_like(o_ref)

        o_ref[0] += _mm_tn(a_ref[...], b_ref[...])

    return _call(
        body, "wgrad_pool", (NG, t // bt),
        [pl.BlockSpec((bt, GW), lambda g, k: (k, g)), pl.BlockSpec((bt, GW), lambda g, k: (k, g))],
        [pl.BlockSpec((1, GW, GW), lambda g, k: (g, 0, 0))], [SDS((NG, GW, GW), F32)], [], [z, dzl])[0][0]


def cast_shards(name, names, shards, exchanges=()):
    n = len(shards)

    def body(*refs):
        srcs, dsts, bufs, sems = refs[:n], refs[n:2 * n], refs[2 * n:3 * n], refs[3 * n]
        me = 2 * lax.axis_index("x") + lax.axis_index("y")
        copies = []
        for w, mat in enumerate(names):
            bufs[w][...] = srcs[w][...].astype(BF16)
            for h in range(2):
                cp = pltpu.make_async_copy(bufs[w].at[_shard_half(mat, h)], dsts[w].at[_window(mat, me, h)], sems.at[w, h])
                cp.start()
                copies.append(cp)
        for cp in copies:
            cp.wait()

    return _call(body, name, (), [VMEM_SPEC] * n, [ANY] * n, [SDS(GEOM[mat][0], BF16) for mat in names],
                 [pltpu.VMEM(s.shape, BF16) for s in shards] + [pltpu.SemaphoreType.DMA((n, 2))], list(shards), exchanges)


VEC_ROWS = 24
ROW = dict(mlp_pre_g=0, mlp_post_g=1, loss=2, mix_post_g=8, conv_ln_g=9, conv_ln_b=10, dw_bias=11, pool_scale=12,
           mix_pre_g=16)


NDEV = 8


class GatherSmall:
    has_mid = False
    aliases = ()

    def __init__(self, part):
        self.ins = [part]
        self.out_shapes = [SDS((NDEV, *part.shape), F32)]
        self.sems = [pltpu.SemaphoreType.DMA((NDEV,)), pltpu.SemaphoreType.DMA((NDEV,))]

    def _copies(self, ins, outs, sems):
        send_sems, recv_sems = sems
        x, y, c = lax.axis_index("x"), lax.axis_index("y"), lax.axis_index("c")
        me = 4 * x + 2 * y + c
        own = pltpu.make_async_copy(ins[0], outs[0].at[me], send_sems.at[0])
        sends = [_remote(ins[0], outs[0].at[me], send_sems.at[m], recv_sems.at[m],
                         (x ^ (m >> 2), y ^ ((m >> 1) & 1), c ^ (m & 1))) for m in range(1, NDEV)]
        recvs = [_remote(outs[0].at[me ^ m], outs[0].at[me ^ m], send_sems.at[m], recv_sems.at[m], (x, y, c))
                 for m in range(1, NDEV)]
        return own, sends, recvs

    def start(self, ins, outs, sems):
        own, sends, _ = self._copies(ins, outs, sems)
        for cp in sends + [own]:
            cp.start()

    def finish(self, ins, outs, sems):
        own, sends, recvs = self._copies(ins, outs, sems)
        for cp in recvs:
            cp.wait_recv()
        for cp in sends:
            cp.wait_send()
        own.wait()


def sum_small(parts, exchanges=()):
    def body(parts_ref, sum_ref, loss_ref):
        total = parts_ref[0]
        for d in range(1, NDEV):
            total = total + parts_ref[d]
        sum_ref[...] = total
        r = ROW["loss"]
        loss_ref[...] = jnp.zeros_like(loss_ref) + jnp.sum(total[r:r + 1, :])

    return _call(body, "sum_small", (), [VMEM_SPEC], [VMEM_SPEC, VMEM_SPEC],
                 [SDS(parts.shape[1:], F32), SDS((8, 128), F32)], [], [parts], exchanges)


def pair_add(tag, names, grads, from_sibling, exchanges=()):
    n = len(names)
    in_specs, wire_specs, own_specs, wire_shapes, own_shapes = [], [], [], [], []
    for name in names:
        _, gblk, idx = GEOM[name]
        blk = _half_shape(name)
        zeros = (0,) * len(blk)
        in_specs.append(pl.BlockSpec(gblk, lambda k, idx=idx: idx(k, lax.axis_index("c"))))
        wire_specs.append(pl.BlockSpec((1, *blk), lambda k, zeros=zeros: (k, *zeros)))
        own_specs.append(pl.BlockSpec(blk, lambda k, zeros=zeros: zeros))
        wire_shapes.append(SDS((NCHIP, *blk), BF16))
        own_shapes.append(SDS(blk, F32))

    def body(*refs):
        g, s, wire, own = refs[:n], refs[n:2 * n], refs[2 * n:3 * n], refs[3 * n:]
        mine = pl.program_id(0) == 2 * lax.axis_index("x") + lax.axis_index("y")
        for w in range(n):
            total = g[w][...] + s[w][0]
            wire[w][0] = total.astype(BF16)

            @pl.when(mine)
            def _(w=w, total=total):
                own[w][...] = total

    outs, xouts = _call(body, "pair_add_" + tag, (NCHIP,), in_specs + wire_specs, wire_specs + own_specs,
                        wire_shapes + own_shapes, [], list(grads) + list(from_sibling), exchanges)
    return outs[:n], outs[n:], xouts


SUM_STEPS = 4


def sum_partials(tag, names, owns, from_chips, exchanges=()):
    n = len(names)
    own_specs, part_specs, out_specs, out_shapes, part_args, counts = [], [], [], [], [], []
    for name, parts in zip(names, from_chips):
        half = _half_shape(name)
        blk = half[:-2] + (half[-2] // SUM_STEPS, half[-1])
        lead = (0,) * (len(half) - 2)
        own_specs.append(pl.BlockSpec(blk, lambda i, lead=lead: (*lead, i, 0)))
        for p in parts:
            part_specs.append(pl.BlockSpec((p.shape[0], *blk), lambda i, lead=lead: (0, *lead, i, 0)))
            part_args.append(p)
        counts.append(len(parts))
        out_specs.append(pl.BlockSpec((2, *blk), lambda i, lead=lead: (0, *lead, i, 0)))
        out_shapes.append(SDS((2, *half), F32))

    def body(*refs):
        own, parts, out = refs[:n], list(refs[n:n + len(part_args)]), refs[n + len(part_args):]
        c = lax.axis_index("c")
        for w in range(n):
            total = own[w][...]
            for p in [parts.pop(0) for _ in range(counts[w])]:
                for j in range(p.shape[0]):
                    total = total + p[j].astype(F32)
            out[w][c] = total

    return _call(body, "sum_partials_" + tag, (SUM_STEPS,), own_specs + part_specs, out_specs, out_shapes, [],
                 list(owns) + part_args, exchanges)


class SwapHalves:
    has_mid = False

    def __init__(self, halves):
        self.ins = list(halves)
        self.out_shapes = [SDS(h.shape, h.dtype) for h in halves]
        self.aliases = [(i, i) for i in range(len(halves))]
        self.sems = [pltpu.SemaphoreType.DMA((len(halves),)), pltpu.SemaphoreType.DMA((len(halves),))]

    def start(self, ins, outs, sems):
        send_sems, recv_sems = sems
        x, y, c = lax.axis_index("x"), lax.axis_index("y"), lax.axis_index("c")
        for w in range(len(self.ins)):
            _remote(ins[w].at[c], outs[w].at[c], send_sems.at[w], recv_sems.at[w], (x, y, 1 - c)).start()

    def finish(self, ins, outs, sems):
        send_sems, recv_sems = sems
        x, y, c = lax.axis_index("x"), lax.axis_index("y"), lax.axis_index("c")
        for w in range(len(self.ins)):
            _remote(ins[w].at[c], outs[w].at[c], send_sems.at[w], recv_sems.at[w], (x, y, 1 - c)).wait_send()
            _remote(ins[w].at[1 - c], outs[w].at[1 - c], send_sems.at[w], recv_sems.at[w], (x, y, 1 - c)).wait_recv()


def adamw(tag, ws, gs, ms, vs, steps, exchanges=()):
    n = len(ws)
    specs = [pl.BlockSpec((a.shape[0] // steps, a.shape[1]), lambda i: (i, 0)) for a in ws]
    assert all(a.shape[0] % (steps * SUBLANES) == 0 for a in ws), [a.shape for a in ws]

    def body(*refs):
        w_, g_, m_, v_ = refs[:n], refs[n:2 * n], refs[2 * n:3 * n], refs[3 * n:4 * n]
        d_, nm_, nv_ = refs[4 * n:5 * n], refs[5 * n:6 * n], refs[6 * n:]
        for i in range(n):
            gv = g_[i][...]
            mn = B1 * m_[i][...] + (1.0 - B1) * gv
            vn = B2 * v_[i][...] + (1.0 - B2) * (gv * gv)
            m_hat = mn / (1.0 - B1 ** STEP)
            v_hat = vn / (1.0 - B2 ** STEP)
            d_[i][...] = -LR * (m_hat / (jnp.sqrt(v_hat) + ADAM_EPS) + WD * w_[i][...])
            nm_[i][...] = mn
            nv_[i][...] = vn

    outs, xouts = _call(body, "adamw_" + tag, (steps,), specs * 4, specs * 3, [SDS(a.shape, F32) for a in ws] * 3, [],
                        list(ws) + list(gs) + list(ms) + list(vs), exchanges)
    return list(zip(outs[:n], outs[n:2 * n], outs[2 * n:])), xouts


VECS = ("mix_pre_g", "dw_bias", "conv_ln_g", "conv_ln_b", "pool_scale", "mix_post_g", "mlp_pre_g", "mlp_post_g")
WEIGHTS = ("mix_pre_g", "w_in", "dw_kernel", "dw_bias", "conv_ln_g", "conv_ln_b", "w_conv_out", "pool_w", "pool_scale",
           "w_pool_out", "w_o", "mix_post_g", "mlp_pre_g", "w_ff1", "w_ff2", "mlp_post_g")
MIX_MATS = ("w_conv_out", "pool_w", "w_pool_out", "w_o")
FF_MATS = ("w_ff1", "w_ff2")


def kernel(x, mix_pre_g, w_in, dw_kernel, dw_bias, conv_ln_g, conv_ln_b, w_conv_out, pool_w, pool_scale, w_pool_out, w_o, mix_post_g, mlp_pre_g, w_ff1, w_ff2, mlp_post_g, loss_target, m_mix_pre_g, m_w_in, m_dw_kernel, m_dw_bias, m_conv_ln_g, m_conv_ln_b, m_w_conv_out, m_pool_w, m_pool_scale, m_w_pool_out, m_w_o, m_mix_post_g, m_mlp_pre_g, m_w_ff1, m_w_ff2, m_mlp_post_g, v_mix_pre_g, v_w_in, v_dw_kernel, v_dw_bias, v_conv_ln_g, v_conv_ln_b, v_w_conv_out, v_pool_w, v_pool_scale, v_w_pool_out, v_w_o, v_mix_post_g, v_mlp_pre_g, v_w_ff1, v_w_ff2, v_mlp_post_g):
    w = dict(mix_pre_g=mix_pre_g, w_in=w_in, dw_kernel=dw_kernel, dw_bias=dw_bias, conv_ln_g=conv_ln_g,
             conv_ln_b=conv_ln_b, w_conv_out=w_conv_out, pool_w=pool_w, pool_scale=pool_scale, w_pool_out=w_pool_out,
             w_o=w_o, mix_post_g=mix_post_g, mlp_pre_g=mlp_pre_g, w_ff1=w_ff1, w_ff2=w_ff2, mlp_post_g=mlp_post_g)
    m = dict(mix_pre_g=m_mix_pre_g, w_in=m_w_in, dw_kernel=m_dw_kernel, dw_bias=m_dw_bias, conv_ln_g=m_conv_ln_g,
             conv_ln_b=m_conv_ln_b, w_conv_out=m_w_conv_out, pool_w=m_pool_w, pool_scale=m_pool_scale,
             w_pool_out=m_w_pool_out, w_o=m_w_o, mix_post_g=m_mix_post_g, mlp_pre_g=m_mlp_pre_g, w_ff1=m_w_ff1,
             w_ff2=m_w_ff2, mlp_post_g=m_mlp_post_g)
    v = dict(mix_pre_g=v_mix_pre_g, w_in=v_w_in, dw_kernel=v_dw_kernel, dw_bias=v_dw_bias, conv_ln_g=v_conv_ln_g,
             conv_ln_b=v_conv_ln_b, w_conv_out=v_w_conv_out, pool_w=v_pool_w, pool_scale=v_pool_scale,
             w_pool_out=v_w_pool_out, w_o=v_w_o, mix_post_g=v_mix_post_g, mlp_pre_g=v_mlp_pre_g, w_ff1=v_w_ff1,
             w_ff2=v_w_ff2, mlp_post_g=v_mlp_post_g)
    chip = 2 * lax.axis_index("x") + lax.axis_index("y")
    xs, tgt = x[0], loss_target[0]
    vecs = {name: w[name].reshape(1, D) for name in VECS}

    taps = lax.dynamic_update_slice(jnp.zeros((KW_PAD, D), F32), dw_kernel, (0, chip * DSH))
    mine, full = {}, {}
    (mine["w_in"],), _ = cast_shards("cast_w_in", ("w_in",), [w["w_in"]])
    rest = MATS[1:]
    cast, ((full["w_in"], dwk),) = cast_shards(
        "cast_rest", rest, [w[name] for name in rest], [GatherWeights(("w_in",), [mine["w_in"]], taps)])
    mine.update(zip(rest, cast))
    (u, glu, ag, p, gt), (got,) = fwd_in(
        xs, vecs["mix_pre_g"], full["w_in"], TM_IN, [GatherWeights(MIX_MATS, [mine[n] for n in MIX_MATS])])
    full.update(zip(MIX_MATS, got))
    (cv, sw, z, zl, zs, yc, yp, mg, mo, h1), (got,) = fwd_mix(
        xs, glu, p, gt, dwk, vecs["dw_bias"], vecs["conv_ln_g"], vecs["conv_ln_b"], full["w_conv_out"], full["pool_w"],
        vecs["pool_scale"], full["w_pool_out"], full["w_o"], vecs["mix_post_g"], TM,
        [GatherWeights(FF_MATS, [mine[n] for n in FF_MATS])])
    full.update(zip(FF_MATS, got))

    (v_, a2, df2, df1, dh1, vec_mlp), _ = mlp_fwd_bwd(
        h1, tgt, vecs["mlp_pre_g"], vecs["mlp_post_g"], full["w_ff1"], full["w_ff2"], TM)
    grads, g, delta, new_m, new_v = {}, {}, {}, {}, {}
    rest_mats = FF_MATS + MIX_MATS
    grads["w_ff1"], _ = wgrad(v_, df1, "wgrad_ff1", bn=CFF, chip_major=True)
    grads["w_ff2"], _ = wgrad(a2, df2, "wgrad_ff2")
    (dmo, dgt, dyc, dyp, dcv, dzl, dz, vec_mix), (from_sibling,) = bwd_mix(
        dh1, mo, cv, zl, yc, yp, gt, vecs["conv_ln_g"], vecs["conv_ln_b"], vecs["pool_scale"], vecs["mix_post_g"],
        full["w_conv_out"], full["pool_w"], full["w_pool_out"], full["w_o"], TM,
        [ExchangePair(FF_MATS, [grads[n] for n in FF_MATS])])
    grads["w_conv_out"], _ = wgrad(sw, dyc, "wgrad_conv_out", bt=1024)
    grads["pool_w"] = wgrad_pool(z, dzl)
    grads["w_pool_out"], _ = wgrad(zs, dyp, "wgrad_pool_out", bt=1024)
    grads["w_o"], _ = wgrad(mg, dmo, "wgrad_o", bt=1024)
    wire_ff, own_ff, (from_sibling,) = pair_add(
        "ff", FF_MATS, [grads[n] for n in FF_MATS], from_sibling,
        exchanges=[ExchangePair(MIX_MATS, [grads[n] for n in MIX_MATS])])
    wire_mix, own_mix, _ = pair_add("mix", MIX_MATS, [grads[n] for n in MIX_MATS], from_sibling)
    (dx, dproj, vec_in, dk), (chips_ff, chips_mix) = bwd_in(
        xs, dh1, dcv, dz, glu, ag, dgt, vecs["mix_pre_g"], dwk, full["w_in"], TM,
        [ExchangeChips(FF_MATS, wire_ff), ExchangeChips(MIX_MATS, wire_mix)])
    small_part = jnp.concatenate([vec_mlp, vec_mix, vec_in, dk], axis=0)
    grads["w_in"], ((small_parts,),) = wgrad(
        u, dproj, "wgrad_in", bn=CIN, chip_major=True, exchanges=[GatherSmall(small_part)])
    halves_rest, (from_sibling,) = sum_partials(
        "rest", rest_mats, list(own_ff) + list(own_mix), [[a] for a in list(chips_ff) + list(chips_mix)],
        [ExchangePair(("w_in",), [grads["w_in"]])])
    wire_in, own_in, (reduced_rest,) = pair_add(
        "in", ("w_in",), [grads["w_in"]], from_sibling, exchanges=[SwapHalves(halves_rest)])
    for n, red in zip(rest_mats, reduced_rest):
        g[n] = red.reshape(w[n].shape)
    (small, loss8), _ = sum_small(small_parts)
    loss = loss8[0, 0]
    for name in VECS:
        g[name] = small[ROW[name]]
    g["dw_kernel"] = lax.dynamic_slice(small[VEC_ROWS:VEC_ROWS + KW_PAD], (0, chip * DSH), (KW_PAD, DSH))

    def update(tag, names, steps, exchanges=()):
        two_d = lambda a: a.reshape(-1, a.shape[-1])
        res, xouts = adamw(tag, [two_d(w[n]) for n in names], [two_d(g[n]) for n in names],
                           [two_d(m[n]) for n in names], [two_d(v[n]) for n in names], steps, exchanges)
        for n, (d_, m_, v_new) in zip(names, res):
            delta[n], new_m[n], new_v[n] = d_.reshape(w[n].shape), m_.reshape(w[n].shape), v_new.reshape(w[n].shape)
        return xouts

    (chips_in,) = update("rest", rest_mats, 8, [ExchangeChips(("w_in",), wire_in)])
    halves_in, _ = sum_partials("in", ("w_in",), own_in, [chips_in])
    (reduced_in,) = exchange("swap_w_in", SwapHalves(halves_in))
    g["w_in"] = reduced_in.reshape(w["w_in"].shape)
    update("in", ("w_in",), 8)
    stack = lambda d: jnp.concatenate([d[name].reshape(1, D) for name in VECS], axis=0)
    (res,), _ = adamw("vectors", [stack(w)], [stack(g)], [stack(m)], [stack(v)], 1)
    for i, name in enumerate(VECS):
        delta[name], new_m[name], new_v[name] = [r[i] for r in res]
    padk = lambda a: jnp.pad(a, ((0, KW_PAD - KW), (0, 0)))
    (res,), _ = adamw("dw_kernel", [padk(w["dw_kernel"])], [g["dw_kernel"]], [padk(m["dw_kernel"])],
                      [padk(v["dw_kernel"])], 1)
    delta["dw_kernel"], new_m["dw_kernel"], new_v["dw_kernel"] = [r[:KW] for r in res]
    g["dw_kernel"] = g["dw_kernel"][:KW]

    return (loss, dx[None], *[g[n] for n in WEIGHTS], *[delta[n] for n in WEIGHTS], *[new_m[n] for n in WEIGHTS],
            *[new_v[n] for n in WEIGHTS])
```

```python
import math

import jax
import jax.numpy as jnp
from jax import lax
from jax.experimental import pallas as pl
from jax.experimental.pallas import tpu as pltpu

F32 = jnp.float32
BF16 = jnp.bfloat16

D = 1024
FF = 4096
NPROJ = 5
KW = 31
KW_PAD = 32
SUBLANES = 8
LANES = 128
HALO = 32
POOL_WINDOWS = (2, 4, 8, 16)
NG = 4
GW = D // NG
RMS_EPS = 1e-6
LN_EPS = 1e-5
LR, B1, B2, ADAM_EPS, WD, STEP = 0.001, 0.9, 0.999, 1e-08, 0.01, 10
NCHIP = 4
VMEM_LIMIT = 60 * 1024 * 1024
MESH = pl.DeviceIdType.MESH
TM = 256
TM_IN = 512

ANY = pl.BlockSpec(memory_space=pl.ANY)
VMEM_SPEC = pl.BlockSpec(memory_space=pltpu.VMEM)
SDS = jax.ShapeDtypeStruct


def _cp(**kw):
    return pltpu.CompilerParams(vmem_limit_bytes=VMEM_LIMIT, **kw)


def _mm(a, b):
    return jnp.dot(a, b, preferred_element_type=F32)


def _mm_nt(a, b):
    return lax.dot_general(a, b, (((1,), (1,)), ((), ())), preferred_element_type=F32)


def _mm_tn(a, b):
    return lax.dot_general(a, b, (((0,), (0,)), ((), ())), preferred_element_type=F32)


def _sigmoid(x):
    return 1.0 / (1.0 + jnp.exp(-x))


def _rowsum(x):
    return jnp.sum(x, axis=0, keepdims=True)


def _full(shape):
    return pl.BlockSpec(shape, lambda i: (0,) * len(shape))


def _tile(tm, cols):
    return pl.BlockSpec((tm, cols), lambda i: (i, 0))


def _prev_halo(tm):
    return pl.BlockSpec((HALO, D), lambda i: (jnp.maximum(i * (tm // HALO) - 1, 0), 0))


def _next_halo(tm, nt):
    return pl.BlockSpec((HALO, D), lambda i: (jnp.minimum((i + 1) * (tm // HALO), nt * (tm // HALO) - 1), 0))


MATS = ("w_in", "w_conv_out", "pool_w", "w_pool_out", "w_o", "w_ff1", "w_ff2")
CIN = NPROJ * D // NCHIP
CFF = FF // NCHIP
_ROWS = lambda k, h: (2 * k + h, 0)
_CHIP_MAJOR = lambda k, h: (k, h, 0)
GEOM = dict(
    w_in=((NCHIP, D, CIN), (None, D // 2, CIN), _CHIP_MAJOR),
    w_conv_out=((D, D), (D // (2 * NCHIP), D), _ROWS),
    pool_w=((NG, GW, GW), (NG // 2, GW // NCHIP, GW), lambda k, h: (h, k, 0)),
    w_pool_out=((D, D), (D // (2 * NCHIP), D), _ROWS),
    w_o=((D, D), (D // (2 * NCHIP), D), _ROWS),
    w_ff1=((NCHIP, D, CFF), (None, D // 2, CFF), _CHIP_MAJOR),
    w_ff2=((FF, D), (FF // (2 * NCHIP), D), _ROWS),
)
DSH = D // NCHIP


def _half_shape(name):
    return tuple(b for b in GEOM[name][1] if b is not None)


def _window(name, k, h):
    _, blk, idx = GEOM[name]
    return tuple(i if b is None else pl.ds(i * b, b) for i, b in zip(idx(k, h), blk))


def _shard_half(name, h):
    n0 = _half_shape(name)[0]
    return (pl.ds(h * n0, n0),) + (slice(None),) * (len(_half_shape(name)) - 1)


def _place():
    x, y, c = lax.axis_index("x"), lax.axis_index("y"), lax.axis_index("c")
    chips = [(1 - x, y), (x, 1 - y), (1 - x, 1 - y)]
    return x, y, c, 2 * x + y, chips, [2 * px + py for px, py in chips]


def _remote(src, dst, send_sem, recv_sem, to):
    return pltpu.make_async_remote_copy(src_ref=src, dst_ref=dst, send_sem=send_sem, recv_sem=recv_sem,
                                        device_id=to, device_id_type=MESH)


class GatherWeights:
    has_mid = True

    def __init__(self, names, fulls, taps=None):
        self.names = names
        self.ins = list(fulls) + ([taps] if taps is not None else [])
        self.has_taps = taps is not None
        self.out_shapes = [SDS(a.shape, a.dtype) for a in self.ins]
        self.aliases = [(i, i) for i in range(len(self.ins))]
        n = len(self.ins)
        self.sems = [pltpu.SemaphoreType.DMA((n, 6)), pltpu.SemaphoreType.DMA((n, 6))]

    def _copies(self, ins, outs, sems):
        send_sems, recv_sems = sems
        x, y, c, me, chips, chip_ids = _place()
        sibling = (x, y, 1 - c)
        ici, ici_recv, d2d, d2d_recv = [], [], [], []
        for w, name in enumerate(self.names):
            for j, chip in enumerate(chips):
                ici.append(_remote(ins[w].at[_window(name, me, c)], outs[w].at[_window(name, me, c)],
                                   send_sems.at[w, j], recv_sems.at[w, j], (*chip, c)))
                got = outs[w].at[_window(name, chip_ids[j], c)]
                ici_recv.append(_remote(got, got, send_sems.at[w, j], recv_sems.at[w, j], sibling))
                d2d.append(_remote(got, got, send_sems.at[w, 3 + j], recv_sems.at[w, 3 + j], sibling))
                got = outs[w].at[_window(name, chip_ids[j], 1 - c)]
                d2d_recv.append(_remote(got, got, send_sems.at[w, 3 + j], recv_sems.at[w, 3 + j], sibling))
        if self.has_taps:
            w = len(self.names)
            for j, chip in enumerate(chips):
                ici.append(_remote(ins[w].at[:, pl.ds(me * DSH, DSH)], outs[w].at[:, pl.ds(me * DSH, DSH)],
                                   send_sems.at[w, j], recv_sems.at[w, j], (*chip, c)))
                got = outs[w].at[:, pl.ds(chip_ids[j] * DSH, DSH)]
                d2d_recv.append(_remote(got, got, send_sems.at[w, j], recv_sems.at[w, j], sibling))
        return ici, ici_recv, d2d, d2d_recv

    def start(self, ins, outs, sems):
        for cp in self._copies(ins, outs, sems)[0]:
            cp.start()

    def mid(self, ins, outs, sems):
        _, ici_recv, d2d, _ = self._copies(ins, outs, sems)
        for got, fwd in zip(ici_recv, d2d):
            got.wait_recv()
            fwd.start()

    def finish(self, ins, outs, sems):
        ici, _, d2d, d2d_recv = self._copies(ins, outs, sems)
        for cp in d2d_recv:
            cp.wait_recv()
        for cp in ici + d2d:
            cp.wait_send()


class ExchangePair:
    has_mid = False
    aliases = ()

    def __init__(self, names, grads):
        self.names, self.ins = names, list(grads)
        self.out_shapes = [SDS((NCHIP, *_half_shape(n)), F32) for n in names]
        self.sems = [pltpu.SemaphoreType.DMA((len(names),)), pltpu.SemaphoreType.DMA((len(names),))]

    def start(self, ins, outs, sems):
        send_sems, recv_sems = sems
        x, y, c, me, chips, chip_ids = _place()
        for w, name in enumerate(self.names):
            for k in range(NCHIP):
                _remote(ins[w].at[_window(name, k, 1 - c)], outs[w].at[k], send_sems.at[w], recv_sems.at[w],
                        (x, y, 1 - c)).start()

    def finish(self, ins, outs, sems):
        send_sems, recv_sems = sems
        x, y, c, me, chips, chip_ids = _place()
        for w in range(len(self.names)):
            _remote(outs[w], outs[w], send_sems.at[w], recv_sems.at[w], (x, y, 1 - c)).wait()


class ExchangeChips:
    has_mid = False
    aliases = ()

    def __init__(self, names, wires):
        self.names, self.ins = names, list(wires)
        self.out_shapes = [SDS((NCHIP - 1, *_half_shape(n)), BF16) for n in names]
        self.sems = [pltpu.SemaphoreType.DMA((len(names), NCHIP - 1)), pltpu.SemaphoreType.DMA((len(names), NCHIP - 1))]

    def _copies(self, ins, outs, sems):
        send_sems, recv_sems = sems
        x, y, c, me, chips, chip_ids = _place()
        return [_remote(ins[w].at[chip_ids[j]], outs[w].at[j], send_sems.at[w, j], recv_sems.at[w, j], (*chip, c))
                for w in range(len(self.names)) for j, chip in enumerate(chips)]

    def start(self, ins, outs, sems):
        for cp in self._copies(ins, outs, sems):
            cp.start()

    def finish(self, ins, outs, sems):
        for cp in self._copies(ins, outs, sems):
            cp.wait()


def _call(body, name, grid, in_specs, out_specs, out_shape, scratch, args, exchanges=(), after=()):
    n_in, n_out, n_scr = len(in_specs), len(out_specs), len(scratch)
    x_in = [a for e in exchanges for a in e.ins]
    x_out = [s for e in exchanges for s in e.out_shapes]
    x_sem = [s for e in exchanges for s in e.sems]
    nsteps = math.prod(grid)

    def wrapped(*refs):
        ins, rest = refs[:n_in], refs[n_in:]
        xin, rest = rest[:len(x_in)], rest[len(x_in) + len(after):]
        outs, rest = rest[:n_out], rest[n_out:]
        xout, rest = rest[:len(x_out)], rest[len(x_out):]
        scr, xsem = rest[:n_scr], rest[n_scr:]
        parts = []
        for e in exchanges:
            parts.append((xin[:len(e.ins)], xout[:len(e.out_shapes)], xsem[:len(e.sems)]))
            xin, xout, xsem = xin[len(e.ins):], xout[len(e.out_shapes):], xsem[len(e.sems):]
        if not grid:
            for e, p in zip(exchanges, parts):
                e.start(*p)
            body(*ins, *outs, *scr)
            for e, p in zip(exchanges, parts):
                if e.has_mid:
                    e.mid(*p)
            for e, p in zip(exchanges, parts):
                e.finish(*p)
            return
        step = 0
        for axis, extent in enumerate(grid):
            step = step * extent + pl.program_id(axis)
        if exchanges:
            @pl.when(step == 0)
            def _():
                for e, p in zip(exchanges, parts):
                    e.start(*p)

        body(*ins, *outs, *scr)
        if any(e.has_mid for e in exchanges):
            @pl.when(step == max(nsteps - 2, 0))
            def _():
                for e, p in zip(exchanges, parts):
                    if e.has_mid:
                        e.mid(*p)

        if exchanges:
            @pl.when(step == nsteps - 1)
            def _():
                for e, p in zip(exchanges, parts):
                    e.finish(*p)

    kw = dict(grid=grid, compiler_params=_cp(dimension_semantics=("arbitrary",) * len(grid))) if grid else dict(
        compiler_params=_cp())
    aliases, i0, o0 = {}, n_in, n_out
    for e in exchanges:
        aliases.update({i0 + i: o0 + o for i, o in e.aliases})
        i0, o0 = i0 + len(e.ins), o0 + len(e.out_shapes)
    res = pl.pallas_call(
        wrapped, name=name, in_specs=list(in_specs) + [ANY] * (len(x_in) + len(after)),
        out_specs=list(out_specs) + [ANY] * len(x_out),
        out_shape=list(out_shape) + x_out, scratch_shapes=list(scratch) + x_sem, input_output_aliases=aliases, **kw,
    )(*args, *x_in, *after)
    outs, rest = res[:n_out], res[n_out:]
    xouts = []
    for e in exchanges:
        xouts.append(rest[:len(e.out_shapes)])
        rest = rest[len(e.out_shapes):]
    return outs, xouts


def exchange(name, ex):
    return _call(lambda: None, name, (), [], [], [], [], [], [ex])[1][0]


def _taps_scratch(tm):
    return pltpu.VMEM((SUBLANES, tm + HALO, LANES), F32)


def _taps(src_ref, k_ref, dst_ref, sh_ref, tm, off0, reverse):
    rc, cw = 64, LANES

    def col_chunk(cc, carry):
        cols = pl.ds(pl.multiple_of(cc * cw, cw), cw)
        for q in range(SUBLANES):
            n = tm + SUBLANES * (len(range(q, KW, SUBLANES)) - 1)
            sh_ref[q, 0:n, :] = src_ref[pl.ds(off0 + q, n), cols]
        for r in range(tm // rc):
            acc = jnp.zeros((rc, cw), F32)
            for q in range(SUBLANES):
                for a, j in enumerate(range(q, KW, SUBLANES)):
                    kj = KW - 1 - j if reverse else j
                    acc = acc + k_ref[kj:kj + 1, cols] * sh_ref[q, pl.ds(r * rc + SUBLANES * a, rc), :]
            dst_ref[pl.ds(r * rc, rc), cols] = acc
        return carry

    lax.fori_loop(0, D // cw, col_chunk, 0)


def _proj_pieces(g):
    lo, hi, pieces = g * D, (g + 1) * D, []
    while lo < hi:
        k = lo // CIN
        b = min(hi - k * CIN, CIN)
        pieces.append((k, lo - k * CIN, b))
        lo = k * CIN + b
    return pieces


def fwd_in(x, g1, w_in, tm, exchanges=()):
    t = x.shape[0]

    def body(x_ref, g_ref, w_ref, u_ref, glu_ref, ag_ref, p_ref, gt_ref):
        xf = x_ref[...]
        r = lax.rsqrt(jnp.mean(xf * xf, axis=-1, keepdims=True) + RMS_EPS)
        u = (xf * r * g_ref[...]).astype(BF16)
        u_ref[...] = u
        proj = lambda g: jnp.concatenate([_mm(u, w_ref[k, :, lo:hi]) for k, lo, hi in _proj_pieces(g)], axis=1)
        a = proj(0)
        gate = proj(1)
        glu_ref[...] = a * _sigmoid(gate)
        ag_ref[:, 0:D] = a.astype(BF16)
        ag_ref[:, D:2 * D] = gate.astype(BF16)
        p_ref[...] = proj(2)
        gt_ref[:, 0:D] = proj(3).astype(BF16)
        gt_ref[:, D:2 * D] = proj(4).astype(BF16)

    return _call(
        body, "fwd_in", (t // tm,),
        [_tile(tm, D), _full((1, D)), _full((NCHIP, D, CIN))],
        [_tile(tm, D), _tile(tm, D), _tile(tm, 2 * D), _tile(tm, D), _tile(tm, 2 * D)],
        [SDS((t, D), BF16), SDS((t, D), F32), SDS((t, 2 * D), BF16), SDS((t, D), F32), SDS((t, 2 * D), BF16)],
        [], [x, g1, w_in], exchanges)


def _pool_inv_count(i, tm, w):
    pos = i * tm + lax.broadcasted_iota(jnp.int32, (tm, 1), 0) + 1
    return 1.0 / jnp.minimum(pos, w).astype(F32)


def _window_sum(src_ref, tmp_ref, cols, tm, w, causal):
    lo, hi = 0, tm + HALO
    cur, span = None, 1
    while span < w:
        new_lo, new_hi = (lo + SUBLANES, hi) if causal else (lo, hi - SUBLANES)
        far = new_lo - span if causal else new_lo + span
        n = new_hi - new_lo
        if cur is None:
            near_v, far_v = src_ref[pl.ds(new_lo, n), cols], src_ref[pl.ds(far, n), cols]
        else:
            near_v = cur[new_lo - lo:new_lo - lo + n]
            if span % SUBLANES == 0:
                far_v = cur[far - lo:far - lo + n]
            else:
                tmp_ref[pl.ds(lo, hi - lo), :] = cur
                far_v = tmp_ref[pl.ds(far, n), :]
        cur, lo, hi, span = near_v + far_v, new_lo, new_hi, 2 * span
    off = HALO if causal else 0
    return cur[off - lo:off - lo + tm]


def fwd_mix(x, glu, p, gt, dwk, dwb, lng, lnb, w_co, pool_w, ps, w_po, w_o, g2, tm, exchanges=()):
    t = x.shape[0]

    def body(x_ref, glu_ref, gluh_ref, p_ref, ph_ref, gt_ref, k_ref, b_ref, lg_ref, lb_ref, wco_ref, pw_ref,
             ps_ref, wpo_ref, wo_ref, g2_ref,
             cv_ref, sw_ref, z_ref, zl_ref, zs_ref, yc_ref, yp_ref, mg_ref, mo_ref, h1_ref, ext_ref, win_ref, sh_ref):
        i = pl.program_id(0)
        keep = (i > 0).astype(F32)
        ext_ref[0:HALO, :] = gluh_ref[...] * keep
        ext_ref[HALO:HALO + tm, :] = glu_ref[...]
        _taps(ext_ref, k_ref, cv_ref, sh_ref, tm, HALO - (KW - 1), False)
        cv = cv_ref[...] + b_ref[...]
        cv_ref[...] = cv
        mu = jnp.mean(cv, axis=-1, keepdims=True)
        cen = cv - mu
        rstd = lax.rsqrt(jnp.mean(cen * cen, axis=-1, keepdims=True) + LN_EPS)
        ln = cen * rstd * lg_ref[...] + lb_ref[...]
        sw = (ln * _sigmoid(ln)).astype(BF16)
        sw_ref[...] = sw
        yc = _mm(sw, wco_ref[...])
        yc_ref[...] = yc.astype(BF16)
        ext_ref[0:HALO, :] = ph_ref[...] * keep
        ext_ref[HALO:HALO + tm, :] = p_ref[...]
        for g, w in enumerate(POOL_WINDOWS):
            cols = pl.ds(g * GW, GW)
            acc = _window_sum(ext_ref, win_ref, cols, tm, w, True)
            zg = (acc * _pool_inv_count(i, tm, w) - p_ref[:, cols]).astype(BF16)
            z_ref[:, cols] = zg
            zl_ref[:, cols] = _mm(zg, pw_ref[g])
        zl = zl_ref[...]
        zs = (zl * ps_ref[...]).astype(BF16)
        zs_ref[...] = zs
        yp = _mm(zs, wpo_ref[...])
        yp_ref[...] = yp.astype(BF16)
        gc = _sigmoid(gt_ref[:, 0:D].astype(F32))
        gp = _sigmoid(gt_ref[:, D:2 * D].astype(F32))
        mg = (gc * yc + gp * yp).astype(BF16)
        mg_ref[...] = mg
        mo = _mm(mg, wo_ref[...])
        mo_ref[...] = mo
        r2 = lax.rsqrt(jnp.mean(mo * mo, axis=-1, keepdims=True) + RMS_EPS)
        h1_ref[...] = x_ref[...] + mo * r2 * g2_ref[...]

    vec = _full((1, D))
    act = lambda dt: SDS((t, D), dt)
    return _call(
        body, "fwd_mix", (t // tm,),
        [_tile(tm, D), _tile(tm, D), _prev_halo(tm), _tile(tm, D), _prev_halo(tm), _tile(tm, 2 * D),
         _full((KW_PAD, D)), vec, vec, vec, _full((D, D)), _full((NG, GW, GW)), vec, _full((D, D)), _full((D, D)), vec],
        [_tile(tm, D)] * 10,
        [act(F32), act(BF16), act(BF16), act(F32), act(BF16), act(BF16), act(BF16), act(BF16), act(F32), act(F32)],
        [pltpu.VMEM((tm + HALO, D), F32), pltpu.VMEM((tm + HALO, GW), F32), _taps_scratch(tm)],
        [x, glu, glu, p, p, gt, dwk, dwb, lng, lnb, w_co, pool_w, ps, w_po, w_o, g2], exchanges)


def mlp_fwd_bwd(h1, tgt, g3, g4, w1, w2, tm, exchanges=()):
    t = h1.shape[0]
    fc = CFF

    def body(h1_ref, tgt_ref, g3_ref, g4_ref, w1_ref, w2_ref,
             v_ref, a2_ref, df2_ref, df1_ref, dh1_ref, vec_ref, f1_ref):
        i = pl.program_id(0)

        @pl.when(i == 0)
        def _():
            vec_ref[...] = jnp.zeros_like(vec_ref)

        h1v = h1_ref[...]
        r3 = lax.rsqrt(jnp.mean(h1v * h1v, axis=-1, keepdims=True) + RMS_EPS)
        n3 = h1v * r3
        v = (n3 * g3_ref[...]).astype(BF16)
        v_ref[...] = v
        f2 = jnp.zeros((tm, D), F32)
        for c in range(FF // fc):
            cols = pl.ds(c * fc, fc)
            f1 = jnp.maximum(_mm(v, w1_ref[c]), 0.0)
            f1_ref[:, cols] = f1
            a2 = (f1 * f1).astype(BF16)
            a2_ref[:, cols] = a2
            f2 = f2 + _mm(a2, w2_ref[cols, :])
        r4 = lax.rsqrt(jnp.mean(f2 * f2, axis=-1, keepdims=True) + RMS_EPS)
        n4 = f2 * r4
        err = h1v + n4 * g4_ref[...] - tgt_ref[...]
        vec_ref[2:3, :] += _rowsum(err * err) * (0.5 / D)
        dh2 = err * (1.0 / D)
        vec_ref[1:2, :] += _rowsum(dh2 * n4)
        dn4 = dh2 * g4_ref[...]
        df2 = (r4 * (dn4 - n4 * jnp.mean(dn4 * n4, axis=-1, keepdims=True))).astype(BF16)
        df2_ref[...] = df2
        dv = jnp.zeros((tm, D), F32)
        for c in range(FF // fc):
            cols = pl.ds(c * fc, fc)
            da2 = _mm_nt(df2, w2_ref[cols, :])
            df1 = (da2 * (2.0 * f1_ref[:, cols])).astype(BF16)
            df1_ref[:, cols] = df1
            dv = dv + _mm_nt(df1, w1_ref[c])
        vec_ref[0:1, :] += _rowsum(dv * n3)
        dn3 = dv * g3_ref[...]
        dh1_ref[...] = dh2 + r3 * (dn3 - n3 * jnp.mean(dn3 * n3, axis=-1, keepdims=True))

    vec = _full((1, D))
    return _call(
        body, "mlp_fwd_bwd", (t // tm,),
        [_tile(tm, D), _tile(tm, D), vec, vec, _full((NCHIP, D, CFF)), _full((FF, D))],
        [_tile(tm, D), _tile(tm, FF), _tile(tm, D), _tile(tm, FF), _tile(tm, D), _full((8, D))],
        [SDS((t, D), BF16), SDS((t, FF), BF16), SDS((t, D), BF16), SDS((t, FF), BF16), SDS((t, D), F32),
         SDS((8, D), F32)],
        [pltpu.VMEM((tm, FF), F32)], [h1, tgt, g3, g4, w1, w2], exchanges)


def bwd_mix(dh1, mo, cv, zl, yc, yp, gt, lng, lnb, ps, g2, w_co, pool_w, w_po, w_o, tm, after=()):
    t = dh1.shape[0]

    def body(dh1_ref, mo_ref, cv_ref, zl_ref, yc_ref, yp_ref, gt_ref, lg_ref, lb_ref, ps_ref, g2_ref,
             wco_ref, pw_ref, wpo_ref, wo_ref,
             dmo_ref, dgt_ref, dyc_ref, dyp_ref, dcv_ref, dzl_ref, dz_ref, vec_ref):
        i = pl.program_id(0)

        @pl.when(i == 0)
        def _():
            vec_ref[...] = jnp.zeros_like(vec_ref)

        dh1v = dh1_ref[...]
        mo = mo_ref[...]
        r2 = lax.rsqrt(jnp.mean(mo * mo, axis=-1, keepdims=True) + RMS_EPS)
        n2 = mo * r2
        vec_ref[0:1, :] += _rowsum(dh1v * n2)
        dn2 = dh1v * g2_ref[...]
        dmo = (r2 * (dn2 - n2 * jnp.mean(dn2 * n2, axis=-1, keepdims=True))).astype(BF16)
        dmo_ref[...] = dmo
        dmg = _mm_nt(dmo, wo_ref[...])
        gc = _sigmoid(gt_ref[:, 0:D].astype(F32))
        gp = _sigmoid(gt_ref[:, D:2 * D].astype(F32))
        dgt_ref[:, 0:D] = (dmg * yc_ref[...].astype(F32) * gc * (1.0 - gc)).astype(BF16)
        dgt_ref[:, D:2 * D] = (dmg * yp_ref[...].astype(F32) * gp * (1.0 - gp)).astype(BF16)
        dyc = (dmg * gc).astype(BF16)
        dyp = (dmg * gp).astype(BF16)
        dyc_ref[...] = dyc
        dyp_ref[...] = dyp
        dsw = _mm_nt(dyc, wco_ref[...])
        cv = cv_ref[...]
        mu = jnp.mean(cv, axis=-1, keepdims=True)
        cen = cv - mu
        rstd = lax.rsqrt(jnp.mean(cen * cen, axis=-1, keepdims=True) + LN_EPS)
        y = cen * rstd
        ln = y * lg_ref[...] + lb_ref[...]
        sg = _sigmoid(ln)
        dln = dsw * (sg * (1.0 + ln * (1.0 - sg)))
        vec_ref[1:2, :] += _rowsum(dln * y)
        vec_ref[2:3, :] += _rowsum(dln)
        dy = dln * lg_ref[...]
        dcv = rstd * (dy - jnp.mean(dy, axis=-1, keepdims=True) - y * jnp.mean(dy * y, axis=-1, keepdims=True))
        dcv_ref[...] = dcv
        vec_ref[3:4, :] += _rowsum(dcv)
        dzs = _mm_nt(dyp, wpo_ref[...])
        vec_ref[4:5, :] += _rowsum(dzs * zl_ref[...])
        dzl = (dzs * ps_ref[...]).astype(BF16)
        dzl_ref[...] = dzl
        for g in range(NG):
            cols = pl.ds(g * GW, GW)
            dz_ref[:, cols] = _mm_nt(dzl_ref[:, cols], pw_ref[g])

    vec = _full((1, D))
    act = lambda dt: SDS((t, D), dt)
    return _call(
        body, "bwd_mix", (t // tm,),
        [_tile(tm, D)] * 6 + [_tile(tm, 2 * D), vec, vec, vec, vec, _full((D, D)), _full((NG, GW, GW)), _full((D, D)),
                              _full((D, D))],
        [_tile(tm, D), _tile(tm, 2 * D)] + [_tile(tm, D)] * 5 + [_full((8, D))],
        [act(BF16), SDS((t, 2 * D), BF16), act(BF16), act(BF16), act(F32), act(BF16), act(F32), SDS((8, D), F32)],
        [], [dh1, mo, cv, zl, yc, yp, gt, lng, lnb, ps, g2, w_co, pool_w, w_po, w_o], after=after)[0]


def bwd_in(x, dh1, dcv, dz, glu, ag, dgt, g1, dwk, w_in, tm, after=()):
    t = x.shape[0]
    nt = t // tm

    def body(x_ref, dh1_ref, dcv_ref, dcvh_ref, dz_ref, dzh_ref, glu_ref, gluh_ref, ag_ref, dgt_ref, g1_ref,
             k_ref, w_ref, dx_ref, dproj_ref, vec_ref, dk_ref, ext_ref, tmp_ref, win_ref, sh_ref):
        i = pl.program_id(0)

        @pl.when(i == 0)
        def _():
            vec_ref[...] = jnp.zeros_like(vec_ref)
            dk_ref[...] = jnp.zeros_like(dk_ref)

        first = (i > 0).astype(F32)
        last = (i < nt - 1).astype(F32)
        ext_ref[0:HALO, :] = gluh_ref[...] * first
        ext_ref[HALO:HALO + tm, :] = glu_ref[...]
        rc, cw = 64, LANES

        def dk_chunk(cc, carry):
            cols = pl.ds(pl.multiple_of(cc * cw, cw), cw)
            for q in range(SUBLANES):
                taps = range(q, KW, SUBLANES)
                n = tm + SUBLANES * (len(taps) - 1)
                sh_ref[q, 0:n, :] = ext_ref[pl.ds(HALO - (KW - 1) + q, n), cols]
                accs = [jnp.zeros((SUBLANES, cw), F32) for _ in taps]
                for r in range(tm // rc):
                    dchunk = dcv_ref[pl.ds(r * rc, rc), cols]
                    for a in range(len(taps)):
                        prod = dchunk * sh_ref[q, pl.ds(r * rc + SUBLANES * a, rc), :]
                        accs[a] = accs[a] + jnp.sum(prod.reshape(rc // SUBLANES, SUBLANES, cw), axis=0)
                for a, j in enumerate(taps):
                    dk_ref[j:j + 1, cols] += _rowsum(accs[a])
            return carry

        lax.fori_loop(0, D // cw, dk_chunk, 0)
        ext_ref[0:tm, :] = dcv_ref[...]
        ext_ref[tm:tm + HALO, :] = dcvh_ref[...] * last
        _taps(ext_ref, k_ref, tmp_ref, sh_ref, tm, 0, True)
        dglu = tmp_ref[...]
        a = ag_ref[:, 0:D].astype(F32)
        sg = _sigmoid(ag_ref[:, D:2 * D].astype(F32))
        dproj_ref[:, 0:D] = (dglu * sg).astype(BF16)
        dproj_ref[:, D:2 * D] = (dglu * a * sg * (1.0 - sg)).astype(BF16)
        for g, w in enumerate(POOL_WINDOWS):
            cols = pl.ds(g * GW, GW)
            pos = i * tm + lax.broadcasted_iota(jnp.int32, (tm + HALO, 1), 0) + 1
            inv = 1.0 / jnp.minimum(pos, w).astype(F32)
            ext_ref[0:tm, cols] = dz_ref[:, cols] * inv[0:tm]
            ext_ref[tm:tm + HALO, cols] = dzh_ref[:, cols] * inv[tm:tm + HALO] * last
            acc = _window_sum(ext_ref, win_ref, cols, tm, w, False)
            dproj_ref[:, pl.ds(2 * D + g * GW, GW)] = (acc - dz_ref[:, cols]).astype(BF16)
        dproj_ref[:, 3 * D:5 * D] = dgt_ref[...]
        du = jnp.zeros((tm, D), F32)
        for k in range(NCHIP):
            du = du + _mm_nt(dproj_ref[:, k * CIN:(k + 1) * CIN], w_ref[k])
        xf = x_ref[...]
        r1 = lax.rsqrt(jnp.mean(xf * xf, axis=-1, keepdims=True) + RMS_EPS)
        n1 = xf * r1
        vec_ref[0:1, :] += _rowsum(du * n1)
        dn1 = du * g1_ref[...]
        dx_ref[...] = dh1_ref[...] + r1 * (dn1 - n1 * jnp.mean(dn1 * n1, axis=-1, keepdims=True))

    return _call(
        body, "bwd_in", (nt,),
        [_tile(tm, D), _tile(tm, D), _tile(tm, D), _next_halo(tm, nt), _tile(tm, D), _next_halo(tm, nt),
         _tile(tm, D), _prev_halo(tm), _tile(tm, 2 * D), _tile(tm, 2 * D), _full((1, D)),
         _full((KW_PAD, D)), _full((NCHIP, D, CIN))],
        [_tile(tm, D), _tile(tm, NPROJ * D), _full((8, D)), _full((KW_PAD, D))],
        [SDS((t, D), F32), SDS((t, NPROJ * D), BF16), SDS((8, D), F32), SDS((KW_PAD, D), F32)],
        [pltpu.VMEM((tm + HALO, D), F32), pltpu.VMEM((tm, D), F32), pltpu.VMEM((tm + HALO, GW), F32),
         _taps_scratch(tm)],
        [x, dh1, dcv, dcv, dz, dz, glu, glu, ag, dgt, g1, dwk, w_in], after=after)[0]


def wgrad(a, b, name, bm=1024, bn=1024, bt=2048, chip_major=False, exchanges=()):
    t, m = a.shape
    n = b.shape[1]
    bm, bn, bt = min(bm, m), min(bn, n), min(bt, t)
    assert m % bm == 0 and n % bn == 0 and t % bt == 0, (a.shape, b.shape, bm, bn, bt)

    def body(a_ref, b_ref, o_ref):
        k = pl.program_id(2)

        @pl.when(k == 0)
        def _():
            o_ref[...] = jnp.zeros_like(o_ref)

        o_ref[...] += _mm_tn(a_ref[...], b_ref[...])

    if chip_major:
        out_spec, out_shape = pl.BlockSpec((None, bm, bn), lambda i, j, k: (j, i, 0)), SDS((n // bn, m, bn), F32)
    else:
        out_spec, out_shape = pl.BlockSpec((bm, bn), lambda i, j, k: (i, j)), SDS((m, n), F32)
    outs, xouts = _call(
        body, name, (m // bm, n // bn, t // bt),
        [pl.BlockSpec((bt, bm), lambda i, j, k: (k, i)), pl.BlockSpec((bt, bn), lambda i, j, k: (k, j))],
        [out_spec], [out_shape], [], [a, b], exchanges)
    return outs[0], xouts


def wgrad_pool(z, dzl, bt=1024):
    t = z.shape[0]
    bt = min(bt, t)
    assert t % bt == 0

    def body(a_ref, b_ref, o_ref):
        k = pl.program_id(1)

        @pl.when(k == 0)
        def _():
            o_ref[...] = jnp.zeros_like(o_ref)

        o_ref[0] += _mm_tn(a_ref[...], b_ref[...])

    return _call(
        body, "wgrad_pool", (NG, t // bt),
        [pl.BlockSpec((bt, GW), lambda g, k: (k, g)), pl.BlockSpec((bt, GW), lambda g, k: (k, g))],
        [pl.BlockSpec((1, GW, GW), lambda g, k: (g, 0, 0))], [SDS((NG, GW, GW), F32)], [], [z, dzl])[0][0]


def cast_shards(name, names, shards, exchanges=()):
    n = len(shards)

    def body(*refs):
        srcs, dsts, bufs, sems = refs[:n], refs[n:2 * n], refs[2 * n:3 * n], refs[3 * n]
        me = 2 * lax.axis_index("x") + lax.axis_index("y")
        copies = []
        for w, mat in enumerate(names):
            bufs[w][...] = srcs[w][...].astype(BF16)
            for h in range(2):
                cp = pltpu.make_async_copy(bufs[w].at[_shard_half(mat, h)], dsts[w].at[_window(mat, me, h)], sems.at[w, h])
                cp.start()
                copies.append(cp)
        for cp in copies:
            cp.wait()

    return _call(body, name, (), [VMEM_SPEC] * n, [ANY] * n, [SDS(GEOM[mat][0], BF16) for mat in names],
                 [pltpu.VMEM(s.shape, BF16) for s in shards] + [pltpu.SemaphoreType.DMA((n, 2))], list(shards), exchanges)


VEC_ROWS = 24
ROW = dict(mlp_pre_g=0, mlp_post_g=1, loss=2, mix_post_g=8, conv_ln_g=9, conv_ln_b=10, dw_bias=11, pool_scale=12,
           mix_pre_g=16)


NDEV = 8


class GatherSmall:
    has_mid = False
    aliases = ()

    def __init__(self, part):
        self.ins = [part]
        self.out_shapes = [SDS((NDEV, *part.shape), F32)]
        self.sems = [pltpu.SemaphoreType.DMA((NDEV,)), pltpu.SemaphoreType.DMA((NDEV,))]

    def _copies(self, ins, outs, sems):
        send_sems, recv_sems = sems
        x, y, c = lax.axis_index("x"), lax.axis_index("y"), lax.axis_index("c")
        me = 4 * x + 2 * y + c
        own = pltpu.make_async_copy(ins[0], outs[0].at[me], send_sems.at[0])
        sends = [_remote(ins[0], outs[0].at[me], send_sems.at[m], recv_sems.at[m],
                         (x ^ (m >> 2), y ^ ((m >> 1) & 1), c ^ (m & 1))) for m in range(1, NDEV)]
        recvs = [_remote(outs[0].at[me ^ m], outs[0].at[me ^ m], send_sems.at[m], recv_sems.at[m], (x, y, c))
                 for m in range(1, NDEV)]
        return own, sends, recvs

    def start(self, ins, outs, sems):
        own, sends, _ = self._copies(ins, outs, sems)
        for cp in sends + [own]:
            cp.start()

    def finish(self, ins, outs, sems):
        own, sends, recvs = self._copies(ins, outs, sems)
        for cp in recvs:
            cp.wait_recv()
        for cp in sends:
            cp.wait_send()
        own.wait()


def sum_small(parts, exchanges=()):
    def body(parts_ref, sum_ref, loss_ref):
        total = parts_ref[0]
        for d in range(1, NDEV):
            total = total + parts_ref[d]
        sum_ref[...] = total
        r = ROW["loss"]
        loss_ref[...] = jnp.zeros_like(loss_ref) + jnp.sum(total[r:r + 1, :])

    return _call(body, "sum_small", (), [VMEM_SPEC], [VMEM_SPEC, VMEM_SPEC],
                 [SDS(parts.shape[1:], F32), SDS((8, 128), F32)], [], [parts], exchanges)


def pair_add(tag, names, grads, from_sibling, exchanges=()):
    n = len(names)
    in_specs, wire_specs, own_specs, wire_shapes, own_shapes = [], [], [], [], []
    for name in names:
        _, gblk, idx = GEOM[name]
        blk = _half_shape(name)
        zeros = (0,) * len(blk)
        in_specs.append(pl.BlockSpec(gblk, lambda k, idx=idx: idx(k, lax.axis_index("c"))))
        wire_specs.append(pl.BlockSpec((1, *blk), lambda k, zeros=zeros: (k, *zeros)))
        own_specs.append(pl.BlockSpec(blk, lambda k, zeros=zeros: zeros))
        wire_shapes.append(SDS((NCHIP, *blk), BF16))
        own_shapes.append(SDS(blk, F32))

    def body(*refs):
        g, s, wire, own = refs[:n], refs[n:2 * n], refs[2 * n:3 * n], refs[3 * n:]
        mine = pl.program_id(0) == 2 * lax.axis_index("x") + lax.axis_index("y")
        for w in range(n):
            total = g[w][...] + s[w][0]
            wire[w][0] = total.astype(BF16)

            @pl.when(mine)
            def _(w=w, total=total):
                own[w][...] = total

    outs, xouts = _call(body, "pair_add_" + tag, (NCHIP,), in_specs + wire_specs, wire_specs + own_specs,
                        wire_shapes + own_shapes, [], list(grads) + list(from_sibling), exchanges)
    return outs[:n], outs[n:], xouts


SUM_STEPS = 4


def sum_partials(tag, names, owns, from_chips, exchanges=()):
    n = len(names)
    own_specs, part_specs, out_specs, out_shapes, part_args, counts = [], [], [], [], [], []
    for name, parts in zip(names, from_chips):
        half = _half_shape(name)
        blk = half[:-2] + (half[-2] // SUM_STEPS, half[-1])
        lead = (0,) * (len(half) - 2)
        own_specs.append(pl.BlockSpec(blk, lambda i, lead=lead: (*lead, i, 0)))
        for p in parts:
            part_specs.append(pl.BlockSpec((p.shape[0], *blk), lambda i, lead=lead: (0, *lead, i, 0)))
            part_args.append(p)
        counts.append(len(parts))
        out_specs.append(pl.BlockSpec((2, *blk), lambda i, lead=lead: (0, *lead, i, 0)))
        out_shapes.append(SDS((2, *half), F32))

    def body(*refs):
        own, parts, out = refs[:n], list(refs[n:n + len(part_args)]), refs[n + len(part_args):]
        c = lax.axis_index("c")
        for w in range(n):
            total = own[w][...]
            for p in [parts.pop(0) for _ in range(counts[w])]:
                for j in range(p.shape[0]):
                    total = total + p[j].astype(F32)
            out[w][c] = total

    return _call(body, "sum_partials_" + tag, (SUM_STEPS,), own_specs + part_specs, out_specs, out_shapes, [],
                 list(owns) + part_args, exchanges)


class SwapHalves:
    has_mid = False

    def __init__(self, halves):
        self.ins = list(halves)
        self.out_shapes = [SDS(h.shape, h.dtype) for h in halves]
        self.aliases = [(i, i) for i in range(len(halves))]
        self.sems = [pltpu.SemaphoreType.DMA((len(halves),)), pltpu.SemaphoreType.DMA((len(halves),))]

    def start(self, ins, outs, sems):
        send_sems, recv_sems = sems
        x, y, c = lax.axis_index("x"), lax.axis_index("y"), lax.axis_index("c")
        for w in range(len(self.ins)):
            _remote(ins[w].at[c], outs[w].at[c], send_sems.at[w], recv_sems.at[w], (x, y, 1 - c)).start()

    def finish(self, ins, outs, sems):
        send_sems, recv_sems = sems
        x, y, c = lax.axis_index("x"), lax.axis_index("y"), lax.axis_index("c")
        for w in range(len(self.ins)):
            _remote(ins[w].at[c], outs[w].at[c], send_sems.at[w], recv_sems.at[w], (x, y, 1 - c)).wait_send()
            _remote(ins[w].at[1 - c], outs[w].at[1 - c], send_sems.at[w], recv_sems.at[w], (x, y, 1 - c)).wait_recv()


SEM_SPEC = pl.BlockSpec(memory_space=pltpu.SEMAPHORE)
HBM_SPEC = pl.BlockSpec(memory_space=pltpu.HBM)
DATAFLOW = pltpu.SideEffectType.DATAFLOW_SIDE_EFFECTING


def plan_chips(names):
    def plan(refs):
        n = len(names)
        x, y, c, me, chips, chip_ids = _place()
        return [(refs[w].at[chip_ids[j]], refs[n + w].at[j], (*chip, c)) for w in range(n) for j, chip in enumerate(chips)]
    return plan


def plan_pair(names):
    def plan(refs):
        n = len(names)
        x, y, c, me, chips, chip_ids = _place()
        return [(refs[w].at[_window(name, k, 1 - c)], refs[n + w].at[k], (x, y, 1 - c))
                for w, name in enumerate(names) for k in range(NCHIP)]
    return plan


def split_start(tag, arrays, plan, ncopies):
    arrays = [pltpu.with_memory_space_constraint(a, pltpu.HBM) for a in arrays]
    n = len(arrays)

    def body(*refs):
        sems, token = refs[n:n + 2 * ncopies], refs[-1]
        for s, (src, dst, to) in enumerate(plan(refs[:n])):
            _remote(src, dst, sems[2 * s], sems[2 * s + 1], to).start()
        token[...] = jnp.zeros_like(token)

    res = pl.pallas_call(
        body, name="start_" + tag, in_specs=[HBM_SPEC] * n,
        out_specs=[SEM_SPEC] * (2 * ncopies) + [HBM_SPEC] * n + [VMEM_SPEC],
        out_shape=[pltpu.SemaphoreType.DMA(())] * (2 * ncopies) + [pltpu.HBM(a.shape, a.dtype) for a in arrays]
        + [SDS((8, 128), F32)],
        input_output_aliases={i: 2 * ncopies + i for i in range(n)},
        compiler_params=pltpu.CompilerParams(has_side_effects=DATAFLOW),
    )(*arrays)
    return (res[:2 * ncopies], res[2 * ncopies:-1]), res[-1]


def split_wait(tag, started, plan, after):
    sems, arrays = started
    n = len(arrays)

    def body(*refs):
        sem = refs[n:n + len(sems)]
        for s, (src, dst, to) in enumerate(plan(refs[:n])):
            cp = _remote(src, dst, sem[2 * s], sem[2 * s + 1], to)
            cp.wait_send()
            cp.wait_recv()

    return pl.pallas_call(
        body, name="wait_" + tag, in_specs=[HBM_SPEC] * n + [SEM_SPEC] * len(sems) + [ANY] * len(after),
        out_specs=[HBM_SPEC] * n, out_shape=[pltpu.HBM(a.shape, a.dtype) for a in arrays],
        input_output_aliases={i: i for i in range(n)},
        compiler_params=pltpu.CompilerParams(has_side_effects=DATAFLOW),
    )(*arrays, *sems, *after)


def adamw(tag, ws, gs, ms, vs, steps, after=()):
    n = len(ws)
    specs = [pl.BlockSpec((a.shape[0] // steps, a.shape[1]), lambda i: (i, 0)) for a in ws]
    assert all(a.shape[0] % (steps * SUBLANES) == 0 for a in ws), [a.shape for a in ws]

    def body(*refs):
        w_, g_, m_, v_ = refs[:n], refs[n:2 * n], refs[2 * n:3 * n], refs[3 * n:4 * n]
        d_, nm_, nv_ = refs[4 * n:5 * n], refs[5 * n:6 * n], refs[6 * n:]
        for i in range(n):
            gv = g_[i][...]
            mn = B1 * m_[i][...] + (1.0 - B1) * gv
            vn = B2 * v_[i][...] + (1.0 - B2) * (gv * gv)
            m_hat = mn / (1.0 - B1 ** STEP)
            v_hat = vn / (1.0 - B2 ** STEP)
            d_[i][...] = -LR * (m_hat / (jnp.sqrt(v_hat) + ADAM_EPS) + WD * w_[i][...])
            nm_[i][...] = mn
            nv_[i][...] = vn

    outs, _ = _call(body, "adamw_" + tag, (steps,), specs * 4, specs * 3, [SDS(a.shape, F32) for a in ws] * 3, [],
                    list(ws) + list(gs) + list(ms) + list(vs), after=after)
    return list(zip(outs[:n], outs[n:2 * n], outs[2 * n:]))


VECS = ("mix_pre_g", "dw_bias", "conv_ln_g", "conv_ln_b", "pool_scale", "mix_post_g", "mlp_pre_g", "mlp_post_g")
WEIGHTS = ("mix_pre_g", "w_in", "dw_kernel", "dw_bias", "conv_ln_g", "conv_ln_b", "w_conv_out", "pool_w", "pool_scale",
           "w_pool_out", "w_o", "mix_post_g", "mlp_pre_g", "w_ff1", "w_ff2", "mlp_post_g")
MIX_MATS = ("w_conv_out", "pool_w", "w_pool_out", "w_o")
FF_MATS = ("w_ff1", "w_ff2")


def kernel(x, mix_pre_g, w_in, dw_kernel, dw_bias, conv_ln_g, conv_ln_b, w_conv_out, pool_w, pool_scale, w_pool_out, w_o, mix_post_g, mlp_pre_g, w_ff1, w_ff2, mlp_post_g, loss_target, m_mix_pre_g, m_w_in, m_dw_kernel, m_dw_bias, m_conv_ln_g, m_conv_ln_b, m_w_conv_out, m_pool_w, m_pool_scale, m_w_pool_out, m_w_o, m_mix_post_g, m_mlp_pre_g, m_w_ff1, m_w_ff2, m_mlp_post_g, v_mix_pre_g, v_w_in, v_dw_kernel, v_dw_bias, v_conv_ln_g, v_conv_ln_b, v_w_conv_out, v_pool_w, v_pool_scale, v_w_pool_out, v_w_o, v_mix_post_g, v_mlp_pre_g, v_w_ff1, v_w_ff2, v_mlp_post_g):
    w = dict(mix_pre_g=mix_pre_g, w_in=w_in, dw_kernel=dw_kernel, dw_bias=dw_bias, conv_ln_g=conv_ln_g,
             conv_ln_b=conv_ln_b, w_conv_out=w_conv_out, pool_w=pool_w, pool_scale=pool_scale, w_pool_out=w_pool_out,
             w_o=w_o, mix_post_g=mix_post_g, mlp_pre_g=mlp_pre_g, w_ff1=w_ff1, w_ff2=w_ff2, mlp_post_g=mlp_post_g)
    m = dict(mix_pre_g=m_mix_pre_g, w_in=m_w_in, dw_kernel=m_dw_kernel, dw_bias=m_dw_bias, conv_ln_g=m_conv_ln_g,
             conv_ln_b=m_conv_ln_b, w_conv_out=m_w_conv_out, pool_w=m_pool_w, pool_scale=m_pool_scale,
             w_pool_out=m_w_pool_out, w_o=m_w_o, mix_post_g=m_mix_post_g, mlp_pre_g=m_mlp_pre_g, w_ff1=m_w_ff1,
             w_ff2=m_w_ff2, mlp_post_g=m_mlp_post_g)
    v = dict(mix_pre_g=v_mix_pre_g, w_in=v_w_in, dw_kernel=v_dw_kernel, dw_bias=v_dw_bias, conv_ln_g=v_conv_ln_g,
             conv_ln_b=v_conv_ln_b, w_conv_out=v_w_conv_out, pool_w=v_pool_w, pool_scale=v_pool_scale,
             w_pool_out=v_w_pool_out, w_o=v_w_o, mix_post_g=v_mix_post_g, mlp_pre_g=v_mlp_pre_g, w_ff1=v_w_ff1,
             w_ff2=v_w_ff2, mlp_post_g=v_mlp_post_g)
    chip = 2 * lax.axis_index("x") + lax.axis_index("y")
    xs, tgt = x[0], loss_target[0]
    vecs = {name: w[name].reshape(1, D) for name in VECS}

    taps = lax.dynamic_update_slice(jnp.zeros((KW_PAD, D), F32), dw_kernel, (0, chip * DSH))
    mine, full = {}, {}
    (mine["w_in"],), _ = cast_shards("cast_w_in", ("w_in",), [w["w_in"]])
    rest = MATS[1:]
    cast, ((full["w_in"], dwk),) = cast_shards(
        "cast_rest", rest, [w[name] for name in rest], [GatherWeights(("w_in",), [mine["w_in"]], taps)])
    mine.update(zip(rest, cast))
    (u, glu, ag, p, gt), (got,) = fwd_in(
        xs, vecs["mix_pre_g"], full["w_in"], TM_IN, [GatherWeights(MIX_MATS, [mine[n] for n in MIX_MATS])])
    full.update(zip(MIX_MATS, got))
    (cv, sw, z, zl, zs, yc, yp, mg, mo, h1), (got,) = fwd_mix(
        xs, glu, p, gt, dwk, vecs["dw_bias"], vecs["conv_ln_g"], vecs["conv_ln_b"], full["w_conv_out"], full["pool_w"],
        vecs["pool_scale"], full["w_pool_out"], full["w_o"], vecs["mix_post_g"], TM,
        [GatherWeights(FF_MATS, [mine[n] for n in FF_MATS])])
    full.update(zip(FF_MATS, got))

    (v_, a2, df2, df1, dh1, vec_mlp), _ = mlp_fwd_bwd(
        h1, tgt, vecs["mlp_pre_g"], vecs["mlp_post_g"], full["w_ff1"], full["w_ff2"], TM)
    grads, g, delta, new_m, new_v = {}, {}, {}, {}, {}
    rest_mats = FF_MATS + MIX_MATS
    grads["w_ff1"], _ = wgrad(v_, df1, "wgrad_ff1", bn=CFF, chip_major=True)
    grads["w_ff2"], _ = wgrad(a2, df2, "wgrad_ff2")
    landing = lambda names, slots, dt: [lax.empty((slots, *_half_shape(n)), dt) for n in names]
    pair_ff, token = split_start("pair_ff", [grads[n] for n in FF_MATS] + landing(FF_MATS, NCHIP, F32),
                                 plan_pair(FF_MATS), len(FF_MATS) * NCHIP)
    dmo, dgt, dyc, dyp, dcv, dzl, dz, vec_mix = bwd_mix(
        dh1, mo, cv, zl, yc, yp, gt, vecs["conv_ln_g"], vecs["conv_ln_b"], vecs["pool_scale"], vecs["mix_post_g"],
        full["w_conv_out"], full["pool_w"], full["w_pool_out"], full["w_o"], TM, after=[token])
    got = split_wait("pair_ff", pair_ff, plan_pair(FF_MATS), after=[vec_mix])
    grads_ff, from_sibling = got[:len(FF_MATS)], got[len(FF_MATS):]
    grads["w_conv_out"], _ = wgrad(sw, dyc, "wgrad_conv_out", bt=1024)
    grads["pool_w"] = wgrad_pool(z, dzl)
    grads["w_pool_out"], _ = wgrad(zs, dyp, "wgrad_pool_out", bt=1024)
    grads["w_o"], _ = wgrad(mg, dmo, "wgrad_o", bt=1024)
    wire_ff, own_ff, (from_sibling,) = pair_add(
        "ff", FF_MATS, grads_ff, from_sibling, exchanges=[ExchangePair(MIX_MATS, [grads[n] for n in MIX_MATS])])
    wire_mix, own_mix, _ = pair_add("mix", MIX_MATS, [grads[n] for n in MIX_MATS], from_sibling)
    chips_rest, token = split_start("chips_rest", list(wire_ff) + list(wire_mix) + landing(rest_mats, NCHIP - 1, BF16),
                                    plan_chips(rest_mats), len(rest_mats) * (NCHIP - 1))
    dx, dproj, vec_in, dk = bwd_in(
        xs, dh1, dcv, dz, glu, ag, dgt, vecs["mix_pre_g"], dwk, full["w_in"], TM, after=[token])
    small_part = jnp.concatenate([vec_mlp, vec_mix, vec_in, dk], axis=0)
    grads["w_in"], ((small_parts,),) = wgrad(
        u, dproj, "wgrad_in", bn=CIN, chip_major=True, exchanges=[GatherSmall(small_part)])
    got = split_wait("chips_rest", chips_rest, plan_chips(rest_mats), after=[grads["w_in"]])[len(rest_mats):]
    chips_ff, chips_mix = got[:len(FF_MATS)], got[len(FF_MATS):]
    halves_rest, (from_sibling,) = sum_partials(
        "rest", rest_mats, list(own_ff) + list(own_mix), [[a] for a in list(chips_ff) + list(chips_mix)],
        [ExchangePair(("w_in",), [grads["w_in"]])])
    wire_in, own_in, (reduced_rest,) = pair_add(
        "in", ("w_in",), [grads["w_in"]], from_sibling, exchanges=[SwapHalves(halves_rest)])
    for n, red in zip(rest_mats, reduced_rest):
        g[n] = red.reshape(w[n].shape)
    (small, loss8), _ = sum_small(small_parts)
    loss = loss8[0, 0]
    for name in VECS:
        g[name] = small[ROW[name]]
    g["dw_kernel"] = lax.dynamic_slice(small[VEC_ROWS:VEC_ROWS + KW_PAD], (0, chip * DSH), (KW_PAD, DSH))

    def update(tag, names, steps, after=()):
        two_d = lambda a: a.reshape(-1, a.shape[-1])
        res = adamw(tag, [two_d(w[n]) for n in names], [two_d(g[n]) for n in names],
                    [two_d(m[n]) for n in names], [two_d(v[n]) for n in names], steps, after)
        for n, (d_, m_, v_new) in zip(names, res):
            delta[n], new_m[n], new_v[n] = d_.reshape(w[n].shape), m_.reshape(w[n].shape), v_new.reshape(w[n].shape)

    chips_in, token = split_start("chips_in", list(wire_in) + landing(("w_in",), NCHIP - 1, BF16),
                                  plan_chips(("w_in",)), NCHIP - 1)
    update("rest", rest_mats, 8, after=[token])
    chips_in = split_wait("chips_in", chips_in, plan_chips(("w_in",)), after=[delta[rest_mats[-1]]])[1:]
    halves_in, _ = sum_partials("in", ("w_in",), own_in, [chips_in])
    (reduced_in,) = exchange("swap_w_in", SwapHalves(halves_in))
    g["w_in"] = reduced_in.reshape(w["w_in"].shape)
    update("in", ("w_in",), 8)
    stack = lambda d: jnp.concatenate([d[name].reshape(1, D) for name in VECS], axis=0)
    (res,) = adamw("vectors", [stack(w)], [stack(g)], [stack(m)], [stack(v)], 1)
    for i, name in enumerate(VECS):
        delta[name], new_m[name], new_v[name] = [r[i] for r in res]
    padk = lambda a: jnp.pad(a, ((0, KW_PAD - KW), (0, 0)))
    (res,) = adamw("dw_kernel", [padk(w["dw_kernel"])], [g["dw_kernel"]], [padk(m["dw_kernel"])],
                   [padk(v["dw_kernel"])], 1)
    delta["dw_kernel"], new_m["dw_kernel"], new_v["dw_kernel"] = [r[:KW] for r in res]
    g["dw_kernel"] = g["dw_kernel"][:KW]

    return (loss, dx[None], *[g[n] for n in WEIGHTS], *[delta[n] for n in WEIGHTS], *[new_m[n] for n in WEIGHTS],
            *[new_v[n] for n in WEIGHTS])
```

```python
import math

import jax
import jax.numpy as jnp
from jax import lax
from jax.experimental import pallas as pl
from jax.experimental.pallas import tpu as pltpu

F32 = jnp.float32
BF16 = jnp.bfloat16

D = 1024
FF = 4096
NPROJ = 5
KW = 31
KW_PAD = 32
SUBLANES = 8
LANES = 128
HALO = 32
POOL_WINDOWS = (2, 4, 8, 16)
NG = 4
GW = D // NG
RMS_EPS = 1e-6
LN_EPS = 1e-5
LR, B1, B2, ADAM_EPS, WD, STEP = 0.001, 0.9, 0.999, 1e-08, 0.01, 10
NCHIP = 4
VMEM_LIMIT = 60 * 1024 * 1024
MESH = pl.DeviceIdType.MESH
TM = 256
TM_IN = 512

ANY = pl.BlockSpec(memory_space=pl.ANY)
VMEM_SPEC = pl.BlockSpec(memory_space=pltpu.VMEM)
SDS = jax.ShapeDtypeStruct


def _cp(**kw):
    return pltpu.CompilerParams(vmem_limit_bytes=VMEM_LIMIT, **kw)


def _mm(a, b):
    return jnp.dot(a, b, preferred_element_type=F32)


def _mm_nt(a, b):
    return lax.dot_general(a, b, (((1,), (1,)), ((), ())), preferred_element_type=F32)


def _mm_tn(a, b):
    return lax.dot_general(a, b, (((0,), (0,)), ((), ())), preferred_element_type=F32)


def _sigmoid(x):
    return 1.0 / (1.0 + jnp.exp(-x))


def _rowsum(x):
    return jnp.sum(x, axis=0, keepdims=True)


def _full(shape):
    return pl.BlockSpec(shape, lambda i: (0,) * len(shape))


def _tile(tm, cols):
    return pl.BlockSpec((tm, cols), lambda i: (i, 0))


def _prev_halo(tm):
    return pl.BlockSpec((HALO, D), lambda i: (jnp.maximum(i * (tm // HALO) - 1, 0), 0))


def _next_halo(tm, nt):
    return pl.BlockSpec((HALO, D), lambda i: (jnp.minimum((i + 1) * (tm // HALO), nt * (tm // HALO) - 1), 0))


MATS = ("w_in", "w_conv_out", "pool_w", "w_pool_out", "w_o", "w_ff1", "w_ff2")
CIN = NPROJ * D // NCHIP
CFF = FF // NCHIP
_ROWS = lambda k, h: (2 * k + h, 0)
_CHIP_MAJOR = lambda k, h: (k, h, 0)
GEOM = dict(
    w_in=((NCHIP, D, CIN), (None, D // 2, CIN), _CHIP_MAJOR),
    w_conv_out=((D, D), (D // (2 * NCHIP), D), _ROWS),
    pool_w=((NG, GW, GW), (NG // 2, GW // NCHIP, GW), lambda k, h: (h, k, 0)),
    w_pool_out=((D, D), (D // (2 * NCHIP), D), _ROWS),
    w_o=((D, D), (D // (2 * NCHIP), D), _ROWS),
    w_ff1=((NCHIP, D, CFF), (None, D // 2, CFF), _CHIP_MAJOR),
    w_ff2=((FF, D), (FF // (2 * NCHIP), D), _ROWS),
)
DSH = D // NCHIP


def _half_shape(name):
    return tuple(b for b in GEOM[name][1] if b is not None)


def _window(name, k, h):
    _, blk, idx = GEOM[name]
    return tuple(i if b is None else pl.ds(i * b, b) for i, b in zip(idx(k, h), blk))


def _shard_half(name, h):
    n0 = _half_shape(name)[0]
    return (pl.ds(h * n0, n0),) + (slice(None),) * (len(_half_shape(name)) - 1)


def _place():
    x, y, c = lax.axis_index("x"), lax.axis_index("y"), lax.axis_index("c")
    chips = [(1 - x, y), (x, 1 - y), (1 - x, 1 - y)]
    return x, y, c, 2 * x + y, chips, [2 * px + py for px, py in chips]


def _remote(src, dst, send_sem, recv_sem, to):
    return pltpu.make_async_remote_copy(src_ref=src, dst_ref=dst, send_sem=send_sem, recv_sem=recv_sem,
                                        device_id=to, device_id_type=MESH)


class GatherWeights:
    has_mid = True

    def __init__(self, names, fulls, taps=None):
        self.names = names
        self.ins = list(fulls) + ([taps] if taps is not None else [])
        self.has_taps = taps is not None
        self.out_shapes = [SDS(a.shape, a.dtype) for a in self.ins]
        self.aliases = [(i, i) for i in range(len(self.ins))]
        n = len(self.ins)
        self.sems = [pltpu.SemaphoreType.DMA((n, 6)), pltpu.SemaphoreType.DMA((n, 6))]

    def _copies(self, ins, outs, sems):
        send_sems, recv_sems = sems
        x, y, c, me, chips, chip_ids = _place()
        sibling = (x, y, 1 - c)
        ici, ici_recv, d2d, d2d_recv = [], [], [], []
        for w, name in enumerate(self.names):
            for j, chip in enumerate(chips):
                ici.append(_remote(ins[w].at[_window(name, me, c)], outs[w].at[_window(name, me, c)],
                                   send_sems.at[w, j], recv_sems.at[w, j], (*chip, c)))
                got = outs[w].at[_window(name, chip_ids[j], c)]
                ici_recv.append(_remote(got, got, send_sems.at[w, j], recv_sems.at[w, j], sibling))
                d2d.append(_remote(got, got, send_sems.at[w, 3 + j], recv_sems.at[w, 3 + j], sibling))
                got = outs[w].at[_window(name, chip_ids[j], 1 - c)]
                d2d_recv.append(_remote(got, got, send_sems.at[w, 3 + j], recv_sems.at[w, 3 + j], sibling))
        if self.has_taps:
            w = len(self.names)
            for j, chip in enumerate(chips):
                ici.append(_remote(ins[w].at[:, pl.ds(me * DSH, DSH)], outs[w].at[:, pl.ds(me * DSH, DSH)],
                                   send_sems.at[w, j], recv_sems.at[w, j], (*chip, c)))
                got = outs[w].at[:, pl.ds(chip_ids[j] * DSH, DSH)]
                d2d_recv.append(_remote(got, got, send_sems.at[w, j], recv_sems.at[w, j], sibling))
        return ici, ici_recv, d2d, d2d_recv

    def start(self, ins, outs, sems):
        for cp in self._copies(ins, outs, sems)[0]:
            cp.start()

    def mid(self, ins, outs, sems):
        _, ici_recv, d2d, _ = self._copies(ins, outs, sems)
        for got, fwd in zip(ici_recv, d2d):
            got.wait_recv()
            fwd.start()

    def finish(self, ins, outs, sems):
        ici, _, d2d, d2d_recv = self._copies(ins, outs, sems)
        for cp in d2d_recv:
            cp.wait_recv()
        for cp in ici + d2d:
            cp.wait_send()


class ExchangePair:
    has_mid = False
    aliases = ()

    def __init__(self, names, grads):
        self.names, self.ins = names, list(grads)
        self.out_shapes = [SDS((NCHIP, *_half_shape(n)), F32) for n in names]
        self.sems = [pltpu.SemaphoreType.DMA((len(names),)), pltpu.SemaphoreType.DMA((len(names),))]

    def start(self, ins, outs, sems):
        send_sems, recv_sems = sems
        x, y, c, me, chips, chip_ids = _place()
        for w, name in enumerate(self.names):
            for k in range(NCHIP):
                _remote(ins[w].at[_window(name, k, 1 - c)], outs[w].at[k], send_sems.at[w], recv_sems.at[w],
                        (x, y, 1 - c)).start()

    def finish(self, ins, outs, sems):
        send_sems, recv_sems = sems
        x, y, c, me, chips, chip_ids = _place()
        for w in range(len(self.names)):
            _remote(outs[w], outs[w], send_sems.at[w], recv_sems.at[w], (x, y, 1 - c)).wait()


class ExchangeChips:
    has_mid = False
    aliases = ()

    def __init__(self, names, wires):
        self.names, self.ins = names, list(wires)
        self.out_shapes = [SDS((NCHIP - 1, *_half_shape(n)), BF16) for n in names]
        self.sems = [pltpu.SemaphoreType.DMA((len(names), NCHIP - 1)), pltpu.SemaphoreType.DMA((len(names), NCHIP - 1))]

    def _copies(self, ins, outs, sems):
        send_sems, recv_sems = sems
        x, y, c, me, chips, chip_ids = _place()
        return [_remote(ins[w].at[chip_ids[j]], outs[w].at[j], send_sems.at[w, j], recv_sems.at[w, j], (*chip, c))
                for w in range(len(self.names)) for j, chip in enumerate(chips)]

    def start(self, ins, outs, sems):
        for cp in self._copies(ins, outs, sems):
            cp.start()

    def finish(self, ins, outs, sems):
        for cp in self._copies(ins, outs, sems):
            cp.wait()


def _call(body, name, grid, in_specs, out_specs, out_shape, scratch, args, exchanges=(), after=()):
    n_in, n_out, n_scr = len(in_specs), len(out_specs), len(scratch)
    x_in = [a for e in exchanges for a in e.ins]
    x_out = [s for e in exchanges for s in e.out_shapes]
    x_sem = [s for e in exchanges for s in e.sems]
    nsteps = math.prod(grid)

    def wrapped(*refs):
        ins, rest = refs[:n_in], refs[n_in:]
        xin, rest = rest[:len(x_in)], rest[len(x_in) + len(after):]
        outs, rest = rest[:n_out], rest[n_out:]
        xout, rest = rest[:len(x_out)], rest[len(x_out):]
        scr, xsem = rest[:n_scr], rest[n_scr:]
        parts = []
        for e in exchanges:
            parts.append((xin[:len(e.ins)], xout[:len(e.out_shapes)], xsem[:len(e.sems)]))
            xin, xout, xsem = xin[len(e.ins):], xout[len(e.out_shapes):], xsem[len(e.sems):]
        if not grid:
            for e, p in zip(exchanges, parts):
                e.start(*p)
            body(*ins, *outs, *scr)
            for e, p in zip(exchanges, parts):
                if e.has_mid:
                    e.mid(*p)
            for e, p in zip(exchanges, parts):
                e.finish(*p)
            return
        step = 0
        for axis, extent in enumerate(grid):
            step = step * extent + pl.program_id(axis)
        if exchanges:
            @pl.when(step == 0)
            def _():
                for e, p in zip(exchanges, parts):
                    e.start(*p)

        body(*ins, *outs, *scr)
        if any(e.has_mid for e in exchanges):
            @pl.when(step == max(nsteps - 2, 0))
            def _():
                for e, p in zip(exchanges, parts):
                    if e.has_mid:
                        e.mid(*p)

        if exchanges:
            @pl.when(step == nsteps - 1)
            def _():
                for e, p in zip(exchanges, parts):
                    e.finish(*p)

    kw = dict(grid=grid, compiler_params=_cp(dimension_semantics=("arbitrary",) * len(grid))) if grid else dict(
        compiler_params=_cp())
    aliases, i0, o0 = {}, n_in, n_out
    for e in exchanges:
        aliases.update({i0 + i: o0 + o for i, o in e.aliases})
        i0, o0 = i0 + len(e.ins), o0 + len(e.out_shapes)
    res = pl.pallas_call(
        wrapped, name=name, in_specs=list(in_specs) + [ANY] * (len(x_in) + len(after)),
        out_specs=list(out_specs) + [ANY] * len(x_out),
        out_shape=list(out_shape) + x_out, scratch_shapes=list(scratch) + x_sem, input_output_aliases=aliases, **kw,
    )(*args, *x_in, *after)
    outs, rest = res[:n_out], res[n_out:]
    xouts = []
    for e in exchanges:
        xouts.append(rest[:len(e.out_shapes)])
        rest = rest[len(e.out_shapes):]
    return outs, xouts


def exchange(name, ex):
    return _call(lambda: None, name, (), [], [], [], [], [], [ex])[1][0]


def _taps_scratch(tm):
    return pltpu.VMEM((SUBLANES, tm + HALO, LANES), F32)


def _taps(src_ref, k_ref, dst_ref, sh_ref, tm, off0, reverse):
    rc, cw = 64, LANES

    def col_chunk(cc, carry):
        cols = pl.ds(pl.multiple_of(cc * cw, cw), cw)
        for q in range(SUBLANES):
            n = tm + SUBLANES * (len(range(q, KW, SUBLANES)) - 1)
            sh_ref[q, 0:n, :] = src_ref[pl.ds(off0 + q, n), cols]
        for r in range(tm // rc):
            acc = jnp.zeros((rc, cw), F32)
            for q in range(SUBLANES):
                for a, j in enumerate(range(q, KW, SUBLANES)):
                    kj = KW - 1 - j if reverse else j
                    acc = acc + k_ref[kj:kj + 1, cols] * sh_ref[q, pl.ds(r * rc + SUBLANES * a, rc), :]
            dst_ref[pl.ds(r * rc, rc), cols] = acc
        return carry

    lax.fori_loop(0, D // cw, col_chunk, 0)


def _proj_pieces(g):
    lo, hi, pieces = g * D, (g + 1) * D, []
    while lo < hi:
        k = lo // CIN
        b = min(hi - k * CIN, CIN)
        pieces.append((k, lo - k * CIN, b))
        lo = k * CIN + b
    return pieces


def fwd_in(x, g1, w_in, tm, exchanges=()):
    t = x.shape[0]

    def body(x_ref, g_ref, w_ref, u_ref, glu_ref, ag_ref, p_ref, gt_ref):
        xf = x_ref[...]
        r = lax.rsqrt(jnp.mean(xf * xf, axis=-1, keepdims=True) + RMS_EPS)
        u = (xf * r * g_ref[...]).astype(BF16)
        u_ref[...] = u
        proj = lambda g: jnp.concatenate([_mm(u, w_ref[k, :, lo:hi]) for k, lo, hi in _proj_pieces(g)], axis=1)
        a = proj(0)
        gate = proj(1)
        glu_ref[...] = a * _sigmoid(gate)
        ag_ref[:, 0:D] = a.astype(BF16)
        ag_ref[:, D:2 * D] = gate.astype(BF16)
        p_ref[...] = proj(2)
        gt_ref[:, 0:D] = proj(3).astype(BF16)
        gt_ref[:, D:2 * D] = proj(4).astype(BF16)

    return _call(
        body, "fwd_in", (t // tm,),
        [_tile(tm, D), _full((1, D)), _full((NCHIP, D, CIN))],
        [_tile(tm, D), _tile(tm, D), _tile(tm, 2 * D), _tile(tm, D), _tile(tm, 2 * D)],
        [SDS((t, D), BF16), SDS((t, D), F32), SDS((t, 2 * D), BF16), SDS((t, D), F32), SDS((t, 2 * D), BF16)],
        [], [x, g1, w_in], exchanges)


def _pool_inv_count(i, tm, w):
    pos = i * tm + lax.broadcasted_iota(jnp.int32, (tm, 1), 0) + 1
    return 1.0 / jnp.minimum(pos, w).astype(F32)


def _window_sum(src_ref, tmp_ref, cols, tm, w, causal):
    lo, hi = 0, tm + HALO
    cur, span = None, 1
    while span < w:
        new_lo, new_hi = (lo + SUBLANES, hi) if causal else (lo, hi - SUBLANES)
        far = new_lo - span if causal else new_lo + span
        n = new_hi - new_lo
        if cur is None:
            near_v, far_v = src_ref[pl.ds(new_lo, n), cols], src_ref[pl.ds(far, n), cols]
        else:
            near_v = cur[new_lo - lo:new_lo - lo + n]
            if span % SUBLANES == 0:
                far_v = cur[far - lo:far - lo + n]
            else:
                tmp_ref[pl.ds(lo, hi - lo), :] = cur
                far_v = tmp_ref[pl.ds(far, n), :]
        cur, lo, hi, span = near_v + far_v, new_lo, new_hi, 2 * span
    off = HALO if causal else 0
    return cur[off - lo:off - lo + tm]


def fwd_mix(x, glu, p, gt, dwk, dwb, lng, lnb, w_co, pool_w, ps, w_po, w_o, g2, tm, exchanges=()):
    t = x.shape[0]

    def body(x_ref, glu_ref, gluh_ref, p_ref, ph_ref, gt_ref, k_ref, b_ref, lg_ref, lb_ref, wco_ref, pw_ref,
             ps_ref, wpo_ref, wo_ref, g2_ref,
             cv_ref, sw_ref, z_ref, zl_ref, zs_ref, yc_ref, yp_ref, mg_ref, mo_ref, h1_ref, ext_ref, win_ref, sh_ref):
        i = pl.program_id(0)
        keep = (i > 0).astype(F32)
        ext_ref[0:HALO, :] = gluh_ref[...] * keep
        ext_ref[HALO:HALO + tm, :] = glu_ref[...]
        _taps(ext_ref, k_ref, cv_ref, sh_ref, tm, HALO - (KW - 1), False)
        cv = cv_ref[...] + b_ref[...]
        cv_ref[...] = cv
        mu = jnp.mean(cv, axis=-1, keepdims=True)
        cen = cv - mu
        rstd = lax.rsqrt(jnp.mean(cen * cen, axis=-1, keepdims=True) + LN_EPS)
        ln = cen * rstd * lg_ref[...] + lb_ref[...]
        sw = (ln * _sigmoid(ln)).astype(BF16)
        sw_ref[...] = sw
        yc = _mm(sw, wco_ref[...])
        yc_ref[...] = yc.astype(BF16)
        ext_ref[0:HALO, :] = ph_ref[...] * keep
        ext_ref[HALO:HALO + tm, :] = p_ref[...]
        for g, w in enumerate(POOL_WINDOWS):
            cols = pl.ds(g * GW, GW)
            acc = _window_sum(ext_ref, win_ref, cols, tm, w, True)
            zg = (acc * _pool_inv_count(i, tm, w) - p_ref[:, cols]).astype(BF16)
            z_ref[:, cols] = zg
            zl_ref[:, cols] = _mm(zg, pw_ref[g])
        zl = zl_ref[...]
        zs = (zl * ps_ref[...]).astype(BF16)
        zs_ref[...] = zs
        yp = _mm(zs, wpo_ref[...])
        yp_ref[...] = yp.astype(BF16)
        gc = _sigmoid(gt_ref[:, 0:D].astype(F32))
        gp = _sigmoid(gt_ref[:, D:2 * D].astype(F32))
        mg = (gc * yc + gp * yp).astype(BF16)
        mg_ref[...] = mg
        mo = _mm(mg, wo_ref[...])
        mo_ref[...] = mo
        r2 = lax.rsqrt(jnp.mean(mo * mo, axis=-1, keepdims=True) + RMS_EPS)
        h1_ref[...] = x_ref[...] + mo * r2 * g2_ref[...]

    vec = _full((1, D))
    act = lambda dt: SDS((t, D), dt)
    return _call(
        body, "fwd_mix", (t // tm,),
        [_tile(tm, D), _tile(tm, D), _prev_halo(tm), _tile(tm, D), _prev_halo(tm), _tile(tm, 2 * D),
         _full((KW_PAD, D)), vec, vec, vec, _full((D, D)), _full((NG, GW, GW)), vec, _full((D, D)), _full((D, D)), vec],
        [_tile(tm, D)] * 10,
        [act(F32), act(BF16), act(BF16), act(F32), act(BF16), act(BF16), act(BF16), act(BF16), act(F32), act(F32)],
        [pltpu.VMEM((tm + HALO, D), F32), pltpu.VMEM((tm + HALO, GW), F32), _taps_scratch(tm)],
        [x, glu, glu, p, p, gt, dwk, dwb, lng, lnb, w_co, pool_w, ps, w_po, w_o, g2], exchanges)


def mlp_fwd_bwd(h1, tgt, g3, g4, w1, w2, tm, exchanges=()):
    t = h1.shape[0]
    fc = CFF

    def body(h1_ref, tgt_ref, g3_ref, g4_ref, w1_ref, w2_ref,
             v_ref, a2_ref, df2_ref, df1_ref, dh1_ref, vec_ref, f1_ref):
        i = pl.program_id(0)

        @pl.when(i == 0)
        def _():
            vec_ref[...] = jnp.zeros_like(vec_ref)

        h1v = h1_ref[...]
        r3 = lax.rsqrt(jnp.mean(h1v * h1v, axis=-1, keepdims=True) + RMS_EPS)
        n3 = h1v * r3
        v = (n3 * g3_ref[...]).astype(BF16)
        v_ref[...] = v
        f2 = jnp.zeros((tm, D), F32)
        for c in range(FF // fc):
            cols = pl.ds(c * fc, fc)
            f1 = jnp.maximum(_mm(v, w1_ref[c]), 0.0)
            f1_ref[:, cols] = f1
            a2 = (f1 * f1).astype(BF16)
            a2_ref[:, cols] = a2
            f2 = f2 + _mm(a2, w2_ref[cols, :])
        r4 = lax.rsqrt(jnp.mean(f2 * f2, axis=-1, keepdims=True) + RMS_EPS)
        n4 = f2 * r4
        err = h1v + n4 * g4_ref[...] - tgt_ref[...]
        vec_ref[2:3, :] += _rowsum(err * err) * (0.5 / D)
        dh2 = err * (1.0 / D)
        vec_ref[1:2, :] += _rowsum(dh2 * n4)
        dn4 = dh2 * g4_ref[...]
        df2 = (r4 * (dn4 - n4 * jnp.mean(dn4 * n4, axis=-1, keepdims=True))).astype(BF16)
        df2_ref[...] = df2
        dv = jnp.zeros((tm, D), F32)
        for c in range(FF // fc):
            cols = pl.ds(c * fc, fc)
            da2 = _mm_nt(df2, w2_ref[cols, :])
            df1 = (da2 * (2.0 * f1_ref[:, cols])).astype(BF16)
            df1_ref[:, cols] = df1
            dv = dv + _mm_nt(df1, w1_ref[c])
        vec_ref[0:1, :] += _rowsum(dv * n3)
        dn3 = dv * g3_ref[...]
        dh1_ref[...] = dh2 + r3 * (dn3 - n3 * jnp.mean(dn3 * n3, axis=-1, keepdims=True))

    vec = _full((1, D))
    return _call(
        body, "mlp_fwd_bwd", (t // tm,),
        [_tile(tm, D), _tile(tm, D), vec, vec, _full((NCHIP, D, CFF)), _full((FF, D))],
        [_tile(tm, D), _tile(tm, FF), _tile(tm, D), _tile(tm, FF), _tile(tm, D), _full((8, D))],
        [SDS((t, D), BF16), SDS((t, FF), BF16), SDS((t, D), BF16), SDS((t, FF), BF16), SDS((t, D), F32),
         SDS((8, D), F32)],
        [pltpu.VMEM((tm, FF), F32)], [h1, tgt, g3, g4, w1, w2], exchanges)


def bwd_mix(dh1, mo, cv, zl, yc, yp, gt, lng, lnb, ps, g2, w_co, pool_w, w_po, w_o, tm, after=()):
    t = dh1.shape[0]

    def body(dh1_ref, mo_ref, cv_ref, zl_ref, yc_ref, yp_ref, gt_ref, lg_ref, lb_ref, ps_ref, g2_ref,
             wco_ref, pw_ref, wpo_ref, wo_ref,
             dmo_ref, dgt_ref, dyc_ref, dyp_ref, dcv_ref, dzl_ref, dz_ref, vec_ref):
        i = pl.program_id(0)

        @pl.when(i == 0)
        def _():
            vec_ref[...] = jnp.zeros_like(vec_ref)

        dh1v = dh1_ref[...]
        mo = mo_ref[...]
        r2 = lax.rsqrt(jnp.mean(mo * mo, axis=-1, keepdims=True) + RMS_EPS)
        n2 = mo * r2
        vec_ref[0:1, :] += _rowsum(dh1v * n2)
        dn2 = dh1v * g2_ref[...]
        dmo = (r2 * (dn2 - n2 * jnp.mean(dn2 * n2, axis=-1, keepdims=True))).astype(BF16)
        dmo_ref[...] = dmo
        dmg = _mm_nt(dmo, wo_ref[...])
        gc = _sigmoid(gt_ref[:, 0:D].astype(F32))
        gp = _sigmoid(gt_ref[:, D:2 * D].astype(F32))
        dgt_ref[:, 0:D] = (dmg * yc_ref[...].astype(F32) * gc * (1.0 - gc)).astype(BF16)
        dgt_ref[:, D:2 * D] = (dmg * yp_ref[...].astype(F32) * gp * (1.0 - gp)).astype(BF16)
        dyc = (dmg * gc).astype(BF16)
        dyp = (dmg * gp).astype(BF16)
        dyc_ref[...] = dyc
        dyp_ref[...] = dyp
        dsw = _mm_nt(dyc, wco_ref[...])
        cv = cv_ref[...]
        mu = jnp.mean(cv, axis=-1, keepdims=True)
        cen = cv - mu
        rstd = lax.rsqrt(jnp.mean(cen * cen, axis=-1, keepdims=True) + LN_EPS)
        y = cen * rstd
        ln = y * lg_ref[...] + lb_ref[...]
        sg = _sigmoid(ln)
        dln = dsw * (sg * (1.0 + ln * (1.0 - sg)))
        vec_ref[1:2, :] += _rowsum(dln * y)
        vec_ref[2:3, :] += _rowsum(dln)
        dy = dln * lg_ref[...]
        dcv = rstd * (dy - jnp.mean(dy, axis=-1, keepdims=True) - y * jnp.mean(dy * y, axis=-1, keepdims=True))
        dcv_ref[...] = dcv
        vec_ref[3:4, :] += _rowsum(dcv)
        dzs = _mm_nt(dyp, wpo_ref[...])
        vec_ref[4:5, :] += _rowsum(dzs * zl_ref[...])
        dzl = (dzs * ps_ref[...]).astype(BF16)
        dzl_ref[...] = dzl
        for g in range(NG):
            cols = pl.ds(g * GW, GW)
            dz_ref[:, cols] = _mm_nt(dzl_ref[:, cols], pw_ref[g])

    vec = _full((1, D))
    act = lambda dt: SDS((t, D), dt)
    return _call(
        body, "bwd_mix", (t // tm,),
        [_tile(tm, D)] * 6 + [_tile(tm, 2 * D), vec, vec, vec, vec, _full((D, D)), _full((NG, GW, GW)), _full((D, D)),
                              _full((D, D))],
        [_tile(tm, D), _tile(tm, 2 * D)] + [_tile(tm, D)] * 5 + [_full((8, D))],
        [act(BF16), SDS((t, 2 * D), BF16), act(BF16), act(BF16), act(F32), act(BF16), act(F32), SDS((8, D), F32)],
        [], [dh1, mo, cv, zl, yc, yp, gt, lng, lnb, ps, g2, w_co, pool_w, w_po, w_o], after=after)[0]


def bwd_in(x, dh1, dcv, dz, glu, ag, dgt, g1, dwk, w_in, tm, after=()):
    t = x.shape[0]
    nt = t // tm

    def body(x_ref, dh1_ref, dcv_ref, dcvh_ref, dz_ref, dzh_ref, glu_ref, gluh_ref, ag_ref, dgt_ref, g1_ref,
             k_ref, w_ref, dx_ref, dproj_ref, vec_ref, dk_ref, ext_ref, tmp_ref, win_ref, sh_ref, gext_ref, gsh_ref):
        i = pl.program_id(0)

        @pl.when(i == 0)
        def _():
            vec_ref[...] = jnp.zeros_like(vec_ref)
            dk_ref[...] = jnp.zeros_like(dk_ref)

        first = (i > 0).astype(F32)
        last = (i < nt - 1).astype(F32)
        gext_ref[0:HALO, :] = gluh_ref[...] * first
        gext_ref[HALO:HALO + tm, :] = glu_ref[...]
        rc, cw = 32, LANES

        def dk_chunk(cc):
            cols = pl.ds(cc * cw, cw)
            for q in range(SUBLANES):
                taps = range(q, KW, SUBLANES)
                n = tm + SUBLANES * (len(taps) - 1)
                gsh_ref[q, 0:n, :] = gext_ref[pl.ds(HALO - (KW - 1) + q, n), cols]
                accs = [jnp.zeros((SUBLANES, cw), F32) for _ in taps]
                for r in range(tm // rc):
                    dchunk = dcv_ref[pl.ds(r * rc, rc), cols]
                    for a in range(len(taps)):
                        prod = dchunk * gsh_ref[q, pl.ds(r * rc + SUBLANES * a, rc), :]
                        accs[a] = accs[a] + jnp.sum(prod.reshape(rc // SUBLANES, SUBLANES, cw), axis=0)
                for a, j in enumerate(taps):
                    dk_ref[j:j + 1, cols] += _rowsum(accs[a])

        ext_ref[0:tm, :] = dcv_ref[...]
        ext_ref[tm:tm + HALO, :] = dcvh_ref[...] * last
        _taps(ext_ref, k_ref, tmp_ref, sh_ref, tm, 0, True)
        dglu = tmp_ref[...]
        a = ag_ref[:, 0:D].astype(F32)
        sg = _sigmoid(ag_ref[:, D:2 * D].astype(F32))
        dproj_ref[:, 0:D] = (dglu * sg).astype(BF16)
        dproj_ref[:, D:2 * D] = (dglu * a * sg * (1.0 - sg)).astype(BF16)
        for g, w in enumerate(POOL_WINDOWS):
            cols = pl.ds(g * GW, GW)
            pos = i * tm + lax.broadcasted_iota(jnp.int32, (tm + HALO, 1), 0) + 1
            inv = 1.0 / jnp.minimum(pos, w).astype(F32)
            ext_ref[0:tm, cols] = dz_ref[:, cols] * inv[0:tm]
            ext_ref[tm:tm + HALO, cols] = dzh_ref[:, cols] * inv[tm:tm + HALO] * last
            acc = _window_sum(ext_ref, win_ref, cols, tm, w, False)
            dproj_ref[:, pl.ds(2 * D + g * GW, GW)] = (acc - dz_ref[:, cols]).astype(BF16)
        dproj_ref[:, 3 * D:5 * D] = dgt_ref[...]
        du = jnp.zeros((tm, D), F32)
        chunks_per_matmul = D // cw // NCHIP
        for k in range(NCHIP):
            du = du + _mm_nt(dproj_ref[:, k * CIN:(k + 1) * CIN], w_ref[k])
            for cc in range(k * chunks_per_matmul, (k + 1) * chunks_per_matmul):
                dk_chunk(cc)
        xf = x_ref[...]
        r1 = lax.rsqrt(jnp.mean(xf * xf, axis=-1, keepdims=True) + RMS_EPS)
        n1 = xf * r1
        vec_ref[0:1, :] += _rowsum(du * n1)
        dn1 = du * g1_ref[...]
        dx_ref[...] = dh1_ref[...] + r1 * (dn1 - n1 * jnp.mean(dn1 * n1, axis=-1, keepdims=True))

    return _call(
        body, "bwd_in", (nt,),
        [_tile(tm, D), _tile(tm, D), _tile(tm, D), _next_halo(tm, nt), _tile(tm, D), _next_halo(tm, nt),
         _tile(tm, D), _prev_halo(tm), _tile(tm, 2 * D), _tile(tm, 2 * D), _full((1, D)),
         _full((KW_PAD, D)), _full((NCHIP, D, CIN))],
        [_tile(tm, D), _tile(tm, NPROJ * D), _full((8, D)), _full((KW_PAD, D))],
        [SDS((t, D), F32), SDS((t, NPROJ * D), BF16), SDS((8, D), F32), SDS((KW_PAD, D), F32)],
        [pltpu.VMEM((tm + HALO, D), F32), pltpu.VMEM((tm, D), F32), pltpu.VMEM((tm + HALO, GW), F32),
         _taps_scratch(tm), pltpu.VMEM((tm + HALO, D), F32), _taps_scratch(tm)],
        [x, dh1, dcv, dcv, dz, dz, glu, glu, ag, dgt, g1, dwk, w_in], after=after)[0]


def wgrad(a, b, name, bm=1024, bn=1024, bt=2048, chip_major=False, exchanges=()):
    t, m = a.shape
    n = b.shape[1]
    bm, bn, bt = min(bm, m), min(bn, n), min(bt, t)
    assert m % bm == 0 and n % bn == 0 and t % bt == 0, (a.shape, b.shape, bm, bn, bt)

    def body(a_ref, b_ref, o_ref):
        k = pl.program_id(2)

        @pl.when(k == 0)
        def _():
            o_ref[...] = jnp.zeros_like(o_ref)

        o_ref[...] += _mm_tn(a_ref[...], b_ref[...])

    if chip_major:
        out_spec, out_shape = pl.BlockSpec((None, bm, bn), lambda i, j, k: (j, i, 0)), SDS((n // bn, m, bn), F32)
    else:
        out_spec, out_shape = pl.BlockSpec((bm, bn), lambda i, j, k: (i, j)), SDS((m, n), F32)
    outs, xouts = _call(
        body, name, (m // bm, n // bn, t // bt),
        [pl.BlockSpec((bt, bm), lambda i, j, k: (k, i)), pl.BlockSpec((bt, bn), lambda i, j, k: (k, j))],
        [out_spec], [out_shape], [], [a, b], exchanges)
    return outs[0], xouts


def wgrad_pool(z, dzl, bt=1024):
    t = z.shape[0]
    bt = min(bt, t)
    assert t % bt == 0

    def body(a_ref, b_ref, o_ref):
        k = pl.program_id(1)

        @pl.when(k == 0)
        def _():
            o_ref[...] = jnp.zeros_like(o_ref)

        o_ref[0] += _mm_tn(a_ref[...], b_ref[...])

    return _call(
        body, "wgrad_pool", (NG, t // bt),
        [pl.BlockSpec((bt, GW), lambda g, k: (k, g)), pl.BlockSpec((bt, GW), lambda g, k: (k, g))],
        [pl.BlockSpec((1, GW, GW), lambda g, k: (g, 0, 0))], [SDS((NG, GW, GW), F32)], [], [z, dzl])[0][0]


def cast_shards(name, names, shards, exchanges=()):
    n = len(shards)

    def body(*refs):
        srcs, dsts, bufs, sems = refs[:n], refs[n:2 * n], refs[2 * n:3 * n], refs[3 * n]
        me = 2 * lax.axis_index("x") + lax.axis_index("y")
        copies = []
        for w, mat in enumerate(names):
            bufs[w][...] = srcs[w][...].astype(BF16)
            for h in range(2):
                cp = pltpu.make_async_copy(bufs[w].at[_shard_half(mat, h)], dsts[w].at[_window(mat, me, h)], sems.at[w, h])
                cp.start()
                copies.append(cp)
        for cp in copies:
            cp.wait()

    return _call(body, name, (), [VMEM_SPEC] * n, [ANY] * n, [SDS(GEOM[mat][0], BF16) for mat in names],
                 [pltpu.VMEM(s.shape, BF16) for s in shards] + [pltpu.SemaphoreType.DMA((n, 2))], list(shards), exchanges)


VEC_ROWS = 24
ROW = dict(mlp_pre_g=0, mlp_post_g=1, loss=2, mix_post_g=8, conv_ln_g=9, conv_ln_b=10, dw_bias=11, pool_scale=12,
           mix_pre_g=16)


NDEV = 8


class GatherSmall:
    has_mid = False
    aliases = ()

    def __init__(self, part):
        self.ins = [part]
        self.out_shapes = [SDS((NDEV, *part.shape), F32)]
        self.sems = [pltpu.SemaphoreType.DMA((NDEV,)), pltpu.SemaphoreType.DMA((NDEV,))]

    def _copies(self, ins, outs, sems):
        send_sems, recv_sems = sems
        x, y, c = lax.axis_index("x"), lax.axis_index("y"), lax.axis_index("c")
        me = 4 * x + 2 * y + c
        own = pltpu.make_async_copy(ins[0], outs[0].at[me], send_sems.at[0])
        sends = [_remote(ins[0], outs[0].at[me], send_sems.at[m], recv_sems.at[m],
                         (x ^ (m >> 2), y ^ ((m >> 1) & 1), c ^ (m & 1))) for m in range(1, NDEV)]
        recvs = [_remote(outs[0].at[me ^ m], outs[0].at[me ^ m], send_sems.at[m], recv_sems.at[m], (x, y, c))
                 for m in range(1, NDEV)]
        return own, sends, recvs

    def start(self, ins, outs, sems):
        own, sends, _ = self._copies(ins, outs, sems)
        for cp in sends + [own]:
            cp.start()

    def finish(self, ins, outs, sems):
        own, sends, recvs = self._copies(ins, outs, sems)
        for cp in recvs:
            cp.wait_recv()
        for cp in sends:
            cp.wait_send()
        own.wait()


def sum_small(parts, after=()):
    def body(parts_ref, sum_ref, loss_ref):
        total = parts_ref[0]
        for d in range(1, NDEV):
            total = total + parts_ref[d]
        sum_ref[...] = total
        r = ROW["loss"]
        loss_ref[...] = jnp.zeros_like(loss_ref) + jnp.sum(total[r:r + 1, :])

    return _call(body, "sum_small", (), [VMEM_SPEC], [VMEM_SPEC, VMEM_SPEC],
                 [SDS(parts.shape[1:], F32), SDS((8, 128), F32)], [], [parts], after=after)[0]


def pair_add(tag, names, grads, from_sibling, exchanges=()):
    n = len(names)
    in_specs, wire_specs, own_specs, wire_shapes, own_shapes = [], [], [], [], []
    for name in names:
        _, gblk, idx = GEOM[name]
        blk = _half_shape(name)
        zeros = (0,) * len(blk)
        in_specs.append(pl.BlockSpec(gblk, lambda k, idx=idx: idx(k, lax.axis_index("c"))))
        wire_specs.append(pl.BlockSpec((1, *blk), lambda k, zeros=zeros: (k, *zeros)))
        own_specs.append(pl.BlockSpec(blk, lambda k, zeros=zeros: zeros))
        wire_shapes.append(SDS((NCHIP, *blk), BF16))
        own_shapes.append(SDS(blk, F32))

    def body(*refs):
        g, s, wire, own = refs[:n], refs[n:2 * n], refs[2 * n:3 * n], refs[3 * n:]
        mine = pl.program_id(0) == 2 * lax.axis_index("x") + lax.axis_index("y")
        for w in range(n):
            total = g[w][...] + s[w][0]
            wire[w][0] = total.astype(BF16)

            @pl.when(mine)
            def _(w=w, total=total):
                own[w][...] = total

    outs, xouts = _call(body, "pair_add_" + tag, (NCHIP,), in_specs + wire_specs, wire_specs + own_specs,
                        wire_shapes + own_shapes, [], list(grads) + list(from_sibling), exchanges)
    return outs[:n], outs[n:], xouts


SUM_STEPS = 4


def sum_partials(tag, names, owns, from_chips, exchanges=()):
    n = len(names)
    own_specs, part_specs, out_specs, out_shapes, part_args, counts = [], [], [], [], [], []
    for name, parts in zip(names, from_chips):
        half = _half_shape(name)
        blk = half[:-2] + (half[-2] // SUM_STEPS, half[-1])
        lead = (0,) * (len(half) - 2)
        own_specs.append(pl.BlockSpec(blk, lambda i, lead=lead: (*lead, i, 0)))
        for p in parts:
            part_specs.append(pl.BlockSpec((p.shape[0], *blk), lambda i, lead=lead: (0, *lead, i, 0)))
            part_args.append(p)
        counts.append(len(parts))
        out_specs.append(pl.BlockSpec((2, *blk), lambda i, lead=lead: (0, *lead, i, 0)))
        out_shapes.append(SDS((2, *half), F32))

    def body(*refs):
        own, parts, out = refs[:n], list(refs[n:n + len(part_args)]), refs[n + len(part_args):]
        c = lax.axis_index("c")
        for w in range(n):
            total = own[w][...]
            for p in [parts.pop(0) for _ in range(counts[w])]:
                for j in range(p.shape[0]):
                    total = total + p[j].astype(F32)
            out[w][c] = total

    return _call(body, "sum_partials_" + tag, (SUM_STEPS,), own_specs + part_specs, out_specs, out_shapes, [],
                 list(owns) + part_args, exchanges)


class SwapHalves:
    has_mid = False

    def __init__(self, halves):
        self.ins = list(halves)
        self.out_shapes = [SDS(h.shape, h.dtype) for h in halves]
        self.aliases = [(i, i) for i in range(len(halves))]
        self.sems = [pltpu.SemaphoreType.DMA((len(halves),)), pltpu.SemaphoreType.DMA((len(halves),))]

    def start(self, ins, outs, sems):
        send_sems, recv_sems = sems
        x, y, c = lax.axis_index("x"), lax.axis_index("y"), lax.axis_index("c")
        for w in range(len(self.ins)):
            _remote(ins[w].at[c], outs[w].at[c], send_sems.at[w], recv_sems.at[w], (x, y, 1 - c)).start()

    def finish(self, ins, outs, sems):
        send_sems, recv_sems = sems
        x, y, c = lax.axis_index("x"), lax.axis_index("y"), lax.axis_index("c")
        for w in range(len(self.ins)):
            _remote(ins[w].at[c], outs[w].at[c], send_sems.at[w], recv_sems.at[w], (x, y, 1 - c)).wait_send()
            _remote(ins[w].at[1 - c], outs[w].at[1 - c], send_sems.at[w], recv_sems.at[w], (x, y, 1 - c)).wait_recv()


SEM_SPEC = pl.BlockSpec(memory_space=pltpu.SEMAPHORE)
HBM_SPEC = pl.BlockSpec(memory_space=pltpu.HBM)
DATAFLOW = pltpu.SideEffectType.DATAFLOW_SIDE_EFFECTING


def plan_chips(names):
    def plan(refs):
        n = len(names)
        x, y, c, me, chips, chip_ids = _place()
        return [(refs[w].at[chip_ids[j]], refs[n + w].at[j], (*chip, c)) for w in range(n) for j, chip in enumerate(chips)]
    return plan


def plan_pair(names):
    def plan(refs):
        n = len(names)
        x, y, c, me, chips, chip_ids = _place()
        return [(refs[w].at[_window(name, k, 1 - c)], refs[n + w].at[k], (x, y, 1 - c))
                for w, name in enumerate(names) for k in range(NCHIP)]
    return plan


def split_start(tag, arrays, plan, ncopies):
    n = len(arrays)

    def body(*refs):
        sems, token = refs[n:n + 2 * ncopies], refs[-1]
        for s, (src, dst, to) in enumerate(plan(refs[:n])):
            _remote(src, dst, sems[2 * s], sems[2 * s + 1], to).start()
        token[...] = jnp.zeros_like(token)

    res = pl.pallas_call(
        body, name="start_" + tag, in_specs=[HBM_SPEC] * n,
        out_specs=[SEM_SPEC] * (2 * ncopies) + [HBM_SPEC] * n + [VMEM_SPEC],
        out_shape=[pltpu.SemaphoreType.DMA(())] * (2 * ncopies) + [pltpu.HBM(a.shape, a.dtype) for a in arrays]
        + [SDS((8, 128), F32)],
        input_output_aliases={i: 2 * ncopies + i for i in range(n)},
        compiler_params=pltpu.CompilerParams(has_side_effects=DATAFLOW),
    )(*arrays)
    return (res[:2 * ncopies], res[2 * ncopies:-1]), res[-1]


def split_wait(tag, started, plan, after):
    sems, arrays = started
    n = len(arrays)

    def body(*refs):
        sem = refs[n:n + len(sems)]
        for s, (src, dst, to) in enumerate(plan(refs[:n])):
            cp = _remote(src, dst, sem[2 * s], sem[2 * s + 1], to)
            cp.wait_send()
            cp.wait_recv()

    return pl.pallas_call(
        body, name="wait_" + tag, in_specs=[HBM_SPEC] * n + [SEM_SPEC] * len(sems) + [ANY] * len(after),
        out_specs=[HBM_SPEC] * n, out_shape=[pltpu.HBM(a.shape, a.dtype) for a in arrays],
        input_output_aliases={i: i for i in range(n)},
        compiler_params=pltpu.CompilerParams(has_side_effects=DATAFLOW),
    )(*arrays, *sems, *after)


def adamw(tag, ws, gs, ms, vs, steps, after=()):
    n = len(ws)
    specs = [pl.BlockSpec((a.shape[0] // steps, a.shape[1]), lambda i: (i, 0)) for a in ws]
    assert all(a.shape[0] % (steps * SUBLANES) == 0 for a in ws), [a.shape for a in ws]

    def body(*refs):
        w_, g_, m_, v_ = refs[:n], refs[n:2 * n], refs[2 * n:3 * n], refs[3 * n:4 * n]
        d_, nm_, nv_ = refs[4 * n:5 * n], refs[5 * n:6 * n], refs[6 * n:]
        for i in range(n):
            gv = g_[i][...]
            mn = B1 * m_[i][...] + (1.0 - B1) * gv
            vn = B2 * v_[i][...] + (1.0 - B2) * (gv * gv)
            m_hat = mn / (1.0 - B1 ** STEP)
            v_hat = vn / (1.0 - B2 ** STEP)
            d_[i][...] = -LR * (m_hat / (jnp.sqrt(v_hat) + ADAM_EPS) + WD * w_[i][...])
            nm_[i][...] = mn
            nv_[i][...] = vn

    outs, _ = _call(body, "adamw_" + tag, (steps,), specs * 4, specs * 3, [SDS(a.shape, F32) for a in ws] * 3, [],
                    list(ws) + list(gs) + list(ms) + list(vs), after=after)
    return list(zip(outs[:n], outs[n:2 * n], outs[2 * n:]))


VECS = ("mix_pre_g", "dw_bias", "conv_ln_g", "conv_ln_b", "pool_scale", "mix_post_g", "mlp_pre_g", "mlp_post_g")
WEIGHTS = ("mix_pre_g", "w_in", "dw_kernel", "dw_bias", "conv_ln_g", "conv_ln_b", "w_conv_out", "pool_w", "pool_scale",
           "w_pool_out", "w_o", "mix_post_g", "mlp_pre_g", "w_ff1", "w_ff2", "mlp_post_g")
MIX_MATS = ("w_conv_out", "pool_w", "w_pool_out", "w_o")
FF_MATS = ("w_ff1", "w_ff2")


def kernel(x, mix_pre_g, w_in, dw_kernel, dw_bias, conv_ln_g, conv_ln_b, w_conv_out, pool_w, pool_scale, w_pool_out, w_o, mix_post_g, mlp_pre_g, w_ff1, w_ff2, mlp_post_g, loss_target, m_mix_pre_g, m_w_in, m_dw_kernel, m_dw_bias, m_conv_ln_g, m_conv_ln_b, m_w_conv_out, m_pool_w, m_pool_scale, m_w_pool_out, m_w_o, m_mix_post_g, m_mlp_pre_g, m_w_ff1, m_w_ff2, m_mlp_post_g, v_mix_pre_g, v_w_in, v_dw_kernel, v_dw_bias, v_conv_ln_g, v_conv_ln_b, v_w_conv_out, v_pool_w, v_pool_scale, v_w_pool_out, v_w_o, v_mix_post_g, v_mlp_pre_g, v_w_ff1, v_w_ff2, v_mlp_post_g):
    w = dict(mix_pre_g=mix_pre_g, w_in=w_in, dw_kernel=dw_kernel, dw_bias=dw_bias, conv_ln_g=conv_ln_g,
             conv_ln_b=conv_ln_b, w_conv_out=w_conv_out, pool_w=pool_w, pool_scale=pool_scale, w_pool_out=w_pool_out,
             w_o=w_o, mix_post_g=mix_post_g, mlp_pre_g=mlp_pre_g, w_ff1=w_ff1, w_ff2=w_ff2, mlp_post_g=mlp_post_g)
    m = dict(mix_pre_g=m_mix_pre_g, w_in=m_w_in, dw_kernel=m_dw_kernel, dw_bias=m_dw_bias, conv_ln_g=m_conv_ln_g,
             conv_ln_b=m_conv_ln_b, w_conv_out=m_w_conv_out, pool_w=m_pool_w, pool_scale=m_pool_scale,
             w_pool_out=m_w_pool_out, w_o=m_w_o, mix_post_g=m_mix_post_g, mlp_pre_g=m_mlp_pre_g, w_ff1=m_w_ff1,
             w_ff2=m_w_ff2, mlp_post_g=m_mlp_post_g)
    v = dict(mix_pre_g=v_mix_pre_g, w_in=v_w_in, dw_kernel=v_dw_kernel, dw_bias=v_dw_bias, conv_ln_g=v_conv_ln_g,
             conv_ln_b=v_conv_ln_b, w_conv_out=v_w_conv_out, pool_w=v_pool_w, pool_scale=v_pool_scale,
             w_pool_out=v_w_pool_out, w_o=v_w_o, mix_post_g=v_mix_post_g, mlp_pre_g=v_mlp_pre_g, w_ff1=v_w_ff1,
             w_ff2=v_w_ff2, mlp_post_g=v_mlp_post_g)
    chip = 2 * lax.axis_index("x") + lax.axis_index("y")
    xs, tgt = x[0], loss_target[0]
    vecs = {name: w[name].reshape(1, D) for name in VECS}

    taps = lax.dynamic_update_slice(jnp.zeros((KW_PAD, D), F32), dw_kernel, (0, chip * DSH))
    mine, full = {}, {}
    (mine["w_in"],), _ = cast_shards("cast_w_in", ("w_in",), [w["w_in"]])
    rest = MATS[1:]
    cast, ((full["w_in"], dwk),) = cast_shards(
        "cast_rest", rest, [w[name] for name in rest], [GatherWeights(("w_in",), [mine["w_in"]], taps)])
    mine.update(zip(rest, cast))
    (u, glu, ag, p, gt), (got,) = fwd_in(
        xs, vecs["mix_pre_g"], full["w_in"], TM_IN, [GatherWeights(MIX_MATS, [mine[n] for n in MIX_MATS])])
    full.update(zip(MIX_MATS, got))
    (cv, sw, z, zl, zs, yc, yp, mg, mo, h1), (got,) = fwd_mix(
        xs, glu, p, gt, dwk, vecs["dw_bias"], vecs["conv_ln_g"], vecs["conv_ln_b"], full["w_conv_out"], full["pool_w"],
        vecs["pool_scale"], full["w_pool_out"], full["w_o"], vecs["mix_post_g"], TM,
        [GatherWeights(FF_MATS, [mine[n] for n in FF_MATS])])
    full.update(zip(FF_MATS, got))

    (v_, a2, df2, df1, dh1, vec_mlp), _ = mlp_fwd_bwd(
        h1, tgt, vecs["mlp_pre_g"], vecs["mlp_post_g"], full["w_ff1"], full["w_ff2"], TM)
    grads, g, delta, new_m, new_v = {}, {}, {}, {}, {}
    rest_mats = FF_MATS + MIX_MATS
    grads["w_ff1"], _ = wgrad(v_, df1, "wgrad_ff1", bn=CFF, chip_major=True)
    grads["w_ff2"], _ = wgrad(a2, df2, "wgrad_ff2")
    landing = lambda names, slots, dt: [lax.empty((slots, *_half_shape(n)), dt) for n in names]
    pair_ff, token = split_start("pair_ff", [grads[n] for n in FF_MATS] + landing(FF_MATS, NCHIP, F32),
                                 plan_pair(FF_MATS), len(FF_MATS) * NCHIP)
    dmo, dgt, dyc, dyp, dcv, dzl, dz, vec_mix = bwd_mix(
        dh1, mo, cv, zl, yc, yp, gt, vecs["conv_ln_g"], vecs["conv_ln_b"], vecs["pool_scale"], vecs["mix_post_g"],
        full["w_conv_out"], full["pool_w"], full["w_pool_out"], full["w_o"], TM, after=[token])
    got = split_wait("pair_ff", pair_ff, plan_pair(FF_MATS), after=[vec_mix])
    grads_ff, from_sibling = got[:len(FF_MATS)], got[len(FF_MATS):]
    grads["w_conv_out"], _ = wgrad(sw, dyc, "wgrad_conv_out", bt=1024)
    grads["pool_w"] = wgrad_pool(z, dzl)
    grads["w_pool_out"], _ = wgrad(zs, dyp, "wgrad_pool_out", bt=1024)
    grads["w_o"], _ = wgrad(mg, dmo, "wgrad_o", bt=1024)
    wire_ff, own_ff, (from_sibling,) = pair_add(
        "ff", FF_MATS, grads_ff, from_sibling, exchanges=[ExchangePair(MIX_MATS, [grads[n] for n in MIX_MATS])])
    wire_mix, own_mix, _ = pair_add("mix", MIX_MATS, [grads[n] for n in MIX_MATS], from_sibling)
    chips_rest, token = split_start("chips_rest", list(wire_ff) + list(wire_mix) + landing(rest_mats, NCHIP - 1, BF16),
                                    plan_chips(rest_mats), len(rest_mats) * (NCHIP - 1))
    dx, dproj, vec_in, dk = bwd_in(
        xs, dh1, dcv, dz, glu, ag, dgt, vecs["mix_pre_g"], dwk, full["w_in"], TM, after=[token])
    small_part = jnp.concatenate([vec_mlp, vec_mix, vec_in, dk], axis=0)
    grads["w_in"], ((small_parts,),) = wgrad(
        u, dproj, "wgrad_in", bn=CIN, chip_major=True, exchanges=[GatherSmall(small_part)])
    got = split_wait("chips_rest", chips_rest, plan_chips(rest_mats), after=[grads["w_in"]])[len(rest_mats):]
    chips_ff, chips_mix = got[:len(FF_MATS)], got[len(FF_MATS):]
    halves_rest, (from_sibling,) = sum_partials(
        "rest", rest_mats, list(own_ff) + list(own_mix), [[a] for a in list(chips_ff) + list(chips_mix)],
        [ExchangePair(("w_in",), [grads["w_in"]])])
    wire_in, own_in, (reduced_rest,) = pair_add(
        "in", ("w_in",), [grads["w_in"]], from_sibling, exchanges=[SwapHalves(halves_rest)])
    for n, red in zip(rest_mats, reduced_rest):
        g[n] = red.reshape(w[n].shape)

    def update(tag, names, steps, after=()):
        two_d = lambda a: a.reshape(-1, a.shape[-1])
        res = adamw(tag, [two_d(w[n]) for n in names], [two_d(g[n]) for n in names],
                    [two_d(m[n]) for n in names], [two_d(v[n]) for n in names], steps, after)
        for n, (d_, m_, v_new) in zip(names, res):
            delta[n], new_m[n], new_v[n] = d_.reshape(w[n].shape), m_.reshape(w[n].shape), v_new.reshape(w[n].shape)

    chips_in, token = split_start("chips_in", list(wire_in) + landing(("w_in",), NCHIP - 1, BF16),
                                  plan_chips(("w_in",)), NCHIP - 1)
    small, loss8 = sum_small(small_parts, after=[token])
    loss = loss8[0, 0]
    for name in VECS:
        g[name] = small[ROW[name]]
    g["dw_kernel"] = lax.dynamic_slice(small[VEC_ROWS:VEC_ROWS + KW_PAD], (0, chip * DSH), (KW_PAD, DSH))
    update("rest", rest_mats, 8, after=[token])
    stack = lambda d: jnp.concatenate([d[name].reshape(1, D) for name in VECS], axis=0)
    (res,) = adamw("vectors", [stack(w)], [stack(g)], [stack(m)], [stack(v)], 1, after=[token])
    for i, name in enumerate(VECS):
        delta[name], new_m[name], new_v[name] = [r[i] for r in res]
    padk = lambda a: jnp.pad(a, ((0, KW_PAD - KW), (0, 0)))
    (res,) = adamw("dw_kernel", [padk(w["dw_kernel"])], [g["dw_kernel"]], [padk(m["dw_kernel"])],
                   [padk(v["dw_kernel"])], 1, after=[token])
    delta["dw_kernel"], new_m["dw_kernel"], new_v["dw_kernel"] = [r[:KW] for r in res]
    g["dw_kernel"] = g["dw_kernel"][:KW]
    chips_in = split_wait("chips_in", chips_in, plan_chips(("w_in",)),
                          after=[delta[rest_mats[-1]], delta["dw_kernel"], delta[VECS[0]]])[1:]
    halves_in, _ = sum_partials("in", ("w_in",), own_in, [chips_in])
    (reduced_in,) = exchange("swap_w_in", SwapHalves(halves_in))
    g["w_in"] = reduced_in.reshape(w["w_in"].shape)
    update("in", ("w_in",), 8)

    return (loss, dx[None], *[g[n] for n in WEIGHTS], *[delta[n] for n in WEIGHTS], *[new_m[n] for n in WEIGHTS],
            *[new_v[n] for n in WEIGHTS])
```

```python
import math

import jax
import jax.numpy as jnp
from jax import lax
from jax.experimental import pallas as pl
from jax.experimental.pallas import tpu as pltpu

F32 = jnp.float32
BF16 = jnp.bfloat16

D = 1024
FF = 4096
NPROJ = 5
KW = 31
KW_PAD = 32
SUBLANES = 8
LANES = 128
HALO = 32
POOL_WINDOWS = (2, 4, 8, 16)
NG = 4
GW = D // NG
RMS_EPS = 1e-6
LN_EPS = 1e-5
LR, B1, B2, ADAM_EPS, WD, STEP = 0.001, 0.9, 0.999, 1e-08, 0.01, 10
NCHIP = 4
VMEM_LIMIT = 60 * 1024 * 1024
MESH = pl.DeviceIdType.MESH
TM = 256
TM_IN = 512

ANY = pl.BlockSpec(memory_space=pl.ANY)
VMEM_SPEC = pl.BlockSpec(memory_space=pltpu.VMEM)
SDS = jax.ShapeDtypeStruct


def _cp(**kw):
    return pltpu.CompilerParams(vmem_limit_bytes=VMEM_LIMIT, **kw)


def _mm(a, b):
    return jnp.dot(a, b, preferred_element_type=F32)


def _mm_nt(a, b):
    return lax.dot_general(a, b, (((1,), (1,)), ((), ())), preferred_element_type=F32)


def _mm_tn(a, b):
    return lax.dot_general(a, b, (((0,), (0,)), ((), ())), preferred_element_type=F32)


def _sigmoid(x):
    return 1.0 / (1.0 + jnp.exp(-x))


def _rowsum(x):
    return jnp.sum(x, axis=0, keepdims=True)


def _full(shape):
    return pl.BlockSpec(shape, lambda i: (0,) * len(shape))


def _tile(tm, cols):
    return pl.BlockSpec((tm, cols), lambda i: (i, 0))


def _prev_halo(tm):
    return pl.BlockSpec((HALO, D), lambda i: (jnp.maximum(i * (tm // HALO) - 1, 0), 0))


def _next_halo(tm, nt):
    return pl.BlockSpec((HALO, D), lambda i: (jnp.minimum((i + 1) * (tm // HALO), nt * (tm // HALO) - 1), 0))


MATS = ("w_in", "w_conv_out", "pool_w", "w_pool_out", "w_o", "w_ff1", "w_ff2")
CIN = NPROJ * D // NCHIP
CFF = FF // NCHIP
_ROWS = lambda k, h: (2 * k + h, 0)
_CHIP_MAJOR = lambda k, h: (k, h, 0)
GEOM = dict(
    w_in=((NCHIP, D, CIN), (None, D // 2, CIN), _CHIP_MAJOR),
    w_conv_out=((D, D), (D // (2 * NCHIP), D), _ROWS),
    pool_w=((NG, GW, GW), (NG // 2, GW // NCHIP, GW), lambda k, h: (h, k, 0)),
    w_pool_out=((D, D), (D // (2 * NCHIP), D), _ROWS),
    w_o=((D, D), (D // (2 * NCHIP), D), _ROWS),
    w_ff1=((NCHIP, D, CFF), (None, D // 2, CFF), _CHIP_MAJOR),
    w_ff2=((FF, D), (FF // (2 * NCHIP), D), _ROWS),
)
DSH = D // NCHIP


def _half_shape(name):
    return tuple(b for b in GEOM[name][1] if b is not None)


def _window(name, k, h):
    _, blk, idx = GEOM[name]
    return tuple(i if b is None else pl.ds(i * b, b) for i, b in zip(idx(k, h), blk))


def _shard_half(name, h):
    n0 = _half_shape(name)[0]
    return (pl.ds(h * n0, n0),) + (slice(None),) * (len(_half_shape(name)) - 1)


def _place():
    x, y, c = lax.axis_index("x"), lax.axis_index("y"), lax.axis_index("c")
    chips = [(1 - x, y), (x, 1 - y), (1 - x, 1 - y)]
    return x, y, c, 2 * x + y, chips, [2 * px + py for px, py in chips]


def _remote(src, dst, send_sem, recv_sem, to):
    return pltpu.make_async_remote_copy(src_ref=src, dst_ref=dst, send_sem=send_sem, recv_sem=recv_sem,
                                        device_id=to, device_id_type=MESH)


class GatherWeights:
    has_mid = True

    def __init__(self, names, fulls, taps=None):
        self.names = names
        self.ins = list(fulls) + ([taps] if taps is not None else [])
        self.has_taps = taps is not None
        self.out_shapes = [SDS(a.shape, a.dtype) for a in self.ins]
        self.aliases = [(i, i) for i in range(len(self.ins))]
        n = len(self.ins)
        self.sems = [pltpu.SemaphoreType.DMA((n, 6)), pltpu.SemaphoreType.DMA((n, 6))]

    def _copies(self, ins, outs, sems):
        send_sems, recv_sems = sems
        x, y, c, me, chips, chip_ids = _place()
        sibling = (x, y, 1 - c)
        ici, ici_recv, d2d, d2d_recv = [], [], [], []
        for w, name in enumerate(self.names):
            for j, chip in enumerate(chips):
                ici.append(_remote(ins[w].at[_window(name, me, c)], outs[w].at[_window(name, me, c)],
                                   send_sems.at[w, j], recv_sems.at[w, j], (*chip, c)))
                got = outs[w].at[_window(name, chip_ids[j], c)]
                ici_recv.append(_remote(got, got, send_sems.at[w, j], recv_sems.at[w, j], sibling))
                d2d.append(_remote(got, got, send_sems.at[w, 3 + j], recv_sems.at[w, 3 + j], sibling))
                got = outs[w].at[_window(name, chip_ids[j], 1 - c)]
                d2d_recv.append(_remote(got, got, send_sems.at[w, 3 + j], recv_sems.at[w, 3 + j], sibling))
        if self.has_taps:
            w = len(self.names)
            for j, chip in enumerate(chips):
                ici.append(_remote(ins[w].at[:, pl.ds(me * DSH, DSH)], outs[w].at[:, pl.ds(me * DSH, DSH)],
                                   send_sems.at[w, j], recv_sems.at[w, j], (*chip, c)))
                got = outs[w].at[:, pl.ds(chip_ids[j] * DSH, DSH)]
                d2d_recv.append(_remote(got, got, send_sems.at[w, j], recv_sems.at[w, j], sibling))
        return ici, ici_recv, d2d, d2d_recv

    def start(self, ins, outs, sems):
        for cp in self._copies(ins, outs, sems)[0]:
            cp.start()

    def mid(self, ins, outs, sems):
        _, ici_recv, d2d, _ = self._copies(ins, outs, sems)
        for got, fwd in zip(ici_recv, d2d):
            got.wait_recv()
            fwd.start()

    def finish(self, ins, outs, sems):
        ici, _, d2d, d2d_recv = self._copies(ins, outs, sems)
        for cp in d2d_recv:
            cp.wait_recv()
        for cp in ici + d2d:
            cp.wait_send()


class ExchangePair:
    has_mid = False
    aliases = ()

    def __init__(self, names, grads):
        self.names, self.ins = names, list(grads)
        self.out_shapes = [SDS((NCHIP, *_half_shape(n)), F32) for n in names]
        self.sems = [pltpu.SemaphoreType.DMA((len(names),)), pltpu.SemaphoreType.DMA((len(names),))]

    def start(self, ins, outs, sems):
        send_sems, recv_sems = sems
        x, y, c, me, chips, chip_ids = _place()
        for w, name in enumerate(self.names):
            for k in range(NCHIP):
                _remote(ins[w].at[_window(name, k, 1 - c)], outs[w].at[k], send_sems.at[w], recv_sems.at[w],
                        (x, y, 1 - c)).start()

    def finish(self, ins, outs, sems):
        send_sems, recv_sems = sems
        x, y, c, me, chips, chip_ids = _place()
        for w in range(len(self.names)):
            _remote(outs[w], outs[w], send_sems.at[w], recv_sems.at[w], (x, y, 1 - c)).wait()


class ExchangeChips:
    has_mid = False
    aliases = ()

    def __init__(self, names, wires):
        self.names, self.ins = names, list(wires)
        self.out_shapes = [SDS((NCHIP - 1, *_half_shape(n)), BF16) for n in names]
        self.sems = [pltpu.SemaphoreType.DMA((len(names), NCHIP - 1)), pltpu.SemaphoreType.DMA((len(names), NCHIP - 1))]

    def _copies(self, ins, outs, sems):
        send_sems, recv_sems = sems
        x, y, c, me, chips, chip_ids = _place()
        return [_remote(ins[w].at[chip_ids[j]], outs[w].at[j], send_sems.at[w, j], recv_sems.at[w, j], (*chip, c))
                for w in range(len(self.names)) for j, chip in enumerate(chips)]

    def start(self, ins, outs, sems):
        for cp in self._copies(ins, outs, sems):
            cp.start()

    def finish(self, ins, outs, sems):
        for cp in self._copies(ins, outs, sems):
            cp.wait()


def _call(body, name, grid, in_specs, out_specs, out_shape, scratch, args, exchanges=(), after=()):
    n_in, n_out, n_scr = len(in_specs), len(out_specs), len(scratch)
    x_in = [a for e in exchanges for a in e.ins]
    x_out = [s for e in exchanges for s in e.out_shapes]
    x_sem = [s for e in exchanges for s in e.sems]
    nsteps = math.prod(grid)

    def wrapped(*refs):
        ins, rest = refs[:n_in], refs[n_in:]
        xin, rest = rest[:len(x_in)], rest[len(x_in) + len(after):]
        outs, rest = rest[:n_out], rest[n_out:]
        xout, rest = rest[:len(x_out)], rest[len(x_out):]
        scr, xsem = rest[:n_scr], rest[n_scr:]
        parts = []
        for e in exchanges:
            parts.append((xin[:len(e.ins)], xout[:len(e.out_shapes)], xsem[:len(e.sems)]))
            xin, xout, xsem = xin[len(e.ins):], xout[len(e.out_shapes):], xsem[len(e.sems):]
        if not grid:
            for e, p in zip(exchanges, parts):
                e.start(*p)
            body(*ins, *outs, *scr)
            for e, p in zip(exchanges, parts):
                if e.has_mid:
                    e.mid(*p)
            for e, p in zip(exchanges, parts):
                e.finish(*p)
            return
        step = 0
        for axis, extent in enumerate(grid):
            step = step * extent + pl.program_id(axis)
        if exchanges:
            @pl.when(step == 0)
            def _():
                for e, p in zip(exchanges, parts):
                    e.start(*p)

        body(*ins, *outs, *scr)
        if any(e.has_mid for e in exchanges):
            @pl.when(step == max(nsteps - 2, 0))
            def _():
                for e, p in zip(exchanges, parts):
                    if e.has_mid:
                        e.mid(*p)

        if exchanges:
            @pl.when(step == nsteps - 1)
            def _():
                for e, p in zip(exchanges, parts):
                    e.finish(*p)

    kw = dict(grid=grid, compiler_params=_cp(dimension_semantics=("arbitrary",) * len(grid))) if grid else dict(
        compiler_params=_cp())
    aliases, i0, o0 = {}, n_in, n_out
    for e in exchanges:
        aliases.update({i0 + i: o0 + o for i, o in e.aliases})
        i0, o0 = i0 + len(e.ins), o0 + len(e.out_shapes)
    res = pl.pallas_call(
        wrapped, name=name, in_specs=list(in_specs) + [ANY] * (len(x_in) + len(after)),
        out_specs=list(out_specs) + [ANY] * len(x_out),
        out_shape=list(out_shape) + x_out, scratch_shapes=list(scratch) + x_sem, input_output_aliases=aliases, **kw,
    )(*args, *x_in, *after)
    outs, rest = res[:n_out], res[n_out:]
    xouts = []
    for e in exchanges:
        xouts.append(rest[:len(e.out_shapes)])
        rest = rest[len(e.out_shapes):]
    return outs, xouts


def exchange(name, ex):
    return _call(lambda: None, name, (), [], [], [], [], [], [ex])[1][0]


def _taps_scratch(tm):
    return pltpu.VMEM((SUBLANES, tm + HALO, LANES), F32)


def _taps(src_ref, k_ref, dst_ref, sh_ref, tm, off0, reverse):
    rc, cw = 64, LANES

    def col_chunk(cc, carry):
        cols = pl.ds(pl.multiple_of(cc * cw, cw), cw)
        for q in range(SUBLANES):
            n = tm + SUBLANES * (len(range(q, KW, SUBLANES)) - 1)
            sh_ref[q, 0:n, :] = src_ref[pl.ds(off0 + q, n), cols]
        for r in range(tm // rc):
            acc = jnp.zeros((rc, cw), F32)
            for q in range(SUBLANES):
                for a, j in enumerate(range(q, KW, SUBLANES)):
                    kj = KW - 1 - j if reverse else j
                    acc = acc + k_ref[kj:kj + 1, cols] * sh_ref[q, pl.ds(r * rc + SUBLANES * a, rc), :]
            dst_ref[pl.ds(r * rc, rc), cols] = acc
        return carry

    lax.fori_loop(0, D // cw, col_chunk, 0)


def _proj_pieces(g):
    lo, hi, pieces = g * D, (g + 1) * D, []
    while lo < hi:
        k = lo // CIN
        b = min(hi - k * CIN, CIN)
        pieces.append((k, lo - k * CIN, b))
        lo = k * CIN + b
    return pieces


def fwd_in(x, g1, w_in, tm, exchanges=()):
    t = x.shape[0]

    def body(x_ref, g_ref, w_ref, u_ref, glu_ref, ag_ref, p_ref, gt_ref):
        xf = x_ref[...]
        r = lax.rsqrt(jnp.mean(xf * xf, axis=-1, keepdims=True) + RMS_EPS)
        u = (xf * r * g_ref[...]).astype(BF16)
        u_ref[...] = u
        proj = lambda g: jnp.concatenate([_mm(u, w_ref[k, :, lo:hi]) for k, lo, hi in _proj_pieces(g)], axis=1)
        a = proj(0)
        gate = proj(1)
        glu_ref[...] = a * _sigmoid(gate)
        ag_ref[:, 0:D] = a.astype(BF16)
        ag_ref[:, D:2 * D] = gate.astype(BF16)
        p_ref[...] = proj(2)
        gt_ref[:, 0:D] = proj(3).astype(BF16)
        gt_ref[:, D:2 * D] = proj(4).astype(BF16)

    return _call(
        body, "fwd_in", (t // tm,),
        [_tile(tm, D), _full((1, D)), _full((NCHIP, D, CIN))],
        [_tile(tm, D), _tile(tm, D), _tile(tm, 2 * D), _tile(tm, D), _tile(tm, 2 * D)],
        [SDS((t, D), BF16), SDS((t, D), F32), SDS((t, 2 * D), BF16), SDS((t, D), F32), SDS((t, 2 * D), BF16)],
        [], [x, g1, w_in], exchanges)


def _pool_inv_count(i, tm, w):
    pos = i * tm + lax.broadcasted_iota(jnp.int32, (tm, 1), 0) + 1
    return 1.0 / jnp.minimum(pos, w).astype(F32)


def _window_sum(src_ref, tmp_ref, cols, tm, w, causal):
    lo, hi = 0, tm + HALO
    cur, span = None, 1
    while span < w:
        new_lo, new_hi = (lo + SUBLANES, hi) if causal else (lo, hi - SUBLANES)
        far = new_lo - span if causal else new_lo + span
        n = new_hi - new_lo
        if cur is None:
            near_v, far_v = src_ref[pl.ds(new_lo, n), cols], src_ref[pl.ds(far, n), cols]
        else:
            near_v = cur[new_lo - lo:new_lo - lo + n]
            if span % SUBLANES == 0:
                far_v = cur[far - lo:far - lo + n]
            else:
                tmp_ref[pl.ds(lo, hi - lo), :] = cur
                far_v = tmp_ref[pl.ds(far, n), :]
        cur, lo, hi, span = near_v + far_v, new_lo, new_hi, 2 * span
    off = HALO if causal else 0
    return cur[off - lo:off - lo + tm]


def fwd_mix(x, glu, p, gt, dwk, dwb, lng, lnb, w_co, pool_w, ps, w_po, w_o, g2, tm, exchanges=()):
    t = x.shape[0]

    def body(x_ref, glu_ref, gluh_ref, p_ref, ph_ref, gt_ref, k_ref, b_ref, lg_ref, lb_ref, wco_ref, pw_ref,
             ps_ref, wpo_ref, wo_ref, g2_ref,
             cv_ref, sw_ref, z_ref, zl_ref, zs_ref, yc_ref, yp_ref, mg_ref, mo_ref, h1_ref, ext_ref, win_ref, sh_ref):
        i = pl.program_id(0)
        keep = (i > 0).astype(F32)
        ext_ref[0:HALO, :] = gluh_ref[...] * keep
        ext_ref[HALO:HALO + tm, :] = glu_ref[...]
        _taps(ext_ref, k_ref, cv_ref, sh_ref, tm, HALO - (KW - 1), False)
        cv = cv_ref[...] + b_ref[...]
        cv_ref[...] = cv
        mu = jnp.mean(cv, axis=-1, keepdims=True)
        cen = cv - mu
        rstd = lax.rsqrt(jnp.mean(cen * cen, axis=-1, keepdims=True) + LN_EPS)
        ln = cen * rstd * lg_ref[...] + lb_ref[...]
        sw = (ln * _sigmoid(ln)).astype(BF16)
        sw_ref[...] = sw
        yc = _mm(sw, wco_ref[...])
        yc_ref[...] = yc.astype(BF16)
        ext_ref[0:HALO, :] = ph_ref[...] * keep
        ext_ref[HALO:HALO + tm, :] = p_ref[...]
        for g, w in enumerate(POOL_WINDOWS):
            cols = pl.ds(g * GW, GW)
            acc = _window_sum(ext_ref, win_ref, cols, tm, w, True)
            zg = (acc * _pool_inv_count(i, tm, w) - p_ref[:, cols]).astype(BF16)
            z_ref[:, cols] = zg
            zl_ref[:, cols] = _mm(zg, pw_ref[g])
        zl = zl_ref[...]
        zs = (zl * ps_ref[...]).astype(BF16)
        zs_ref[...] = zs
        yp = _mm(zs, wpo_ref[...])
        yp_ref[...] = yp.astype(BF16)
        gc = _sigmoid(gt_ref[:, 0:D].astype(F32))
        gp = _sigmoid(gt_ref[:, D:2 * D].astype(F32))
        mg = (gc * yc + gp * yp).astype(BF16)
        mg_ref[...] = mg
        mo = _mm(mg, wo_ref[...])
        mo_ref[...] = mo
        r2 = lax.rsqrt(jnp.mean(mo * mo, axis=-1, keepdims=True) + RMS_EPS)
        h1_ref[...] = x_ref[...] + mo * r2 * g2_ref[...]

    vec = _full((1, D))
    act = lambda dt: SDS((t, D), dt)
    return _call(
        body, "fwd_mix", (t // tm,),
        [_tile(tm, D), _tile(tm, D), _prev_halo(tm), _tile(tm, D), _prev_halo(tm), _tile(tm, 2 * D),
         _full((KW_PAD, D)), vec, vec, vec, _full((D, D)), _full((NG, GW, GW)), vec, _full((D, D)), _full((D, D)), vec],
        [_tile(tm, D)] * 10,
        [act(F32), act(BF16), act(BF16), act(F32), act(BF16), act(BF16), act(BF16), act(BF16), act(F32), act(F32)],
        [pltpu.VMEM((tm + HALO, D), F32), pltpu.VMEM((tm + HALO, GW), F32), _taps_scratch(tm)],
        [x, glu, glu, p, p, gt, dwk, dwb, lng, lnb, w_co, pool_w, ps, w_po, w_o, g2], exchanges)


def mlp_fwd_bwd(h1, tgt, g3, g4, w1, w2, tm, exchanges=()):
    t = h1.shape[0]
    fc = CFF

    def body(h1_ref, tgt_ref, g3_ref, g4_ref, w1_ref, w2_ref,
             v_ref, a2_ref, df2_ref, df1_ref, dh1_ref, vec_ref, f1_ref):
        i = pl.program_id(0)

        @pl.when(i == 0)
        def _():
            vec_ref[...] = jnp.zeros_like(vec_ref)

        h1v = h1_ref[...]
        r3 = lax.rsqrt(jnp.mean(h1v * h1v, axis=-1, keepdims=True) + RMS_EPS)
        n3 = h1v * r3
        v = (n3 * g3_ref[...]).astype(BF16)
        v_ref[...] = v
        f2 = jnp.zeros((tm, D), F32)
        for c in range(FF // fc):
            cols = pl.ds(c * fc, fc)
            f1 = jnp.maximum(_mm(v, w1_ref[c]), 0.0)
            f1_ref[:, cols] = f1
            a2 = (f1 * f1).astype(BF16)
            a2_ref[:, cols] = a2
            f2 = f2 + _mm(a2, w2_ref[cols, :])
        r4 = lax.rsqrt(jnp.mean(f2 * f2, axis=-1, keepdims=True) + RMS_EPS)
        n4 = f2 * r4
        err = h1v + n4 * g4_ref[...] - tgt_ref[...]
        vec_ref[2:3, :] += _rowsum(err * err) * (0.5 / D)
        dh2 = err * (1.0 / D)
        vec_ref[1:2, :] += _rowsum(dh2 * n4)
        dn4 = dh2 * g4_ref[...]
        df2 = (r4 * (dn4 - n4 * jnp.mean(dn4 * n4, axis=-1, keepdims=True))).astype(BF16)
        df2_ref[...] = df2
        dv = jnp.zeros((tm, D), F32)
        for c in range(FF // fc):
            cols = pl.ds(c * fc, fc)
            da2 = _mm_nt(df2, w2_ref[cols, :])
            df1 = (da2 * (2.0 * f1_ref[:, cols])).astype(BF16)
            df1_ref[:, cols] = df1
            dv = dv + _mm_nt(df1, w1_ref[c])
        vec_ref[0:1, :] += _rowsum(dv * n3)
        dn3 = dv * g3_ref[...]
        dh1_ref[...] = dh2 + r3 * (dn3 - n3 * jnp.mean(dn3 * n3, axis=-1, keepdims=True))

    vec = _full((1, D))
    return _call(
        body, "mlp_fwd_bwd", (t // tm,),
        [_tile(tm, D), _tile(tm, D), vec, vec, _full((NCHIP, D, CFF)), _full((FF, D))],
        [_tile(tm, D), _tile(tm, FF), _tile(tm, D), _tile(tm, FF), _tile(tm, D), _full((8, D))],
        [SDS((t, D), BF16), SDS((t, FF), BF16), SDS((t, D), BF16), SDS((t, FF), BF16), SDS((t, D), F32),
         SDS((8, D), F32)],
        [pltpu.VMEM((tm, FF), F32)], [h1, tgt, g3, g4, w1, w2], exchanges)


def bwd_mix(dh1, mo, cv, zl, yc, yp, gt, lng, lnb, ps, g2, w_co, pool_w, w_po, w_o, tm, exchanges=()):
    t = dh1.shape[0]

    def body(dh1_ref, mo_ref, cv_ref, zl_ref, yc_ref, yp_ref, gt_ref, lg_ref, lb_ref, ps_ref, g2_ref,
             wco_ref, pw_ref, wpo_ref, wo_ref,
             dmo_ref, dgt_ref, dyc_ref, dyp_ref, dcv_ref, dzl_ref, dz_ref, vec_ref):
        i = pl.program_id(0)

        @pl.when(i == 0)
        def _():
            vec_ref[...] = jnp.zeros_like(vec_ref)

        dh1v = dh1_ref[...]
        mo = mo_ref[...]
        r2 = lax.rsqrt(jnp.mean(mo * mo, axis=-1, keepdims=True) + RMS_EPS)
        n2 = mo * r2
        vec_ref[0:1, :] += _rowsum(dh1v * n2)
        dn2 = dh1v * g2_ref[...]
        dmo = (r2 * (dn2 - n2 * jnp.mean(dn2 * n2, axis=-1, keepdims=True))).astype(BF16)
        dmo_ref[...] = dmo
        dmg = _mm_nt(dmo, wo_ref[...])
        gc = _sigmoid(gt_ref[:, 0:D].astype(F32))
        gp = _sigmoid(gt_ref[:, D:2 * D].astype(F32))
        dgt_ref[:, 0:D] = (dmg * yc_ref[...].astype(F32) * gc * (1.0 - gc)).astype(BF16)
        dgt_ref[:, D:2 * D] = (dmg * yp_ref[...].astype(F32) * gp * (1.0 - gp)).astype(BF16)
        dyc = (dmg * gc).astype(BF16)
        dyp = (dmg * gp).astype(BF16)
        dyc_ref[...] = dyc
        dyp_ref[...] = dyp
        dsw = _mm_nt(dyc, wco_ref[...])
        cv = cv_ref[...]
        mu = jnp.mean(cv, axis=-1, keepdims=True)
        cen = cv - mu
        rstd = lax.rsqrt(jnp.mean(cen * cen, axis=-1, keepdims=True) + LN_EPS)
        y = cen * rstd
        ln = y * lg_ref[...] + lb_ref[...]
        sg = _sigmoid(ln)
        dln = dsw * (sg * (1.0 + ln * (1.0 - sg)))
        vec_ref[1:2, :] += _rowsum(dln * y)
        vec_ref[2:3, :] += _rowsum(dln)
        dy = dln * lg_ref[...]
        dcv = rstd * (dy - jnp.mean(dy, axis=-1, keepdims=True) - y * jnp.mean(dy * y, axis=-1, keepdims=True))
        dcv_ref[...] = dcv
        vec_ref[3:4, :] += _rowsum(dcv)
        dzs = _mm_nt(dyp, wpo_ref[...])
        vec_ref[4:5, :] += _rowsum(dzs * zl_ref[...])
        dzl = (dzs * ps_ref[...]).astype(BF16)
        dzl_ref[...] = dzl
        for g in range(NG):
            cols = pl.ds(g * GW, GW)
            dz_ref[:, cols] = _mm_nt(dzl_ref[:, cols], pw_ref[g])

    vec = _full((1, D))
    act = lambda dt: SDS((t, D), dt)
    return _call(
        body, "bwd_mix", (t // tm,),
        [_tile(tm, D)] * 6 + [_tile(tm, 2 * D), vec, vec, vec, vec, _full((D, D)), _full((NG, GW, GW)), _full((D, D)),
                              _full((D, D))],
        [_tile(tm, D), _tile(tm, 2 * D)] + [_tile(tm, D)] * 5 + [_full((8, D))],
        [act(BF16), SDS((t, 2 * D), BF16), act(BF16), act(BF16), act(F32), act(BF16), act(F32), SDS((8, D), F32)],
        [], [dh1, mo, cv, zl, yc, yp, gt, lng, lnb, ps, g2, w_co, pool_w, w_po, w_o], exchanges)


def bwd_in(x, dh1, dcv, dz, glu, ag, dgt, g1, dwk, w_in, tm, after=()):
    t = x.shape[0]
    nt = t // tm

    def body(x_ref, dh1_ref, dcv_ref, dcvh_ref, dz_ref, dzh_ref, glu_ref, gluh_ref, ag_ref, dgt_ref, g1_ref,
             k_ref, w_ref, dx_ref, dproj_ref, vec_ref, dk_ref, ext_ref, tmp_ref, win_ref, sh_ref, gext_ref, gsh_ref):
        i = pl.program_id(0)

        @pl.when(i == 0)
        def _():
            vec_ref[...] = jnp.zeros_like(vec_ref)
            dk_ref[...] = jnp.zeros_like(dk_ref)

        first = (i > 0).astype(F32)
        last = (i < nt - 1).astype(F32)
        gext_ref[0:HALO, :] = gluh_ref[...] * first
        gext_ref[HALO:HALO + tm, :] = glu_ref[...]
        rc, cw = 32, LANES

        def dk_chunk(cc):
            cols = pl.ds(cc * cw, cw)
            for q in range(SUBLANES):
                taps = range(q, KW, SUBLANES)
                n = tm + SUBLANES * (len(taps) - 1)
                gsh_ref[q, 0:n, :] = gext_ref[pl.ds(HALO - (KW - 1) + q, n), cols]
                accs = [jnp.zeros((SUBLANES, cw), F32) for _ in taps]
                for r in range(tm // rc):
                    dchunk = dcv_ref[pl.ds(r * rc, rc), cols]
                    for a in range(len(taps)):
                        prod = dchunk * gsh_ref[q, pl.ds(r * rc + SUBLANES * a, rc), :]
                        accs[a] = accs[a] + jnp.sum(prod.reshape(rc // SUBLANES, SUBLANES, cw), axis=0)
                for a, j in enumerate(taps):
                    dk_ref[j:j + 1, cols] += _rowsum(accs[a])

        ext_ref[0:tm, :] = dcv_ref[...]
        ext_ref[tm:tm + HALO, :] = dcvh_ref[...] * last
        _taps(ext_ref, k_ref, tmp_ref, sh_ref, tm, 0, True)
        dglu = tmp_ref[...]
        a = ag_ref[:, 0:D].astype(F32)
        sg = _sigmoid(ag_ref[:, D:2 * D].astype(F32))
        dproj_ref[:, 0:D] = (dglu * sg).astype(BF16)
        dproj_ref[:, D:2 * D] = (dglu * a * sg * (1.0 - sg)).astype(BF16)
        for g, w in enumerate(POOL_WINDOWS):
            cols = pl.ds(g * GW, GW)
            pos = i * tm + lax.broadcasted_iota(jnp.int32, (tm + HALO, 1), 0) + 1
            inv = 1.0 / jnp.minimum(pos, w).astype(F32)
            ext_ref[0:tm, cols] = dz_ref[:, cols] * inv[0:tm]
            ext_ref[tm:tm + HALO, cols] = dzh_ref[:, cols] * inv[tm:tm + HALO] * last
            acc = _window_sum(ext_ref, win_ref, cols, tm, w, False)
            dproj_ref[:, pl.ds(2 * D + g * GW, GW)] = (acc - dz_ref[:, cols]).astype(BF16)
        dproj_ref[:, 3 * D:5 * D] = dgt_ref[...]
        du = jnp.zeros((tm, D), F32)
        chunks_per_matmul = D // cw // NCHIP
        for k in range(NCHIP):
            du = du + _mm_nt(dproj_ref[:, k * CIN:(k + 1) * CIN], w_ref[k])
            for cc in range(k * chunks_per_matmul, (k + 1) * chunks_per_matmul):
                dk_chunk(cc)
        xf = x_ref[...]
        r1 = lax.rsqrt(jnp.mean(xf * xf, axis=-1, keepdims=True) + RMS_EPS)
        n1 = xf * r1
        vec_ref[0:1, :] += _rowsum(du * n1)
        dn1 = du * g1_ref[...]
        dx_ref[...] = dh1_ref[...] + r1 * (dn1 - n1 * jnp.mean(dn1 * n1, axis=-1, keepdims=True))

    return _call(
        body, "bwd_in", (nt,),
        [_tile(tm, D), _tile(tm, D), _tile(tm, D), _next_halo(tm, nt), _tile(tm, D), _next_halo(tm, nt),
         _tile(tm, D), _prev_halo(tm), _tile(tm, 2 * D), _tile(tm, 2 * D), _full((1, D)),
         _full((KW_PAD, D)), _full((NCHIP, D, CIN))],
        [_tile(tm, D), _tile(tm, NPROJ * D), _full((8, D)), _full((KW_PAD, D))],
        [SDS((t, D), F32), SDS((t, NPROJ * D), BF16), SDS((8, D), F32), SDS((KW_PAD, D), F32)],
        [pltpu.VMEM((tm + HALO, D), F32), pltpu.VMEM((tm, D), F32), pltpu.VMEM((tm + HALO, GW), F32),
         _taps_scratch(tm), pltpu.VMEM((tm + HALO, D), F32), _taps_scratch(tm)],
        [x, dh1, dcv, dcv, dz, dz, glu, glu, ag, dgt, g1, dwk, w_in], after=after)[0]


def wgrad(a, b, name, bm=1024, bn=1024, bt=2048, chip_major=False, exchanges=()):
    t, m = a.shape
    n = b.shape[1]
    bm, bn, bt = min(bm, m), min(bn, n), min(bt, t)
    assert m % bm == 0 and n % bn == 0 and t % bt == 0, (a.shape, b.shape, bm, bn, bt)

    def body(a_ref, b_ref, o_ref):
        k = pl.program_id(2)

        @pl.when(k == 0)
        def _():
            o_ref[...] = jnp.zeros_like(o_ref)

        o_ref[...] += _mm_tn(a_ref[...], b_ref[...])

    if chip_major:
        out_spec, out_shape = pl.BlockSpec((None, bm, bn), lambda i, j, k: (j, i, 0)), SDS((n // bn, m, bn), F32)
    else:
        out_spec, out_shape = pl.BlockSpec((bm, bn), lambda i, j, k: (i, j)), SDS((m, n), F32)
    outs, xouts = _call(
        body, name, (m // bm, n // bn, t // bt),
        [pl.BlockSpec((bt, bm), lambda i, j, k: (k, i)), pl.BlockSpec((bt, bn), lambda i, j, k: (k, j))],
        [out_spec], [out_shape], [], [a, b], exchanges)
    return outs[0], xouts


def wgrad_pool(z, dzl, bt=1024):
    t = z.shape[0]
    bt = min(bt, t)
    assert t % bt == 0

    def body(a_ref, b_ref, o_ref):
        k = pl.program_id(1)

        @pl.when(k == 0)
        def _():
            o_ref[...] = jnp.zeros_like(o_ref)

        o_ref[0] += _mm_tn(a_ref[...], b_ref[...])

    return _call(
        body, "wgrad_pool", (NG, t // bt),
        [pl.BlockSpec((bt, GW), lambda g, k: (k, g)), pl.BlockSpec((bt, GW), lambda g, k: (k, g))],
        [pl.BlockSpec((1, GW, GW), lambda g, k: (g, 0, 0))], [SDS((NG, GW, GW), F32)], [], [z, dzl])[0][0]


def cast_shards(name, names, shards, exchanges=()):
    n = len(shards)

    def body(*refs):
        srcs, dsts, bufs, sems = refs[:n], refs[n:2 * n], refs[2 * n:3 * n], refs[3 * n]
        me = 2 * lax.axis_index("x") + lax.axis_index("y")
        copies = []
        for w, mat in enumerate(names):
            bufs[w][...] = srcs[w][...].astype(BF16)
            for h in range(2):
                cp = pltpu.make_async_copy(bufs[w].at[_shard_half(mat, h)], dsts[w].at[_window(mat, me, h)], sems.at[w, h])
                cp.start()
                copies.append(cp)
        for cp in copies:
            cp.wait()

    return _call(body, name, (), [VMEM_SPEC] * n, [ANY] * n, [SDS(GEOM[mat][0], BF16) for mat in names],
                 [pltpu.VMEM(s.shape, BF16) for s in shards] + [pltpu.SemaphoreType.DMA((n, 2))], list(shards), exchanges)


VEC_ROWS = 24
ROW = dict(mlp_pre_g=0, mlp_post_g=1, loss=2, mix_post_g=8, conv_ln_g=9, conv_ln_b=10, dw_bias=11, pool_scale=12,
           mix_pre_g=16)


NDEV = 8


class GatherSmall:
    has_mid = False
    aliases = ()

    def __init__(self, part):
        self.ins = [part]
        self.out_shapes = [SDS((NDEV, *part.shape), F32)]
        self.sems = [pltpu.SemaphoreType.DMA((NDEV,)), pltpu.SemaphoreType.DMA((NDEV,))]

    def _copies(self, ins, outs, sems):
        send_sems, recv_sems = sems
        x, y, c = lax.axis_index("x"), lax.axis_index("y"), lax.axis_index("c")
        me = 4 * x + 2 * y + c
        own = pltpu.make_async_copy(ins[0], outs[0].at[me], send_sems.at[0])
        sends = [_remote(ins[0], outs[0].at[me], send_sems.at[m], recv_sems.at[m],
                         (x ^ (m >> 2), y ^ ((m >> 1) & 1), c ^ (m & 1))) for m in range(1, NDEV)]
        recvs = [_remote(outs[0].at[me ^ m], outs[0].at[me ^ m], send_sems.at[m], recv_sems.at[m], (x, y, c))
                 for m in range(1, NDEV)]
        return own, sends, recvs

    def start(self, ins, outs, sems):
        own, sends, _ = self._copies(ins, outs, sems)
        for cp in sends + [own]:
            cp.start()

    def finish(self, ins, outs, sems):
        own, sends, recvs = self._copies(ins, outs, sems)
        for cp in recvs:
            cp.wait_recv()
        for cp in sends:
            cp.wait_send()
        own.wait()


def sum_small(parts, after=()):
    def body(parts_ref, sum_ref, loss_ref):
        total = parts_ref[0]
        for d in range(1, NDEV):
            total = total + parts_ref[d]
        sum_ref[...] = total
        r = ROW["loss"]
        loss_ref[...] = jnp.zeros_like(loss_ref) + jnp.sum(total[r:r + 1, :])

    return _call(body, "sum_small", (), [VMEM_SPEC], [VMEM_SPEC, VMEM_SPEC],
                 [SDS(parts.shape[1:], F32), SDS((8, 128), F32)], [], [parts], after=after)[0]


def pair_add(tag, names, grads, from_sibling, exchanges=()):
    n = len(names)
    in_specs, wire_specs, own_specs, wire_shapes, own_shapes = [], [], [], [], []
    for name in names:
        _, gblk, idx = GEOM[name]
        blk = _half_shape(name)
        zeros = (0,) * len(blk)
        in_specs.append(pl.BlockSpec(gblk, lambda k, idx=idx: idx(k, lax.axis_index("c"))))
        wire_specs.append(pl.BlockSpec((1, *blk), lambda k, zeros=zeros: (k, *zeros)))
        own_specs.append(pl.BlockSpec(blk, lambda k, zeros=zeros: zeros))
        wire_shapes.append(SDS((NCHIP, *blk), BF16))
        own_shapes.append(SDS(blk, F32))

    def body(*refs):
        g, s, wire, own = refs[:n], refs[n:2 * n], refs[2 * n:3 * n], refs[3 * n:]
        mine = pl.program_id(0) == 2 * lax.axis_index("x") + lax.axis_index("y")
        for w in range(n):
            total = g[w][...] + s[w][0]
            wire[w][0] = total.astype(BF16)

            @pl.when(mine)
            def _(w=w, total=total):
                own[w][...] = total

    outs, xouts = _call(body, "pair_add_" + tag, (NCHIP,), in_specs + wire_specs, wire_specs + own_specs,
                        wire_shapes + own_shapes, [], list(grads) + list(from_sibling), exchanges)
    return outs[:n], outs[n:], xouts


SUM_STEPS = 4


def sum_partials(tag, names, owns, from_chips, exchanges=()):
    n = len(names)
    own_specs, part_specs, out_specs, out_shapes, part_args, counts = [], [], [], [], [], []
    for name, parts in zip(names, from_chips):
        half = _half_shape(name)
        blk = half[:-2] + (half[-2] // SUM_STEPS, half[-1])
        lead = (0,) * (len(half) - 2)
        own_specs.append(pl.BlockSpec(blk, lambda i, lead=lead: (*lead, i, 0)))
        for p in parts:
            part_specs.append(pl.BlockSpec((p.shape[0], *blk), lambda i, lead=lead: (0, *lead, i, 0)))
            part_args.append(p)
        counts.append(len(parts))
        out_specs.append(pl.BlockSpec((2, *blk), lambda i, lead=lead: (0, *lead, i, 0)))
        out_shapes.append(SDS((2, *half), F32))

    def body(*refs):
        own, parts, out = refs[:n], list(refs[n:n + len(part_args)]), refs[n + len(part_args):]
        c = lax.axis_index("c")
        for w in range(n):
            total = own[w][...]
            for p in [parts.pop(0) for _ in range(counts[w])]:
                for j in range(p.shape[0]):
                    total = total + p[j].astype(F32)
            out[w][c] = total

    return _call(body, "sum_partials_" + tag, (SUM_STEPS,), own_specs + part_specs, out_specs, out_shapes, [],
                 list(owns) + part_args, exchanges)


class SwapHalves:
    has_mid = False

    def __init__(self, halves):
        self.ins = list(halves)
        self.out_shapes = [SDS(h.shape, h.dtype) for h in halves]
        self.aliases = [(i, i) for i in range(len(halves))]
        self.sems = [pltpu.SemaphoreType.DMA((len(halves),)), pltpu.SemaphoreType.DMA((len(halves),))]

    def start(self, ins, outs, sems):
        send_sems, recv_sems = sems
        x, y, c = lax.axis_index("x"), lax.axis_index("y"), lax.axis_index("c")
        for w in range(len(self.ins)):
            _remote(ins[w].at[c], outs[w].at[c], send_sems.at[w], recv_sems.at[w], (x, y, 1 - c)).start()

    def finish(self, ins, outs, sems):
        send_sems, recv_sems = sems
        x, y, c = lax.axis_index("x"), lax.axis_index("y"), lax.axis_index("c")
        for w in range(len(self.ins)):
            _remote(ins[w].at[c], outs[w].at[c], send_sems.at[w], recv_sems.at[w], (x, y, 1 - c)).wait_send()
            _remote(ins[w].at[1 - c], outs[w].at[1 - c], send_sems.at[w], recv_sems.at[w], (x, y, 1 - c)).wait_recv()


SEM_SPEC = pl.BlockSpec(memory_space=pltpu.SEMAPHORE)
HBM_SPEC = pl.BlockSpec(memory_space=pltpu.HBM)
DATAFLOW = pltpu.SideEffectType.DATAFLOW_SIDE_EFFECTING


def plan_chips(names):
    def plan(refs):
        n = len(names)
        x, y, c, me, chips, chip_ids = _place()
        return [(refs[w].at[chip_ids[j]], refs[n + w].at[j], (*chip, c)) for w in range(n) for j, chip in enumerate(chips)]
    return plan


def plan_pair(names):
    def plan(refs):
        n = len(names)
        x, y, c, me, chips, chip_ids = _place()
        return [(refs[w].at[_window(name, k, 1 - c)], refs[n + w].at[k], (x, y, 1 - c))
                for w, name in enumerate(names) for k in range(NCHIP)]
    return plan


def split_start(tag, arrays, plan, ncopies):
    n = len(arrays)

    def body(*refs):
        sems, token = refs[n:n + 2 * ncopies], refs[-1]
        for s, (src, dst, to) in enumerate(plan(refs[:n])):
            _remote(src, dst, sems[2 * s], sems[2 * s + 1], to).start()
        token[...] = jnp.zeros_like(token)

    res = pl.pallas_call(
        body, name="start_" + tag, in_specs=[HBM_SPEC] * n,
        out_specs=[SEM_SPEC] * (2 * ncopies) + [HBM_SPEC] * n + [VMEM_SPEC],
        out_shape=[pltpu.SemaphoreType.DMA(())] * (2 * ncopies) + [pltpu.HBM(a.shape, a.dtype) for a in arrays]
        + [SDS((8, 128), F32)],
        input_output_aliases={i: 2 * ncopies + i for i in range(n)},
        compiler_params=pltpu.CompilerParams(has_side_effects=DATAFLOW),
    )(*arrays)
    return (res[:2 * ncopies], res[2 * ncopies:-1]), res[-1]


def split_wait(tag, started, plan, after):
    sems, arrays = started
    n = len(arrays)

    def body(*refs):
        sem = refs[n:n + len(sems)]
        for s, (src, dst, to) in enumerate(plan(refs[:n])):
            cp = _remote(src, dst, sem[2 * s], sem[2 * s + 1], to)
            cp.wait_send()
            cp.wait_recv()

    return pl.pallas_call(
        body, name="wait_" + tag, in_specs=[HBM_SPEC] * n + [SEM_SPEC] * len(sems) + [ANY] * len(after),
        out_specs=[HBM_SPEC] * n, out_shape=[pltpu.HBM(a.shape, a.dtype) for a in arrays],
        input_output_aliases={i: i for i in range(n)},
        compiler_params=pltpu.CompilerParams(has_side_effects=DATAFLOW),
    )(*arrays, *sems, *after)


def adamw(tag, ws, gs, ms, vs, steps, after=()):
    n = len(ws)
    specs = [pl.BlockSpec((a.shape[0] // steps, a.shape[1]), lambda i: (i, 0)) for a in ws]
    assert all(a.shape[0] % (steps * SUBLANES) == 0 for a in ws), [a.shape for a in ws]

    def body(*refs):
        w_, g_, m_, v_ = refs[:n], refs[n:2 * n], refs[2 * n:3 * n], refs[3 * n:4 * n]
        d_, nm_, nv_ = refs[4 * n:5 * n], refs[5 * n:6 * n], refs[6 * n:]
        for i in range(n):
            gv = g_[i][...]
            mn = B1 * m_[i][...] + (1.0 - B1) * gv
            vn = B2 * v_[i][...] + (1.0 - B2) * (gv * gv)
            m_hat = mn / (1.0 - B1 ** STEP)
            v_hat = vn / (1.0 - B2 ** STEP)
            d_[i][...] = -LR * (m_hat / (jnp.sqrt(v_hat) + ADAM_EPS) + WD * w_[i][...])
            nm_[i][...] = mn
            nv_[i][...] = vn

    outs, _ = _call(body, "adamw_" + tag, (steps,), specs * 4, specs * 3, [SDS(a.shape, F32) for a in ws] * 3, [],
                    list(ws) + list(gs) + list(ms) + list(vs), after=after)
    return list(zip(outs[:n], outs[n:2 * n], outs[2 * n:]))


VECS = ("mix_pre_g", "dw_bias", "conv_ln_g", "conv_ln_b", "pool_scale", "mix_post_g", "mlp_pre_g", "mlp_post_g")
WEIGHTS = ("mix_pre_g", "w_in", "dw_kernel", "dw_bias", "conv_ln_g", "conv_ln_b", "w_conv_out", "pool_w", "pool_scale",
           "w_pool_out", "w_o", "mix_post_g", "mlp_pre_g", "w_ff1", "w_ff2", "mlp_post_g")
MIX_MATS = ("w_conv_out", "pool_w", "w_pool_out", "w_o")
FF_MATS = ("w_ff1", "w_ff2")


def kernel(x, mix_pre_g, w_in, dw_kernel, dw_bias, conv_ln_g, conv_ln_b, w_conv_out, pool_w, pool_scale, w_pool_out, w_o, mix_post_g, mlp_pre_g, w_ff1, w_ff2, mlp_post_g, loss_target, m_mix_pre_g, m_w_in, m_dw_kernel, m_dw_bias, m_conv_ln_g, m_conv_ln_b, m_w_conv_out, m_pool_w, m_pool_scale, m_w_pool_out, m_w_o, m_mix_post_g, m_mlp_pre_g, m_w_ff1, m_w_ff2, m_mlp_post_g, v_mix_pre_g, v_w_in, v_dw_kernel, v_dw_bias, v_conv_ln_g, v_conv_ln_b, v_w_conv_out, v_pool_w, v_pool_scale, v_w_pool_out, v_w_o, v_mix_post_g, v_mlp_pre_g, v_w_ff1, v_w_ff2, v_mlp_post_g):
    w = dict(mix_pre_g=mix_pre_g, w_in=w_in, dw_kernel=dw_kernel, dw_bias=dw_bias, conv_ln_g=conv_ln_g,
             conv_ln_b=conv_ln_b, w_conv_out=w_conv_out, pool_w=pool_w, pool_scale=pool_scale, w_pool_out=w_pool_out,
             w_o=w_o, mix_post_g=mix_post_g, mlp_pre_g=mlp_pre_g, w_ff1=w_ff1, w_ff2=w_ff2, mlp_post_g=mlp_post_g)
    m = dict(mix_pre_g=m_mix_pre_g, w_in=m_w_in, dw_kernel=m_dw_kernel, dw_bias=m_dw_bias, conv_ln_g=m_conv_ln_g,
             conv_ln_b=m_conv_ln_b, w_conv_out=m_w_conv_out, pool_w=m_pool_w, pool_scale=m_pool_scale,
             w_pool_out=m_w_pool_out, w_o=m_w_o, mix_post_g=m_mix_post_g, mlp_pre_g=m_mlp_pre_g, w_ff1=m_w_ff1,
             w_ff2=m_w_ff2, mlp_post_g=m_mlp_post_g)
    v = dict(mix_pre_g=v_mix_pre_g, w_in=v_w_in, dw_kernel=v_dw_kernel, dw_bias=v_dw_bias, conv_ln_g=v_conv_ln_g,
             conv_ln_b=v_conv_ln_b, w_conv_out=v_w_conv_out, pool_w=v_pool_w, pool_scale=v_pool_scale,
             w_pool_out=v_w_pool_out, w_o=v_w_o, mix_post_g=v_mix_post_g, mlp_pre_g=v_mlp_pre_g, w_ff1=v_w_ff1,
             w_ff2=v_w_ff2, mlp_post_g=v_mlp_post_g)
    chip = 2 * lax.axis_index("x") + lax.axis_index("y")
    xs, tgt = x[0], loss_target[0]
    vecs = {name: w[name].reshape(1, D) for name in VECS}

    taps = lax.dynamic_update_slice(jnp.zeros((KW_PAD, D), F32), dw_kernel, (0, chip * DSH))
    mine, full = {}, {}
    (mine["w_in"],), _ = cast_shards("cast_w_in", ("w_in",), [w["w_in"]])
    rest = MATS[1:]
    cast, ((full["w_in"], dwk),) = cast_shards(
        "cast_rest", rest, [w[name] for name in rest], [GatherWeights(("w_in",), [mine["w_in"]], taps)])
    mine.update(zip(rest, cast))
    (u, glu, ag, p, gt), (got,) = fwd_in(
        xs, vecs["mix_pre_g"], full["w_in"], TM_IN, [GatherWeights(MIX_MATS, [mine[n] for n in MIX_MATS])])
    full.update(zip(MIX_MATS, got))
    (cv, sw, z, zl, zs, yc, yp, mg, mo, h1), (got,) = fwd_mix(
        xs, glu, p, gt, dwk, vecs["dw_bias"], vecs["conv_ln_g"], vecs["conv_ln_b"], full["w_conv_out"], full["pool_w"],
        vecs["pool_scale"], full["w_pool_out"], full["w_o"], vecs["mix_post_g"], TM,
        [GatherWeights(FF_MATS, [mine[n] for n in FF_MATS])])
    full.update(zip(FF_MATS, got))

    (v_, a2, df2, df1, dh1, vec_mlp), _ = mlp_fwd_bwd(
        h1, tgt, vecs["mlp_pre_g"], vecs["mlp_post_g"], full["w_ff1"], full["w_ff2"], TM)
    grads, g, delta, new_m, new_v = {}, {}, {}, {}, {}
    rest_mats = FF_MATS + MIX_MATS
    grads["w_ff1"], _ = wgrad(v_, df1, "wgrad_ff1", bn=CFF, chip_major=True)
    grads["w_ff2"], _ = wgrad(a2, df2, "wgrad_ff2")
    landing = lambda names, slots, dt: [lax.empty((slots, *_half_shape(n)), dt) for n in names]
    grads_ff = [grads[n] for n in FF_MATS]
    (dmo, dgt, dyc, dyp, dcv, dzl, dz, vec_mix), (from_sibling,) = bwd_mix(
        dh1, mo, cv, zl, yc, yp, gt, vecs["conv_ln_g"], vecs["conv_ln_b"], vecs["pool_scale"], vecs["mix_post_g"],
        full["w_conv_out"], full["pool_w"], full["w_pool_out"], full["w_o"], TM,
        exchanges=[ExchangePair(FF_MATS, grads_ff)])
    grads["w_conv_out"], _ = wgrad(sw, dyc, "wgrad_conv_out", bt=1024)
    grads["pool_w"] = wgrad_pool(z, dzl)
    grads["w_pool_out"], _ = wgrad(zs, dyp, "wgrad_pool_out", bt=1024)
    grads["w_o"], _ = wgrad(mg, dmo, "wgrad_o", bt=1024)
    wire_ff, own_ff, (from_sibling,) = pair_add(
        "ff", FF_MATS, grads_ff, from_sibling, exchanges=[ExchangePair(MIX_MATS, [grads[n] for n in MIX_MATS])])
    wire_mix, own_mix, _ = pair_add("mix", MIX_MATS, [grads[n] for n in MIX_MATS], from_sibling)
    chips_rest, token = split_start("chips_rest", list(wire_ff) + list(wire_mix) + landing(rest_mats, NCHIP - 1, BF16),
                                    plan_chips(rest_mats), len(rest_mats) * (NCHIP - 1))
    dx, dproj, vec_in, dk = bwd_in(
        xs, dh1, dcv, dz, glu, ag, dgt, vecs["mix_pre_g"], dwk, full["w_in"], TM, after=[token])
    small_part = jnp.concatenate([vec_mlp, vec_mix, vec_in, dk], axis=0)
    grads["w_in"], ((small_parts,),) = wgrad(
        u, dproj, "wgrad_in", bn=CIN, chip_major=True, exchanges=[GatherSmall(small_part)])
    got = split_wait("chips_rest", chips_rest, plan_chips(rest_mats), after=[grads["w_in"]])[len(rest_mats):]
    chips_ff, chips_mix = got[:len(FF_MATS)], got[len(FF_MATS):]
    halves_rest, (from_sibling,) = sum_partials(
        "rest", rest_mats, list(own_ff) + list(own_mix), [[a] for a in list(chips_ff) + list(chips_mix)],
        [ExchangePair(("w_in",), [grads["w_in"]])])
    wire_in, own_in, (reduced_rest,) = pair_add(
        "in", ("w_in",), [grads["w_in"]], from_sibling, exchanges=[SwapHalves(halves_rest)])
    for n, red in zip(rest_mats, reduced_rest):
        g[n] = red.reshape(w[n].shape)

    def update(tag, names, steps, after=()):
        two_d = lambda a: a.reshape(-1, a.shape[-1])
        res = adamw(tag, [two_d(w[n]) for n in names], [two_d(g[n]) for n in names],
                    [two_d(m[n]) for n in names], [two_d(v[n]) for n in names], steps, after)
        for n, (d_, m_, v_new) in zip(names, res):
            delta[n], new_m[n], new_v[n] = d_.reshape(w[n].shape), m_.reshape(w[n].shape), v_new.reshape(w[n].shape)

    chips_in, token = split_start("chips_in", list(wire_in) + landing(("w_in",), NCHIP - 1, BF16),
                                  plan_chips(("w_in",)), NCHIP - 1)
    small, loss8 = sum_small(small_parts, after=[token])
    loss = loss8[0, 0]
    for name in VECS:
        g[name] = small[ROW[name]]
    g["dw_kernel"] = lax.dynamic_slice(small[VEC_ROWS:VEC_ROWS + KW_PAD], (0, chip * DSH), (KW_PAD, DSH))
    update("rest", rest_mats, 8, after=[token])
    stack = lambda d: jnp.concatenate([d[name].reshape(1, D) for name in VECS], axis=0)
    (res,) = adamw("vectors", [stack(w)], [stack(g)], [stack(m)], [stack(v)], 1, after=[token])
    for i, name in enumerate(VECS):
        delta[name], new_m[name], new_v[name] = [r[i] for r in res]
    padk = lambda a: jnp.pad(a, ((0, KW_PAD - KW), (0, 0)))
    (res,) = adamw("dw_kernel", [padk(w["dw_kernel"])], [g["dw_kernel"]], [padk(m["dw_kernel"])],
                   [padk(v["dw_kernel"])], 1, after=[token])
    delta["dw_kernel"], new_m["dw_kernel"], new_v["dw_kernel"] = [r[:KW] for r in res]
    g["dw_kernel"] = g["dw_kernel"][:KW]
    chips_in = split_wait("chips_in", chips_in, plan_chips(("w_in",)),
                          after=[delta[rest_mats[-1]], delta["dw_kernel"], delta[VECS[0]]])[1:]
    halves_in, _ = sum_partials("in", ("w_in",), own_in, [chips_in])
    (reduced_in,) = exchange("swap_w_in", SwapHalves(halves_in))
    g["w_in"] = reduced_in.reshape(w["w_in"].shape)
    update("in", ("w_in",), 8)

    return (loss, dx[None], *[g[n] for n in WEIGHTS], *[delta[n] for n in WEIGHTS], *[new_m[n] for n in WEIGHTS],
            *[new_v[n] for n in WEIGHTS])
```

```python
import math

import jax
import jax.numpy as jnp
from jax import lax
from jax.experimental import pallas as pl
from jax.experimental.pallas import tpu as pltpu

F32 = jnp.float32
BF16 = jnp.bfloat16

D = 1024
FF = 4096
NPROJ = 5
KW = 31
KW_PAD = 32
SUBLANES = 8
LANES = 128
HALO = 32
POOL_WINDOWS = (2, 4, 8, 16)
NG = 4
GW = D // NG
RMS_EPS = 1e-6
LN_EPS = 1e-5
LR, B1, B2, ADAM_EPS, WD, STEP = 0.001, 0.9, 0.999, 1e-08, 0.01, 10
NCHIP = 4
VMEM_LIMIT = 60 * 1024 * 1024
MESH = pl.DeviceIdType.MESH
TM = 256
TM_IN = 512

ANY = pl.BlockSpec(memory_space=pl.ANY)
VMEM_SPEC = pl.BlockSpec(memory_space=pltpu.VMEM)
SDS = jax.ShapeDtypeStruct


def _cp(**kw):
    return pltpu.CompilerParams(vmem_limit_bytes=VMEM_LIMIT, **kw)


def _mm(a, b):
    return jnp.dot(a, b, preferred_element_type=F32)


def _mm_nt(a, b):
    return lax.dot_general(a, b, (((1,), (1,)), ((), ())), preferred_element_type=F32)


def _mm_tn(a, b):
    return lax.dot_general(a, b, (((0,), (0,)), ((), ())), preferred_element_type=F32)


def _sigmoid(x):
    return 1.0 / (1.0 + jnp.exp(-x))


def _rowsum(x):
    return jnp.sum(x, axis=0, keepdims=True)


def _full(shape):
    return pl.BlockSpec(shape, lambda i: (0,) * len(shape))


def _tile(tm, cols):
    return pl.BlockSpec((tm, cols), lambda i: (i, 0))


def _prev_halo(tm):
    return pl.BlockSpec((HALO, D), lambda i: (jnp.maximum(i * (tm // HALO) - 1, 0), 0))


def _next_halo(tm, nt):
    return pl.BlockSpec((HALO, D), lambda i: (jnp.minimum((i + 1) * (tm // HALO), nt * (tm // HALO) - 1), 0))


MATS = ("w_in", "w_conv_out", "pool_w", "w_pool_out", "w_o", "w_ff1", "w_ff2")
CIN = NPROJ * D // NCHIP
CFF = FF // NCHIP
_ROWS = lambda k, h: (2 * k + h, 0)
_CHIP_MAJOR = lambda k, h: (k, h, 0)
GEOM = dict(
    w_in=((NCHIP, D, CIN), (None, D // 2, CIN), _CHIP_MAJOR),
    w_conv_out=((D, D), (D // (2 * NCHIP), D), _ROWS),
    pool_w=((NG, GW, GW), (NG // 2, GW // NCHIP, GW), lambda k, h: (h, k, 0)),
    w_pool_out=((D, D), (D // (2 * NCHIP), D), _ROWS),
    w_o=((D, D), (D // (2 * NCHIP), D), _ROWS),
    w_ff1=((NCHIP, D, CFF), (None, D // 2, CFF), _CHIP_MAJOR),
    w_ff2=((FF, D), (FF // (2 * NCHIP), D), _ROWS),
)
DSH = D // NCHIP


def _half_shape(name):
    return tuple(b for b in GEOM[name][1] if b is not None)


def _window(name, k, h):
    _, blk, idx = GEOM[name]
    return tuple(i if b is None else pl.ds(i * b, b) for i, b in zip(idx(k, h), blk))


def _shard_half(name, h):
    n0 = _half_shape(name)[0]
    return (pl.ds(h * n0, n0),) + (slice(None),) * (len(_half_shape(name)) - 1)


def _place():
    x, y, c = lax.axis_index("x"), lax.axis_index("y"), lax.axis_index("c")
    chips = [(1 - x, y), (x, 1 - y), (1 - x, 1 - y)]
    return x, y, c, 2 * x + y, chips, [2 * px + py for px, py in chips]


def _remote(src, dst, send_sem, recv_sem, to):
    return pltpu.make_async_remote_copy(src_ref=src, dst_ref=dst, send_sem=send_sem, recv_sem=recv_sem,
                                        device_id=to, device_id_type=MESH)


class GatherWeights:
    has_mid = True

    def __init__(self, names, fulls, taps=None):
        self.names = names
        self.ins = list(fulls) + ([taps] if taps is not None else [])
        self.has_taps = taps is not None
        self.out_shapes = [SDS(a.shape, a.dtype) for a in self.ins]
        self.aliases = [(i, i) for i in range(len(self.ins))]
        n = len(self.ins)
        self.sems = [pltpu.SemaphoreType.DMA((n, 6)), pltpu.SemaphoreType.DMA((n, 6))]

    def _copies(self, ins, outs, sems):
        send_sems, recv_sems = sems
        x, y, c, me, chips, chip_ids = _place()
        sibling = (x, y, 1 - c)
        ici, ici_recv, d2d, d2d_recv = [], [], [], []
        for w, name in enumerate(self.names):
            for j, chip in enumerate(chips):
                ici.append(_remote(ins[w].at[_window(name, me, c)], outs[w].at[_window(name, me, c)],
                                   send_sems.at[w, j], recv_sems.at[w, j], (*chip, c)))
                got = outs[w].at[_window(name, chip_ids[j], c)]
                ici_recv.append(_remote(got, got, send_sems.at[w, j], recv_sems.at[w, j], sibling))
                d2d.append(_remote(got, got, send_sems.at[w, 3 + j], recv_sems.at[w, 3 + j], sibling))
                got = outs[w].at[_window(name, chip_ids[j], 1 - c)]
                d2d_recv.append(_remote(got, got, send_sems.at[w, 3 + j], recv_sems.at[w, 3 + j], sibling))
        if self.has_taps:
            w = len(self.names)
            for j, chip in enumerate(chips):
                ici.append(_remote(ins[w].at[:, pl.ds(me * DSH, DSH)], outs[w].at[:, pl.ds(me * DSH, DSH)],
                                   send_sems.at[w, j], recv_sems.at[w, j], (*chip, c)))
                got = outs[w].at[:, pl.ds(chip_ids[j] * DSH, DSH)]
                d2d_recv.append(_remote(got, got, send_sems.at[w, j], recv_sems.at[w, j], sibling))
        return ici, ici_recv, d2d, d2d_recv

    def start(self, ins, outs, sems):
        for cp in self._copies(ins, outs, sems)[0]:
            cp.start()

    def mid(self, ins, outs, sems):
        _, ici_recv, d2d, _ = self._copies(ins, outs, sems)
        for got, fwd in zip(ici_recv, d2d):
            got.wait_recv()
            fwd.start()

    def finish(self, ins, outs, sems):
        ici, _, d2d, d2d_recv = self._copies(ins, outs, sems)
        for cp in d2d_recv:
            cp.wait_recv()
        for cp in ici + d2d:
            cp.wait_send()


class ExchangePair:
    has_mid = False
    aliases = ()

    def __init__(self, names, grads):
        self.names, self.ins = names, list(grads)
        self.out_shapes = [SDS((NCHIP, *_half_shape(n)), F32) for n in names]
        self.sems = [pltpu.SemaphoreType.DMA((len(names),)), pltpu.SemaphoreType.DMA((len(names),))]

    def start(self, ins, outs, sems):
        send_sems, recv_sems = sems
        x, y, c, me, chips, chip_ids = _place()
        for w, name in enumerate(self.names):
            for k in range(NCHIP):
                _remote(ins[w].at[_window(name, k, 1 - c)], outs[w].at[k], send_sems.at[w], recv_sems.at[w],
                        (x, y, 1 - c)).start()

    def finish(self, ins, outs, sems):
        send_sems, recv_sems = sems
        x, y, c, me, chips, chip_ids = _place()
        for w in range(len(self.names)):
            _remote(outs[w], outs[w], send_sems.at[w], recv_sems.at[w], (x, y, 1 - c)).wait()


class ExchangeChips:
    has_mid = False
    aliases = ()

    def __init__(self, names, wires):
        self.names, self.ins = names, list(wires)
        self.out_shapes = [SDS((NCHIP - 1, *_half_shape(n)), BF16) for n in names]
        self.sems = [pltpu.SemaphoreType.DMA((len(names), NCHIP - 1)), pltpu.SemaphoreType.DMA((len(names), NCHIP - 1))]

    def _copies(self, ins, outs, sems):
        send_sems, recv_sems = sems
        x, y, c, me, chips, chip_ids = _place()
        return [_remote(ins[w].at[chip_ids[j]], outs[w].at[j], send_sems.at[w, j], recv_sems.at[w, j], (*chip, c))
                for w in range(len(self.names)) for j, chip in enumerate(chips)]

    def start(self, ins, outs, sems):
        for cp in self._copies(ins, outs, sems):
            cp.start()

    def finish(self, ins, outs, sems):
        for cp in self._copies(ins, outs, sems):
            cp.wait()


def _call(body, name, grid, in_specs, out_specs, out_shape, scratch, args, exchanges=(), after=()):
    n_in, n_out, n_scr = len(in_specs), len(out_specs), len(scratch)
    x_in = [a for e in exchanges for a in e.ins]
    x_out = [s for e in exchanges for s in e.out_shapes]
    x_sem = [s for e in exchanges for s in e.sems]
    nsteps = math.prod(grid)

    def wrapped(*refs):
        ins, rest = refs[:n_in], refs[n_in:]
        xin, rest = rest[:len(x_in)], rest[len(x_in) + len(after):]
        outs, rest = rest[:n_out], rest[n_out:]
        xout, rest = rest[:len(x_out)], rest[len(x_out):]
        scr, xsem = rest[:n_scr], rest[n_scr:]
        parts = []
        for e in exchanges:
            parts.append((xin[:len(e.ins)], xout[:len(e.out_shapes)], xsem[:len(e.sems)]))
            xin, xout, xsem = xin[len(e.ins):], xout[len(e.out_shapes):], xsem[len(e.sems):]
        if not grid:
            for e, p in zip(exchanges, parts):
                e.start(*p)
            body(*ins, *outs, *scr)
            for e, p in zip(exchanges, parts):
                if e.has_mid:
                    e.mid(*p)
            for e, p in zip(exchanges, parts):
                e.finish(*p)
            return
        step = 0
        for axis, extent in enumerate(grid):
            step = step * extent + pl.program_id(axis)
        if exchanges:
            @pl.when(step == 0)
            def _():
                for e, p in zip(exchanges, parts):
                    e.start(*p)

        body(*ins, *outs, *scr)
        if any(e.has_mid for e in exchanges):
            @pl.when(step == max(nsteps - 2, 0))
            def _():
                for e, p in zip(exchanges, parts):
                    if e.has_mid:
                        e.mid(*p)

        if exchanges:
            @pl.when(step == nsteps - 1)
            def _():
                for e, p in zip(exchanges, parts):
                    e.finish(*p)

    kw = dict(grid=grid, compiler_params=_cp(dimension_semantics=("arbitrary",) * len(grid))) if grid else dict(
        compiler_params=_cp())
    aliases, i0, o0 = {}, n_in, n_out
    for e in exchanges:
        aliases.update({i0 + i: o0 + o for i, o in e.aliases})
        i0, o0 = i0 + len(e.ins), o0 + len(e.out_shapes)
    res = pl.pallas_call(
        wrapped, name=name, in_specs=list(in_specs) + [ANY] * (len(x_in) + len(after)),
        out_specs=list(out_specs) + [ANY] * len(x_out),
        out_shape=list(out_shape) + x_out, scratch_shapes=list(scratch) + x_sem, input_output_aliases=aliases, **kw,
    )(*args, *x_in, *after)
    outs, rest = res[:n_out], res[n_out:]
    xouts = []
    for e in exchanges:
        xouts.append(rest[:len(e.out_shapes)])
        rest = rest[len(e.out_shapes):]
    return outs, xouts


def exchange(name, ex):
    return _call(lambda: None, name, (), [], [], [], [], [], [ex])[1][0]


def _taps_scratch(tm):
    return pltpu.VMEM((SUBLANES, tm + HALO, LANES), F32)


def _taps(src_ref, k_ref, dst_ref, sh_ref, tm, off0, reverse):
    rc, cw = 64, LANES

    def col_chunk(cc, carry):
        cols = pl.ds(pl.multiple_of(cc * cw, cw), cw)
        for q in range(SUBLANES):
            n = tm + SUBLANES * (len(range(q, KW, SUBLANES)) - 1)
            sh_ref[q, 0:n, :] = src_ref[pl.ds(off0 + q, n), cols]
        for r in range(tm // rc):
            acc = jnp.zeros((rc, cw), F32)
            for q in range(SUBLANES):
                for a, j in enumerate(range(q, KW, SUBLANES)):
                    kj = KW - 1 - j if reverse else j
                    acc = acc + k_ref[kj:kj + 1, cols] * sh_ref[q, pl.ds(r * rc + SUBLANES * a, rc), :]
            dst_ref[pl.ds(r * rc, rc), cols] = acc
        return carry

    lax.fori_loop(0, D // cw, col_chunk, 0)


def _proj_pieces(g):
    lo, hi, pieces = g * D, (g + 1) * D, []
    while lo < hi:
        k = lo // CIN
        b = min(hi - k * CIN, CIN)
        pieces.append((k, lo - k * CIN, b))
        lo = k * CIN + b
    return pieces


def fwd_in(x, g1, w_in, tm, exchanges=()):
    t = x.shape[0]

    def body(x_ref, g_ref, w_ref, u_ref, glu_ref, ag_ref, p_ref, gt_ref):
        xf = x_ref[...]
        r = lax.rsqrt(jnp.mean(xf * xf, axis=-1, keepdims=True) + RMS_EPS)
        u = (xf * r * g_ref[...]).astype(BF16)
        u_ref[...] = u
        proj = lambda g: jnp.concatenate([_mm(u, w_ref[k, :, lo:hi]) for k, lo, hi in _proj_pieces(g)], axis=1)
        a = proj(0)
        gate = proj(1)
        glu_ref[...] = a * _sigmoid(gate)
        ag_ref[:, 0:D] = a.astype(BF16)
        ag_ref[:, D:2 * D] = gate.astype(BF16)
        p_ref[...] = proj(2)
        gt_ref[:, 0:D] = proj(3).astype(BF16)
        gt_ref[:, D:2 * D] = proj(4).astype(BF16)

    return _call(
        body, "fwd_in", (t // tm,),
        [_tile(tm, D), _full((1, D)), _full((NCHIP, D, CIN))],
        [_tile(tm, D), _tile(tm, D), _tile(tm, 2 * D), _tile(tm, D), _tile(tm, 2 * D)],
        [SDS((t, D), BF16), SDS((t, D), F32), SDS((t, 2 * D), BF16), SDS((t, D), F32), SDS((t, 2 * D), BF16)],
        [], [x, g1, w_in], exchanges)


def _pool_inv_count(i, tm, w):
    pos = i * tm + lax.broadcasted_iota(jnp.int32, (tm, 1), 0) + 1
    return 1.0 / jnp.minimum(pos, w).astype(F32)


def _window_sum(src_ref, tmp_ref, cols, tm, w, causal):
    lo, hi = 0, tm + HALO
    cur, span = None, 1
    while span < w:
        new_lo, new_hi = (lo + SUBLANES, hi) if causal else (lo, hi - SUBLANES)
        far = new_lo - span if causal else new_lo + span
        n = new_hi - new_lo
        if cur is None:
            near_v, far_v = src_ref[pl.ds(new_lo, n), cols], src_ref[pl.ds(far, n), cols]
        else:
            near_v = cur[new_lo - lo:new_lo - lo + n]
            if span % SUBLANES == 0:
                far_v = cur[far - lo:far - lo + n]
            else:
                tmp_ref[pl.ds(lo, hi - lo), :] = cur
                far_v = tmp_ref[pl.ds(far, n), :]
        cur, lo, hi, span = near_v + far_v, new_lo, new_hi, 2 * span
    off = HALO if causal else 0
    return cur[off - lo:off - lo + tm]


def fwd_mix(x, glu, p, gt, dwk, dwb, lng, lnb, w_co, pool_w, ps, w_po, w_o, g2, tm, exchanges=()):
    t = x.shape[0]

    def body(x_ref, glu_ref, gluh_ref, p_ref, ph_ref, gt_ref, k_ref, b_ref, lg_ref, lb_ref, wco_ref, pw_ref,
             ps_ref, wpo_ref, wo_ref, g2_ref,
             cv_ref, sw_ref, z_ref, zl_ref, zs_ref, yc_ref, yp_ref, mg_ref, mo_ref, h1_ref, ext_ref, win_ref, sh_ref):
        i = pl.program_id(0)
        keep = (i > 0).astype(F32)
        ext_ref[0:HALO, :] = gluh_ref[...] * keep
        ext_ref[HALO:HALO + tm, :] = glu_ref[...]
        _taps(ext_ref, k_ref, cv_ref, sh_ref, tm, HALO - (KW - 1), False)
        cv = cv_ref[...] + b_ref[...]
        cv_ref[...] = cv
        mu = jnp.mean(cv, axis=-1, keepdims=True)
        cen = cv - mu
        rstd = lax.rsqrt(jnp.mean(cen * cen, axis=-1, keepdims=True) + LN_EPS)
        ln = cen * rstd * lg_ref[...] + lb_ref[...]
        sw = (ln * _sigmoid(ln)).astype(BF16)
        sw_ref[...] = sw
        yc = _mm(sw, wco_ref[...])
        yc_ref[...] = yc.astype(BF16)
        ext_ref[0:HALO, :] = ph_ref[...] * keep
        ext_ref[HALO:HALO + tm, :] = p_ref[...]
        for g, w in enumerate(POOL_WINDOWS):
            cols = pl.ds(g * GW, GW)
            acc = _window_sum(ext_ref, win_ref, cols, tm, w, True)
            zg = (acc * _pool_inv_count(i, tm, w) - p_ref[:, cols]).astype(BF16)
            z_ref[:, cols] = zg
            zl_ref[:, cols] = _mm(zg, pw_ref[g])
        zl = zl_ref[...]
        zs = (zl * ps_ref[...]).astype(BF16)
        zs_ref[...] = zs
        yp = _mm(zs, wpo_ref[...])
        yp_ref[...] = yp.astype(BF16)
        gc = _sigmoid(gt_ref[:, 0:D].astype(F32))
        gp = _sigmoid(gt_ref[:, D:2 * D].astype(F32))
        mg = (gc * yc + gp * yp).astype(BF16)
        mg_ref[...] = mg
        mo = _mm(mg, wo_ref[...])
        mo_ref[...] = mo
        r2 = lax.rsqrt(jnp.mean(mo * mo, axis=-1, keepdims=True) + RMS_EPS)
        h1_ref[...] = x_ref[...] + mo * r2 * g2_ref[...]

    vec = _full((1, D))
    act = lambda dt: SDS((t, D), dt)
    return _call(
        body, "fwd_mix", (t // tm,),
        [_tile(tm, D), _tile(tm, D), _prev_halo(tm), _tile(tm, D), _prev_halo(tm), _tile(tm, 2 * D),
         _full((KW_PAD, D)), vec, vec, vec, _full((D, D)), _full((NG, GW, GW)), vec, _full((D, D)), _full((D, D)), vec],
        [_tile(tm, D)] * 10,
        [act(F32), act(BF16), act(BF16), act(F32), act(BF16), act(BF16), act(BF16), act(BF16), act(F32), act(F32)],
        [pltpu.VMEM((tm + HALO, D), F32), pltpu.VMEM((tm + HALO, GW), F32), _taps_scratch(tm)],
        [x, glu, glu, p, p, gt, dwk, dwb, lng, lnb, w_co, pool_w, ps, w_po, w_o, g2], exchanges)


def mlp_fwd_bwd(h1, tgt, g3, g4, w1, w2, tm, exchanges=()):
    t = h1.shape[0]
    fc = CFF

    def body(h1_ref, tgt_ref, g3_ref, g4_ref, w1_ref, w2_ref,
             v_ref, a2_ref, df2_ref, df1_ref, dh1_ref, vec_ref, f1_ref):
        i = pl.program_id(0)

        @pl.when(i == 0)
        def _():
            vec_ref[...] = jnp.zeros_like(vec_ref)

        h1v = h1_ref[...]
        r3 = lax.rsqrt(jnp.mean(h1v * h1v, axis=-1, keepdims=True) + RMS_EPS)
        n3 = h1v * r3
        v = (n3 * g3_ref[...]).astype(BF16)
        v_ref[...] = v
        f2 = jnp.zeros((tm, D), F32)
        for c in range(FF // fc):
            cols = pl.ds(c * fc, fc)
            f1 = jnp.maximum(_mm(v, w1_ref[c]), 0.0)
            f1_ref[:, cols] = f1
            a2 = (f1 * f1).astype(BF16)
            a2_ref[:, cols] = a2
            f2 = f2 + _mm(a2, w2_ref[cols, :])
        r4 = lax.rsqrt(jnp.mean(f2 * f2, axis=-1, keepdims=True) + RMS_EPS)
        n4 = f2 * r4
        err = h1v + n4 * g4_ref[...] - tgt_ref[...]
        vec_ref[2:3, :] += _rowsum(err * err) * (0.5 / D)
        dh2 = err * (1.0 / D)
        vec_ref[1:2, :] += _rowsum(dh2 * n4)
        dn4 = dh2 * g4_ref[...]
        df2 = (r4 * (dn4 - n4 * jnp.mean(dn4 * n4, axis=-1, keepdims=True))).astype(BF16)
        df2_ref[...] = df2
        dv = jnp.zeros((tm, D), F32)
        for c in range(FF // fc):
            cols = pl.ds(c * fc, fc)
            da2 = _mm_nt(df2, w2_ref[cols, :])
            df1 = (da2 * (2.0 * f1_ref[:, cols])).astype(BF16)
            df1_ref[:, cols] = df1
            dv = dv + _mm_nt(df1, w1_ref[c])
        vec_ref[0:1, :] += _rowsum(dv * n3)
        dn3 = dv * g3_ref[...]
        dh1_ref[...] = dh2 + r3 * (dn3 - n3 * jnp.mean(dn3 * n3, axis=-1, keepdims=True))

    vec = _full((1, D))
    return _call(
        body, "mlp_fwd_bwd", (t // tm,),
        [_tile(tm, D), _tile(tm, D), vec, vec, _full((NCHIP, D, CFF)), _full((FF, D))],
        [_tile(tm, D), _tile(tm, FF), _tile(tm, D), _tile(tm, FF), _tile(tm, D), _full((8, D))],
        [SDS((t, D), BF16), SDS((t, FF), BF16), SDS((t, D), BF16), SDS((t, FF), BF16), SDS((t, D), F32),
         SDS((8, D), F32)],
        [pltpu.VMEM((tm, FF), F32)], [h1, tgt, g3, g4, w1, w2], exchanges)


def bwd_mix(dh1, mo, cv, zl, yc, yp, gt, lng, lnb, ps, g2, w_co, pool_w, w_po, w_o, tm, exchanges=()):
    t = dh1.shape[0]

    def body(dh1_ref, mo_ref, cv_ref, zl_ref, yc_ref, yp_ref, gt_ref, lg_ref, lb_ref, ps_ref, g2_ref,
             wco_ref, pw_ref, wpo_ref, wo_ref,
             dmo_ref, dgt_ref, dyc_ref, dyp_ref, dcv_ref, dzl_ref, dz_ref, vec_ref):
        i = pl.program_id(0)

        @pl.when(i == 0)
        def _():
            vec_ref[...] = jnp.zeros_like(vec_ref)

        dh1v = dh1_ref[...]
        mo = mo_ref[...]
        r2 = lax.rsqrt(jnp.mean(mo * mo, axis=-1, keepdims=True) + RMS_EPS)
        n2 = mo * r2
        vec_ref[0:1, :] += _rowsum(dh1v * n2)
        dn2 = dh1v * g2_ref[...]
        dmo = (r2 * (dn2 - n2 * jnp.mean(dn2 * n2, axis=-1, keepdims=True))).astype(BF16)
        dmo_ref[...] = dmo
        dmg = _mm_nt(dmo, wo_ref[...])
        gc = _sigmoid(gt_ref[:, 0:D].astype(F32))
        gp = _sigmoid(gt_ref[:, D:2 * D].astype(F32))
        dgt_ref[:, 0:D] = (dmg * yc_ref[...].astype(F32) * gc * (1.0 - gc)).astype(BF16)
        dgt_ref[:, D:2 * D] = (dmg * yp_ref[...].astype(F32) * gp * (1.0 - gp)).astype(BF16)
        dyc = (dmg * gc).astype(BF16)
        dyp = (dmg * gp).astype(BF16)
        dyc_ref[...] = dyc
        dyp_ref[...] = dyp
        dsw = _mm_nt(dyc, wco_ref[...])
        cv = cv_ref[...]
        mu = jnp.mean(cv, axis=-1, keepdims=True)
        cen = cv - mu
        rstd = lax.rsqrt(jnp.mean(cen * cen, axis=-1, keepdims=True) + LN_EPS)
        y = cen * rstd
        ln = y * lg_ref[...] + lb_ref[...]
        sg = _sigmoid(ln)
        dln = dsw * (sg * (1.0 + ln * (1.0 - sg)))
        vec_ref[1:2, :] += _rowsum(dln * y)
        vec_ref[2:3, :] += _rowsum(dln)
        dy = dln * lg_ref[...]
        dcv = rstd * (dy - jnp.mean(dy, axis=-1, keepdims=True) - y * jnp.mean(dy * y, axis=-1, keepdims=True))
        dcv_ref[...] = dcv
        vec_ref[3:4, :] += _rowsum(dcv)
        dzs = _mm_nt(dyp, wpo_ref[...])
        vec_ref[4:5, :] += _rowsum(dzs * zl_ref[...])
        dzl = (dzs * ps_ref[...]).astype(BF16)
        dzl_ref[...] = dzl
        for g in range(NG):
            cols = pl.ds(g * GW, GW)
            dz_ref[:, cols] = _mm_nt(dzl_ref[:, cols], pw_ref[g])

    vec = _full((1, D))
    act = lambda dt: SDS((t, D), dt)
    return _call(
        body, "bwd_mix", (t // tm,),
        [_tile(tm, D)] * 6 + [_tile(tm, 2 * D), vec, vec, vec, vec, _full((D, D)), _full((NG, GW, GW)), _full((D, D)),
                              _full((D, D))],
        [_tile(tm, D), _tile(tm, 2 * D)] + [_tile(tm, D)] * 5 + [_full((8, D))],
        [act(BF16), SDS((t, 2 * D), BF16), act(BF16), act(BF16), act(F32), act(BF16), act(F32), SDS((8, D), F32)],
        [], [dh1, mo, cv, zl, yc, yp, gt, lng, lnb, ps, g2, w_co, pool_w, w_po, w_o], exchanges)


def bwd_in(x, dh1, dcv, dz, glu, ag, dgt, g1, dwk, w_in, tm, after=()):
    t = x.shape[0]
    nt = t // tm

    def body(x_ref, dh1_ref, dcv_ref, dcvh_ref, dz_ref, dzh_ref, glu_ref, gluh_ref, ag_ref, dgt_ref, g1_ref,
             k_ref, w_ref, dx_ref, dproj_ref, vec_ref, dk_ref, ext_ref, tmp_ref, win_ref, sh_ref, gext_ref, gsh_ref):
        i = pl.program_id(0)

        @pl.when(i == 0)
        def _():
            vec_ref[...] = jnp.zeros_like(vec_ref)
            dk_ref[...] = jnp.zeros_like(dk_ref)

        first = (i > 0).astype(F32)
        last = (i < nt - 1).astype(F32)
        gext_ref[0:HALO, :] = gluh_ref[...] * first
        gext_ref[HALO:HALO + tm, :] = glu_ref[...]
        rc, cw = 32, LANES

        def dk_chunk(cc):
            cols = pl.ds(cc * cw, cw)
            for q in range(SUBLANES):
                taps = range(q, KW, SUBLANES)
                n = tm + SUBLANES * (len(taps) - 1)
                gsh_ref[q, 0:n, :] = gext_ref[pl.ds(HALO - (KW - 1) + q, n), cols]
                accs = [jnp.zeros((SUBLANES, cw), F32) for _ in taps]
                for r in range(tm // rc):
                    dchunk = dcv_ref[pl.ds(r * rc, rc), cols]
                    for a in range(len(taps)):
                        prod = dchunk * gsh_ref[q, pl.ds(r * rc + SUBLANES * a, rc), :]
                        accs[a] = accs[a] + jnp.sum(prod.reshape(rc // SUBLANES, SUBLANES, cw), axis=0)
                for a, j in enumerate(taps):
                    dk_ref[j:j + 1, cols] += _rowsum(accs[a])

        ext_ref[0:tm, :] = dcv_ref[...]
        ext_ref[tm:tm + HALO, :] = dcvh_ref[...] * last
        _taps(ext_ref, k_ref, tmp_ref, sh_ref, tm, 0, True)
        dglu = tmp_ref[...]
        a = ag_ref[:, 0:D].astype(F32)
        sg = _sigmoid(ag_ref[:, D:2 * D].astype(F32))
        dproj_ref[:, 0:D] = (dglu * sg).astype(BF16)
        dproj_ref[:, D:2 * D] = (dglu * a * sg * (1.0 - sg)).astype(BF16)
        for g, w in enumerate(POOL_WINDOWS):
            cols = pl.ds(g * GW, GW)
            pos = i * tm + lax.broadcasted_iota(jnp.int32, (tm + HALO, 1), 0) + 1
            inv = 1.0 / jnp.minimum(pos, w).astype(F32)
            ext_ref[0:tm, cols] = dz_ref[:, cols] * inv[0:tm]
            ext_ref[tm:tm + HALO, cols] = dzh_ref[:, cols] * inv[tm:tm + HALO] * last
            acc = _window_sum(ext_ref, win_ref, cols, tm, w, False)
            dproj_ref[:, pl.ds(2 * D + g * GW, GW)] = (acc - dz_ref[:, cols]).astype(BF16)
        dproj_ref[:, 3 * D:5 * D] = dgt_ref[...]
        du = jnp.zeros((tm, D), F32)
        chunks_per_matmul = D // cw // NCHIP
        for k in range(NCHIP):
            du = du + _mm_nt(dproj_ref[:, k * CIN:(k + 1) * CIN], w_ref[k])
            for cc in range(k * chunks_per_matmul, (k + 1) * chunks_per_matmul):
                dk_chunk(cc)
        xf = x_ref[...]
        r1 = lax.rsqrt(jnp.mean(xf * xf, axis=-1, keepdims=True) + RMS_EPS)
        n1 = xf * r1
        vec_ref[0:1, :] += _rowsum(du * n1)
        dn1 = du * g1_ref[...]
        dx_ref[...] = dh1_ref[...] + r1 * (dn1 - n1 * jnp.mean(dn1 * n1, axis=-1, keepdims=True))

    return _call(
        body, "bwd_in", (nt,),
        [_tile(tm, D), _tile(tm, D), _tile(tm, D), _next_halo(tm, nt), _tile(tm, D), _next_halo(tm, nt),
         _tile(tm, D), _prev_halo(tm), _tile(tm, 2 * D), _tile(tm, 2 * D), _full((1, D)),
         _full((KW_PAD, D)), _full((NCHIP, D, CIN))],
        [_tile(tm, D), _tile(tm, NPROJ * D), _full((8, D)), _full((KW_PAD, D))],
        [SDS((t, D), F32), SDS((t, NPROJ * D), BF16), SDS((8, D), F32), SDS((KW_PAD, D), F32)],
        [pltpu.VMEM((tm + HALO, D), F32), pltpu.VMEM((tm, D), F32), pltpu.VMEM((tm + HALO, GW), F32),
         _taps_scratch(tm), pltpu.VMEM((tm + HALO, D), F32), _taps_scratch(tm)],
        [x, dh1, dcv, dcv, dz, dz, glu, glu, ag, dgt, g1, dwk, w_in], after=after)[0]


def wgrad(a, b, name, bm=1024, bn=1024, bt=2048, chip_major=False, exchanges=()):
    t, m = a.shape
    n = b.shape[1]
    bm, bn, bt = min(bm, m), min(bn, n), min(bt, t)
    assert m % bm == 0 and n % bn == 0 and t % bt == 0, (a.shape, b.shape, bm, bn, bt)

    def body(a_ref, b_ref, o_ref):
        k = pl.program_id(2)

        @pl.when(k == 0)
        def _():
            o_ref[...] = jnp.zeros_like(o_ref)

        o_ref[...] += _mm_tn(a_ref[...], b_ref[...])

    if chip_major:
        out_spec, out_shape = pl.BlockSpec((None, bm, bn), lambda i, j, k: (j, i, 0)), SDS((n // bn, m, bn), F32)
    else:
        out_spec, out_shape = pl.BlockSpec((bm, bn), lambda i, j, k: (i, j)), SDS((m, n), F32)
    outs, xouts = _call(
        body, name, (m // bm, n // bn, t // bt),
        [pl.BlockSpec((bt, bm), lambda i, j, k: (k, i)), pl.BlockSpec((bt, bn), lambda i, j, k: (k, j))],
        [out_spec], [out_shape], [], [a, b], exchanges)
    return outs[0], xouts


def wgrad_pool(z, dzl, bt=4096):
    t = z.shape[0]
    bt = min(bt, t)
    assert t % bt == 0

    def body(a_ref, b_ref, o_ref):
        k = pl.program_id(1)

        @pl.when(k == 0)
        def _():
            o_ref[...] = jnp.zeros_like(o_ref)

        o_ref[0] += _mm_tn(a_ref[...], b_ref[...])

    return _call(
        body, "wgrad_pool", (NG, t // bt),
        [pl.BlockSpec((bt, GW), lambda g, k: (k, g)), pl.BlockSpec((bt, GW), lambda g, k: (k, g))],
        [pl.BlockSpec((1, GW, GW), lambda g, k: (g, 0, 0))], [SDS((NG, GW, GW), F32)], [], [z, dzl])[0][0]


def cast_shards(name, names, shards, exchanges=()):
    n = len(shards)

    def body(*refs):
        srcs, dsts, bufs, sems = refs[:n], refs[n:2 * n], refs[2 * n:3 * n], refs[3 * n]
        me = 2 * lax.axis_index("x") + lax.axis_index("y")
        copies = []
        for w, mat in enumerate(names):
            bufs[w][...] = srcs[w][...].astype(BF16)
            for h in range(2):
                cp = pltpu.make_async_copy(bufs[w].at[_shard_half(mat, h)], dsts[w].at[_window(mat, me, h)], sems.at[w, h])
                cp.start()
                copies.append(cp)
        for cp in copies:
            cp.wait()

    return _call(body, name, (), [VMEM_SPEC] * n, [ANY] * n, [SDS(GEOM[mat][0], BF16) for mat in names],
                 [pltpu.VMEM(s.shape, BF16) for s in shards] + [pltpu.SemaphoreType.DMA((n, 2))], list(shards), exchanges)


VEC_ROWS = 24
ROW = dict(mlp_pre_g=0, mlp_post_g=1, loss=2, mix_post_g=8, conv_ln_g=9, conv_ln_b=10, dw_bias=11, pool_scale=12,
           mix_pre_g=16)


NDEV = 8


class GatherSmall:
    has_mid = False
    aliases = ()

    def __init__(self, part):
        self.ins = [part]
        self.out_shapes = [SDS((NDEV, *part.shape), F32)]
        self.sems = [pltpu.SemaphoreType.DMA((NDEV,)), pltpu.SemaphoreType.DMA((NDEV,))]

    def _copies(self, ins, outs, sems):
        send_sems, recv_sems = sems
        x, y, c = lax.axis_index("x"), lax.axis_index("y"), lax.axis_index("c")
        me = 4 * x + 2 * y + c
        own = pltpu.make_async_copy(ins[0], outs[0].at[me], send_sems.at[0])
        sends = [_remote(ins[0], outs[0].at[me], send_sems.at[m], recv_sems.at[m],
                         (x ^ (m >> 2), y ^ ((m >> 1) & 1), c ^ (m & 1))) for m in range(1, NDEV)]
        recvs = [_remote(outs[0].at[me ^ m], outs[0].at[me ^ m], send_sems.at[m], recv_sems.at[m], (x, y, c))
                 for m in range(1, NDEV)]
        return own, sends, recvs

    def start(self, ins, outs, sems):
        own, sends, _ = self._copies(ins, outs, sems)
        for cp in sends + [own]:
            cp.start()

    def finish(self, ins, outs, sems):
        own, sends, recvs = self._copies(ins, outs, sems)
        for cp in recvs:
            cp.wait_recv()
        for cp in sends:
            cp.wait_send()
        own.wait()


def sum_small(parts, after=()):
    def body(parts_ref, sum_ref, loss_ref):
        total = parts_ref[0]
        for d in range(1, NDEV):
            total = total + parts_ref[d]
        sum_ref[...] = total
        r = ROW["loss"]
        loss_ref[...] = jnp.zeros_like(loss_ref) + jnp.sum(total[r:r + 1, :])

    return _call(body, "sum_small", (), [VMEM_SPEC], [VMEM_SPEC, VMEM_SPEC],
                 [SDS(parts.shape[1:], F32), SDS((8, 128), F32)], [], [parts], after=after)[0]


def pair_add(tag, names, grads, from_sibling, exchanges=()):
    n = len(names)
    in_specs, wire_specs, own_specs, wire_shapes, own_shapes = [], [], [], [], []
    for name in names:
        _, gblk, idx = GEOM[name]
        blk = _half_shape(name)
        zeros = (0,) * len(blk)
        in_specs.append(pl.BlockSpec(gblk, lambda k, idx=idx: idx(k, lax.axis_index("c"))))
        wire_specs.append(pl.BlockSpec((1, *blk), lambda k, zeros=zeros: (k, *zeros)))
        own_specs.append(pl.BlockSpec(blk, lambda k, zeros=zeros: zeros))
        wire_shapes.append(SDS((NCHIP, *blk), BF16))
        own_shapes.append(SDS(blk, F32))

    def body(*refs):
        g, s, wire, own = refs[:n], refs[n:2 * n], refs[2 * n:3 * n], refs[3 * n:]
        mine = pl.program_id(0) == 2 * lax.axis_index("x") + lax.axis_index("y")
        for w in range(n):
            total = g[w][...] + s[w][0]
            wire[w][0] = total.astype(BF16)

            @pl.when(mine)
            def _(w=w, total=total):
                own[w][...] = total

    outs, xouts = _call(body, "pair_add_" + tag, (NCHIP,), in_specs + wire_specs, wire_specs + own_specs,
                        wire_shapes + own_shapes, [], list(grads) + list(from_sibling), exchanges)
    return outs[:n], outs[n:], xouts


SUM_STEPS = 4


def sum_partials(tag, names, owns, from_chips, exchanges=()):
    n = len(names)
    own_specs, part_specs, out_specs, out_shapes, part_args, counts = [], [], [], [], [], []
    for name, parts in zip(names, from_chips):
        half = _half_shape(name)
        blk = half[:-2] + (half[-2] // SUM_STEPS, half[-1])
        lead = (0,) * (len(half) - 2)
        own_specs.append(pl.BlockSpec(blk, lambda i, lead=lead: (*lead, i, 0)))
        for p in parts:
            part_specs.append(pl.BlockSpec((p.shape[0], *blk), lambda i, lead=lead: (0, *lead, i, 0)))
            part_args.append(p)
        counts.append(len(parts))
        out_specs.append(pl.BlockSpec((2, *blk), lambda i, lead=lead: (0, *lead, i, 0)))
        out_shapes.append(SDS((2, *half), F32))

    def body(*refs):
        own, parts, out = refs[:n], list(refs[n:n + len(part_args)]), refs[n + len(part_args):]
        c = lax.axis_index("c")
        for w in range(n):
            total = own[w][...]
            for p in [parts.pop(0) for _ in range(counts[w])]:
                for j in range(p.shape[0]):
                    total = total + p[j].astype(F32)
            out[w][c] = total

    return _call(body, "sum_partials_" + tag, (SUM_STEPS,), own_specs + part_specs, out_specs, out_shapes, [],
                 list(owns) + part_args, exchanges)


class SwapHalves:
    has_mid = False

    def __init__(self, halves):
        self.ins = list(halves)
        self.out_shapes = [SDS(h.shape, h.dtype) for h in halves]
        self.aliases = [(i, i) for i in range(len(halves))]
        self.sems = [pltpu.SemaphoreType.DMA((len(halves),)), pltpu.SemaphoreType.DMA((len(halves),))]

    def start(self, ins, outs, sems):
        send_sems, recv_sems = sems
        x, y, c = lax.axis_index("x"), lax.axis_index("y"), lax.axis_index("c")
        for w in range(len(self.ins)):
            _remote(ins[w].at[c], outs[w].at[c], send_sems.at[w], recv_sems.at[w], (x, y, 1 - c)).start()

    def finish(self, ins, outs, sems):
        send_sems, recv_sems = sems
        x, y, c = lax.axis_index("x"), lax.axis_index("y"), lax.axis_index("c")
        for w in range(len(self.ins)):
            _remote(ins[w].at[c], outs[w].at[c], send_sems.at[w], recv_sems.at[w], (x, y, 1 - c)).wait_send()
            _remote(ins[w].at[1 - c], outs[w].at[1 - c], send_sems.at[w], recv_sems.at[w], (x, y, 1 - c)).wait_recv()


SEM_SPEC = pl.BlockSpec(memory_space=pltpu.SEMAPHORE)
HBM_SPEC = pl.BlockSpec(memory_space=pltpu.HBM)
DATAFLOW = pltpu.SideEffectType.DATAFLOW_SIDE_EFFECTING


def plan_chips(names):
    def plan(refs):
        n = len(names)
        x, y, c, me, chips, chip_ids = _place()
        return [(refs[w].at[chip_ids[j]], refs[n + w].at[j], (*chip, c)) for w in range(n) for j, chip in enumerate(chips)]
    return plan


def plan_pair(names):
    def plan(refs):
        n = len(names)
        x, y, c, me, chips, chip_ids = _place()
        return [(refs[w].at[_window(name, k, 1 - c)], refs[n + w].at[k], (x, y, 1 - c))
                for w, name in enumerate(names) for k in range(NCHIP)]
    return plan


def split_start(tag, arrays, plan, ncopies):
    n = len(arrays)

    def body(*refs):
        sems, token = refs[n:n + 2 * ncopies], refs[-1]
        for s, (src, dst, to) in enumerate(plan(refs[:n])):
            _remote(src, dst, sems[2 * s], sems[2 * s + 1], to).start()
        token[...] = jnp.zeros_like(token)

    res = pl.pallas_call(
        body, name="start_" + tag, in_specs=[HBM_SPEC] * n,
        out_specs=[SEM_SPEC] * (2 * ncopies) + [HBM_SPEC] * n + [VMEM_SPEC],
        out_shape=[pltpu.SemaphoreType.DMA(())] * (2 * ncopies) + [pltpu.HBM(a.shape, a.dtype) for a in arrays]
        + [SDS((8, 128), F32)],
        input_output_aliases={i: 2 * ncopies + i for i in range(n)},
        compiler_params=pltpu.CompilerParams(has_side_effects=DATAFLOW),
    )(*arrays)
    return (res[:2 * ncopies], res[2 * ncopies:-1]), res[-1]


def split_wait(tag, started, plan, after):
    sems, arrays = started
    n = len(arrays)

    def body(*refs):
        sem = refs[n:n + len(sems)]
        for s, (src, dst, to) in enumerate(plan(refs[:n])):
            cp = _remote(src, dst, sem[2 * s], sem[2 * s + 1], to)
            cp.wait_send()
            cp.wait_recv()

    return pl.pallas_call(
        body, name="wait_" + tag, in_specs=[HBM_SPEC] * n + [SEM_SPEC] * len(sems) + [ANY] * len(after),
        out_specs=[HBM_SPEC] * n, out_shape=[pltpu.HBM(a.shape, a.dtype) for a in arrays],
        input_output_aliases={i: i for i in range(n)},
        compiler_params=pltpu.CompilerParams(has_side_effects=DATAFLOW),
    )(*arrays, *sems, *after)


def adamw(tag, ws, gs, ms, vs, steps, after=()):
    n = len(ws)
    specs = [pl.BlockSpec((a.shape[0] // steps, a.shape[1]), lambda i: (i, 0)) for a in ws]
    assert all(a.shape[0] % (steps * SUBLANES) == 0 for a in ws), [a.shape for a in ws]

    def body(*refs):
        w_, g_, m_, v_ = refs[:n], refs[n:2 * n], refs[2 * n:3 * n], refs[3 * n:4 * n]
        d_, nm_, nv_, gout_ = refs[4 * n:5 * n], refs[5 * n:6 * n], refs[6 * n:7 * n], refs[7 * n:]
        for i in range(n):
            gv = g_[i][...]
            gout_[i][...] = gv
            mn = B1 * m_[i][...] + (1.0 - B1) * gv
            vn = B2 * v_[i][...] + (1.0 - B2) * (gv * gv)
            m_hat = mn / (1.0 - B1 ** STEP)
            v_hat = vn / (1.0 - B2 ** STEP)
            d_[i][...] = -LR * (m_hat / (jnp.sqrt(v_hat) + ADAM_EPS) + WD * w_[i][...])
            nm_[i][...] = mn
            nv_[i][...] = vn

    outs, _ = _call(body, "adamw_" + tag, (steps,), specs * 4, specs * 4, [SDS(a.shape, F32) for a in ws] * 4, [],
                    list(ws) + list(gs) + list(ms) + list(vs), after=after)
    return list(zip(outs[:n], outs[n:2 * n], outs[2 * n:3 * n], outs[3 * n:]))


VECS = ("mix_pre_g", "dw_bias", "conv_ln_g", "conv_ln_b", "pool_scale", "mix_post_g", "mlp_pre_g", "mlp_post_g")
WEIGHTS = ("mix_pre_g", "w_in", "dw_kernel", "dw_bias", "conv_ln_g", "conv_ln_b", "w_conv_out", "pool_w", "pool_scale",
           "w_pool_out", "w_o", "mix_post_g", "mlp_pre_g", "w_ff1", "w_ff2", "mlp_post_g")
MIX_MATS = ("w_conv_out", "pool_w", "w_pool_out", "w_o")
FF_MATS = ("w_ff1", "w_ff2")


def kernel(x, mix_pre_g, w_in, dw_kernel, dw_bias, conv_ln_g, conv_ln_b, w_conv_out, pool_w, pool_scale, w_pool_out, w_o, mix_post_g, mlp_pre_g, w_ff1, w_ff2, mlp_post_g, loss_target, m_mix_pre_g, m_w_in, m_dw_kernel, m_dw_bias, m_conv_ln_g, m_conv_ln_b, m_w_conv_out, m_pool_w, m_pool_scale, m_w_pool_out, m_w_o, m_mix_post_g, m_mlp_pre_g, m_w_ff1, m_w_ff2, m_mlp_post_g, v_mix_pre_g, v_w_in, v_dw_kernel, v_dw_bias, v_conv_ln_g, v_conv_ln_b, v_w_conv_out, v_pool_w, v_pool_scale, v_w_pool_out, v_w_o, v_mix_post_g, v_mlp_pre_g, v_w_ff1, v_w_ff2, v_mlp_post_g):
    w = dict(mix_pre_g=mix_pre_g, w_in=w_in, dw_kernel=dw_kernel, dw_bias=dw_bias, conv_ln_g=conv_ln_g,
             conv_ln_b=conv_ln_b, w_conv_out=w_conv_out, pool_w=pool_w, pool_scale=pool_scale, w_pool_out=w_pool_out,
             w_o=w_o, mix_post_g=mix_post_g, mlp_pre_g=mlp_pre_g, w_ff1=w_ff1, w_ff2=w_ff2, mlp_post_g=mlp_post_g)
    m = dict(mix_pre_g=m_mix_pre_g, w_in=m_w_in, dw_kernel=m_dw_kernel, dw_bias=m_dw_bias, conv_ln_g=m_conv_ln_g,
             conv_ln_b=m_conv_ln_b, w_conv_out=m_w_conv_out, pool_w=m_pool_w, pool_scale=m_pool_scale,
             w_pool_out=m_w_pool_out, w_o=m_w_o, mix_post_g=m_mix_post_g, mlp_pre_g=m_mlp_pre_g, w_ff1=m_w_ff1,
             w_ff2=m_w_ff2, mlp_post_g=m_mlp_post_g)
    v = dict(mix_pre_g=v_mix_pre_g, w_in=v_w_in, dw_kernel=v_dw_kernel, dw_bias=v_dw_bias, conv_ln_g=v_conv_ln_g,
             conv_ln_b=v_conv_ln_b, w_conv_out=v_w_conv_out, pool_w=v_pool_w, pool_scale=v_pool_scale,
             w_pool_out=v_w_pool_out, w_o=v_w_o, mix_post_g=v_mix_post_g, mlp_pre_g=v_mlp_pre_g, w_ff1=v_w_ff1,
             w_ff2=v_w_ff2, mlp_post_g=v_mlp_post_g)
    chip = 2 * lax.axis_index("x") + lax.axis_index("y")
    xs, tgt = x[0], loss_target[0]
    vecs = {name: w[name].reshape(1, D) for name in VECS}

    taps = lax.dynamic_update_slice(jnp.zeros((KW_PAD, D), F32), dw_kernel, (0, chip * DSH))
    mine, full = {}, {}
    (mine["w_in"],), _ = cast_shards("cast_w_in", ("w_in",), [w["w_in"]])
    rest = MATS[1:]
    cast, ((full["w_in"], dwk),) = cast_shards(
        "cast_rest", rest, [w[name] for name in rest], [GatherWeights(("w_in",), [mine["w_in"]], taps)])
    mine.update(zip(rest, cast))
    (u, glu, ag, p, gt), (got,) = fwd_in(
        xs, vecs["mix_pre_g"], full["w_in"], TM_IN, [GatherWeights(MIX_MATS, [mine[n] for n in MIX_MATS])])
    full.update(zip(MIX_MATS, got))
    (cv, sw, z, zl, zs, yc, yp, mg, mo, h1), (got,) = fwd_mix(
        xs, glu, p, gt, dwk, vecs["dw_bias"], vecs["conv_ln_g"], vecs["conv_ln_b"], full["w_conv_out"], full["pool_w"],
        vecs["pool_scale"], full["w_pool_out"], full["w_o"], vecs["mix_post_g"], TM,
        [GatherWeights(FF_MATS, [mine[n] for n in FF_MATS])])
    full.update(zip(FF_MATS, got))

    (v_, a2, df2, df1, dh1, vec_mlp), _ = mlp_fwd_bwd(
        h1, tgt, vecs["mlp_pre_g"], vecs["mlp_post_g"], full["w_ff1"], full["w_ff2"], TM)
    grads, g, delta, new_m, new_v = {}, {}, {}, {}, {}
    rest_mats = FF_MATS + MIX_MATS
    grads["w_ff1"], _ = wgrad(v_, df1, "wgrad_ff1", bn=CFF, chip_major=True)
    grads["w_ff2"], _ = wgrad(a2, df2, "wgrad_ff2")
    landing = lambda names, slots, dt: [lax.empty((slots, *_half_shape(n)), dt) for n in names]
    grads_ff = [grads[n] for n in FF_MATS]
    (dmo, dgt, dyc, dyp, dcv, dzl, dz, vec_mix), (from_sibling,) = bwd_mix(
        dh1, mo, cv, zl, yc, yp, gt, vecs["conv_ln_g"], vecs["conv_ln_b"], vecs["pool_scale"], vecs["mix_post_g"],
        full["w_conv_out"], full["pool_w"], full["w_pool_out"], full["w_o"], TM,
        exchanges=[ExchangePair(FF_MATS, grads_ff)])
    grads["w_conv_out"], _ = wgrad(sw, dyc, "wgrad_conv_out", bt=1024)
    grads["pool_w"] = wgrad_pool(z, dzl)
    grads["w_pool_out"], _ = wgrad(zs, dyp, "wgrad_pool_out", bt=1024)
    grads["w_o"], _ = wgrad(mg, dmo, "wgrad_o", bt=1024)
    wire_ff, own_ff, (from_sibling,) = pair_add(
        "ff", FF_MATS, grads_ff, from_sibling, exchanges=[ExchangePair(MIX_MATS, [grads[n] for n in MIX_MATS])])
    wire_mix, own_mix, _ = pair_add("mix", MIX_MATS, [grads[n] for n in MIX_MATS], from_sibling)
    chips_rest, token = split_start("chips_rest", list(wire_ff) + list(wire_mix) + landing(rest_mats, NCHIP - 1, BF16),
                                    plan_chips(rest_mats), len(rest_mats) * (NCHIP - 1))
    dx, dproj, vec_in, dk = bwd_in(
        xs, dh1, dcv, dz, glu, ag, dgt, vecs["mix_pre_g"], dwk, full["w_in"], TM, after=[token])
    small_part = jnp.concatenate([vec_mlp, vec_mix, vec_in, dk], axis=0)
    grads["w_in"], ((small_parts,),) = wgrad(
        u, dproj, "wgrad_in", bn=CIN, chip_major=True, exchanges=[GatherSmall(small_part)])
    got = split_wait("chips_rest", chips_rest, plan_chips(rest_mats), after=[grads["w_in"]])[len(rest_mats):]
    chips_ff, chips_mix = got[:len(FF_MATS)], got[len(FF_MATS):]
    halves_rest, (from_sibling,) = sum_partials(
        "rest", rest_mats, list(own_ff) + list(own_mix), [[a] for a in list(chips_ff) + list(chips_mix)],
        [ExchangePair(("w_in",), [grads["w_in"]])])
    wire_in, own_in, (reduced_rest,) = pair_add(
        "in", ("w_in",), [grads["w_in"]], from_sibling, exchanges=[SwapHalves(halves_rest)])
    for n, red in zip(rest_mats, reduced_rest):
        g[n] = red.reshape(w[n].shape)

    def update(tag, names, steps, after=()):
        two_d = lambda a: a.reshape(-1, a.shape[-1])
        res = adamw(tag, [two_d(w[n]) for n in names], [two_d(g[n]) for n in names],
                    [two_d(m[n]) for n in names], [two_d(v[n]) for n in names], steps, after)
        for n, outs in zip(names, res):
            delta[n], new_m[n], new_v[n], g[n] = [a.reshape(w[n].shape) for a in outs]

    chips_in, token = split_start("chips_in", list(wire_in) + landing(("w_in",), NCHIP - 1, BF16),
                                  plan_chips(("w_in",)), NCHIP - 1)
    small, loss8 = sum_small(small_parts, after=[token])
    loss = loss8[0, 0]
    for name in VECS:
        g[name] = small[ROW[name]]
    g["dw_kernel"] = lax.dynamic_slice(small[VEC_ROWS:VEC_ROWS + KW_PAD], (0, chip * DSH), (KW_PAD, DSH))
    update("rest", rest_mats, 8, after=[token])
    stack = lambda d: jnp.concatenate([d[name].reshape(1, D) for name in VECS], axis=0)
    (res,) = adamw("vectors", [stack(w)], [stack(g)], [stack(m)], [stack(v)], 1, after=[token])
    for i, name in enumerate(VECS):
        delta[name], new_m[name], new_v[name] = [r[i] for r in res[:3]]
    padk = lambda a: jnp.pad(a, ((0, KW_PAD - KW), (0, 0)))
    (res,) = adamw("dw_kernel", [padk(w["dw_kernel"])], [g["dw_kernel"]], [padk(m["dw_kernel"])],
                   [padk(v["dw_kernel"])], 1, after=[token])
    delta["dw_kernel"], new_m["dw_kernel"], new_v["dw_kernel"] = [r[:KW] for r in res[:3]]
    g["dw_kernel"] = g["dw_kernel"][:KW]
    chips_in = split_wait("chips_in", chips_in, plan_chips(("w_in",)),
                          after=[delta[rest_mats[-1]], delta["dw_kernel"], delta[VECS[0]]])[1:]
    halves_in, _ = sum_partials("in", ("w_in",), own_in, [chips_in])
    (reduced_in,) = exchange("swap_w_in", SwapHalves(halves_in))
    g["w_in"] = reduced_in.reshape(w["w_in"].shape)
    update("in", ("w_in",), 8)

    return (loss, dx[None], *[g[n] for n in WEIGHTS], *[delta[n] for n in WEIGHTS], *[new_m[n] for n in WEIGHTS],
            *[new_v[n] for n in WEIGHTS])
```

```python
import math

import jax
import jax.numpy as jnp
from jax import lax
from jax.experimental import pallas as pl
from jax.experimental.pallas import tpu as pltpu

F32 = jnp.float32
BF16 = jnp.bfloat16

D = 1024
FF = 4096
NPROJ = 5
KW = 31
KW_PAD = 32
SUBLANES = 8
LANES = 128
HALO = 32
POOL_WINDOWS = (2, 4, 8, 16)
NG = 4
GW = D // NG
RMS_EPS = 1e-6
LN_EPS = 1e-5
LR, B1, B2, ADAM_EPS, WD, STEP = 0.001, 0.9, 0.999, 1e-08, 0.01, 10
NCHIP = 4
VMEM_LIMIT = 60 * 1024 * 1024
MESH = pl.DeviceIdType.MESH
TM = 256
TM_IN = 512

ANY = pl.BlockSpec(memory_space=pl.ANY)
VMEM_SPEC = pl.BlockSpec(memory_space=pltpu.VMEM)
SDS = jax.ShapeDtypeStruct


def _cp(**kw):
    return pltpu.CompilerParams(vmem_limit_bytes=VMEM_LIMIT, **kw)


def _mm(a, b):
    return jnp.dot(a, b, preferred_element_type=F32)


def _mm_nt(a, b):
    return lax.dot_general(a, b, (((1,), (1,)), ((), ())), preferred_element_type=F32)


def _mm_tn(a, b):
    return lax.dot_general(a, b, (((0,), (0,)), ((), ())), preferred_element_type=F32)


def _sigmoid(x):
    return 1.0 / (1.0 + jnp.exp(-x))


def _rowsum(x):
    return jnp.sum(x, axis=0, keepdims=True)


def _full(shape):
    return pl.BlockSpec(shape, lambda i: (0,) * len(shape))


def _tile(tm, cols):
    return pl.BlockSpec((tm, cols), lambda i: (i, 0))


def _prev_halo(tm):
    return pl.BlockSpec((HALO, D), lambda i: (jnp.maximum(i * (tm // HALO) - 1, 0), 0))


def _next_halo(tm, nt):
    return pl.BlockSpec((HALO, D), lambda i: (jnp.minimum((i + 1) * (tm // HALO), nt * (tm // HALO) - 1), 0))


MATS = ("w_in", "w_conv_out", "pool_w", "w_pool_out", "w_o", "w_ff1", "w_ff2")
CIN = NPROJ * D // NCHIP
CFF = FF // NCHIP
_ROWS = lambda k, h: (2 * k + h, 0)
_CHIP_MAJOR = lambda k, h: (k, h, 0)
GEOM = dict(
    w_in=((NCHIP, D, CIN), (None, D // 2, CIN), _CHIP_MAJOR),
    w_conv_out=((D, D), (D // (2 * NCHIP), D), _ROWS),
    pool_w=((NG, GW, GW), (NG // 2, GW // NCHIP, GW), lambda k, h: (h, k, 0)),
    w_pool_out=((D, D), (D // (2 * NCHIP), D), _ROWS),
    w_o=((D, D), (D // (2 * NCHIP), D), _ROWS),
    w_ff1=((NCHIP, D, CFF), (None, D // 2, CFF), _CHIP_MAJOR),
    w_ff2=((FF, D), (FF // (2 * NCHIP), D), _ROWS),
)
DSH = D // NCHIP


def _half_shape(name):
    return tuple(b for b in GEOM[name][1] if b is not None)


def _window(name, k, h):
    _, blk, idx = GEOM[name]
    return tuple(i if b is None else pl.ds(i * b, b) for i, b in zip(idx(k, h), blk))


def _shard_half(name, h):
    n0 = _half_shape(name)[0]
    return (pl.ds(h * n0, n0),) + (slice(None),) * (len(_half_shape(name)) - 1)


def _place():
    x, y, c = lax.axis_index("x"), lax.axis_index("y"), lax.axis_index("c")
    chips = [(1 - x, y), (x, 1 - y), (1 - x, 1 - y)]
    return x, y, c, 2 * x + y, chips, [2 * px + py for px, py in chips]


def _remote(src, dst, send_sem, recv_sem, to):
    return pltpu.make_async_remote_copy(src_ref=src, dst_ref=dst, send_sem=send_sem, recv_sem=recv_sem,
                                        device_id=to, device_id_type=MESH)


class GatherWeights:
    has_mid = True

    def __init__(self, names, fulls, taps=None):
        self.names = names
        self.ins = list(fulls) + ([taps] if taps is not None else [])
        self.has_taps = taps is not None
        self.out_shapes = [SDS(a.shape, a.dtype) for a in self.ins]
        self.aliases = [(i, i) for i in range(len(self.ins))]
        n = len(self.ins)
        self.sems = [pltpu.SemaphoreType.DMA((n, 6)), pltpu.SemaphoreType.DMA((n, 6))]

    def _copies(self, ins, outs, sems):
        send_sems, recv_sems = sems
        x, y, c, me, chips, chip_ids = _place()
        sibling = (x, y, 1 - c)
        ici, ici_recv, d2d, d2d_recv = [], [], [], []
        for w, name in enumerate(self.names):
            for j, chip in enumerate(chips):
                ici.append(_remote(ins[w].at[_window(name, me, c)], outs[w].at[_window(name, me, c)],
                                   send_sems.at[w, j], recv_sems.at[w, j], (*chip, c)))
                got = outs[w].at[_window(name, chip_ids[j], c)]
                ici_recv.append(_remote(got, got, send_sems.at[w, j], recv_sems.at[w, j], sibling))
                d2d.append(_remote(got, got, send_sems.at[w, 3 + j], recv_sems.at[w, 3 + j], sibling))
                got = outs[w].at[_window(name, chip_ids[j], 1 - c)]
                d2d_recv.append(_remote(got, got, send_sems.at[w, 3 + j], recv_sems.at[w, 3 + j], sibling))
        if self.has_taps:
            w = len(self.names)
            for j, chip in enumerate(chips):
                ici.append(_remote(ins[w].at[:, pl.ds(me * DSH, DSH)], outs[w].at[:, pl.ds(me * DSH, DSH)],
                                   send_sems.at[w, j], recv_sems.at[w, j], (*chip, c)))
                got = outs[w].at[:, pl.ds(chip_ids[j] * DSH, DSH)]
                d2d_recv.append(_remote(got, got, send_sems.at[w, j], recv_sems.at[w, j], sibling))
        return ici, ici_recv, d2d, d2d_recv

    def start(self, ins, outs, sems):
        for cp in self._copies(ins, outs, sems)[0]:
            cp.start()

    def mid(self, ins, outs, sems):
        _, ici_recv, d2d, _ = self._copies(ins, outs, sems)
        for got, fwd in zip(ici_recv, d2d):
            got.wait_recv()
            fwd.start()

    def finish(self, ins, outs, sems):
        ici, _, d2d, d2d_recv = self._copies(ins, outs, sems)
        for cp in d2d_recv:
            cp.wait_recv()
        for cp in ici + d2d:
            cp.wait_send()


class ExchangePair:
    has_mid = False
    aliases = ()

    def __init__(self, names, grads):
        self.names, self.ins = names, list(grads)
        self.out_shapes = [SDS((NCHIP, *_half_shape(n)), F32) for n in names]
        self.sems = [pltpu.SemaphoreType.DMA((len(names),)), pltpu.SemaphoreType.DMA((len(names),))]

    def start(self, ins, outs, sems):
        send_sems, recv_sems = sems
        x, y, c, me, chips, chip_ids = _place()
        for w, name in enumerate(self.names):
            for k in range(NCHIP):
                _remote(ins[w].at[_window(name, k, 1 - c)], outs[w].at[k], send_sems.at[w], recv_sems.at[w],
                        (x, y, 1 - c)).start()

    def finish(self, ins, outs, sems):
        send_sems, recv_sems = sems
        x, y, c, me, chips, chip_ids = _place()
        for w in range(len(self.names)):
            _remote(outs[w], outs[w], send_sems.at[w], recv_sems.at[w], (x, y, 1 - c)).wait()


class ExchangeChips:
    has_mid = False
    aliases = ()

    def __init__(self, names, wires):
        self.names, self.ins = names, list(wires)
        self.out_shapes = [SDS((NCHIP - 1, *_half_shape(n)), BF16) for n in names]
        self.sems = [pltpu.SemaphoreType.DMA((len(names), NCHIP - 1)), pltpu.SemaphoreType.DMA((len(names), NCHIP - 1))]

    def _copies(self, ins, outs, sems):
        send_sems, recv_sems = sems
        x, y, c, me, chips, chip_ids = _place()
        return [_remote(ins[w].at[chip_ids[j]], outs[w].at[j], send_sems.at[w, j], recv_sems.at[w, j], (*chip, c))
                for w in range(len(self.names)) for j, chip in enumerate(chips)]

    def start(self, ins, outs, sems):
        for cp in self._copies(ins, outs, sems):
            cp.start()

    def finish(self, ins, outs, sems):
        for cp in self._copies(ins, outs, sems):
            cp.wait()


def _call(body, name, grid, in_specs, out_specs, out_shape, scratch, args, exchanges=(), after=()):
    n_in, n_out, n_scr = len(in_specs), len(out_specs), len(scratch)
    x_in = [a for e in exchanges for a in e.ins]
    x_out = [s for e in exchanges for s in e.out_shapes]
    x_sem = [s for e in exchanges for s in e.sems]
    nsteps = math.prod(grid)

    def wrapped(*refs):
        ins, rest = refs[:n_in], refs[n_in:]
        xin, rest = rest[:len(x_in)], rest[len(x_in) + len(after):]
        outs, rest = rest[:n_out], rest[n_out:]
        xout, rest = rest[:len(x_out)], rest[len(x_out):]
        scr, xsem = rest[:n_scr], rest[n_scr:]
        parts = []
        for e in exchanges:
            parts.append((xin[:len(e.ins)], xout[:len(e.out_shapes)], xsem[:len(e.sems)]))
            xin, xout, xsem = xin[len(e.ins):], xout[len(e.out_shapes):], xsem[len(e.sems):]
        if not grid:
            for e, p in zip(exchanges, parts):
                e.start(*p)
            body(*ins, *outs, *scr)
            for e, p in zip(exchanges, parts):
                if e.has_mid:
                    e.mid(*p)
            for e, p in zip(exchanges, parts):
                e.finish(*p)
            return
        step = 0
        for axis, extent in enumerate(grid):
            step = step * extent + pl.program_id(axis)
        if exchanges:
            @pl.when(step == 0)
            def _():
                for e, p in zip(exchanges, parts):
                    e.start(*p)

        body(*ins, *outs, *scr)
        if any(e.has_mid for e in exchanges):
            @pl.when(step == max(nsteps - 2, 0))
            def _():
                for e, p in zip(exchanges, parts):
                    if e.has_mid:
                        e.mid(*p)

        if exchanges:
            @pl.when(step == nsteps - 1)
            def _():
                for e, p in zip(exchanges, parts):
                    e.finish(*p)

    kw = dict(grid=grid, compiler_params=_cp(dimension_semantics=("arbitrary",) * len(grid))) if grid else dict(
        compiler_params=_cp())
    aliases, i0, o0 = {}, n_in, n_out
    for e in exchanges:
        aliases.update({i0 + i: o0 + o for i, o in e.aliases})
        i0, o0 = i0 + len(e.ins), o0 + len(e.out_shapes)
    res = pl.pallas_call(
        wrapped, name=name, in_specs=list(in_specs) + [ANY] * (len(x_in) + len(after)),
        out_specs=list(out_specs) + [ANY] * len(x_out),
        out_shape=list(out_shape) + x_out, scratch_shapes=list(scratch) + x_sem, input_output_aliases=aliases, **kw,
    )(*args, *x_in, *after)
    outs, rest = res[:n_out], res[n_out:]
    xouts = []
    for e in exchanges:
        xouts.append(rest[:len(e.out_shapes)])
        rest = rest[len(e.out_shapes):]
    return outs, xouts


def exchange(name, ex):
    return _call(lambda: None, name, (), [], [], [], [], [], [ex])[1][0]


def _taps_scratch(tm):
    return pltpu.VMEM((SUBLANES, tm + HALO, LANES), F32)


def _taps(src_ref, k_ref, dst_ref, sh_ref, tm, off0, reverse):
    rc, cw = 64, LANES

    def col_chunk(cc, carry):
        cols = pl.ds(pl.multiple_of(cc * cw, cw), cw)
        for q in range(SUBLANES):
            n = tm + SUBLANES * (len(range(q, KW, SUBLANES)) - 1)
            sh_ref[q, 0:n, :] = src_ref[pl.ds(off0 + q, n), cols]
        for r in range(tm // rc):
            acc = jnp.zeros((rc, cw), F32)
            for q in range(SUBLANES):
                for a, j in enumerate(range(q, KW, SUBLANES)):
                    kj = KW - 1 - j if reverse else j
                    acc = acc + k_ref[kj:kj + 1, cols] * sh_ref[q, pl.ds(r * rc + SUBLANES * a, rc), :]
            dst_ref[pl.ds(r * rc, rc), cols] = acc
        return carry

    lax.fori_loop(0, D // cw, col_chunk, 0)


def _proj_pieces(g):
    lo, hi, pieces = g * D, (g + 1) * D, []
    while lo < hi:
        k = lo // CIN
        b = min(hi - k * CIN, CIN)
        pieces.append((k, lo - k * CIN, b))
        lo = k * CIN + b
    return pieces


def fwd_in(x, g1, w_in, tm, exchanges=()):
    t = x.shape[0]

    def body(x_ref, g_ref, w_ref, u_ref, glu_ref, ag_ref, p_ref, gt_ref):
        xf = x_ref[...]
        r = lax.rsqrt(jnp.mean(xf * xf, axis=-1, keepdims=True) + RMS_EPS)
        u = (xf * r * g_ref[...]).astype(BF16)
        u_ref[...] = u
        proj = lambda g: jnp.concatenate([_mm(u, w_ref[k, :, lo:hi]) for k, lo, hi in _proj_pieces(g)], axis=1)
        a = proj(0)
        gate = proj(1)
        glu_ref[...] = a * _sigmoid(gate)
        ag_ref[:, 0:D] = a.astype(BF16)
        ag_ref[:, D:2 * D] = gate.astype(BF16)
        p_ref[...] = proj(2)
        gt_ref[:, 0:D] = proj(3).astype(BF16)
        gt_ref[:, D:2 * D] = proj(4).astype(BF16)

    return _call(
        body, "fwd_in", (t // tm,),
        [_tile(tm, D), _full((1, D)), _full((NCHIP, D, CIN))],
        [_tile(tm, D), _tile(tm, D), _tile(tm, 2 * D), _tile(tm, D), _tile(tm, 2 * D)],
        [SDS((t, D), BF16), SDS((t, D), F32), SDS((t, 2 * D), BF16), SDS((t, D), F32), SDS((t, 2 * D), BF16)],
        [], [x, g1, w_in], exchanges)


def _pool_inv_count(i, tm, w):
    pos = i * tm + lax.broadcasted_iota(jnp.int32, (tm, 1), 0) + 1
    return 1.0 / jnp.minimum(pos, w).astype(F32)


def _window_sum(src_ref, tmp_ref, cols, tm, w, causal):
    lo, hi = 0, tm + HALO
    cur, span = None, 1
    while span < w:
        new_lo, new_hi = (lo + SUBLANES, hi) if causal else (lo, hi - SUBLANES)
        far = new_lo - span if causal else new_lo + span
        n = new_hi - new_lo
        if cur is None:
            near_v, far_v = src_ref[pl.ds(new_lo, n), cols], src_ref[pl.ds(far, n), cols]
        else:
            near_v = cur[new_lo - lo:new_lo - lo + n]
            if span % SUBLANES == 0:
                far_v = cur[far - lo:far - lo + n]
            else:
                tmp_ref[pl.ds(lo, hi - lo), :] = cur
                far_v = tmp_ref[pl.ds(far, n), :]
        cur, lo, hi, span = near_v + far_v, new_lo, new_hi, 2 * span
    off = HALO if causal else 0
    return cur[off - lo:off - lo + tm]


def fwd_mix(x, glu, p, gt, dwk, dwb, lng, lnb, w_co, pool_w, ps, w_po, w_o, g2, tm, exchanges=()):
    t = x.shape[0]

    def body(x_ref, glu_ref, gluh_ref, p_ref, ph_ref, gt_ref, k_ref, b_ref, lg_ref, lb_ref, wco_ref, pw_ref,
             ps_ref, wpo_ref, wo_ref, g2_ref,
             cv_ref, sw_ref, z_ref, zl_ref, zs_ref, yc_ref, yp_ref, mg_ref, mo_ref, h1_ref, ext_ref, win_ref, sh_ref):
        i = pl.program_id(0)
        keep = (i > 0).astype(F32)
        ext_ref[0:HALO, :] = gluh_ref[...] * keep
        ext_ref[HALO:HALO + tm, :] = glu_ref[...]
        _taps(ext_ref, k_ref, cv_ref, sh_ref, tm, HALO - (KW - 1), False)
        cv = cv_ref[...] + b_ref[...]
        cv_ref[...] = cv
        mu = jnp.mean(cv, axis=-1, keepdims=True)
        cen = cv - mu
        rstd = lax.rsqrt(jnp.mean(cen * cen, axis=-1, keepdims=True) + LN_EPS)
        ln = cen * rstd * lg_ref[...] + lb_ref[...]
        sw = (ln * _sigmoid(ln)).astype(BF16)
        sw_ref[...] = sw
        yc = _mm(sw, wco_ref[...])
        yc_ref[...] = yc.astype(BF16)
        ext_ref[0:HALO, :] = ph_ref[...] * keep
        ext_ref[HALO:HALO + tm, :] = p_ref[...]
        for g, w in enumerate(POOL_WINDOWS):
            cols = pl.ds(g * GW, GW)
            acc = _window_sum(ext_ref, win_ref, cols, tm, w, True)
            zg = (acc * _pool_inv_count(i, tm, w) - p_ref[:, cols]).astype(BF16)
            z_ref[:, cols] = zg
            zl = _mm(zg, pw_ref[g])
            zl_ref[:, cols] = zl.astype(BF16)
            zs_ref[:, cols] = (zl * ps_ref[:, cols]).astype(BF16)
        yp = _mm(zs_ref[...], wpo_ref[...])
        yp_ref[...] = yp.astype(BF16)
        gc = _sigmoid(gt_ref[:, 0:D].astype(F32))
        gp = _sigmoid(gt_ref[:, D:2 * D].astype(F32))
        mg = (gc * yc + gp * yp).astype(BF16)
        mg_ref[...] = mg
        mo = _mm(mg, wo_ref[...])
        mo_ref[...] = mo
        r2 = lax.rsqrt(jnp.mean(mo * mo, axis=-1, keepdims=True) + RMS_EPS)
        h1_ref[...] = x_ref[...] + mo * r2 * g2_ref[...]

    vec = _full((1, D))
    act = lambda dt: SDS((t, D), dt)
    return _call(
        body, "fwd_mix", (t // tm,),
        [_tile(tm, D), _tile(tm, D), _prev_halo(tm), _tile(tm, D), _prev_halo(tm), _tile(tm, 2 * D),
         _full((KW_PAD, D)), vec, vec, vec, _full((D, D)), _full((NG, GW, GW)), vec, _full((D, D)), _full((D, D)), vec],
        [_tile(tm, D)] * 10,
        [act(F32), act(BF16), act(BF16), act(BF16), act(BF16), act(BF16), act(BF16), act(BF16), act(F32), act(F32)],
        [pltpu.VMEM((tm + HALO, D), F32), pltpu.VMEM((tm + HALO, GW), F32), _taps_scratch(tm)],
        [x, glu, glu, p, p, gt, dwk, dwb, lng, lnb, w_co, pool_w, ps, w_po, w_o, g2], exchanges)


def mlp_fwd_bwd(h1, tgt, g3, g4, w1, w2, tm, exchanges=()):
    t = h1.shape[0]
    fc = CFF

    def body(h1_ref, tgt_ref, g3_ref, g4_ref, w1_ref, w2_ref,
             v_ref, a2_ref, df2_ref, df1_ref, dh1_ref, vec_ref, f1_ref):
        i = pl.program_id(0)

        @pl.when(i == 0)
        def _():
            vec_ref[...] = jnp.zeros_like(vec_ref)

        h1v = h1_ref[...]
        r3 = lax.rsqrt(jnp.mean(h1v * h1v, axis=-1, keepdims=True) + RMS_EPS)
        n3 = h1v * r3
        v = (n3 * g3_ref[...]).astype(BF16)
        v_ref[...] = v
        f2 = jnp.zeros((tm, D), F32)
        for c in range(FF // fc):
            cols = pl.ds(c * fc, fc)
            f1 = jnp.maximum(_mm(v, w1_ref[c]), 0.0)
            f1_ref[:, cols] = f1
            a2 = (f1 * f1).astype(BF16)
            a2_ref[:, cols] = a2
            f2 = f2 + _mm(a2, w2_ref[cols, :])
        r4 = lax.rsqrt(jnp.mean(f2 * f2, axis=-1, keepdims=True) + RMS_EPS)
        n4 = f2 * r4
        err = h1v + n4 * g4_ref[...] - tgt_ref[...]
        vec_ref[2:3, :] += _rowsum(err * err) * (0.5 / D)
        dh2 = err * (1.0 / D)
        vec_ref[1:2, :] += _rowsum(dh2 * n4)
        dn4 = dh2 * g4_ref[...]
        df2 = (r4 * (dn4 - n4 * jnp.mean(dn4 * n4, axis=-1, keepdims=True))).astype(BF16)
        df2_ref[...] = df2
        dv = jnp.zeros((tm, D), F32)
        for c in range(FF // fc):
            cols = pl.ds(c * fc, fc)
            da2 = _mm_nt(df2, w2_ref[cols, :])
            df1 = (da2 * (2.0 * f1_ref[:, cols])).astype(BF16)
            df1_ref[:, cols] = df1
            dv = dv + _mm_nt(df1, w1_ref[c])
        vec_ref[0:1, :] += _rowsum(dv * n3)
        dn3 = dv * g3_ref[...]
        dh1_ref[...] = dh2 + r3 * (dn3 - n3 * jnp.mean(dn3 * n3, axis=-1, keepdims=True))

    vec = _full((1, D))
    return _call(
        body, "mlp_fwd_bwd", (t // tm,),
        [_tile(tm, D), _tile(tm, D), vec, vec, _full((NCHIP, D, CFF)), _full((FF, D))],
        [_tile(tm, D), _tile(tm, FF), _tile(tm, D), _tile(tm, FF), _tile(tm, D), _full((8, D))],
        [SDS((t, D), BF16), SDS((t, FF), BF16), SDS((t, D), BF16), SDS((t, FF), BF16), SDS((t, D), F32),
         SDS((8, D), F32)],
        [pltpu.VMEM((tm, FF), F32)], [h1, tgt, g3, g4, w1, w2], exchanges)


def bwd_mix(dh1, mo, cv, zl, yc, yp, gt, lng, lnb, ps, g2, w_co, pool_w, w_po, w_o, tm, exchanges=()):
    t = dh1.shape[0]

    def body(dh1_ref, mo_ref, cv_ref, zl_ref, yc_ref, yp_ref, gt_ref, lg_ref, lb_ref, ps_ref, g2_ref,
             wco_ref, pw_ref, wpo_ref, wo_ref,
             dmo_ref, dgt_ref, dyc_ref, dyp_ref, dcv_ref, dzl_ref, dz_ref, vec_ref):
        i = pl.program_id(0)

        @pl.when(i == 0)
        def _():
            vec_ref[...] = jnp.zeros_like(vec_ref)

        dh1v = dh1_ref[...]
        mo = mo_ref[...]
        r2 = lax.rsqrt(jnp.mean(mo * mo, axis=-1, keepdims=True) + RMS_EPS)
        n2 = mo * r2
        vec_ref[0:1, :] += _rowsum(dh1v * n2)
        dn2 = dh1v * g2_ref[...]
        dmo = (r2 * (dn2 - n2 * jnp.mean(dn2 * n2, axis=-1, keepdims=True))).astype(BF16)
        dmo_ref[...] = dmo
        dmg = _mm_nt(dmo, wo_ref[...])
        gc = _sigmoid(gt_ref[:, 0:D].astype(F32))
        gp = _sigmoid(gt_ref[:, D:2 * D].astype(F32))
        dgt_ref[:, 0:D] = (dmg * yc_ref[...].astype(F32) * gc * (1.0 - gc)).astype(BF16)
        dgt_ref[:, D:2 * D] = (dmg * yp_ref[...].astype(F32) * gp * (1.0 - gp)).astype(BF16)
        dyc = (dmg * gc).astype(BF16)
        dyp = (dmg * gp).astype(BF16)
        dyc_ref[...] = dyc
        dyp_ref[...] = dyp
        dsw = _mm_nt(dyc, wco_ref[...])
        cv = cv_ref[...]
        mu = jnp.mean(cv, axis=-1, keepdims=True)
        cen = cv - mu
        rstd = lax.rsqrt(jnp.mean(cen * cen, axis=-1, keepdims=True) + LN_EPS)
        y = cen * rstd
        ln = y * lg_ref[...] + lb_ref[...]
        sg = _sigmoid(ln)
        dln = dsw * (sg * (1.0 + ln * (1.0 - sg)))
        vec_ref[1:2, :] += _rowsum(dln * y)
        vec_ref[2:3, :] += _rowsum(dln)
        dy = dln * lg_ref[...]
        dcv = rstd * (dy - jnp.mean(dy, axis=-1, keepdims=True) - y * jnp.mean(dy * y, axis=-1, keepdims=True))
        dcv_ref[...] = dcv.astype(BF16)
        vec_ref[3:4, :] += _rowsum(dcv)
        dzs = _mm_nt(dyp, wpo_ref[...])
        vec_ref[4:5, :] += _rowsum(dzs * zl_ref[...].astype(F32))
        dzl = (dzs * ps_ref[...]).astype(BF16)
        dzl_ref[...] = dzl
        for g in range(NG):
            cols = pl.ds(g * GW, GW)
            dz_ref[:, cols] = _mm_nt(dzl_ref[:, cols], pw_ref[g]).astype(BF16)

    vec = _full((1, D))
    act = lambda dt: SDS((t, D), dt)
    return _call(
        body, "bwd_mix", (t // tm,),
        [_tile(tm, D)] * 6 + [_tile(tm, 2 * D), vec, vec, vec, vec, _full((D, D)), _full((NG, GW, GW)), _full((D, D)),
                              _full((D, D))],
        [_tile(tm, D), _tile(tm, 2 * D)] + [_tile(tm, D)] * 5 + [_full((8, D))],
        [act(BF16), SDS((t, 2 * D), BF16), act(BF16), act(BF16), act(BF16), act(BF16), act(BF16), SDS((8, D), F32)],
        [], [dh1, mo, cv, zl, yc, yp, gt, lng, lnb, ps, g2, w_co, pool_w, w_po, w_o], exchanges)


def bwd_in(x, dh1, dcv, dz, glu, ag, dgt, g1, dwk, w_in, tm, after=()):
    t = x.shape[0]
    nt = t // tm

    def body(x_ref, dh1_ref, dcv_ref, dcvh_ref, dz_ref, dzh_ref, glu_ref, gluh_ref, ag_ref, dgt_ref, g1_ref,
             k_ref, w_ref, dx_ref, dproj_ref, vec_ref, dk_ref, ext_ref, tmp_ref, win_ref, sh_ref, gext_ref, gsh_ref):
        i = pl.program_id(0)

        @pl.when(i == 0)
        def _():
            vec_ref[...] = jnp.zeros_like(vec_ref)
            dk_ref[...] = jnp.zeros_like(dk_ref)

        first = (i > 0).astype(F32)
        last = (i < nt - 1).astype(F32)
        gext_ref[0:HALO, :] = gluh_ref[...] * first
        gext_ref[HALO:HALO + tm, :] = glu_ref[...]
        rc, cw = 32, LANES

        def dk_chunk(cc):
            cols = pl.ds(cc * cw, cw)
            for q in range(SUBLANES):
                taps = range(q, KW, SUBLANES)
                n = tm + SUBLANES * (len(taps) - 1)
                gsh_ref[q, 0:n, :] = gext_ref[pl.ds(HALO - (KW - 1) + q, n), cols]
                accs = [jnp.zeros((SUBLANES, cw), F32) for _ in taps]
                for r in range(tm // rc):
                    dchunk = dcv_ref[pl.ds(r * rc, rc), cols].astype(F32)
                    for a in range(len(taps)):
                        prod = dchunk * gsh_ref[q, pl.ds(r * rc + SUBLANES * a, rc), :]
                        accs[a] = accs[a] + jnp.sum(prod.reshape(rc // SUBLANES, SUBLANES, cw), axis=0)
                for a, j in enumerate(taps):
                    dk_ref[j:j + 1, cols] += _rowsum(accs[a])

        ext_ref[0:tm, :] = dcv_ref[...].astype(F32)
        ext_ref[tm:tm + HALO, :] = dcvh_ref[...].astype(F32) * last
        _taps(ext_ref, k_ref, tmp_ref, sh_ref, tm, 0, True)
        dglu = tmp_ref[...]
        a = ag_ref[:, 0:D].astype(F32)
        sg = _sigmoid(ag_ref[:, D:2 * D].astype(F32))
        dproj_ref[:, 0:D] = (dglu * sg).astype(BF16)
        dproj_ref[:, D:2 * D] = (dglu * a * sg * (1.0 - sg)).astype(BF16)
        for g, w in enumerate(POOL_WINDOWS):
            cols = pl.ds(g * GW, GW)
            pos = i * tm + lax.broadcasted_iota(jnp.int32, (tm + HALO, 1), 0) + 1
            inv = 1.0 / jnp.minimum(pos, w).astype(F32)
            dzg = dz_ref[:, cols].astype(F32)
            ext_ref[0:tm, cols] = dzg * inv[0:tm]
            ext_ref[tm:tm + HALO, cols] = dzh_ref[:, cols].astype(F32) * inv[tm:tm + HALO] * last
            acc = _window_sum(ext_ref, win_ref, cols, tm, w, False)
            dproj_ref[:, pl.ds(2 * D + g * GW, GW)] = (acc - dzg).astype(BF16)
        dproj_ref[:, 3 * D:5 * D] = dgt_ref[...]
        chunks_per_matmul = D // cw // NCHIP
        for k in range(NCHIP):
            part = _mm_nt(dproj_ref[:, k * CIN:(k + 1) * CIN], w_ref[k])
            if k == 0:
                tmp_ref[...] = part
            else:
                tmp_ref[...] += part
            for cc in range(k * chunks_per_matmul, (k + 1) * chunks_per_matmul):
                dk_chunk(cc)
        du = tmp_ref[...]
        xf = x_ref[...]
        r1 = lax.rsqrt(jnp.mean(xf * xf, axis=-1, keepdims=True) + RMS_EPS)
        n1 = xf * r1
        vec_ref[0:1, :] += _rowsum(du * n1)
        dn1 = du * g1_ref[...]
        dx_ref[...] = dh1_ref[...] + r1 * (dn1 - n1 * jnp.mean(dn1 * n1, axis=-1, keepdims=True))

    return _call(
        body, "bwd_in", (nt,),
        [_tile(tm, D), _tile(tm, D), _tile(tm, D), _next_halo(tm, nt), _tile(tm, D), _next_halo(tm, nt),
         _tile(tm, D), _prev_halo(tm), _tile(tm, 2 * D), _tile(tm, 2 * D), _full((1, D)),
         _full((KW_PAD, D)), _full((NCHIP, D, CIN))],
        [_tile(tm, D), _tile(tm, NPROJ * D), _full((8, D)), _full((KW_PAD, D))],
        [SDS((t, D), F32), SDS((t, NPROJ * D), BF16), SDS((8, D), F32), SDS((KW_PAD, D), F32)],
        [pltpu.VMEM((tm + HALO, D), F32), pltpu.VMEM((tm, D), F32), pltpu.VMEM((tm + HALO, GW), F32),
         _taps_scratch(tm), pltpu.VMEM((tm + HALO, D), F32), _taps_scratch(tm)],
        [x, dh1, dcv, dcv, dz, dz, glu, glu, ag, dgt, g1, dwk, w_in], after=after)[0]


def wgrad(a, b, name, bm=1024, bn=1024, bt=2048, chip_major=False, exchanges=()):
    t, m = a.shape
    n = b.shape[1]
    bm, bn, bt = min(bm, m), min(bn, n), min(bt, t)
    assert m % bm == 0 and n % bn == 0 and t % bt == 0, (a.shape, b.shape, bm, bn, bt)

    def body(a_ref, b_ref, o_ref):
        k = pl.program_id(2)

        @pl.when(k == 0)
        def _():
            o_ref[...] = jnp.zeros_like(o_ref)

        o_ref[...] += _mm_tn(a_ref[...], b_ref[...])

    if chip_major:
        out_spec, out_shape = pl.BlockSpec((None, bm, bn), lambda i, j, k: (j, i, 0)), SDS((n // bn, m, bn), F32)
    else:
        out_spec, out_shape = pl.BlockSpec((bm, bn), lambda i, j, k: (i, j)), SDS((m, n), F32)
    outs, xouts = _call(
        body, name, (m // bm, n // bn, t // bt),
        [pl.BlockSpec((bt, bm), lambda i, j, k: (k, i)), pl.BlockSpec((bt, bn), lambda i, j, k: (k, j))],
        [out_spec], [out_shape], [], [a, b], exchanges)
    return outs[0], xouts


def wgrad_pool(z, dzl, bt=4096):
    t = z.shape[0]
    bt = min(bt, t)
    assert t % bt == 0

    def body(a_ref, b_ref, o_ref):
        k = pl.program_id(1)

        @pl.when(k == 0)
        def _():
            o_ref[...] = jnp.zeros_like(o_ref)

        o_ref[0] += _mm_tn(a_ref[...], b_ref[...])

    return _call(
        body, "wgrad_pool", (NG, t // bt),
        [pl.BlockSpec((bt, GW), lambda g, k: (k, g)), pl.BlockSpec((bt, GW), lambda g, k: (k, g))],
        [pl.BlockSpec((1, GW, GW), lambda g, k: (g, 0, 0))], [SDS((NG, GW, GW), F32)], [], [z, dzl])[0][0]


def cast_shards(name, names, shards, exchanges=()):
    n = len(shards)

    def body(*refs):
        srcs, dsts, bufs, sems = refs[:n], refs[n:2 * n], refs[2 * n:3 * n], refs[3 * n]
        me = 2 * lax.axis_index("x") + lax.axis_index("y")
        copies = []
        for w, mat in enumerate(names):
            bufs[w][...] = srcs[w][...].astype(BF16)
            for h in range(2):
                cp = pltpu.make_async_copy(bufs[w].at[_shard_half(mat, h)], dsts[w].at[_window(mat, me, h)], sems.at[w, h])
                cp.start()
                copies.append(cp)
        for cp in copies:
            cp.wait()

    return _call(body, name, (), [VMEM_SPEC] * n, [ANY] * n, [SDS(GEOM[mat][0], BF16) for mat in names],
                 [pltpu.VMEM(s.shape, BF16) for s in shards] + [pltpu.SemaphoreType.DMA((n, 2))], list(shards), exchanges)


VEC_ROWS = 24
ROW = dict(mlp_pre_g=0, mlp_post_g=1, loss=2, mix_post_g=8, conv_ln_g=9, conv_ln_b=10, dw_bias=11, pool_scale=12,
           mix_pre_g=16)


NDEV = 8


class GatherSmall:
    has_mid = False
    aliases = ()

    def __init__(self, part):
        self.ins = [part]
        self.out_shapes = [SDS((NDEV, *part.shape), F32)]
        self.sems = [pltpu.SemaphoreType.DMA((NDEV,)), pltpu.SemaphoreType.DMA((NDEV,))]

    def _copies(self, ins, outs, sems):
        send_sems, recv_sems = sems
        x, y, c = lax.axis_index("x"), lax.axis_index("y"), lax.axis_index("c")
        me = 4 * x + 2 * y + c
        own = pltpu.make_async_copy(ins[0], outs[0].at[me], send_sems.at[0])
        sends = [_remote(ins[0], outs[0].at[me], send_sems.at[m], recv_sems.at[m],
                         (x ^ (m >> 2), y ^ ((m >> 1) & 1), c ^ (m & 1))) for m in range(1, NDEV)]
        recvs = [_remote(outs[0].at[me ^ m], outs[0].at[me ^ m], send_sems.at[m], recv_sems.at[m], (x, y, c))
                 for m in range(1, NDEV)]
        return own, sends, recvs

    def start(self, ins, outs, sems):
        own, sends, _ = self._copies(ins, outs, sems)
        for cp in sends + [own]:
            cp.start()

    def finish(self, ins, outs, sems):
        own, sends, recvs = self._copies(ins, outs, sems)
        for cp in recvs:
            cp.wait_recv()
        for cp in sends:
            cp.wait_send()
        own.wait()


def sum_small(parts, after=()):
    def body(parts_ref, sum_ref, loss_ref):
        total = parts_ref[0]
        for d in range(1, NDEV):
            total = total + parts_ref[d]
        sum_ref[...] = total
        r = ROW["loss"]
        loss_ref[...] = jnp.zeros_like(loss_ref) + jnp.sum(total[r:r + 1, :])

    return _call(body, "sum_small", (), [VMEM_SPEC], [VMEM_SPEC, VMEM_SPEC],
                 [SDS(parts.shape[1:], F32), SDS((8, 128), F32)], [], [parts], after=after)[0]


def pair_add(tag, names, grads, from_sibling, exchanges=()):
    n = len(names)
    in_specs, wire_specs, own_specs, wire_shapes, own_shapes = [], [], [], [], []
    for name in names:
        _, gblk, idx = GEOM[name]
        blk = _half_shape(name)
        zeros = (0,) * len(blk)
        in_specs.append(pl.BlockSpec(gblk, lambda k, idx=idx: idx(k, lax.axis_index("c"))))
        wire_specs.append(pl.BlockSpec((1, *blk), lambda k, zeros=zeros: (k, *zeros)))
        own_specs.append(pl.BlockSpec(blk, lambda k, zeros=zeros: zeros))
        wire_shapes.append(SDS((NCHIP, *blk), BF16))
        own_shapes.append(SDS(blk, F32))

    def body(*refs):
        g, s, wire, own = refs[:n], refs[n:2 * n], refs[2 * n:3 * n], refs[3 * n:]
        mine = pl.program_id(0) == 2 * lax.axis_index("x") + lax.axis_index("y")
        for w in range(n):
            total = g[w][...] + s[w][0]
            wire[w][0] = total.astype(BF16)

            @pl.when(mine)
            def _(w=w, total=total):
                own[w][...] = total

    outs, xouts = _call(body, "pair_add_" + tag, (NCHIP,), in_specs + wire_specs, wire_specs + own_specs,
                        wire_shapes + own_shapes, [], list(grads) + list(from_sibling), exchanges)
    return outs[:n], outs[n:], xouts


SUM_STEPS = 4


def sum_partials(tag, names, owns, from_chips, exchanges=()):
    n = len(names)
    own_specs, part_specs, out_specs, out_shapes, part_args, counts = [], [], [], [], [], []
    for name, parts in zip(names, from_chips):
        half = _half_shape(name)
        blk = half[:-2] + (half[-2] // SUM_STEPS, half[-1])
        lead = (0,) * (len(half) - 2)
        own_specs.append(pl.BlockSpec(blk, lambda i, lead=lead: (*lead, i, 0)))
        for p in parts:
            part_specs.append(pl.BlockSpec((p.shape[0], *blk), lambda i, lead=lead: (0, *lead, i, 0)))
            part_args.append(p)
        counts.append(len(parts))
        out_specs.append(pl.BlockSpec((2, *blk), lambda i, lead=lead: (0, *lead, i, 0)))
        out_shapes.append(SDS((2, *half), F32))

    def body(*refs):
        own, parts, out = refs[:n], list(refs[n:n + len(part_args)]), refs[n + len(part_args):]
        c = lax.axis_index("c")
        for w in range(n):
            total = own[w][...]
            for p in [parts.pop(0) for _ in range(counts[w])]:
                for j in range(p.shape[0]):
                    total = total + p[j].astype(F32)
            out[w][c] = total

    return _call(body, "sum_partials_" + tag, (SUM_STEPS,), own_specs + part_specs, out_specs, out_shapes, [],
                 list(owns) + part_args, exchanges)


class SwapHalves:
    has_mid = False

    def __init__(self, halves):
        self.ins = list(halves)
        self.out_shapes = [SDS(h.shape, h.dtype) for h in halves]
        self.aliases = [(i, i) for i in range(len(halves))]
        self.sems = [pltpu.SemaphoreType.DMA((len(halves),)), pltpu.SemaphoreType.DMA((len(halves),))]

    def start(self, ins, outs, sems):
        send_sems, recv_sems = sems
        x, y, c = lax.axis_index("x"), lax.axis_index("y"), lax.axis_index("c")
        for w in range(len(self.ins)):
            _remote(ins[w].at[c], outs[w].at[c], send_sems.at[w], recv_sems.at[w], (x, y, 1 - c)).start()

    def finish(self, ins, outs, sems):
        send_sems, recv_sems = sems
        x, y, c = lax.axis_index("x"), lax.axis_index("y"), lax.axis_index("c")
        for w in range(len(self.ins)):
            _remote(ins[w].at[c], outs[w].at[c], send_sems.at[w], recv_sems.at[w], (x, y, 1 - c)).wait_send()
            _remote(ins[w].at[1 - c], outs[w].at[1 - c], send_sems.at[w], recv_sems.at[w], (x, y, 1 - c)).wait_recv()


SEM_SPEC = pl.BlockSpec(memory_space=pltpu.SEMAPHORE)
HBM_SPEC = pl.BlockSpec(memory_space=pltpu.HBM)
DATAFLOW = pltpu.SideEffectType.DATAFLOW_SIDE_EFFECTING


def plan_chips(names):
    def plan(refs):
        n = len(names)
        x, y, c, me, chips, chip_ids = _place()
        return [(refs[w].at[chip_ids[j]], refs[n + w].at[j], (*chip, c)) for w in range(n) for j, chip in enumerate(chips)]
    return plan


def plan_pair(names):
    def plan(refs):
        n = len(names)
        x, y, c, me, chips, chip_ids = _place()
        return [(refs[w].at[_window(name, k, 1 - c)], refs[n + w].at[k], (x, y, 1 - c))
                for w, name in enumerate(names) for k in range(NCHIP)]
    return plan


def split_start(tag, arrays, plan, ncopies):
    n = len(arrays)

    def body(*refs):
        sems, token = refs[n:n + 2 * ncopies], refs[-1]
        for s, (src, dst, to) in enumerate(plan(refs[:n])):
            _remote(src, dst, sems[2 * s], sems[2 * s + 1], to).start()
        token[...] = jnp.zeros_like(token)

    res = pl.pallas_call(
        body, name="start_" + tag, in_specs=[HBM_SPEC] * n,
        out_specs=[SEM_SPEC] * (2 * ncopies) + [HBM_SPEC] * n + [VMEM_SPEC],
        out_shape=[pltpu.SemaphoreType.DMA(())] * (2 * ncopies) + [pltpu.HBM(a.shape, a.dtype) for a in arrays]
        + [SDS((8, 128), F32)],
        input_output_aliases={i: 2 * ncopies + i for i in range(n)},
        compiler_params=pltpu.CompilerParams(has_side_effects=DATAFLOW),
    )(*arrays)
    return (res[:2 * ncopies], res[2 * ncopies:-1]), res[-1]


def split_wait(tag, started, plan, after):
    sems, arrays = started
    n = len(arrays)

    def body(*refs):
        sem = refs[n:n + len(sems)]
        for s, (src, dst, to) in enumerate(plan(refs[:n])):
            cp = _remote(src, dst, sem[2 * s], sem[2 * s + 1], to)
            cp.wait_send()
            cp.wait_recv()

    return pl.pallas_call(
        body, name="wait_" + tag, in_specs=[HBM_SPEC] * n + [SEM_SPEC] * len(sems) + [ANY] * len(after),
        out_specs=[HBM_SPEC] * n, out_shape=[pltpu.HBM(a.shape, a.dtype) for a in arrays],
        input_output_aliases={i: i for i in range(n)},
        compiler_params=pltpu.CompilerParams(has_side_effects=DATAFLOW),
    )(*arrays, *sems, *after)


def adamw(tag, ws, gs, ms, vs, steps, after=()):
    n = len(ws)
    specs = [pl.BlockSpec((a.shape[0] // steps, a.shape[1]), lambda i: (i, 0)) for a in ws]
    assert all(a.shape[0] % (steps * SUBLANES) == 0 for a in ws), [a.shape for a in ws]

    def body(*refs):
        w_, g_, m_, v_ = refs[:n], refs[n:2 * n], refs[2 * n:3 * n], refs[3 * n:4 * n]
        d_, nm_, nv_, gout_ = refs[4 * n:5 * n], refs[5 * n:6 * n], refs[6 * n:7 * n], refs[7 * n:]
        for i in range(n):
            gv = g_[i][...]
            gout_[i][...] = gv
            mn = B1 * m_[i][...] + (1.0 - B1) * gv
            vn = B2 * v_[i][...] + (1.0 - B2) * (gv * gv)
            m_hat = mn / (1.0 - B1 ** STEP)
            v_hat = vn / (1.0 - B2 ** STEP)
            d_[i][...] = -LR * (m_hat / (jnp.sqrt(v_hat) + ADAM_EPS) + WD * w_[i][...])
            nm_[i][...] = mn
            nv_[i][...] = vn

    outs, _ = _call(body, "adamw_" + tag, (steps,), specs * 4, specs * 4, [SDS(a.shape, F32) for a in ws] * 4, [],
                    list(ws) + list(gs) + list(ms) + list(vs), after=after)
    return list(zip(outs[:n], outs[n:2 * n], outs[2 * n:3 * n], outs[3 * n:]))


VECS = ("mix_pre_g", "dw_bias", "conv_ln_g", "conv_ln_b", "pool_scale", "mix_post_g", "mlp_pre_g", "mlp_post_g")
WEIGHTS = ("mix_pre_g", "w_in", "dw_kernel", "dw_bias", "conv_ln_g", "conv_ln_b", "w_conv_out", "pool_w", "pool_scale",
           "w_pool_out", "w_o", "mix_post_g", "mlp_pre_g", "w_ff1", "w_ff2", "mlp_post_g")
MIX_MATS = ("w_conv_out", "pool_w", "w_pool_out", "w_o")
FF_MATS = ("w_ff1", "w_ff2")


def kernel(x, mix_pre_g, w_in, dw_kernel, dw_bias, conv_ln_g, conv_ln_b, w_conv_out, pool_w, pool_scale, w_pool_out, w_o, mix_post_g, mlp_pre_g, w_ff1, w_ff2, mlp_post_g, loss_target, m_mix_pre_g, m_w_in, m_dw_kernel, m_dw_bias, m_conv_ln_g, m_conv_ln_b, m_w_conv_out, m_pool_w, m_pool_scale, m_w_pool_out, m_w_o, m_mix_post_g, m_mlp_pre_g, m_w_ff1, m_w_ff2, m_mlp_post_g, v_mix_pre_g, v_w_in, v_dw_kernel, v_dw_bias, v_conv_ln_g, v_conv_ln_b, v_w_conv_out, v_pool_w, v_pool_scale, v_w_pool_out, v_w_o, v_mix_post_g, v_mlp_pre_g, v_w_ff1, v_w_ff2, v_mlp_post_g):
    w = dict(mix_pre_g=mix_pre_g, w_in=w_in, dw_kernel=dw_kernel, dw_bias=dw_bias, conv_ln_g=conv_ln_g,
             conv_ln_b=conv_ln_b, w_conv_out=w_conv_out, pool_w=pool_w, pool_scale=pool_scale, w_pool_out=w_pool_out,
             w_o=w_o, mix_post_g=mix_post_g, mlp_pre_g=mlp_pre_g, w_ff1=w_ff1, w_ff2=w_ff2, mlp_post_g=mlp_post_g)
    m = dict(mix_pre_g=m_mix_pre_g, w_in=m_w_in, dw_kernel=m_dw_kernel, dw_bias=m_dw_bias, conv_ln_g=m_conv_ln_g,
             conv_ln_b=m_conv_ln_b, w_conv_out=m_w_conv_out, pool_w=m_pool_w, pool_scale=m_pool_scale,
             w_pool_out=m_w_pool_out, w_o=m_w_o, mix_post_g=m_mix_post_g, mlp_pre_g=m_mlp_pre_g, w_ff1=m_w_ff1,
             w_ff2=m_w_ff2, mlp_post_g=m_mlp_post_g)
    v = dict(mix_pre_g=v_mix_pre_g, w_in=v_w_in, dw_kernel=v_dw_kernel, dw_bias=v_dw_bias, conv_ln_g=v_conv_ln_g,
             conv_ln_b=v_conv_ln_b, w_conv_out=v_w_conv_out, pool_w=v_pool_w, pool_scale=v_pool_scale,
             w_pool_out=v_w_pool_out, w_o=v_w_o, mix_post_g=v_mix_post_g, mlp_pre_g=v_mlp_pre_g, w_ff1=v_w_ff1,
             w_ff2=v_w_ff2, mlp_post_g=v_mlp_post_g)
    chip = 2 * lax.axis_index("x") + lax.axis_index("y")
    xs, tgt = x[0], loss_target[0]
    vecs = {name: w[name].reshape(1, D) for name in VECS}

    taps = lax.dynamic_update_slice(jnp.zeros((KW_PAD, D), F32), dw_kernel, (0, chip * DSH))
    mine, full = {}, {}
    (mine["w_in"],), _ = cast_shards("cast_w_in", ("w_in",), [w["w_in"]])
    rest = MATS[1:]
    cast, ((full["w_in"], dwk),) = cast_shards(
        "cast_rest", rest, [w[name] for name in rest], [GatherWeights(("w_in",), [mine["w_in"]], taps)])
    mine.update(zip(rest, cast))
    (u, glu, ag, p, gt), (got,) = fwd_in(
        xs, vecs["mix_pre_g"], full["w_in"], TM_IN, [GatherWeights(MIX_MATS, [mine[n] for n in MIX_MATS])])
    full.update(zip(MIX_MATS, got))
    (cv, sw, z, zl, zs, yc, yp, mg, mo, h1), (got,) = fwd_mix(
        xs, glu, p, gt, dwk, vecs["dw_bias"], vecs["conv_ln_g"], vecs["conv_ln_b"], full["w_conv_out"], full["pool_w"],
        vecs["pool_scale"], full["w_pool_out"], full["w_o"], vecs["mix_post_g"], TM,
        [GatherWeights(FF_MATS, [mine[n] for n in FF_MATS])])
    full.update(zip(FF_MATS, got))

    (v_, a2, df2, df1, dh1, vec_mlp), _ = mlp_fwd_bwd(
        h1, tgt, vecs["mlp_pre_g"], vecs["mlp_post_g"], full["w_ff1"], full["w_ff2"], TM)
    grads, g, delta, new_m, new_v = {}, {}, {}, {}, {}
    rest_mats = FF_MATS + MIX_MATS
    grads["w_ff1"], _ = wgrad(v_, df1, "wgrad_ff1", bn=CFF, chip_major=True)
    grads["w_ff2"], _ = wgrad(a2, df2, "wgrad_ff2")
    landing = lambda names, slots, dt: [lax.empty((slots, *_half_shape(n)), dt) for n in names]
    grads_ff = [grads[n] for n in FF_MATS]
    (dmo, dgt, dyc, dyp, dcv, dzl, dz, vec_mix), (from_sibling,) = bwd_mix(
        dh1, mo, cv, zl, yc, yp, gt, vecs["conv_ln_g"], vecs["conv_ln_b"], vecs["pool_scale"], vecs["mix_post_g"],
        full["w_conv_out"], full["pool_w"], full["w_pool_out"], full["w_o"], TM,
        exchanges=[ExchangePair(FF_MATS, grads_ff)])
    grads["w_conv_out"], _ = wgrad(sw, dyc, "wgrad_conv_out", bt=1024)
    grads["pool_w"] = wgrad_pool(z, dzl)
    grads["w_pool_out"], _ = wgrad(zs, dyp, "wgrad_pool_out", bt=1024)
    grads["w_o"], _ = wgrad(mg, dmo, "wgrad_o", bt=1024)
    wire_ff, own_ff, (from_sibling,) = pair_add(
        "ff", FF_MATS, grads_ff, from_sibling, exchanges=[ExchangePair(MIX_MATS, [grads[n] for n in MIX_MATS])])
    wire_mix, own_mix, _ = pair_add("mix", MIX_MATS, [grads[n] for n in MIX_MATS], from_sibling)
    chips_rest, token = split_start("chips_rest", list(wire_ff) + list(wire_mix) + landing(rest_mats, NCHIP - 1, BF16),
                                    plan_chips(rest_mats), len(rest_mats) * (NCHIP - 1))
    dx, dproj, vec_in, dk = bwd_in(
        xs, dh1, dcv, dz, glu, ag, dgt, vecs["mix_pre_g"], dwk, full["w_in"], TM, after=[token])
    small_part = jnp.concatenate([vec_mlp, vec_mix, vec_in, dk], axis=0)
    grads["w_in"], ((small_parts,),) = wgrad(
        u, dproj, "wgrad_in", bn=CIN, chip_major=True, exchanges=[GatherSmall(small_part)])
    got = split_wait("chips_rest", chips_rest, plan_chips(rest_mats), after=[grads["w_in"]])[len(rest_mats):]
    chips_ff, chips_mix = got[:len(FF_MATS)], got[len(FF_MATS):]
    halves_rest, (from_sibling,) = sum_partials(
        "rest", rest_mats, list(own_ff) + list(own_mix), [[a] for a in list(chips_ff) + list(chips_mix)],
        [ExchangePair(("w_in",), [grads["w_in"]])])
    wire_in, own_in, (reduced_rest,) = pair_add(
        "in", ("w_in",), [grads["w_in"]], from_sibling, exchanges=[SwapHalves(halves_rest)])
    for n, red in zip(rest_mats, reduced_rest):
        g[n] = red.reshape(w[n].shape)

    def update(tag, names, steps, after=()):
        two_d = lambda a: a.reshape(-1, a.shape[-1])
        res = adamw(tag, [two_d(w[n]) for n in names], [two_d(g[n]) for n in names],
                    [two_d(m[n]) for n in names], [two_d(v[n]) for n in names], steps, after)
        for n, outs in zip(names, res):
            delta[n], new_m[n], new_v[n], g[n] = [a.reshape(w[n].shape) for a in outs]

    chips_in, token = split_start("chips_in", list(wire_in) + landing(("w_in",), NCHIP - 1, BF16),
                                  plan_chips(("w_in",)), NCHIP - 1)
    small, loss8 = sum_small(small_parts, after=[token])
    loss = loss8[0, 0]
    for name in VECS:
        g[name] = small[ROW[name]]
    g["dw_kernel"] = lax.dynamic_slice(small[VEC_ROWS:VEC_ROWS + KW_PAD], (0, chip * DSH), (KW_PAD, DSH))
    update("rest", rest_mats, 8, after=[token])
    stack = lambda d: jnp.concatenate([d[name].reshape(1, D) for name in VECS], axis=0)
    (res,) = adamw("vectors", [stack(w)], [stack(g)], [stack(m)], [stack(v)], 1, after=[token])
    for i, name in enumerate(VECS):
        delta[name], new_m[name], new_v[name] = [r[i] for r in res[:3]]
    padk = lambda a: jnp.pad(a, ((0, KW_PAD - KW), (0, 0)))
    (res,) = adamw("dw_kernel", [padk(w["dw_kernel"])], [g["dw_kernel"]], [padk(m["dw_kernel"])],
                   [padk(v["dw_kernel"])], 1, after=[token])
    delta["dw_kernel"], new_m["dw_kernel"], new_v["dw_kernel"] = [r[:KW] for r in res[:3]]
    g["dw_kernel"] = g["dw_kernel"][:KW]
    chips_in = split_wait("chips_in", chips_in, plan_chips(("w_in",)),
                          after=[delta[rest_mats[-1]], delta["dw_kernel"], delta[VECS[0]]])[1:]
    halves_in, _ = sum_partials("in", ("w_in",), own_in, [chips_in])
    (reduced_in,) = exchange("swap_w_in", SwapHalves(halves_in))
    g["w_in"] = reduced_in.reshape(w["w_in"].shape)
    update("in", ("w_in",), 8)

    return (loss, dx[None], *[g[n] for n in WEIGHTS], *[delta[n] for n in WEIGHTS], *[new_m[n] for n in WEIGHTS],
            *[new_v[n] for n in WEIGHTS])
```

```python
import math

import jax
import jax.numpy as jnp
from jax import lax
from jax.experimental import pallas as pl
from jax.experimental.pallas import tpu as pltpu

F32 = jnp.float32
BF16 = jnp.bfloat16

D = 1024
FF = 4096
NPROJ = 5
KW = 31
KW_PAD = 32
SUBLANES = 8
LANES = 128
HALO = 32
POOL_WINDOWS = (2, 4, 8, 16)
NG = 4
GW = D // NG
RMS_EPS = 1e-6
LN_EPS = 1e-5
LR, B1, B2, ADAM_EPS, WD, STEP = 0.001, 0.9, 0.999, 1e-08, 0.01, 10
NCHIP = 4
VMEM_LIMIT = 60 * 1024 * 1024
MESH = pl.DeviceIdType.MESH
SIBLING_COLLECTIVE_ID = 0
TM = 256
TM_IN = 512

ANY = pl.BlockSpec(memory_space=pl.ANY)
VMEM_SPEC = pl.BlockSpec(memory_space=pltpu.VMEM)
SDS = jax.ShapeDtypeStruct


def _cp(**kw):
    return pltpu.CompilerParams(vmem_limit_bytes=VMEM_LIMIT, **kw)


def _mm(a, b):
    return jnp.dot(a, b, preferred_element_type=F32)


def _mm_nt(a, b):
    return lax.dot_general(a, b, (((1,), (1,)), ((), ())), preferred_element_type=F32)


def _mm_tn(a, b):
    return lax.dot_general(a, b, (((0,), (0,)), ((), ())), preferred_element_type=F32)


def _sigmoid(x):
    return 1.0 / (1.0 + jnp.exp(-x))


def _rowsum(x):
    return jnp.sum(x, axis=0, keepdims=True)


def _full(shape):
    return pl.BlockSpec(shape, lambda i: (0,) * len(shape))


def _tile(tm, cols):
    return pl.BlockSpec((tm, cols), lambda i: (i, 0))


def _prev_halo(tm):
    return pl.BlockSpec((HALO, D), lambda i: (jnp.maximum(i * (tm // HALO) - 1, 0), 0))


def _next_halo(tm, nt):
    return pl.BlockSpec((HALO, D), lambda i: (jnp.minimum((i + 1) * (tm // HALO), nt * (tm // HALO) - 1), 0))


MATS = ("w_in", "w_conv_out", "pool_w", "w_pool_out", "w_o", "w_ff1", "w_ff2")
CIN = NPROJ * D // NCHIP
CFF = FF // NCHIP
_ROWS = lambda k, h: (2 * k + h, 0)
_CHIP_MAJOR = lambda k, h: (k, h, 0)
GEOM = dict(
    w_in=((NCHIP, D, CIN), (None, D // 2, CIN), _CHIP_MAJOR),
    w_conv_out=((D, D), (D // (2 * NCHIP), D), _ROWS),
    pool_w=((NG, GW, GW), (NG // 2, GW // NCHIP, GW), lambda k, h: (h, k, 0)),
    w_pool_out=((D, D), (D // (2 * NCHIP), D), _ROWS),
    w_o=((D, D), (D // (2 * NCHIP), D), _ROWS),
    w_ff1=((NCHIP, D, CFF), (None, D // 2, CFF), _CHIP_MAJOR),
    w_ff2=((FF, D), (FF // (2 * NCHIP), D), _ROWS),
)
DSH = D // NCHIP


def _half_shape(name):
    return tuple(b for b in GEOM[name][1] if b is not None)


def _window(name, k, h):
    _, blk, idx = GEOM[name]
    return tuple(i if b is None else pl.ds(i * b, b) for i, b in zip(idx(k, h), blk))


def _shard_half(name, h):
    n0 = _half_shape(name)[0]
    return (pl.ds(h * n0, n0),) + (slice(None),) * (len(_half_shape(name)) - 1)


def _place():
    x, y, c = lax.axis_index("x"), lax.axis_index("y"), lax.axis_index("c")
    chips = [(1 - x, y), (x, 1 - y), (1 - x, 1 - y)]
    return x, y, c, 2 * x + y, chips, [2 * px + py for px, py in chips]


def _remote(src, dst, send_sem, recv_sem, to):
    return pltpu.make_async_remote_copy(src_ref=src, dst_ref=dst, send_sem=send_sem, recv_sem=recv_sem,
                                        device_id=to, device_id_type=MESH)


class GatherWeights:
    has_mid = True
    sibling_only = False

    def __init__(self, names, fulls, taps=None):
        self.names = names
        self.ins = list(fulls) + ([taps] if taps is not None else [])
        self.has_taps = taps is not None
        self.out_shapes = [SDS(a.shape, a.dtype) for a in self.ins]
        self.aliases = [(i, i) for i in range(len(self.ins))]
        n = len(self.ins)
        self.sems = [pltpu.SemaphoreType.DMA((n, 6)), pltpu.SemaphoreType.DMA((n, 6))]

    def _copies(self, ins, outs, sems):
        send_sems, recv_sems = sems
        x, y, c, me, chips, chip_ids = _place()
        sibling = (x, y, 1 - c)
        ici, ici_recv, d2d, d2d_recv = [], [], [], []
        for w, name in enumerate(self.names):
            for j, chip in enumerate(chips):
                ici.append(_remote(ins[w].at[_window(name, me, c)], outs[w].at[_window(name, me, c)],
                                   send_sems.at[w, j], recv_sems.at[w, j], (*chip, c)))
                got = outs[w].at[_window(name, chip_ids[j], c)]
                ici_recv.append(_remote(got, got, send_sems.at[w, j], recv_sems.at[w, j], sibling))
                d2d.append(_remote(got, got, send_sems.at[w, 3 + j], recv_sems.at[w, 3 + j], sibling))
                got = outs[w].at[_window(name, chip_ids[j], 1 - c)]
                d2d_recv.append(_remote(got, got, send_sems.at[w, 3 + j], recv_sems.at[w, 3 + j], sibling))
        if self.has_taps:
            w = len(self.names)
            for j, chip in enumerate(chips):
                ici.append(_remote(ins[w].at[:, pl.ds(me * DSH, DSH)], outs[w].at[:, pl.ds(me * DSH, DSH)],
                                   send_sems.at[w, j], recv_sems.at[w, j], (*chip, c)))
                got = outs[w].at[:, pl.ds(chip_ids[j] * DSH, DSH)]
                d2d_recv.append(_remote(got, got, send_sems.at[w, j], recv_sems.at[w, j], sibling))
        return ici, ici_recv, d2d, d2d_recv

    def start(self, ins, outs, sems):
        for cp in self._copies(ins, outs, sems)[0]:
            cp.start()

    def mid(self, ins, outs, sems):
        _, ici_recv, d2d, _ = self._copies(ins, outs, sems)
        for got, fwd in zip(ici_recv, d2d):
            got.wait_recv()
            fwd.start()

    def finish(self, ins, outs, sems):
        ici, _, d2d, d2d_recv = self._copies(ins, outs, sems)
        for cp in d2d_recv:
            cp.wait_recv()
        for cp in ici + d2d:
            cp.wait_send()


class ExchangePair:
    has_mid = False
    sibling_only = True
    aliases = ()

    def __init__(self, names, grads):
        self.names, self.ins = names, list(grads)
        self.out_shapes = [SDS((NCHIP, *_half_shape(n)), F32) for n in names]
        self.sems = [pltpu.SemaphoreType.DMA((len(names),)), pltpu.SemaphoreType.DMA((len(names),))]

    def start(self, ins, outs, sems):
        send_sems, recv_sems = sems
        x, y, c, me, chips, chip_ids = _place()
        for w, name in enumerate(self.names):
            for k in range(NCHIP):
                _remote(ins[w].at[_window(name, k, 1 - c)], outs[w].at[k], send_sems.at[w], recv_sems.at[w],
                        (x, y, 1 - c)).start()

    def finish(self, ins, outs, sems):
        send_sems, recv_sems = sems
        x, y, c, me, chips, chip_ids = _place()
        for w in range(len(self.names)):
            _remote(outs[w], outs[w], send_sems.at[w], recv_sems.at[w], (x, y, 1 - c)).wait()


class ExchangeChips:
    has_mid = False
    sibling_only = False
    aliases = ()

    def __init__(self, names, wires):
        self.names, self.ins = names, list(wires)
        self.out_shapes = [SDS((NCHIP - 1, *_half_shape(n)), BF16) for n in names]
        self.sems = [pltpu.SemaphoreType.DMA((len(names), NCHIP - 1)), pltpu.SemaphoreType.DMA((len(names), NCHIP - 1))]

    def _copies(self, ins, outs, sems):
        send_sems, recv_sems = sems
        x, y, c, me, chips, chip_ids = _place()
        return [_remote(ins[w].at[chip_ids[j]], outs[w].at[j], send_sems.at[w, j], recv_sems.at[w, j], (*chip, c))
                for w in range(len(self.names)) for j, chip in enumerate(chips)]

    def start(self, ins, outs, sems):
        for cp in self._copies(ins, outs, sems):
            cp.start()

    def finish(self, ins, outs, sems):
        for cp in self._copies(ins, outs, sems):
            cp.wait()


def _call(body, name, grid, in_specs, out_specs, out_shape, scratch, args, exchanges=(), after=()):
    n_in, n_out, n_scr = len(in_specs), len(out_specs), len(scratch)
    x_in = [a for e in exchanges for a in e.ins]
    x_out = [s for e in exchanges for s in e.out_shapes]
    x_sem = [s for e in exchanges for s in e.sems]
    nsteps = math.prod(grid)
    sibling_only = bool(exchanges) and all(e.sibling_only for e in exchanges)

    def wrapped(*refs):
        ins, rest = refs[:n_in], refs[n_in:]
        xin, rest = rest[:len(x_in)], rest[len(x_in) + len(after):]
        outs, rest = rest[:n_out], rest[n_out:]
        xout, rest = rest[:len(x_out)], rest[len(x_out):]
        scr, xsem = rest[:n_scr], rest[n_scr:]
        parts = []
        for e in exchanges:
            parts.append((xin[:len(e.ins)], xout[:len(e.out_shapes)], xsem[:len(e.sems)]))
            xin, xout, xsem = xin[len(e.ins):], xout[len(e.out_shapes):], xsem[len(e.sems):]
        def start_all():
            if sibling_only:
                barrier = pltpu.get_barrier_semaphore()
                sibling = (lax.axis_index("x"), lax.axis_index("y"), 1 - lax.axis_index("c"))
                pl.semaphore_signal(barrier, inc=1, device_id=sibling, device_id_type=MESH)
                pl.semaphore_wait(barrier, 1)
            for e, p in zip(exchanges, parts):
                e.start(*p)

        if not grid:
            start_all()
            body(*ins, *outs, *scr)
            for e, p in zip(exchanges, parts):
                if e.has_mid:
                    e.mid(*p)
            for e, p in zip(exchanges, parts):
                e.finish(*p)
            return
        step = 0
        for axis, extent in enumerate(grid):
            step = step * extent + pl.program_id(axis)
        if exchanges:
            pl.when(step == 0)(start_all)

        body(*ins, *outs, *scr)
        if any(e.has_mid for e in exchanges):
            @pl.when(step == max(nsteps - 2, 0))
            def _():
                for e, p in zip(exchanges, parts):
                    if e.has_mid:
                        e.mid(*p)

        if exchanges:
            @pl.when(step == nsteps - 1)
            def _():
                for e, p in zip(exchanges, parts):
                    e.finish(*p)

    barrier = dict(collective_id=SIBLING_COLLECTIVE_ID) if sibling_only else {}
    kw = dict(grid=grid, compiler_params=_cp(dimension_semantics=("arbitrary",) * len(grid), **barrier)) if grid else dict(
        compiler_params=_cp(**barrier))
    aliases, i0, o0 = {}, n_in, n_out
    for e in exchanges:
        aliases.update({i0 + i: o0 + o for i, o in e.aliases})
        i0, o0 = i0 + len(e.ins), o0 + len(e.out_shapes)
    res = pl.pallas_call(
        wrapped, name=name, in_specs=list(in_specs) + [ANY] * (len(x_in) + len(after)),
        out_specs=list(out_specs) + [ANY] * len(x_out),
        out_shape=list(out_shape) + x_out, scratch_shapes=list(scratch) + x_sem, input_output_aliases=aliases, **kw,
    )(*args, *x_in, *after)
    outs, rest = res[:n_out], res[n_out:]
    xouts = []
    for e in exchanges:
        xouts.append(rest[:len(e.out_shapes)])
        rest = rest[len(e.out_shapes):]
    return outs, xouts


def exchange(name, ex):
    return _call(lambda: None, name, (), [], [], [], [], [], [ex])[1][0]


def _taps_scratch(tm):
    return pltpu.VMEM((SUBLANES, tm + HALO, LANES), F32)


def _taps(src_ref, k_ref, dst_ref, sh_ref, tm, off0, reverse):
    rc, cw = 64, LANES

    def col_chunk(cc, carry):
        cols = pl.ds(pl.multiple_of(cc * cw, cw), cw)
        for q in range(SUBLANES):
            n = tm + SUBLANES * (len(range(q, KW, SUBLANES)) - 1)
            sh_ref[q, 0:n, :] = src_ref[pl.ds(off0 + q, n), cols]
        for r in range(tm // rc):
            acc = jnp.zeros((rc, cw), F32)
            for q in range(SUBLANES):
                for a, j in enumerate(range(q, KW, SUBLANES)):
                    kj = KW - 1 - j if reverse else j
                    acc = acc + k_ref[kj:kj + 1, cols] * sh_ref[q, pl.ds(r * rc + SUBLANES * a, rc), :]
            dst_ref[pl.ds(r * rc, rc), cols] = acc
        return carry

    lax.fori_loop(0, D // cw, col_chunk, 0)


def _proj_pieces(g):
    lo, hi, pieces = g * D, (g + 1) * D, []
    while lo < hi:
        k = lo // CIN
        b = min(hi - k * CIN, CIN)
        pieces.append((k, lo - k * CIN, b))
        lo = k * CIN + b
    return pieces


def fwd_in(x, g1, w_in, tm, exchanges=()):
    t = x.shape[0]

    def body(x_ref, g_ref, w_ref, u_ref, glu_ref, ag_ref, p_ref, gt_ref):
        xf = x_ref[...]
        r = lax.rsqrt(jnp.mean(xf * xf, axis=-1, keepdims=True) + RMS_EPS)
        u = (xf * r * g_ref[...]).astype(BF16)
        u_ref[...] = u
        proj = lambda g: jnp.concatenate([_mm(u, w_ref[k, :, lo:hi]) for k, lo, hi in _proj_pieces(g)], axis=1)
        a = proj(0)
        gate = proj(1)
        glu_ref[...] = a * _sigmoid(gate)
        ag_ref[:, 0:D] = a.astype(BF16)
        ag_ref[:, D:2 * D] = gate.astype(BF16)
        p_ref[...] = proj(2)
        gt_ref[:, 0:D] = proj(3).astype(BF16)
        gt_ref[:, D:2 * D] = proj(4).astype(BF16)

    return _call(
        body, "fwd_in", (t // tm,),
        [_tile(tm, D), _full((1, D)), _full((NCHIP, D, CIN))],
        [_tile(tm, D), _tile(tm, D), _tile(tm, 2 * D), _tile(tm, D), _tile(tm, 2 * D)],
        [SDS((t, D), BF16), SDS((t, D), F32), SDS((t, 2 * D), BF16), SDS((t, D), F32), SDS((t, 2 * D), BF16)],
        [], [x, g1, w_in], exchanges)


def _pool_inv_count(i, tm, w):
    pos = i * tm + lax.broadcasted_iota(jnp.int32, (tm, 1), 0) + 1
    return 1.0 / jnp.minimum(pos, w).astype(F32)


def _window_sum(src_ref, tmp_ref, cols, tm, w, causal):
    lo, hi = 0, tm + HALO
    cur, span = None, 1
    while span < w:
        new_lo, new_hi = (lo + SUBLANES, hi) if causal else (lo, hi - SUBLANES)
        far = new_lo - span if causal else new_lo + span
        n = new_hi - new_lo
        if cur is None:
            near_v, far_v = src_ref[pl.ds(new_lo, n), cols], src_ref[pl.ds(far, n), cols]
        else:
            near_v = cur[new_lo - lo:new_lo - lo + n]
            if span % SUBLANES == 0:
                far_v = cur[far - lo:far - lo + n]
            else:
                tmp_ref[pl.ds(lo, hi - lo), :] = cur
                far_v = tmp_ref[pl.ds(far, n), :]
        cur, lo, hi, span = near_v + far_v, new_lo, new_hi, 2 * span
    off = HALO if causal else 0
    return cur[off - lo:off - lo + tm]


def fwd_mix(x, glu, p, gt, dwk, dwb, lng, lnb, w_co, pool_w, ps, w_po, w_o, g2, tm, exchanges=()):
    t = x.shape[0]

    def body(x_ref, glu_ref, gluh_ref, p_ref, ph_ref, gt_ref, k_ref, b_ref, lg_ref, lb_ref, wco_ref, pw_ref,
             ps_ref, wpo_ref, wo_ref, g2_ref,
             cv_ref, sw_ref, z_ref, zl_ref, zs_ref, yc_ref, yp_ref, mg_ref, mo_ref, h1_ref, ext_ref, win_ref, sh_ref):
        i = pl.program_id(0)
        keep = (i > 0).astype(F32)
        ext_ref[0:HALO, :] = gluh_ref[...] * keep
        ext_ref[HALO:HALO + tm, :] = glu_ref[...]
        _taps(ext_ref, k_ref, cv_ref, sh_ref, tm, HALO - (KW - 1), False)
        cv = cv_ref[...] + b_ref[...]
        cv_ref[...] = cv
        mu = jnp.mean(cv, axis=-1, keepdims=True)
        cen = cv - mu
        rstd = lax.rsqrt(jnp.mean(cen * cen, axis=-1, keepdims=True) + LN_EPS)
        ln = cen * rstd * lg_ref[...] + lb_ref[...]
        sw = (ln * _sigmoid(ln)).astype(BF16)
        sw_ref[...] = sw
        yc = _mm(sw, wco_ref[...])
        yc_ref[...] = yc.astype(BF16)
        ext_ref[0:HALO, :] = ph_ref[...] * keep
        ext_ref[HALO:HALO + tm, :] = p_ref[...]
        for g, w in enumerate(POOL_WINDOWS):
            cols = pl.ds(g * GW, GW)
            acc = _window_sum(ext_ref, win_ref, cols, tm, w, True)
            zg = (acc * _pool_inv_count(i, tm, w) - p_ref[:, cols]).astype(BF16)
            z_ref[:, cols] = zg
            zl = _mm(zg, pw_ref[g])
            zl_ref[:, cols] = zl.astype(BF16)
            zs_ref[:, cols] = (zl * ps_ref[:, cols]).astype(BF16)
        yp = _mm(zs_ref[...], wpo_ref[...])
        yp_ref[...] = yp.astype(BF16)
        gc = _sigmoid(gt_ref[:, 0:D].astype(F32))
        gp = _sigmoid(gt_ref[:, D:2 * D].astype(F32))
        mg = (gc * yc + gp * yp).astype(BF16)
        mg_ref[...] = mg
        mo = _mm(mg, wo_ref[...])
        mo_ref[...] = mo
        r2 = lax.rsqrt(jnp.mean(mo * mo, axis=-1, keepdims=True) + RMS_EPS)
        h1_ref[...] = x_ref[...] + mo * r2 * g2_ref[...]

    vec = _full((1, D))
    act = lambda dt: SDS((t, D), dt)
    return _call(
        body, "fwd_mix", (t // tm,),
        [_tile(tm, D), _tile(tm, D), _prev_halo(tm), _tile(tm, D), _prev_halo(tm), _tile(tm, 2 * D),
         _full((KW_PAD, D)), vec, vec, vec, _full((D, D)), _full((NG, GW, GW)), vec, _full((D, D)), _full((D, D)), vec],
        [_tile(tm, D)] * 10,
        [act(F32), act(BF16), act(BF16), act(BF16), act(BF16), act(BF16), act(BF16), act(BF16), act(F32), act(F32)],
        [pltpu.VMEM((tm + HALO, D), F32), pltpu.VMEM((tm + HALO, GW), F32), _taps_scratch(tm)],
        [x, glu, glu, p, p, gt, dwk, dwb, lng, lnb, w_co, pool_w, ps, w_po, w_o, g2], exchanges)


def mlp_fwd_bwd(h1, tgt, g3, g4, w1, w2, tm, exchanges=()):
    t = h1.shape[0]
    fc = CFF

    def body(h1_ref, tgt_ref, g3_ref, g4_ref, w1_ref, w2_ref,
             v_ref, a2_ref, df2_ref, df1_ref, dh1_ref, vec_ref, f1_ref):
        i = pl.program_id(0)

        @pl.when(i == 0)
        def _():
            vec_ref[...] = jnp.zeros_like(vec_ref)

        h1v = h1_ref[...]
        r3 = lax.rsqrt(jnp.mean(h1v * h1v, axis=-1, keepdims=True) + RMS_EPS)
        n3 = h1v * r3
        v = (n3 * g3_ref[...]).astype(BF16)
        v_ref[...] = v
        f2 = jnp.zeros((tm, D), F32)
        for c in range(FF // fc):
            cols = pl.ds(c * fc, fc)
            f1 = jnp.maximum(_mm(v, w1_ref[c]), 0.0)
            f1_ref[:, cols] = f1
            a2 = (f1 * f1).astype(BF16)
            a2_ref[:, cols] = a2
            f2 = f2 + _mm(a2, w2_ref[cols, :])
        r4 = lax.rsqrt(jnp.mean(f2 * f2, axis=-1, keepdims=True) + RMS_EPS)
        n4 = f2 * r4
        err = h1v + n4 * g4_ref[...] - tgt_ref[...]
        vec_ref[2:3, :] += _rowsum(err * err) * (0.5 / D)
        dh2 = err * (1.0 / D)
        vec_ref[1:2, :] += _rowsum(dh2 * n4)
        dn4 = dh2 * g4_ref[...]
        df2 = (r4 * (dn4 - n4 * jnp.mean(dn4 * n4, axis=-1, keepdims=True))).astype(BF16)
        df2_ref[...] = df2
        dv = jnp.zeros((tm, D), F32)
        for c in range(FF // fc):
            cols = pl.ds(c * fc, fc)
            da2 = _mm_nt(df2, w2_ref[cols, :])
            df1 = (da2 * (2.0 * f1_ref[:, cols])).astype(BF16)
            df1_ref[:, cols] = df1
            dv = dv + _mm_nt(df1, w1_ref[c])
        vec_ref[0:1, :] += _rowsum(dv * n3)
        dn3 = dv * g3_ref[...]
        dh1_ref[...] = dh2 + r3 * (dn3 - n3 * jnp.mean(dn3 * n3, axis=-1, keepdims=True))

    vec = _full((1, D))
    return _call(
        body, "mlp_fwd_bwd", (t // tm,),
        [_tile(tm, D), _tile(tm, D), vec, vec, _full((NCHIP, D, CFF)), _full((FF, D))],
        [_tile(tm, D), _tile(tm, FF), _tile(tm, D), _tile(tm, FF), _tile(tm, D), _full((8, D))],
        [SDS((t, D), BF16), SDS((t, FF), BF16), SDS((t, D), BF16), SDS((t, FF), BF16), SDS((t, D), F32),
         SDS((8, D), F32)],
        [pltpu.VMEM((tm, FF), F32)], [h1, tgt, g3, g4, w1, w2], exchanges)


def bwd_mix(dh1, mo, cv, zl, yc, yp, gt, lng, lnb, ps, g2, w_co, pool_w, w_po, w_o, tm, exchanges=()):
    t = dh1.shape[0]

    def body(dh1_ref, mo_ref, cv_ref, zl_ref, yc_ref, yp_ref, gt_ref, lg_ref, lb_ref, ps_ref, g2_ref,
             wco_ref, pw_ref, wpo_ref, wo_ref,
             dmo_ref, dgt_ref, dyc_ref, dyp_ref, dcv_ref, dzl_ref, dz_ref, vec_ref):
        i = pl.program_id(0)

        @pl.when(i == 0)
        def _():
            vec_ref[...] = jnp.zeros_like(vec_ref)

        dh1v = dh1_ref[...]
        mo = mo_ref[...]
        r2 = lax.rsqrt(jnp.mean(mo * mo, axis=-1, keepdims=True) + RMS_EPS)
        n2 = mo * r2
        vec_ref[0:1, :] += _rowsum(dh1v * n2)
        dn2 = dh1v * g2_ref[...]
        dmo = (r2 * (dn2 - n2 * jnp.mean(dn2 * n2, axis=-1, keepdims=True))).astype(BF16)
        dmo_ref[...] = dmo
        dmg = _mm_nt(dmo, wo_ref[...])
        gc = _sigmoid(gt_ref[:, 0:D].astype(F32))
        gp = _sigmoid(gt_ref[:, D:2 * D].astype(F32))
        dgt_ref[:, 0:D] = (dmg * yc_ref[...].astype(F32) * gc * (1.0 - gc)).astype(BF16)
        dgt_ref[:, D:2 * D] = (dmg * yp_ref[...].astype(F32) * gp * (1.0 - gp)).astype(BF16)
        dyc = (dmg * gc).astype(BF16)
        dyp = (dmg * gp).astype(BF16)
        dyc_ref[...] = dyc
        dyp_ref[...] = dyp
        dsw = _mm_nt(dyc, wco_ref[...])
        cv = cv_ref[...]
        mu = jnp.mean(cv, axis=-1, keepdims=True)
        cen = cv - mu
        rstd = lax.rsqrt(jnp.mean(cen * cen, axis=-1, keepdims=True) + LN_EPS)
        y = cen * rstd
        ln = y * lg_ref[...] + lb_ref[...]
        sg = _sigmoid(ln)
        dln = dsw * (sg * (1.0 + ln * (1.0 - sg)))
        vec_ref[1:2, :] += _rowsum(dln * y)
        vec_ref[2:3, :] += _rowsum(dln)
        dy = dln * lg_ref[...]
        dcv = rstd * (dy - jnp.mean(dy, axis=-1, keepdims=True) - y * jnp.mean(dy * y, axis=-1, keepdims=True))
        dcv_ref[...] = dcv.astype(BF16)
        vec_ref[3:4, :] += _rowsum(dcv)
        dzs = _mm_nt(dyp, wpo_ref[...])
        vec_ref[4:5, :] += _rowsum(dzs * zl_ref[...].astype(F32))
        dzl = (dzs * ps_ref[...]).astype(BF16)
        dzl_ref[...] = dzl
        for g in range(NG):
            cols = pl.ds(g * GW, GW)
            dz_ref[:, cols] = _mm_nt(dzl_ref[:, cols], pw_ref[g]).astype(BF16)

    vec = _full((1, D))
    act = lambda dt: SDS((t, D), dt)
    return _call(
        body, "bwd_mix", (t // tm,),
        [_tile(tm, D)] * 6 + [_tile(tm, 2 * D), vec, vec, vec, vec, _full((D, D)), _full((NG, GW, GW)), _full((D, D)),
                              _full((D, D))],
        [_tile(tm, D), _tile(tm, 2 * D)] + [_tile(tm, D)] * 5 + [_full((8, D))],
        [act(BF16), SDS((t, 2 * D), BF16), act(BF16), act(BF16), act(BF16), act(BF16), act(BF16), SDS((8, D), F32)],
        [], [dh1, mo, cv, zl, yc, yp, gt, lng, lnb, ps, g2, w_co, pool_w, w_po, w_o], exchanges)


def bwd_in(x, dh1, dcv, dz, glu, ag, dgt, g1, dwk, w_in, tm, after=()):
    t = x.shape[0]
    nt = t // tm

    def body(x_ref, dh1_ref, dcv_ref, dcvh_ref, dz_ref, dzh_ref, glu_ref, gluh_ref, ag_ref, dgt_ref, g1_ref,
             k_ref, w_ref, dx_ref, dproj_ref, vec_ref, dk_ref, ext_ref, tmp_ref, win_ref, sh_ref, gext_ref, gsh_ref):
        i = pl.program_id(0)

        @pl.when(i == 0)
        def _():
            vec_ref[...] = jnp.zeros_like(vec_ref)
            dk_ref[...] = jnp.zeros_like(dk_ref)

        first = (i > 0).astype(F32)
        last = (i < nt - 1).astype(F32)
        gext_ref[0:HALO, :] = gluh_ref[...] * first
        gext_ref[HALO:HALO + tm, :] = glu_ref[...]
        rc, cw = 32, LANES

        def dk_chunk(cc):
            cols = pl.ds(cc * cw, cw)
            for q in range(SUBLANES):
                taps = range(q, KW, SUBLANES)
                n = tm + SUBLANES * (len(taps) - 1)
                gsh_ref[q, 0:n, :] = gext_ref[pl.ds(HALO - (KW - 1) + q, n), cols]
                accs = [jnp.zeros((SUBLANES, cw), F32) for _ in taps]
                for r in range(tm // rc):
                    dchunk = dcv_ref[pl.ds(r * rc, rc), cols].astype(F32)
                    for a in range(len(taps)):
                        prod = dchunk * gsh_ref[q, pl.ds(r * rc + SUBLANES * a, rc), :]
                        accs[a] = accs[a] + jnp.sum(prod.reshape(rc // SUBLANES, SUBLANES, cw), axis=0)
                for a, j in enumerate(taps):
                    dk_ref[j:j + 1, cols] += _rowsum(accs[a])

        ext_ref[0:tm, :] = dcv_ref[...].astype(F32)
        ext_ref[tm:tm + HALO, :] = dcvh_ref[...].astype(F32) * last
        _taps(ext_ref, k_ref, tmp_ref, sh_ref, tm, 0, True)
        dglu = tmp_ref[...]
        a = ag_ref[:, 0:D].astype(F32)
        sg = _sigmoid(ag_ref[:, D:2 * D].astype(F32))
        dproj_ref[:, 0:D] = (dglu * sg).astype(BF16)
        dproj_ref[:, D:2 * D] = (dglu * a * sg * (1.0 - sg)).astype(BF16)
        for g, w in enumerate(POOL_WINDOWS):
            cols = pl.ds(g * GW, GW)
            pos = i * tm + lax.broadcasted_iota(jnp.int32, (tm + HALO, 1), 0) + 1
            inv = 1.0 / jnp.minimum(pos, w).astype(F32)
            dzg = dz_ref[:, cols].astype(F32)
            ext_ref[0:tm, cols] = dzg * inv[0:tm]
            ext_ref[tm:tm + HALO, cols] = dzh_ref[:, cols].astype(F32) * inv[tm:tm + HALO] * last
            acc = _window_sum(ext_ref, win_ref, cols, tm, w, False)
            dproj_ref[:, pl.ds(2 * D + g * GW, GW)] = (acc - dzg).astype(BF16)
        dproj_ref[:, 3 * D:5 * D] = dgt_ref[...]
        chunks_per_matmul = D // cw // NCHIP
        for k in range(NCHIP):
            part = _mm_nt(dproj_ref[:, k * CIN:(k + 1) * CIN], w_ref[k])
            if k == 0:
                tmp_ref[...] = part
            else:
                tmp_ref[...] += part
            for cc in range(k * chunks_per_matmul, (k + 1) * chunks_per_matmul):
                dk_chunk(cc)
        du = tmp_ref[...]
        xf = x_ref[...]
        r1 = lax.rsqrt(jnp.mean(xf * xf, axis=-1, keepdims=True) + RMS_EPS)
        n1 = xf * r1
        vec_ref[0:1, :] += _rowsum(du * n1)
        dn1 = du * g1_ref[...]
        dx_ref[...] = dh1_ref[...] + r1 * (dn1 - n1 * jnp.mean(dn1 * n1, axis=-1, keepdims=True))

    return _call(
        body, "bwd_in", (nt,),
        [_tile(tm, D), _tile(tm, D), _tile(tm, D), _next_halo(tm, nt), _tile(tm, D), _next_halo(tm, nt),
         _tile(tm, D), _prev_halo(tm), _tile(tm, 2 * D), _tile(tm, 2 * D), _full((1, D)),
         _full((KW_PAD, D)), _full((NCHIP, D, CIN))],
        [_tile(tm, D), _tile(tm, NPROJ * D), _full((8, D)), _full((KW_PAD, D))],
        [SDS((t, D), F32), SDS((t, NPROJ * D), BF16), SDS((8, D), F32), SDS((KW_PAD, D), F32)],
        [pltpu.VMEM((tm + HALO, D), F32), pltpu.VMEM((tm, D), F32), pltpu.VMEM((tm + HALO, GW), F32),
         _taps_scratch(tm), pltpu.VMEM((tm + HALO, D), F32), _taps_scratch(tm)],
        [x, dh1, dcv, dcv, dz, dz, glu, glu, ag, dgt, g1, dwk, w_in], after=after)[0]


def wgrad(a, b, name, bm=1024, bn=1024, bt=2048, chip_major=False, exchanges=()):
    t, m = a.shape
    n = b.shape[1]
    bm, bn, bt = min(bm, m), min(bn, n), min(bt, t)
    assert m % bm == 0 and n % bn == 0 and t % bt == 0, (a.shape, b.shape, bm, bn, bt)

    def body(a_ref, b_ref, o_ref):
        k = pl.program_id(2)

        @pl.when(k == 0)
        def _():
            o_ref[...] = jnp.zeros_like(o_ref)

        o_ref[...] += _mm_tn(a_ref[...], b_ref[...])

    if chip_major:
        out_spec, out_shape = pl.BlockSpec((None, bm, bn), lambda i, j, k: (j, i, 0)), SDS((n // bn, m, bn), F32)
    else:
        out_spec, out_shape = pl.BlockSpec((bm, bn), lambda i, j, k: (i, j)), SDS((m, n), F32)
    outs, xouts = _call(
        body, name, (m // bm, n // bn, t // bt),
        [pl.BlockSpec((bt, bm), lambda i, j, k: (k, i)), pl.BlockSpec((bt, bn), lambda i, j, k: (k, j))],
        [out_spec], [out_shape], [], [a, b], exchanges)
    return outs[0], xouts


def wgrad_pool(z, dzl, bt=4096):
    t = z.shape[0]
    bt = min(bt, t)
    assert t % bt == 0

    def body(a_ref, b_ref, o_ref):
        k = pl.program_id(1)

        @pl.when(k == 0)
        def _():
            o_ref[...] = jnp.zeros_like(o_ref)

        o_ref[0] += _mm_tn(a_ref[...], b_ref[...])

    return _call(
        body, "wgrad_pool", (NG, t // bt),
        [pl.BlockSpec((bt, GW), lambda g, k: (k, g)), pl.BlockSpec((bt, GW), lambda g, k: (k, g))],
        [pl.BlockSpec((1, GW, GW), lambda g, k: (g, 0, 0))], [SDS((NG, GW, GW), F32)], [], [z, dzl])[0][0]


def cast_shards(name, names, shards, exchanges=()):
    n = len(shards)

    def body(*refs):
        srcs, dsts, bufs, sems = refs[:n], refs[n:2 * n], refs[2 * n:3 * n], refs[3 * n]
        me = 2 * lax.axis_index("x") + lax.axis_index("y")
        copies = []
        for w, mat in enumerate(names):
            bufs[w][...] = srcs[w][...].astype(BF16)
            for h in range(2):
                cp = pltpu.make_async_copy(bufs[w].at[_shard_half(mat, h)], dsts[w].at[_window(mat, me, h)], sems.at[w, h])
                cp.start()
                copies.append(cp)
        for cp in copies:
            cp.wait()

    return _call(body, name, (), [VMEM_SPEC] * n, [ANY] * n, [SDS(GEOM[mat][0], BF16) for mat in names],
                 [pltpu.VMEM(s.shape, BF16) for s in shards] + [pltpu.SemaphoreType.DMA((n, 2))], list(shards), exchanges)


VEC_ROWS = 24
ROW = dict(mlp_pre_g=0, mlp_post_g=1, loss=2, mix_post_g=8, conv_ln_g=9, conv_ln_b=10, dw_bias=11, pool_scale=12,
           mix_pre_g=16)


NDEV = 8


class GatherSmall:
    has_mid = False
    sibling_only = False
    aliases = ()

    def __init__(self, part):
        self.ins = [part]
        self.out_shapes = [SDS((NDEV, *part.shape), F32)]
        self.sems = [pltpu.SemaphoreType.DMA((NDEV,)), pltpu.SemaphoreType.DMA((NDEV,))]

    def _copies(self, ins, outs, sems):
        send_sems, recv_sems = sems
        x, y, c = lax.axis_index("x"), lax.axis_index("y"), lax.axis_index("c")
        me = 4 * x + 2 * y + c
        own = pltpu.make_async_copy(ins[0], outs[0].at[me], send_sems.at[0])
        sends = [_remote(ins[0], outs[0].at[me], send_sems.at[m], recv_sems.at[m],
                         (x ^ (m >> 2), y ^ ((m >> 1) & 1), c ^ (m & 1))) for m in range(1, NDEV)]
        recvs = [_remote(outs[0].at[me ^ m], outs[0].at[me ^ m], send_sems.at[m], recv_sems.at[m], (x, y, c))
                 for m in range(1, NDEV)]
        return own, sends, recvs

    def start(self, ins, outs, sems):
        own, sends, _ = self._copies(ins, outs, sems)
        for cp in sends + [own]:
            cp.start()

    def finish(self, ins, outs, sems):
        own, sends, recvs = self._copies(ins, outs, sems)
        for cp in recvs:
            cp.wait_recv()
        for cp in sends:
            cp.wait_send()
        own.wait()


def sum_small(parts, after=()):
    def body(parts_ref, sum_ref, loss_ref):
        total = parts_ref[0]
        for d in range(1, NDEV):
            total = total + parts_ref[d]
        sum_ref[...] = total
        r = ROW["loss"]
        loss_ref[...] = jnp.zeros_like(loss_ref) + jnp.sum(total[r:r + 1, :])

    return _call(body, "sum_small", (), [VMEM_SPEC], [VMEM_SPEC, VMEM_SPEC],
                 [SDS(parts.shape[1:], F32), SDS((8, 128), F32)], [], [parts], after=after)[0]


def pair_add(tag, names, grads, from_sibling, exchanges=()):
    n = len(names)
    in_specs, wire_specs, own_specs, wire_shapes, own_shapes = [], [], [], [], []
    for name in names:
        _, gblk, idx = GEOM[name]
        blk = _half_shape(name)
        zeros = (0,) * len(blk)
        in_specs.append(pl.BlockSpec(gblk, lambda k, idx=idx: idx(k, lax.axis_index("c"))))
        wire_specs.append(pl.BlockSpec((1, *blk), lambda k, zeros=zeros: (k, *zeros)))
        own_specs.append(pl.BlockSpec(blk, lambda k, zeros=zeros: zeros))
        wire_shapes.append(SDS((NCHIP, *blk), BF16))
        own_shapes.append(SDS(blk, F32))

    def body(*refs):
        g, s, wire, own = refs[:n], refs[n:2 * n], refs[2 * n:3 * n], refs[3 * n:]
        mine = pl.program_id(0) == 2 * lax.axis_index("x") + lax.axis_index("y")
        for w in range(n):
            total = g[w][...] + s[w][0]
            wire[w][0] = total.astype(BF16)

            @pl.when(mine)
            def _(w=w, total=total):
                own[w][...] = total

    outs, xouts = _call(body, "pair_add_" + tag, (NCHIP,), in_specs + wire_specs, wire_specs + own_specs,
                        wire_shapes + own_shapes, [], list(grads) + list(from_sibling), exchanges)
    return outs[:n], outs[n:], xouts


SUM_STEPS = 4


def sum_partials(tag, names, owns, from_chips, exchanges=()):
    n = len(names)
    own_specs, part_specs, out_specs, out_shapes, part_args, counts = [], [], [], [], [], []
    for name, parts in zip(names, from_chips):
        half = _half_shape(name)
        blk = half[:-2] + (half[-2] // SUM_STEPS, half[-1])
        lead = (0,) * (len(half) - 2)
        own_specs.append(pl.BlockSpec(blk, lambda i, lead=lead: (*lead, i, 0)))
        for p in parts:
            part_specs.append(pl.BlockSpec((p.shape[0], *blk), lambda i, lead=lead: (0, *lead, i, 0)))
            part_args.append(p)
        counts.append(len(parts))
        out_specs.append(pl.BlockSpec((2, *blk), lambda i, lead=lead: (0, *lead, i, 0)))
        out_shapes.append(SDS((2, *half), F32))

    def body(*refs):
        own, parts, out = refs[:n], list(refs[n:n + len(part_args)]), refs[n + len(part_args):]
        c = lax.axis_index("c")
        for w in range(n):
            total = own[w][...]
            for p in [parts.pop(0) for _ in range(counts[w])]:
                for j in range(p.shape[0]):
                    total = total + p[j].astype(F32)
            out[w][c] = total

    return _call(body, "sum_partials_" + tag, (SUM_STEPS,), own_specs + part_specs, out_specs, out_shapes, [],
                 list(owns) + part_args, exchanges)


class SwapHalves:
    has_mid = False
    sibling_only = True

    def __init__(self, halves):
        self.ins = list(halves)
        self.out_shapes = [SDS(h.shape, h.dtype) for h in halves]
        self.aliases = [(i, i) for i in range(len(halves))]
        self.sems = [pltpu.SemaphoreType.DMA((len(halves),)), pltpu.SemaphoreType.DMA((len(halves),))]

    def start(self, ins, outs, sems):
        send_sems, recv_sems = sems
        x, y, c = lax.axis_index("x"), lax.axis_index("y"), lax.axis_index("c")
        for w in range(len(self.ins)):
            _remote(ins[w].at[c], outs[w].at[c], send_sems.at[w], recv_sems.at[w], (x, y, 1 - c)).start()

    def finish(self, ins, outs, sems):
        send_sems, recv_sems = sems
        x, y, c = lax.axis_index("x"), lax.axis_index("y"), lax.axis_index("c")
        for w in range(len(self.ins)):
            _remote(ins[w].at[c], outs[w].at[c], send_sems.at[w], recv_sems.at[w], (x, y, 1 - c)).wait_send()
            _remote(ins[w].at[1 - c], outs[w].at[1 - c], send_sems.at[w], recv_sems.at[w], (x, y, 1 - c)).wait_recv()


SEM_SPEC = pl.BlockSpec(memory_space=pltpu.SEMAPHORE)
HBM_SPEC = pl.BlockSpec(memory_space=pltpu.HBM)
DATAFLOW = pltpu.SideEffectType.DATAFLOW_SIDE_EFFECTING


def plan_chips(names):
    def plan(refs):
        n = len(names)
        x, y, c, me, chips, chip_ids = _place()
        return [(refs[w].at[chip_ids[j]], refs[n + w].at[j], (*chip, c)) for w in range(n) for j, chip in enumerate(chips)]
    return plan


def split_start(tag, arrays, plan, ncopies):
    n = len(arrays)

    def body(*refs):
        sems, token = refs[n:n + 2 * ncopies], refs[-1]
        for s, (src, dst, to) in enumerate(plan(refs[:n])):
            _remote(src, dst, sems[2 * s], sems[2 * s + 1], to).start()
        token[...] = jnp.zeros_like(token)

    res = pl.pallas_call(
        body, name="start_" + tag, in_specs=[HBM_SPEC] * n,
        out_specs=[SEM_SPEC] * (2 * ncopies) + [HBM_SPEC] * n + [VMEM_SPEC],
        out_shape=[pltpu.SemaphoreType.DMA(())] * (2 * ncopies) + [pltpu.HBM(a.shape, a.dtype) for a in arrays]
        + [SDS((8, 128), F32)],
        input_output_aliases={i: 2 * ncopies + i for i in range(n)},
        compiler_params=pltpu.CompilerParams(has_side_effects=DATAFLOW),
    )(*arrays)
    return (res[:2 * ncopies], res[2 * ncopies:-1]), res[-1]


def split_wait(tag, started, plan, after):
    sems, arrays = started
    n = len(arrays)

    def body(*refs):
        sem = refs[n:n + len(sems)]
        for s, (src, dst, to) in enumerate(plan(refs[:n])):
            cp = _remote(src, dst, sem[2 * s], sem[2 * s + 1], to)
            cp.wait_send()
            cp.wait_recv()

    return pl.pallas_call(
        body, name="wait_" + tag, in_specs=[HBM_SPEC] * n + [SEM_SPEC] * len(sems) + [ANY] * len(after),
        out_specs=[HBM_SPEC] * n, out_shape=[pltpu.HBM(a.shape, a.dtype) for a in arrays],
        input_output_aliases={i: i for i in range(n)},
        compiler_params=pltpu.CompilerParams(has_side_effects=DATAFLOW),
    )(*arrays, *sems, *after)


def adamw(tag, ws, gs, ms, vs, steps, after=()):
    n = len(ws)
    specs = [pl.BlockSpec((a.shape[0] // steps, a.shape[1]), lambda i: (i, 0)) for a in ws]
    assert all(a.shape[0] % (steps * SUBLANES) == 0 for a in ws), [a.shape for a in ws]

    def body(*refs):
        w_, g_, m_, v_ = refs[:n], refs[n:2 * n], refs[2 * n:3 * n], refs[3 * n:4 * n]
        d_, nm_, nv_, gout_ = refs[4 * n:5 * n], refs[5 * n:6 * n], refs[6 * n:7 * n], refs[7 * n:]
        for i in range(n):
            gv = g_[i][...]
            gout_[i][...] = gv
            mn = B1 * m_[i][...] + (1.0 - B1) * gv
            vn = B2 * v_[i][...] + (1.0 - B2) * (gv * gv)
            m_hat = mn / (1.0 - B1 ** STEP)
            v_hat = vn / (1.0 - B2 ** STEP)
            d_[i][...] = -LR * (m_hat / (jnp.sqrt(v_hat) + ADAM_EPS) + WD * w_[i][...])
            nm_[i][...] = mn
            nv_[i][...] = vn

    outs, _ = _call(body, "adamw_" + tag, (steps,), specs * 4, specs * 4, [SDS(a.shape, F32) for a in ws] * 4, [],
                    list(ws) + list(gs) + list(ms) + list(vs), after=after)
    return list(zip(outs[:n], outs[n:2 * n], outs[2 * n:3 * n], outs[3 * n:]))


VECS = ("mix_pre_g", "dw_bias", "conv_ln_g", "conv_ln_b", "pool_scale", "mix_post_g", "mlp_pre_g", "mlp_post_g")
WEIGHTS = ("mix_pre_g", "w_in", "dw_kernel", "dw_bias", "conv_ln_g", "conv_ln_b", "w_conv_out", "pool_w", "pool_scale",
           "w_pool_out", "w_o", "mix_post_g", "mlp_pre_g", "w_ff1", "w_ff2", "mlp_post_g")
MIX_MATS = ("w_conv_out", "pool_w", "w_pool_out", "w_o")
FF_MATS = ("w_ff1", "w_ff2")


def kernel(x, mix_pre_g, w_in, dw_kernel, dw_bias, conv_ln_g, conv_ln_b, w_conv_out, pool_w, pool_scale, w_pool_out, w_o, mix_post_g, mlp_pre_g, w_ff1, w_ff2, mlp_post_g, loss_target, m_mix_pre_g, m_w_in, m_dw_kernel, m_dw_bias, m_conv_ln_g, m_conv_ln_b, m_w_conv_out, m_pool_w, m_pool_scale, m_w_pool_out, m_w_o, m_mix_post_g, m_mlp_pre_g, m_w_ff1, m_w_ff2, m_mlp_post_g, v_mix_pre_g, v_w_in, v_dw_kernel, v_dw_bias, v_conv_ln_g, v_conv_ln_b, v_w_conv_out, v_pool_w, v_pool_scale, v_w_pool_out, v_w_o, v_mix_post_g, v_mlp_pre_g, v_w_ff1, v_w_ff2, v_mlp_post_g):
    w = dict(mix_pre_g=mix_pre_g, w_in=w_in, dw_kernel=dw_kernel, dw_bias=dw_bias, conv_ln_g=conv_ln_g,
             conv_ln_b=conv_ln_b, w_conv_out=w_conv_out, pool_w=pool_w, pool_scale=pool_scale, w_pool_out=w_pool_out,
             w_o=w_o, mix_post_g=mix_post_g, mlp_pre_g=mlp_pre_g, w_ff1=w_ff1, w_ff2=w_ff2, mlp_post_g=mlp_post_g)
    m = dict(mix_pre_g=m_mix_pre_g, w_in=m_w_in, dw_kernel=m_dw_kernel, dw_bias=m_dw_bias, conv_ln_g=m_conv_ln_g,
             conv_ln_b=m_conv_ln_b, w_conv_out=m_w_conv_out, pool_w=m_pool_w, pool_scale=m_pool_scale,
             w_pool_out=m_w_pool_out, w_o=m_w_o, mix_post_g=m_mix_post_g, mlp_pre_g=m_mlp_pre_g, w_ff1=m_w_ff1,
             w_ff2=m_w_ff2, mlp_post_g=m_mlp_post_g)
    v = dict(mix_pre_g=v_mix_pre_g, w_in=v_w_in, dw_kernel=v_dw_kernel, dw_bias=v_dw_bias, conv_ln_g=v_conv_ln_g,
             conv_ln_b=v_conv_ln_b, w_conv_out=v_w_conv_out, pool_w=v_pool_w, pool_scale=v_pool_scale,
             w_pool_out=v_w_pool_out, w_o=v_w_o, mix_post_g=v_mix_post_g, mlp_pre_g=v_mlp_pre_g, w_ff1=v_w_ff1,
             w_ff2=v_w_ff2, mlp_post_g=v_mlp_post_g)
    chip = 2 * lax.axis_index("x") + lax.axis_index("y")
    xs, tgt = x[0], loss_target[0]
    vecs = {name: w[name].reshape(1, D) for name in VECS}

    taps = lax.dynamic_update_slice(jnp.zeros((KW_PAD, D), F32), dw_kernel, (0, chip * DSH))
    mine, full = {}, {}
    (mine["w_in"],), _ = cast_shards("cast_w_in", ("w_in",), [w["w_in"]])
    rest = MATS[1:]
    cast, ((full["w_in"], dwk),) = cast_shards(
        "cast_rest", rest, [w[name] for name in rest], [GatherWeights(("w_in",), [mine["w_in"]], taps)])
    mine.update(zip(rest, cast))
    (u, glu, ag, p, gt), (got,) = fwd_in(
        xs, vecs["mix_pre_g"], full["w_in"], TM_IN, [GatherWeights(MIX_MATS, [mine[n] for n in MIX_MATS])])
    full.update(zip(MIX_MATS, got))
    (cv, sw, z, zl, zs, yc, yp, mg, mo, h1), (got,) = fwd_mix(
        xs, glu, p, gt, dwk, vecs["dw_bias"], vecs["conv_ln_g"], vecs["conv_ln_b"], full["w_conv_out"], full["pool_w"],
        vecs["pool_scale"], full["w_pool_out"], full["w_o"], vecs["mix_post_g"], TM,
        [GatherWeights(FF_MATS, [mine[n] for n in FF_MATS])])
    full.update(zip(FF_MATS, got))

    (v_, a2, df2, df1, dh1, vec_mlp), _ = mlp_fwd_bwd(
        h1, tgt, vecs["mlp_pre_g"], vecs["mlp_post_g"], full["w_ff1"], full["w_ff2"], TM)
    grads, g, delta, new_m, new_v = {}, {}, {}, {}, {}
    rest_mats = FF_MATS + MIX_MATS
    grads["w_ff1"], _ = wgrad(v_, df1, "wgrad_ff1", bn=CFF, chip_major=True)
    grads["w_ff2"], _ = wgrad(a2, df2, "wgrad_ff2")
    landing = lambda names, slots, dt: [lax.empty((slots, *_half_shape(n)), dt) for n in names]
    grads_ff = [grads[n] for n in FF_MATS]
    (dmo, dgt, dyc, dyp, dcv, dzl, dz, vec_mix), (from_sibling,) = bwd_mix(
        dh1, mo, cv, zl, yc, yp, gt, vecs["conv_ln_g"], vecs["conv_ln_b"], vecs["pool_scale"], vecs["mix_post_g"],
        full["w_conv_out"], full["pool_w"], full["w_pool_out"], full["w_o"], TM,
        exchanges=[ExchangePair(FF_MATS, grads_ff)])
    grads["w_conv_out"], _ = wgrad(sw, dyc, "wgrad_conv_out", bt=1024)
    grads["pool_w"] = wgrad_pool(z, dzl)
    grads["w_pool_out"], _ = wgrad(zs, dyp, "wgrad_pool_out", bt=1024)
    grads["w_o"], _ = wgrad(mg, dmo, "wgrad_o", bt=1024)
    wire_ff, own_ff, (from_sibling,) = pair_add(
        "ff", FF_MATS, grads_ff, from_sibling, exchanges=[ExchangePair(MIX_MATS, [grads[n] for n in MIX_MATS])])
    wire_mix, own_mix, _ = pair_add("mix", MIX_MATS, [grads[n] for n in MIX_MATS], from_sibling)
    chips_rest, token = split_start("chips_rest", list(wire_ff) + list(wire_mix) + landing(rest_mats, NCHIP - 1, BF16),
                                    plan_chips(rest_mats), len(rest_mats) * (NCHIP - 1))
    dx, dproj, vec_in, dk = bwd_in(
        xs, dh1, dcv, dz, glu, ag, dgt, vecs["mix_pre_g"], dwk, full["w_in"], TM, after=[token])
    small_part = jnp.concatenate([vec_mlp, vec_mix, vec_in, dk], axis=0)
    grads["w_in"], ((small_parts,),) = wgrad(
        u, dproj, "wgrad_in", bn=CIN, chip_major=True, exchanges=[GatherSmall(small_part)])
    got = split_wait("chips_rest", chips_rest, plan_chips(rest_mats), after=[grads["w_in"]])[len(rest_mats):]
    chips_ff, chips_mix = got[:len(FF_MATS)], got[len(FF_MATS):]
    halves_rest, (from_sibling,) = sum_partials(
        "rest", rest_mats, list(own_ff) + list(own_mix), [[a] for a in list(chips_ff) + list(chips_mix)],
        [ExchangePair(("w_in",), [grads["w_in"]])])
    wire_in, own_in, (reduced_rest,) = pair_add(
        "in", ("w_in",), [grads["w_in"]], from_sibling, exchanges=[SwapHalves(halves_rest)])
    for n, red in zip(rest_mats, reduced_rest):
        g[n] = red.reshape(w[n].shape)

    def update(tag, names, steps, after=()):
        two_d = lambda a: a.reshape(-1, a.shape[-1])
        res = adamw(tag, [two_d(w[n]) for n in names], [two_d(g[n]) for n in names],
                    [two_d(m[n]) for n in names], [two_d(v[n]) for n in names], steps, after)
        for n, outs in zip(names, res):
            delta[n], new_m[n], new_v[n], g[n] = [a.reshape(w[n].shape) for a in outs]

    chips_in, token = split_start("chips_in", list(wire_in) + landing(("w_in",), NCHIP - 1, BF16),
                                  plan_chips(("w_in",)), NCHIP - 1)
    small, loss8 = sum_small(small_parts, after=[token])
    loss = loss8[0, 0]
    for name in VECS:
        g[name] = small[ROW[name]]
    g["dw_kernel"] = lax.dynamic_slice(small[VEC_ROWS:VEC_ROWS + KW_PAD], (0, chip * DSH), (KW_PAD, DSH))
    update("rest", rest_mats, 8, after=[token])
    stack = lambda d: jnp.concatenate([d[name].reshape(1, D) for name in VECS], axis=0)
    (res,) = adamw("vectors", [stack(w)], [stack(g)], [stack(m)], [stack(v)], 1, after=[token])
    for i, name in enumerate(VECS):
        delta[name], new_m[name], new_v[name] = [r[i] for r in res[:3]]
    padk = lambda a: jnp.pad(a, ((0, KW_PAD - KW), (0, 0)))
    (res,) = adamw("dw_kernel", [padk(w["dw_kernel"])], [g["dw_kernel"]], [padk(m["dw_kernel"])],
                   [padk(v["dw_kernel"])], 1, after=[token])
    delta["dw_kernel"], new_m["dw_kernel"], new_v["dw_kernel"] = [r[:KW] for r in res[:3]]
    g["dw_kernel"] = g["dw_kernel"][:KW]
    chips_in = split_wait("chips_in", chips_in, plan_chips(("w_in",)),
                          after=[delta[rest_mats[-1]], delta["dw_kernel"], delta[VECS[0]]])[1:]
    halves_in, _ = sum_partials("in", ("w_in",), own_in, [chips_in])
    (reduced_in,) = exchange("swap_w_in", SwapHalves(halves_in))
    g["w_in"] = reduced_in.reshape(w["w_in"].shape)
    update("in", ("w_in",), 8)

    return (loss, dx[None], *[g[n] for n in WEIGHTS], *[delta[n] for n in WEIGHTS], *[new_m[n] for n in WEIGHTS],
            *[new_v[n] for n in WEIGHTS])
```

```python
import math

import jax
import jax.numpy as jnp
from jax import lax
from jax.experimental import pallas as pl
from jax.experimental.pallas import tpu as pltpu

F32 = jnp.float32
BF16 = jnp.bfloat16

D = 1024
FF = 4096
NPROJ = 5
KW = 31
KW_PAD = 32
SUBLANES = 8
LANES = 128
HALO = 32
POOL_WINDOWS = (2, 4, 8, 16)
NG = 4
GW = D // NG
RMS_EPS = 1e-6
LN_EPS = 1e-5
LR, B1, B2, ADAM_EPS, WD, STEP = 0.001, 0.9, 0.999, 1e-08, 0.01, 10
NCHIP = 4
VMEM_LIMIT = 60 * 1024 * 1024
MESH = pl.DeviceIdType.MESH
SIBLING_COLLECTIVE_ID = 0
GATHER_COLLECTIVE_ID = 1
TM = 256
TM_IN = 512

ANY = pl.BlockSpec(memory_space=pl.ANY)
VMEM_SPEC = pl.BlockSpec(memory_space=pltpu.VMEM)
SDS = jax.ShapeDtypeStruct


def _cp(**kw):
    return pltpu.CompilerParams(vmem_limit_bytes=VMEM_LIMIT, **kw)


def _mm(a, b):
    return jnp.dot(a, b, preferred_element_type=F32)


def _mm_nt(a, b):
    return lax.dot_general(a, b, (((1,), (1,)), ((), ())), preferred_element_type=F32)


def _mm_tn(a, b):
    return lax.dot_general(a, b, (((0,), (0,)), ((), ())), preferred_element_type=F32)


def _sigmoid(x):
    return 1.0 / (1.0 + jnp.exp(-x))


def _rowsum(x):
    return jnp.sum(x, axis=0, keepdims=True)


def _full(shape):
    return pl.BlockSpec(shape, lambda i: (0,) * len(shape))


def _tile(tm, cols):
    return pl.BlockSpec((tm, cols), lambda i: (i, 0))


def _prev_halo(tm):
    return pl.BlockSpec((HALO, D), lambda i: (jnp.maximum(i * (tm // HALO) - 1, 0), 0))


def _next_halo(tm, nt):
    return pl.BlockSpec((HALO, D), lambda i: (jnp.minimum((i + 1) * (tm // HALO), nt * (tm // HALO) - 1), 0))


MATS = ("w_in", "w_conv_out", "pool_w", "w_pool_out", "w_o", "w_ff1", "w_ff2")
CIN = NPROJ * D // NCHIP
CFF = FF // NCHIP
_ROWS = lambda k, h: (2 * k + h, 0)
_CHIP_MAJOR = lambda k, h: (k, h, 0)
GEOM = dict(
    w_in=((NCHIP, D, CIN), (None, D // 2, CIN), _CHIP_MAJOR),
    w_conv_out=((D, D), (D // (2 * NCHIP), D), _ROWS),
    pool_w=((NG, GW, GW), (NG // 2, GW // NCHIP, GW), lambda k, h: (h, k, 0)),
    w_pool_out=((D, D), (D // (2 * NCHIP), D), _ROWS),
    w_o=((D, D), (D // (2 * NCHIP), D), _ROWS),
    w_ff1=((NCHIP, D, CFF), (None, D // 2, CFF), _CHIP_MAJOR),
    w_ff2=((FF, D), (FF // (2 * NCHIP), D), _ROWS),
)
DSH = D // NCHIP


def _half_shape(name):
    return tuple(b for b in GEOM[name][1] if b is not None)


def _window(name, k, h):
    _, blk, idx = GEOM[name]
    return tuple(i if b is None else pl.ds(i * b, b) for i, b in zip(idx(k, h), blk))


def _shard_half(name, h):
    n0 = _half_shape(name)[0]
    return (pl.ds(h * n0, n0),) + (slice(None),) * (len(_half_shape(name)) - 1)


def _place():
    x, y, c = lax.axis_index("x"), lax.axis_index("y"), lax.axis_index("c")
    chips = [(1 - x, y), (x, 1 - y), (1 - x, 1 - y)]
    return x, y, c, 2 * x + y, chips, [2 * px + py for px, py in chips]


def _remote(src, dst, send_sem, recv_sem, to):
    return pltpu.make_async_remote_copy(src_ref=src, dst_ref=dst, send_sem=send_sem, recv_sem=recv_sem,
                                        device_id=to, device_id_type=MESH)


class GatherWeights:
    has_mid = True
    sibling_only = False

    def __init__(self, names, fulls, taps=None):
        self.names = names
        self.ins = list(fulls) + ([taps] if taps is not None else [])
        self.has_taps = taps is not None
        self.out_shapes = [SDS(a.shape, a.dtype) for a in self.ins]
        self.aliases = [(i, i) for i in range(len(self.ins))]
        n = len(self.ins)
        self.sems = [pltpu.SemaphoreType.DMA((n, 6)), pltpu.SemaphoreType.DMA((n, 6))]

    def _copies(self, ins, outs, sems):
        send_sems, recv_sems = sems
        x, y, c, me, chips, chip_ids = _place()
        sibling = (x, y, 1 - c)
        ici, ici_recv, d2d, d2d_recv = [], [], [], []
        for w, name in enumerate(self.names):
            for j, chip in enumerate(chips):
                ici.append(_remote(ins[w].at[_window(name, me, c)], outs[w].at[_window(name, me, c)],
                                   send_sems.at[w, j], recv_sems.at[w, j], (*chip, c)))
                got = outs[w].at[_window(name, chip_ids[j], c)]
                ici_recv.append(_remote(got, got, send_sems.at[w, j], recv_sems.at[w, j], sibling))
                d2d.append(_remote(got, got, send_sems.at[w, 3 + j], recv_sems.at[w, 3 + j], sibling))
                got = outs[w].at[_window(name, chip_ids[j], 1 - c)]
                d2d_recv.append(_remote(got, got, send_sems.at[w, 3 + j], recv_sems.at[w, 3 + j], sibling))
        if self.has_taps:
            w = len(self.names)
            for j, chip in enumerate(chips):
                ici.append(_remote(ins[w].at[:, pl.ds(me * DSH, DSH)], outs[w].at[:, pl.ds(me * DSH, DSH)],
                                   send_sems.at[w, j], recv_sems.at[w, j], (*chip, c)))
                got = outs[w].at[:, pl.ds(chip_ids[j] * DSH, DSH)]
                d2d_recv.append(_remote(got, got, send_sems.at[w, j], recv_sems.at[w, j], sibling))
        return ici, ici_recv, d2d, d2d_recv

    def start(self, ins, outs, sems):
        for cp in self._copies(ins, outs, sems)[0]:
            cp.start()

    def mid(self, ins, outs, sems):
        _, ici_recv, d2d, _ = self._copies(ins, outs, sems)
        for got, fwd in zip(ici_recv, d2d):
            got.wait_recv()
            fwd.start()

    def finish(self, ins, outs, sems):
        ici, _, d2d, d2d_recv = self._copies(ins, outs, sems)
        for cp in d2d_recv:
            cp.wait_recv()
        for cp in ici + d2d:
            cp.wait_send()


class ExchangePair:
    has_mid = False
    sibling_only = True
    aliases = ()

    def __init__(self, names, grads):
        self.names, self.ins = names, list(grads)
        self.out_shapes = [SDS((NCHIP, *_half_shape(n)), F32) for n in names]
        self.sems = [pltpu.SemaphoreType.DMA((len(names),)), pltpu.SemaphoreType.DMA((len(names),))]

    def start(self, ins, outs, sems):
        send_sems, recv_sems = sems
        x, y, c, me, chips, chip_ids = _place()
        for w, name in enumerate(self.names):
            for k in range(NCHIP):
                _remote(ins[w].at[_window(name, k, 1 - c)], outs[w].at[k], send_sems.at[w], recv_sems.at[w],
                        (x, y, 1 - c)).start()

    def finish(self, ins, outs, sems):
        send_sems, recv_sems = sems
        x, y, c, me, chips, chip_ids = _place()
        for w in range(len(self.names)):
            _remote(outs[w], outs[w], send_sems.at[w], recv_sems.at[w], (x, y, 1 - c)).wait()


class ExchangeChips:
    has_mid = False
    sibling_only = False
    aliases = ()

    def __init__(self, names, wires):
        self.names, self.ins = names, list(wires)
        self.out_shapes = [SDS((NCHIP - 1, *_half_shape(n)), BF16) for n in names]
        self.sems = [pltpu.SemaphoreType.DMA((len(names), NCHIP - 1)), pltpu.SemaphoreType.DMA((len(names), NCHIP - 1))]

    def _copies(self, ins, outs, sems):
        send_sems, recv_sems = sems
        x, y, c, me, chips, chip_ids = _place()
        return [_remote(ins[w].at[chip_ids[j]], outs[w].at[j], send_sems.at[w, j], recv_sems.at[w, j], (*chip, c))
                for w in range(len(self.names)) for j, chip in enumerate(chips)]

    def start(self, ins, outs, sems):
        for cp in self._copies(ins, outs, sems):
            cp.start()

    def finish(self, ins, outs, sems):
        for cp in self._copies(ins, outs, sems):
            cp.wait()


def _call(body, name, grid, in_specs, out_specs, out_shape, scratch, args, exchanges=(), after=()):
    n_in, n_out, n_scr = len(in_specs), len(out_specs), len(scratch)
    x_in = [a for e in exchanges for a in e.ins]
    x_out = [s for e in exchanges for s in e.out_shapes]
    x_sem = [s for e in exchanges for s in e.sems]
    nsteps = math.prod(grid)
    sibling_only = bool(exchanges) and all(e.sibling_only for e in exchanges)
    gather_only = bool(exchanges) and all(isinstance(e, GatherWeights) for e in exchanges)

    def wrapped(*refs):
        ins, rest = refs[:n_in], refs[n_in:]
        xin, rest = rest[:len(x_in)], rest[len(x_in) + len(after):]
        outs, rest = rest[:n_out], rest[n_out:]
        xout, rest = rest[:len(x_out)], rest[len(x_out):]
        scr, xsem = rest[:n_scr], rest[n_scr:]
        parts = []
        for e in exchanges:
            parts.append((xin[:len(e.ins)], xout[:len(e.out_shapes)], xsem[:len(e.sems)]))
            xin, xout, xsem = xin[len(e.ins):], xout[len(e.out_shapes):], xsem[len(e.sems):]
        def start_all():
            if sibling_only or gather_only:
                barrier = pltpu.get_barrier_semaphore()
                x, y, c = lax.axis_index("x"), lax.axis_index("y"), lax.axis_index("c")
                peers = [(x, y, 1 - c)] + ([(1 - x, y, c), (x, 1 - y, c), (1 - x, 1 - y, c)] if gather_only else [])
                for peer in peers:
                    pl.semaphore_signal(barrier, inc=1, device_id=peer, device_id_type=MESH)
                pl.semaphore_wait(barrier, len(peers))
            for e, p in zip(exchanges, parts):
                e.start(*p)

        if not grid:
            start_all()
            body(*ins, *outs, *scr)
            for e, p in zip(exchanges, parts):
                if e.has_mid:
                    e.mid(*p)
            for e, p in zip(exchanges, parts):
                e.finish(*p)
            return
        step = 0
        for axis, extent in enumerate(grid):
            step = step * extent + pl.program_id(axis)
        if exchanges:
            pl.when(step == 0)(start_all)

        body(*ins, *outs, *scr)
        if any(e.has_mid for e in exchanges):
            @pl.when(step == max(nsteps - 2, 0))
            def _():
                for e, p in zip(exchanges, parts):
                    if e.has_mid:
                        e.mid(*p)

        if exchanges:
            @pl.when(step == nsteps - 1)
            def _():
                for e, p in zip(exchanges, parts):
                    e.finish(*p)

    barrier = (dict(collective_id=SIBLING_COLLECTIVE_ID) if sibling_only else
               dict(collective_id=GATHER_COLLECTIVE_ID) if gather_only else {})
    kw = dict(grid=grid, compiler_params=_cp(dimension_semantics=("arbitrary",) * len(grid), **barrier)) if grid else dict(
        compiler_params=_cp(**barrier))
    aliases, i0, o0 = {}, n_in, n_out
    for e in exchanges:
        aliases.update({i0 + i: o0 + o for i, o in e.aliases})
        i0, o0 = i0 + len(e.ins), o0 + len(e.out_shapes)
    res = pl.pallas_call(
        wrapped, name=name, in_specs=list(in_specs) + [ANY] * (len(x_in) + len(after)),
        out_specs=list(out_specs) + [ANY] * len(x_out),
        out_shape=list(out_shape) + x_out, scratch_shapes=list(scratch) + x_sem, input_output_aliases=aliases, **kw,
    )(*args, *x_in, *after)
    outs, rest = res[:n_out], res[n_out:]
    xouts = []
    for e in exchanges:
        xouts.append(rest[:len(e.out_shapes)])
        rest = rest[len(e.out_shapes):]
    return outs, xouts


def exchange(name, ex):
    return _call(lambda: None, name, (), [], [], [], [], [], [ex])[1][0]


def _taps_scratch(tm):
    return pltpu.VMEM((SUBLANES, tm + HALO, LANES), F32)


def _taps(src_ref, k_ref, dst_ref, sh_ref, tm, off0, reverse):
    rc, cw = 64, LANES

    def col_chunk(cc, carry):
        cols = pl.ds(pl.multiple_of(cc * cw, cw), cw)
        for q in range(SUBLANES):
            n = tm + SUBLANES * (len(range(q, KW, SUBLANES)) - 1)
            sh_ref[q, 0:n, :] = src_ref[pl.ds(off0 + q, n), cols]
        for r in range(tm // rc):
            acc = jnp.zeros((rc, cw), F32)
            for q in range(SUBLANES):
                for a, j in enumerate(range(q, KW, SUBLANES)):
                    kj = KW - 1 - j if reverse else j
                    acc = acc + k_ref[kj:kj + 1, cols] * sh_ref[q, pl.ds(r * rc + SUBLANES * a, rc), :]
            dst_ref[pl.ds(r * rc, rc), cols] = acc
        return carry

    lax.fori_loop(0, D // cw, col_chunk, 0)


def _proj_pieces(g):
    lo, hi, pieces = g * D, (g + 1) * D, []
    while lo < hi:
        k = lo // CIN
        b = min(hi - k * CIN, CIN)
        pieces.append((k, lo - k * CIN, b))
        lo = k * CIN + b
    return pieces


def fwd_in(x, g1, w_in, tm, exchanges=()):
    t = x.shape[0]

    def body(x_ref, g_ref, w_ref, u_ref, glu_ref, ag_ref, p_ref, gt_ref):
        xf = x_ref[...]
        r = lax.rsqrt(jnp.mean(xf * xf, axis=-1, keepdims=True) + RMS_EPS)
        u = (xf * r * g_ref[...]).astype(BF16)
        u_ref[...] = u
        proj = lambda g: jnp.concatenate([_mm(u, w_ref[k, :, lo:hi]) for k, lo, hi in _proj_pieces(g)], axis=1)
        a = proj(0)
        gate = proj(1)
        glu_ref[...] = a * _sigmoid(gate)
        ag_ref[:, 0:D] = a.astype(BF16)
        ag_ref[:, D:2 * D] = gate.astype(BF16)
        p_ref[...] = proj(2)
        gt_ref[:, 0:D] = proj(3).astype(BF16)
        gt_ref[:, D:2 * D] = proj(4).astype(BF16)

    return _call(
        body, "fwd_in", (t // tm,),
        [_tile(tm, D), _full((1, D)), _full((NCHIP, D, CIN))],
        [_tile(tm, D), _tile(tm, D), _tile(tm, 2 * D), _tile(tm, D), _tile(tm, 2 * D)],
        [SDS((t, D), BF16), SDS((t, D), F32), SDS((t, 2 * D), BF16), SDS((t, D), F32), SDS((t, 2 * D), BF16)],
        [], [x, g1, w_in], exchanges)


def _pool_inv_count(i, tm, w):
    pos = i * tm + lax.broadcasted_iota(jnp.int32, (tm, 1), 0) + 1
    return 1.0 / jnp.minimum(pos, w).astype(F32)


def _window_sum(src_ref, tmp_ref, cols, tm, w, causal):
    lo, hi = 0, tm + HALO
    cur, span = None, 1
    while span < w:
        new_lo, new_hi = (lo + SUBLANES, hi) if causal else (lo, hi - SUBLANES)
        far = new_lo - span if causal else new_lo + span
        n = new_hi - new_lo
        if cur is None:
            near_v, far_v = src_ref[pl.ds(new_lo, n), cols], src_ref[pl.ds(far, n), cols]
        else:
            near_v = cur[new_lo - lo:new_lo - lo + n]
            if span % SUBLANES == 0:
                far_v = cur[far - lo:far - lo + n]
            else:
                tmp_ref[pl.ds(lo, hi - lo), :] = cur
                far_v = tmp_ref[pl.ds(far, n), :]
        cur, lo, hi, span = near_v + far_v, new_lo, new_hi, 2 * span
    off = HALO if causal else 0
    return cur[off - lo:off - lo + tm]


def fwd_mix(x, glu, p, gt, dwk, dwb, lng, lnb, w_co, pool_w, ps, w_po, w_o, g2, tm, exchanges=()):
    t = x.shape[0]

    def body(x_ref, glu_ref, gluh_ref, p_ref, ph_ref, gt_ref, k_ref, b_ref, lg_ref, lb_ref, wco_ref, pw_ref,
             ps_ref, wpo_ref, wo_ref, g2_ref,
             cv_ref, sw_ref, z_ref, zl_ref, zs_ref, yc_ref, yp_ref, mg_ref, mo_ref, h1_ref, ext_ref, win_ref, sh_ref):
        i = pl.program_id(0)
        keep = (i > 0).astype(F32)
        ext_ref[0:HALO, :] = gluh_ref[...] * keep
        ext_ref[HALO:HALO + tm, :] = glu_ref[...]
        _taps(ext_ref, k_ref, cv_ref, sh_ref, tm, HALO - (KW - 1), False)
        cv = cv_ref[...] + b_ref[...]
        cv_ref[...] = cv
        mu = jnp.mean(cv, axis=-1, keepdims=True)
        cen = cv - mu
        rstd = lax.rsqrt(jnp.mean(cen * cen, axis=-1, keepdims=True) + LN_EPS)
        ln = cen * rstd * lg_ref[...] + lb_ref[...]
        sw = (ln * _sigmoid(ln)).astype(BF16)
        sw_ref[...] = sw
        yc = _mm(sw, wco_ref[...])
        yc_ref[...] = yc.astype(BF16)
        ext_ref[0:HALO, :] = ph_ref[...] * keep
        ext_ref[HALO:HALO + tm, :] = p_ref[...]
        for g, w in enumerate(POOL_WINDOWS):
            cols = pl.ds(g * GW, GW)
            acc = _window_sum(ext_ref, win_ref, cols, tm, w, True)
            zg = (acc * _pool_inv_count(i, tm, w) - p_ref[:, cols]).astype(BF16)
            z_ref[:, cols] = zg
            zl = _mm(zg, pw_ref[g])
            zl_ref[:, cols] = zl.astype(BF16)
            zs_ref[:, cols] = (zl * ps_ref[:, cols]).astype(BF16)
        yp = _mm(zs_ref[...], wpo_ref[...])
        yp_ref[...] = yp.astype(BF16)
        gc = _sigmoid(gt_ref[:, 0:D].astype(F32))
        gp = _sigmoid(gt_ref[:, D:2 * D].astype(F32))
        mg = (gc * yc + gp * yp).astype(BF16)
        mg_ref[...] = mg
        mo = _mm(mg, wo_ref[...])
        mo_ref[...] = mo
        r2 = lax.rsqrt(jnp.mean(mo * mo, axis=-1, keepdims=True) + RMS_EPS)
        h1_ref[...] = x_ref[...] + mo * r2 * g2_ref[...]

    vec = _full((1, D))
    act = lambda dt: SDS((t, D), dt)
    return _call(
        body, "fwd_mix", (t // tm,),
        [_tile(tm, D), _tile(tm, D), _prev_halo(tm), _tile(tm, D), _prev_halo(tm), _tile(tm, 2 * D),
         _full((KW_PAD, D)), vec, vec, vec, _full((D, D)), _full((NG, GW, GW)), vec, _full((D, D)), _full((D, D)), vec],
        [_tile(tm, D)] * 10,
        [act(F32), act(BF16), act(BF16), act(BF16), act(BF16), act(BF16), act(BF16), act(BF16), act(F32), act(F32)],
        [pltpu.VMEM((tm + HALO, D), F32), pltpu.VMEM((tm + HALO, GW), F32), _taps_scratch(tm)],
        [x, glu, glu, p, p, gt, dwk, dwb, lng, lnb, w_co, pool_w, ps, w_po, w_o, g2], exchanges)


def mlp_fwd_bwd(h1, tgt, g3, g4, w1, w2, tm, exchanges=()):
    t = h1.shape[0]
    fc = CFF

    def body(h1_ref, tgt_ref, g3_ref, g4_ref, w1_ref, w2_ref,
             v_ref, a2_ref, df2_ref, df1_ref, dh1_ref, vec_ref, f1_ref):
        i = pl.program_id(0)

        @pl.when(i == 0)
        def _():
            vec_ref[...] = jnp.zeros_like(vec_ref)

        h1v = h1_ref[...]
        r3 = lax.rsqrt(jnp.mean(h1v * h1v, axis=-1, keepdims=True) + RMS_EPS)
        n3 = h1v * r3
        v = (n3 * g3_ref[...]).astype(BF16)
        v_ref[...] = v
        f2 = jnp.zeros((tm, D), F32)
        for c in range(FF // fc):
            cols = pl.ds(c * fc, fc)
            f1 = jnp.maximum(_mm(v, w1_ref[c]), 0.0)
            f1_ref[:, cols] = f1
            a2 = (f1 * f1).astype(BF16)
            a2_ref[:, cols] = a2
            f2 = f2 + _mm(a2, w2_ref[cols, :])
        r4 = lax.rsqrt(jnp.mean(f2 * f2, axis=-1, keepdims=True) + RMS_EPS)
        n4 = f2 * r4
        err = h1v + n4 * g4_ref[...] - tgt_ref[...]
        vec_ref[2:3, :] += _rowsum(err * err) * (0.5 / D)
        dh2 = err * (1.0 / D)
        vec_ref[1:2, :] += _rowsum(dh2 * n4)
        dn4 = dh2 * g4_ref[...]
        df2 = (r4 * (dn4 - n4 * jnp.mean(dn4 * n4, axis=-1, keepdims=True))).astype(BF16)
        df2_ref[...] = df2
        dv = jnp.zeros((tm, D), F32)
        for c in range(FF // fc):
            cols = pl.ds(c * fc, fc)
            da2 = _mm_nt(df2, w2_ref[cols, :])
            df1 = (da2 * (2.0 * f1_ref[:, cols])).astype(BF16)
            df1_ref[:, cols] = df1
            dv = dv + _mm_nt(df1, w1_ref[c])
        vec_ref[0:1, :] += _rowsum(dv * n3)
        dn3 = dv * g3_ref[...]
        dh1_ref[...] = dh2 + r3 * (dn3 - n3 * jnp.mean(dn3 * n3, axis=-1, keepdims=True))

    vec = _full((1, D))
    return _call(
        body, "mlp_fwd_bwd", (t // tm,),
        [_tile(tm, D), _tile(tm, D), vec, vec, _full((NCHIP, D, CFF)), _full((FF, D))],
        [_tile(tm, D), _tile(tm, FF), _tile(tm, D), _tile(tm, FF), _tile(tm, D), _full((8, D))],
        [SDS((t, D), BF16), SDS((t, FF), BF16), SDS((t, D), BF16), SDS((t, FF), BF16), SDS((t, D), F32),
         SDS((8, D), F32)],
        [pltpu.VMEM((tm, FF), F32)], [h1, tgt, g3, g4, w1, w2], exchanges)


def bwd_mix(dh1, mo, cv, zl, yc, yp, gt, lng, lnb, ps, g2, w_co, pool_w, w_po, w_o, tm, exchanges=()):
    t = dh1.shape[0]

    def body(dh1_ref, mo_ref, cv_ref, zl_ref, yc_ref, yp_ref, gt_ref, lg_ref, lb_ref, ps_ref, g2_ref,
             wco_ref, pw_ref, wpo_ref, wo_ref,
             dmo_ref, dgt_ref, dyc_ref, dyp_ref, dcv_ref, dzl_ref, dz_ref, vec_ref):
        i = pl.program_id(0)

        @pl.when(i == 0)
        def _():
            vec_ref[...] = jnp.zeros_like(vec_ref)

        dh1v = dh1_ref[...]
        mo = mo_ref[...]
        r2 = lax.rsqrt(jnp.mean(mo * mo, axis=-1, keepdims=True) + RMS_EPS)
        n2 = mo * r2
        vec_ref[0:1, :] += _rowsum(dh1v * n2)
        dn2 = dh1v * g2_ref[...]
        dmo = (r2 * (dn2 - n2 * jnp.mean(dn2 * n2, axis=-1, keepdims=True))).astype(BF16)
        dmo_ref[...] = dmo
        dmg = _mm_nt(dmo, wo_ref[...])
        gc = _sigmoid(gt_ref[:, 0:D].astype(F32))
        gp = _sigmoid(gt_ref[:, D:2 * D].astype(F32))
        dgt_ref[:, 0:D] = (dmg * yc_ref[...].astype(F32) * gc * (1.0 - gc)).astype(BF16)
        dgt_ref[:, D:2 * D] = (dmg * yp_ref[...].astype(F32) * gp * (1.0 - gp)).astype(BF16)
        dyc = (dmg * gc).astype(BF16)
        dyp = (dmg * gp).astype(BF16)
        dyc_ref[...] = dyc
        dyp_ref[...] = dyp
        dsw = _mm_nt(dyc, wco_ref[...])
        cv = cv_ref[...]
        mu = jnp.mean(cv, axis=-1, keepdims=True)
        cen = cv - mu
        rstd = lax.rsqrt(jnp.mean(cen * cen, axis=-1, keepdims=True) + LN_EPS)
        y = cen * rstd
        ln = y * lg_ref[...] + lb_ref[...]
        sg = _sigmoid(ln)
        dln = dsw * (sg * (1.0 + ln * (1.0 - sg)))
        vec_ref[1:2, :] += _rowsum(dln * y)
        vec_ref[2:3, :] += _rowsum(dln)
        dy = dln * lg_ref[...]
        dcv = rstd * (dy - jnp.mean(dy, axis=-1, keepdims=True) - y * jnp.mean(dy * y, axis=-1, keepdims=True))
        dcv_ref[...] = dcv.astype(BF16)
        vec_ref[3:4, :] += _rowsum(dcv)
        dzs = _mm_nt(dyp, wpo_ref[...])
        vec_ref[4:5, :] += _rowsum(dzs * zl_ref[...].astype(F32))
        dzl = (dzs * ps_ref[...]).astype(BF16)
        dzl_ref[...] = dzl
        for g in range(NG):
            cols = pl.ds(g * GW, GW)
            dz_ref[:, cols] = _mm_nt(dzl_ref[:, cols], pw_ref[g]).astype(BF16)

    vec = _full((1, D))
    act = lambda dt: SDS((t, D), dt)
    return _call(
        body, "bwd_mix", (t // tm,),
        [_tile(tm, D)] * 6 + [_tile(tm, 2 * D), vec, vec, vec, vec, _full((D, D)), _full((NG, GW, GW)), _full((D, D)),
                              _full((D, D))],
        [_tile(tm, D), _tile(tm, 2 * D)] + [_tile(tm, D)] * 5 + [_full((8, D))],
        [act(BF16), SDS((t, 2 * D), BF16), act(BF16), act(BF16), act(BF16), act(BF16), act(BF16), SDS((8, D), F32)],
        [], [dh1, mo, cv, zl, yc, yp, gt, lng, lnb, ps, g2, w_co, pool_w, w_po, w_o], exchanges)


def bwd_in(x, dh1, dcv, dz, glu, ag, dgt, g1, dwk, w_in, tm, after=()):
    t = x.shape[0]
    nt = t // tm

    def body(x_ref, dh1_ref, dcv_ref, dcvh_ref, dz_ref, dzh_ref, glu_ref, gluh_ref, ag_ref, dgt_ref, g1_ref,
             k_ref, w_ref, dx_ref, dproj_ref, vec_ref, dk_ref, ext_ref, tmp_ref, win_ref, sh_ref, gext_ref, gsh_ref):
        i = pl.program_id(0)

        @pl.when(i == 0)
        def _():
            vec_ref[...] = jnp.zeros_like(vec_ref)
            dk_ref[...] = jnp.zeros_like(dk_ref)

        first = (i > 0).astype(F32)
        last = (i < nt - 1).astype(F32)
        gext_ref[0:HALO, :] = gluh_ref[...] * first
        gext_ref[HALO:HALO + tm, :] = glu_ref[...]
        rc, cw = 32, LANES

        def dk_chunk(cc):
            cols = pl.ds(cc * cw, cw)
            for q in range(SUBLANES):
                taps = range(q, KW, SUBLANES)
                n = tm + SUBLANES * (len(taps) - 1)
                gsh_ref[q, 0:n, :] = gext_ref[pl.ds(HALO - (KW - 1) + q, n), cols]
                accs = [jnp.zeros((SUBLANES, cw), F32) for _ in taps]
                for r in range(tm // rc):
                    dchunk = dcv_ref[pl.ds(r * rc, rc), cols].astype(F32)
                    for a in range(len(taps)):
                        prod = dchunk * gsh_ref[q, pl.ds(r * rc + SUBLANES * a, rc), :]
                        accs[a] = accs[a] + jnp.sum(prod.reshape(rc // SUBLANES, SUBLANES, cw), axis=0)
                for a, j in enumerate(taps):
                    dk_ref[j:j + 1, cols] += _rowsum(accs[a])

        ext_ref[0:tm, :] = dcv_ref[...].astype(F32)
        ext_ref[tm:tm + HALO, :] = dcvh_ref[...].astype(F32) * last
        _taps(ext_ref, k_ref, tmp_ref, sh_ref, tm, 0, True)
        dglu = tmp_ref[...]
        a = ag_ref[:, 0:D].astype(F32)
        sg = _sigmoid(ag_ref[:, D:2 * D].astype(F32))
        dproj_ref[:, 0:D] = (dglu * sg).astype(BF16)
        dproj_ref[:, D:2 * D] = (dglu * a * sg * (1.0 - sg)).astype(BF16)
        for g, w in enumerate(POOL_WINDOWS):
            cols = pl.ds(g * GW, GW)
            pos = i * tm + lax.broadcasted_iota(jnp.int32, (tm + HALO, 1), 0) + 1
            inv = 1.0 / jnp.minimum(pos, w).astype(F32)
            dzg = dz_ref[:, cols].astype(F32)
            ext_ref[0:tm, cols] = dzg * inv[0:tm]
            ext_ref[tm:tm + HALO, cols] = dzh_ref[:, cols].astype(F32) * inv[tm:tm + HALO] * last
            acc = _window_sum(ext_ref, win_ref, cols, tm, w, False)
            dproj_ref[:, pl.ds(2 * D + g * GW, GW)] = (acc - dzg).astype(BF16)
        dproj_ref[:, 3 * D:5 * D] = dgt_ref[...]
        chunks_per_matmul = D // cw // NCHIP
        for k in range(NCHIP):
            part = _mm_nt(dproj_ref[:, k * CIN:(k + 1) * CIN], w_ref[k])
            if k == 0:
                tmp_ref[...] = part
            else:
                tmp_ref[...] += part
            for cc in range(k * chunks_per_matmul, (k + 1) * chunks_per_matmul):
                dk_chunk(cc)
        du = tmp_ref[...]
        xf = x_ref[...]
        r1 = lax.rsqrt(jnp.mean(xf * xf, axis=-1, keepdims=True) + RMS_EPS)
        n1 = xf * r1
        vec_ref[0:1, :] += _rowsum(du * n1)
        dn1 = du * g1_ref[...]
        dx_ref[...] = dh1_ref[...] + r1 * (dn1 - n1 * jnp.mean(dn1 * n1, axis=-1, keepdims=True))

    return _call(
        body, "bwd_in", (nt,),
        [_tile(tm, D), _tile(tm, D), _tile(tm, D), _next_halo(tm, nt), _tile(tm, D), _next_halo(tm, nt),
         _tile(tm, D), _prev_halo(tm), _tile(tm, 2 * D), _tile(tm, 2 * D), _full((1, D)),
         _full((KW_PAD, D)), _full((NCHIP, D, CIN))],
        [_tile(tm, D), _tile(tm, NPROJ * D), _full((8, D)), _full((KW_PAD, D))],
        [SDS((t, D), F32), SDS((t, NPROJ * D), BF16), SDS((8, D), F32), SDS((KW_PAD, D), F32)],
        [pltpu.VMEM((tm + HALO, D), F32), pltpu.VMEM((tm, D), F32), pltpu.VMEM((tm + HALO, GW), F32),
         _taps_scratch(tm), pltpu.VMEM((tm + HALO, D), F32), _taps_scratch(tm)],
        [x, dh1, dcv, dcv, dz, dz, glu, glu, ag, dgt, g1, dwk, w_in], after=after)[0]


def wgrad(a, b, name, bm=1024, bn=1024, bt=2048, chip_major=False, exchanges=()):
    t, m = a.shape
    n = b.shape[1]
    bm, bn, bt = min(bm, m), min(bn, n), min(bt, t)
    assert m % bm == 0 and n % bn == 0 and t % bt == 0, (a.shape, b.shape, bm, bn, bt)

    def body(a_ref, b_ref, o_ref):
        k = pl.program_id(2)

        @pl.when(k == 0)
        def _():
            o_ref[...] = jnp.zeros_like(o_ref)

        o_ref[...] += _mm_tn(a_ref[...], b_ref[...])

    if chip_major:
        out_spec, out_shape = pl.BlockSpec((None, bm, bn), lambda i, j, k: (j, i, 0)), SDS((n // bn, m, bn), F32)
    else:
        out_spec, out_shape = pl.BlockSpec((bm, bn), lambda i, j, k: (i, j)), SDS((m, n), F32)
    outs, xouts = _call(
        body, name, (m // bm, n // bn, t // bt),
        [pl.BlockSpec((bt, bm), lambda i, j, k: (k, i)), pl.BlockSpec((bt, bn), lambda i, j, k: (k, j))],
        [out_spec], [out_shape], [], [a, b], exchanges)
    return outs[0], xouts


def wgrad_pool(z, dzl, bt=4096):
    t = z.shape[0]
    bt = min(bt, t)
    assert t % bt == 0

    def body(a_ref, b_ref, o_ref):
        k = pl.program_id(1)

        @pl.when(k == 0)
        def _():
            o_ref[...] = jnp.zeros_like(o_ref)

        o_ref[0] += _mm_tn(a_ref[...], b_ref[...])

    return _call(
        body, "wgrad_pool", (NG, t // bt),
        [pl.BlockSpec((bt, GW), lambda g, k: (k, g)), pl.BlockSpec((bt, GW), lambda g, k: (k, g))],
        [pl.BlockSpec((1, GW, GW), lambda g, k: (g, 0, 0))], [SDS((NG, GW, GW), F32)], [], [z, dzl])[0][0]


def cast_shards(name, names, shards, exchanges=()):
    n = len(shards)

    def body(*refs):
        srcs, dsts, bufs, sems = refs[:n], refs[n:2 * n], refs[2 * n:3 * n], refs[3 * n]
        me = 2 * lax.axis_index("x") + lax.axis_index("y")
        copies = []
        for w, mat in enumerate(names):
            bufs[w][...] = srcs[w][...].astype(BF16)
            for h in range(2):
                cp = pltpu.make_async_copy(bufs[w].at[_shard_half(mat, h)], dsts[w].at[_window(mat, me, h)], sems.at[w, h])
                cp.start()
                copies.append(cp)
        for cp in copies:
            cp.wait()

    return _call(body, name, (), [VMEM_SPEC] * n, [ANY] * n, [SDS(GEOM[mat][0], BF16) for mat in names],
                 [pltpu.VMEM(s.shape, BF16) for s in shards] + [pltpu.SemaphoreType.DMA((n, 2))], list(shards), exchanges)


VEC_ROWS = 24
ROW = dict(mlp_pre_g=0, mlp_post_g=1, loss=2, mix_post_g=8, conv_ln_g=9, conv_ln_b=10, dw_bias=11, pool_scale=12,
           mix_pre_g=16)


NDEV = 8


class GatherSmall:
    has_mid = False
    sibling_only = False
    aliases = ()

    def __init__(self, part):
        self.ins = [part]
        self.out_shapes = [SDS((NDEV, *part.shape), F32)]
        self.sems = [pltpu.SemaphoreType.DMA((NDEV,)), pltpu.SemaphoreType.DMA((NDEV,))]

    def _copies(self, ins, outs, sems):
        send_sems, recv_sems = sems
        x, y, c = lax.axis_index("x"), lax.axis_index("y"), lax.axis_index("c")
        me = 4 * x + 2 * y + c
        own = pltpu.make_async_copy(ins[0], outs[0].at[me], send_sems.at[0])
        sends = [_remote(ins[0], outs[0].at[me], send_sems.at[m], recv_sems.at[m],
                         (x ^ (m >> 2), y ^ ((m >> 1) & 1), c ^ (m & 1))) for m in range(1, NDEV)]
        recvs = [_remote(outs[0].at[me ^ m], outs[0].at[me ^ m], send_sems.at[m], recv_sems.at[m], (x, y, c))
                 for m in range(1, NDEV)]
        return own, sends, recvs

    def start(self, ins, outs, sems):
        own, sends, _ = self._copies(ins, outs, sems)
        for cp in sends + [own]:
            cp.start()

    def finish(self, ins, outs, sems):
        own, sends, recvs = self._copies(ins, outs, sems)
        for cp in recvs:
            cp.wait_recv()
        for cp in sends:
            cp.wait_send()
        own.wait()


def sum_small(parts, after=()):
    def body(parts_ref, sum_ref, loss_ref):
        total = parts_ref[0]
        for d in range(1, NDEV):
            total = total + parts_ref[d]
        sum_ref[...] = total
        r = ROW["loss"]
        loss_ref[...] = jnp.zeros_like(loss_ref) + jnp.sum(total[r:r + 1, :])

    return _call(body, "sum_small", (), [VMEM_SPEC], [VMEM_SPEC, VMEM_SPEC],
                 [SDS(parts.shape[1:], F32), SDS((8, 128), F32)], [], [parts], after=after)[0]


def pair_add(tag, names, grads, from_sibling, exchanges=()):
    n = len(names)
    in_specs, wire_specs, own_specs, wire_shapes, own_shapes = [], [], [], [], []
    for name in names:
        _, gblk, idx = GEOM[name]
        blk = _half_shape(name)
        zeros = (0,) * len(blk)
        in_specs.append(pl.BlockSpec(gblk, lambda k, idx=idx: idx(k, lax.axis_index("c"))))
        wire_specs.append(pl.BlockSpec((1, *blk), lambda k, zeros=zeros: (k, *zeros)))
        own_specs.append(pl.BlockSpec(blk, lambda k, zeros=zeros: zeros))
        wire_shapes.append(SDS((NCHIP, *blk), BF16))
        own_shapes.append(SDS(blk, F32))

    def body(*refs):
        g, s, wire, own = refs[:n], refs[n:2 * n], refs[2 * n:3 * n], refs[3 * n:]
        mine = pl.program_id(0) == 2 * lax.axis_index("x") + lax.axis_index("y")
        for w in range(n):
            total = g[w][...] + s[w][0]
            wire[w][0] = total.astype(BF16)

            @pl.when(mine)
            def _(w=w, total=total):
                own[w][...] = total

    outs, xouts = _call(body, "pair_add_" + tag, (NCHIP,), in_specs + wire_specs, wire_specs + own_specs,
                        wire_shapes + own_shapes, [], list(grads) + list(from_sibling), exchanges)
    return outs[:n], outs[n:], xouts


SUM_STEPS = 4


def sum_partials(tag, names, owns, from_chips, exchanges=()):
    n = len(names)
    own_specs, part_specs, out_specs, out_shapes, part_args, counts = [], [], [], [], [], []
    for name, parts in zip(names, from_chips):
        half = _half_shape(name)
        blk = half[:-2] + (half[-2] // SUM_STEPS, half[-1])
        lead = (0,) * (len(half) - 2)
        own_specs.append(pl.BlockSpec(blk, lambda i, lead=lead: (*lead, i, 0)))
        for p in parts:
            part_specs.append(pl.BlockSpec((p.shape[0], *blk), lambda i, lead=lead: (0, *lead, i, 0)))
            part_args.append(p)
        counts.append(len(parts))
        out_specs.append(pl.BlockSpec((2, *blk), lambda i, lead=lead: (0, *lead, i, 0)))
        out_shapes.append(SDS((2, *half), F32))

    def body(*refs):
        own, parts, out = refs[:n], list(refs[n:n + len(part_args)]), refs[n + len(part_args):]
        c = lax.axis_index("c")
        for w in range(n):
            total = own[w][...]
            for p in [parts.pop(0) for _ in range(counts[w])]:
                for j in range(p.shape[0]):
                    total = total + p[j].astype(F32)
            out[w][c] = total

    return _call(body, "sum_partials_" + tag, (SUM_STEPS,), own_specs + part_specs, out_specs, out_shapes, [],
                 list(owns) + part_args, exchanges)


class SwapHalves:
    has_mid = False
    sibling_only = True

    def __init__(self, halves):
        self.ins = list(halves)
        self.out_shapes = [SDS(h.shape, h.dtype) for h in halves]
        self.aliases = [(i, i) for i in range(len(halves))]
        self.sems = [pltpu.SemaphoreType.DMA((len(halves),)), pltpu.SemaphoreType.DMA((len(halves),))]

    def start(self, ins, outs, sems):
        send_sems, recv_sems = sems
        x, y, c = lax.axis_index("x"), lax.axis_index("y"), lax.axis_index("c")
        for w in range(len(self.ins)):
            _remote(ins[w].at[c], outs[w].at[c], send_sems.at[w], recv_sems.at[w], (x, y, 1 - c)).start()

    def finish(self, ins, outs, sems):
        send_sems, recv_sems = sems
        x, y, c = lax.axis_index("x"), lax.axis_index("y"), lax.axis_index("c")
        for w in range(len(self.ins)):
            _remote(ins[w].at[c], outs[w].at[c], send_sems.at[w], recv_sems.at[w], (x, y, 1 - c)).wait_send()
            _remote(ins[w].at[1 - c], outs[w].at[1 - c], send_sems.at[w], recv_sems.at[w], (x, y, 1 - c)).wait_recv()


SEM_SPEC = pl.BlockSpec(memory_space=pltpu.SEMAPHORE)
HBM_SPEC = pl.BlockSpec(memory_space=pltpu.HBM)
DATAFLOW = pltpu.SideEffectType.DATAFLOW_SIDE_EFFECTING


def plan_chips(names):
    def plan(refs):
        n = len(names)
        x, y, c, me, chips, chip_ids = _place()
        return [(refs[w].at[chip_ids[j]], refs[n + w].at[j], (*chip, c)) for w in range(n) for j, chip in enumerate(chips)]
    return plan


def split_start(tag, arrays, plan, ncopies):
    n = len(arrays)

    def body(*refs):
        sems, token = refs[n:n + 2 * ncopies], refs[-1]
        for s, (src, dst, to) in enumerate(plan(refs[:n])):
            _remote(src, dst, sems[2 * s], sems[2 * s + 1], to).start()
        token[...] = jnp.zeros_like(token)

    res = pl.pallas_call(
        body, name="start_" + tag, in_specs=[HBM_SPEC] * n,
        out_specs=[SEM_SPEC] * (2 * ncopies) + [HBM_SPEC] * n + [VMEM_SPEC],
        out_shape=[pltpu.SemaphoreType.DMA(())] * (2 * ncopies) + [pltpu.HBM(a.shape, a.dtype) for a in arrays]
        + [SDS((8, 128), F32)],
        input_output_aliases={i: 2 * ncopies + i for i in range(n)},
        compiler_params=pltpu.CompilerParams(has_side_effects=DATAFLOW),
    )(*arrays)
    return (res[:2 * ncopies], res[2 * ncopies:-1]), res[-1]


def split_wait(tag, started, plan, after):
    sems, arrays = started
    n = len(arrays)

    def body(*refs):
        sem = refs[n:n + len(sems)]
        for s, (src, dst, to) in enumerate(plan(refs[:n])):
            cp = _remote(src, dst, sem[2 * s], sem[2 * s + 1], to)
            cp.wait_send()
            cp.wait_recv()

    return pl.pallas_call(
        body, name="wait_" + tag, in_specs=[HBM_SPEC] * n + [SEM_SPEC] * len(sems) + [ANY] * len(after),
        out_specs=[HBM_SPEC] * n, out_shape=[pltpu.HBM(a.shape, a.dtype) for a in arrays],
        input_output_aliases={i: i for i in range(n)},
        compiler_params=pltpu.CompilerParams(has_side_effects=DATAFLOW),
    )(*arrays, *sems, *after)


def adamw(tag, ws, gs, ms, vs, steps, after=()):
    n = len(ws)
    specs = [pl.BlockSpec((a.shape[0] // steps, a.shape[1]), lambda i: (i, 0)) for a in ws]
    assert all(a.shape[0] % (steps * SUBLANES) == 0 for a in ws), [a.shape for a in ws]

    def body(*refs):
        w_, g_, m_, v_ = refs[:n], refs[n:2 * n], refs[2 * n:3 * n], refs[3 * n:4 * n]
        d_, nm_, nv_, gout_ = refs[4 * n:5 * n], refs[5 * n:6 * n], refs[6 * n:7 * n], refs[7 * n:]
        for i in range(n):
            gv = g_[i][...]
            gout_[i][...] = gv
            mn = B1 * m_[i][...] + (1.0 - B1) * gv
            vn = B2 * v_[i][...] + (1.0 - B2) * (gv * gv)
            m_hat = mn / (1.0 - B1 ** STEP)
            v_hat = vn / (1.0 - B2 ** STEP)
            d_[i][...] = -LR * (m_hat / (jnp.sqrt(v_hat) + ADAM_EPS) + WD * w_[i][...])
            nm_[i][...] = mn
            nv_[i][...] = vn

    outs, _ = _call(body, "adamw_" + tag, (steps,), specs * 4, specs * 4, [SDS(a.shape, F32) for a in ws] * 4, [],
                    list(ws) + list(gs) + list(ms) + list(vs), after=after)
    return list(zip(outs[:n], outs[n:2 * n], outs[2 * n:3 * n], outs[3 * n:]))


VECS = ("mix_pre_g", "dw_bias", "conv_ln_g", "conv_ln_b", "pool_scale", "mix_post_g", "mlp_pre_g", "mlp_post_g")
WEIGHTS = ("mix_pre_g", "w_in", "dw_kernel", "dw_bias", "conv_ln_g", "conv_ln_b", "w_conv_out", "pool_w", "pool_scale",
           "w_pool_out", "w_o", "mix_post_g", "mlp_pre_g", "w_ff1", "w_ff2", "mlp_post_g")
MIX_MATS = ("w_conv_out", "pool_w", "w_pool_out", "w_o")
FF_MATS = ("w_ff1", "w_ff2")


def kernel(x, mix_pre_g, w_in, dw_kernel, dw_bias, conv_ln_g, conv_ln_b, w_conv_out, pool_w, pool_scale, w_pool_out, w_o, mix_post_g, mlp_pre_g, w_ff1, w_ff2, mlp_post_g, loss_target, m_mix_pre_g, m_w_in, m_dw_kernel, m_dw_bias, m_conv_ln_g, m_conv_ln_b, m_w_conv_out, m_pool_w, m_pool_scale, m_w_pool_out, m_w_o, m_mix_post_g, m_mlp_pre_g, m_w_ff1, m_w_ff2, m_mlp_post_g, v_mix_pre_g, v_w_in, v_dw_kernel, v_dw_bias, v_conv_ln_g, v_conv_ln_b, v_w_conv_out, v_pool_w, v_pool_scale, v_w_pool_out, v_w_o, v_mix_post_g, v_mlp_pre_g, v_w_ff1, v_w_ff2, v_mlp_post_g):
    w = dict(mix_pre_g=mix_pre_g, w_in=w_in, dw_kernel=dw_kernel, dw_bias=dw_bias, conv_ln_g=conv_ln_g,
             conv_ln_b=conv_ln_b, w_conv_out=w_conv_out, pool_w=pool_w, pool_scale=pool_scale, w_pool_out=w_pool_out,
             w_o=w_o, mix_post_g=mix_post_g, mlp_pre_g=mlp_pre_g, w_ff1=w_ff1, w_ff2=w_ff2, mlp_post_g=mlp_post_g)
    m = dict(mix_pre_g=m_mix_pre_g, w_in=m_w_in, dw_kernel=m_dw_kernel, dw_bias=m_dw_bias, conv_ln_g=m_conv_ln_g,
             conv_ln_b=m_conv_ln_b, w_conv_out=m_w_conv_out, pool_w=m_pool_w, pool_scale=m_pool_scale,
             w_pool_out=m_w_pool_out, w_o=m_w_o, mix_post_g=m_mix_post_g, mlp_pre_g=m_mlp_pre_g, w_ff1=m_w_ff1,
             w_ff2=m_w_ff2, mlp_post_g=m_mlp_post_g)
    v = dict(mix_pre_g=v_mix_pre_g, w_in=v_w_in, dw_kernel=v_dw_kernel, dw_bias=v_dw_bias, conv_ln_g=v_conv_ln_g,
             conv_ln_b=v_conv_ln_b, w_conv_out=v_w_conv_out, pool_w=v_pool_w, pool_scale=v_pool_scale,
             w_pool_out=v_w_pool_out, w_o=v_w_o, mix_post_g=v_mix_post_g, mlp_pre_g=v_mlp_pre_g, w_ff1=v_w_ff1,
             w_ff2=v_w_ff2, mlp_post_g=v_mlp_post_g)
    chip = 2 * lax.axis_index("x") + lax.axis_index("y")
    xs, tgt = x[0], loss_target[0]
    vecs = {name: w[name].reshape(1, D) for name in VECS}

    taps = lax.dynamic_update_slice(jnp.zeros((KW_PAD, D), F32), dw_kernel, (0, chip * DSH))
    mine, full = {}, {}
    (mine["w_in"],), _ = cast_shards("cast_w_in", ("w_in",), [w["w_in"]])
    rest = MATS[1:]
    cast, ((full["w_in"], dwk),) = cast_shards(
        "cast_rest", rest, [w[name] for name in rest], [GatherWeights(("w_in",), [mine["w_in"]], taps)])
    mine.update(zip(rest, cast))
    (u, glu, ag, p, gt), (got,) = fwd_in(
        xs, vecs["mix_pre_g"], full["w_in"], TM_IN, [GatherWeights(MIX_MATS, [mine[n] for n in MIX_MATS])])
    full.update(zip(MIX_MATS, got))
    (cv, sw, z, zl, zs, yc, yp, mg, mo, h1), (got,) = fwd_mix(
        xs, glu, p, gt, dwk, vecs["dw_bias"], vecs["conv_ln_g"], vecs["conv_ln_b"], full["w_conv_out"], full["pool_w"],
        vecs["pool_scale"], full["w_pool_out"], full["w_o"], vecs["mix_post_g"], TM,
        [GatherWeights(FF_MATS, [mine[n] for n in FF_MATS])])
    full.update(zip(FF_MATS, got))

    (v_, a2, df2, df1, dh1, vec_mlp), _ = mlp_fwd_bwd(
        h1, tgt, vecs["mlp_pre_g"], vecs["mlp_post_g"], full["w_ff1"], full["w_ff2"], TM)
    grads, g, delta, new_m, new_v = {}, {}, {}, {}, {}
    rest_mats = FF_MATS + MIX_MATS
    grads["w_ff1"], _ = wgrad(v_, df1, "wgrad_ff1", bn=CFF, chip_major=True)
    grads["w_ff2"], _ = wgrad(a2, df2, "wgrad_ff2")
    landing = lambda names, slots, dt: [lax.empty((slots, *_half_shape(n)), dt) for n in names]
    grads_ff = [grads[n] for n in FF_MATS]
    (dmo, dgt, dyc, dyp, dcv, dzl, dz, vec_mix), (from_sibling,) = bwd_mix(
        dh1, mo, cv, zl, yc, yp, gt, vecs["conv_ln_g"], vecs["conv_ln_b"], vecs["pool_scale"], vecs["mix_post_g"],
        full["w_conv_out"], full["pool_w"], full["w_pool_out"], full["w_o"], TM,
        exchanges=[ExchangePair(FF_MATS, grads_ff)])
    grads["w_conv_out"], _ = wgrad(sw, dyc, "wgrad_conv_out", bt=1024)
    grads["pool_w"] = wgrad_pool(z, dzl)
    grads["w_pool_out"], _ = wgrad(zs, dyp, "wgrad_pool_out", bt=1024)
    grads["w_o"], _ = wgrad(mg, dmo, "wgrad_o", bt=1024)
    wire_ff, own_ff, (from_sibling,) = pair_add(
        "ff", FF_MATS, grads_ff, from_sibling, exchanges=[ExchangePair(MIX_MATS, [grads[n] for n in MIX_MATS])])
    wire_mix, own_mix, _ = pair_add("mix", MIX_MATS, [grads[n] for n in MIX_MATS], from_sibling)
    chips_rest, token = split_start("chips_rest", list(wire_ff) + list(wire_mix) + landing(rest_mats, NCHIP - 1, BF16),
                                    plan_chips(rest_mats), len(rest_mats) * (NCHIP - 1))
    dx, dproj, vec_in, dk = bwd_in(
        xs, dh1, dcv, dz, glu, ag, dgt, vecs["mix_pre_g"], dwk, full["w_in"], TM, after=[token])
    small_part = jnp.concatenate([vec_mlp, vec_mix, vec_in, dk], axis=0)
    grads["w_in"], ((small_parts,),) = wgrad(
        u, dproj, "wgrad_in", bn=CIN, chip_major=True, exchanges=[GatherSmall(small_part)])
    got = split_wait("chips_rest", chips_rest, plan_chips(rest_mats), after=[grads["w_in"]])[len(rest_mats):]
    chips_ff, chips_mix = got[:len(FF_MATS)], got[len(FF_MATS):]
    halves_rest, (from_sibling,) = sum_partials(
        "rest", rest_mats, list(own_ff) + list(own_mix), [[a] for a in list(chips_ff) + list(chips_mix)],
        [ExchangePair(("w_in",), [grads["w_in"]])])
    wire_in, own_in, (reduced_rest,) = pair_add(
        "in", ("w_in",), [grads["w_in"]], from_sibling, exchanges=[SwapHalves(halves_rest)])
    for n, red in zip(rest_mats, reduced_rest):
        g[n] = red.reshape(w[n].shape)

    def update(tag, names, steps, after=()):
        two_d = lambda a: a.reshape(-1, a.shape[-1])
        res = adamw(tag, [two_d(w[n]) for n in names], [two_d(g[n]) for n in names],
                    [two_d(m[n]) for n in names], [two_d(v[n]) for n in names], steps, after)
        for n, outs in zip(names, res):
            delta[n], new_m[n], new_v[n], g[n] = [a.reshape(w[n].shape) for a in outs]

    chips_in, token = split_start("chips_in", list(wire_in) + landing(("w_in",), NCHIP - 1, BF16),
                                  plan_chips(("w_in",)), NCHIP - 1)
    small, loss8 = sum_small(small_parts, after=[token])
    loss = loss8[0, 0]
    for name in VECS:
        g[name] = small[ROW[name]]
    g["dw_kernel"] = lax.dynamic_slice(small[VEC_ROWS:VEC_ROWS + KW_PAD], (0, chip * DSH), (KW_PAD, DSH))
    update("rest", rest_mats, 8, after=[token])
    stack = lambda d: jnp.concatenate([d[name].reshape(1, D) for name in VECS], axis=0)
    (res,) = adamw("vectors", [stack(w)], [stack(g)], [stack(m)], [stack(v)], 1, after=[token])
    for i, name in enumerate(VECS):
        delta[name], new_m[name], new_v[name] = [r[i] for r in res[:3]]
    padk = lambda a: jnp.pad(a, ((0, KW_PAD - KW), (0, 0)))
    (res,) = adamw("dw_kernel", [padk(w["dw_kernel"])], [g["dw_kernel"]], [padk(m["dw_kernel"])],
                   [padk(v["dw_kernel"])], 1, after=[token])
    delta["dw_kernel"], new_m["dw_kernel"], new_v["dw_kernel"] = [r[:KW] for r in res[:3]]
    g["dw_kernel"] = g["dw_kernel"][:KW]
    chips_in = split_wait("chips_in", chips_in, plan_chips(("w_in",)),
                          after=[delta[rest_mats[-1]], delta["dw_kernel"], delta[VECS[0]]])[1:]
    halves_in, _ = sum_partials("in", ("w_in",), own_in, [chips_in])
    (reduced_in,) = exchange("swap_w_in", SwapHalves(halves_in))
    g["w_in"] = reduced_in.reshape(w["w_in"].shape)
    update("in", ("w_in",), 8)

    return (loss, dx[None], *[g[n] for n in WEIGHTS], *[delta[n] for n in WEIGHTS], *[new_m[n] for n in WEIGHTS],
            *[new_v[n] for n in WEIGHTS])
```

```python
import math

import jax
import jax.numpy as jnp
from jax import lax
from jax.experimental import pallas as pl
from jax.experimental.pallas import tpu as pltpu

F32 = jnp.float32
BF16 = jnp.bfloat16

D = 1024
FF = 4096
NPROJ = 5
KW = 31
KW_PAD = 32
SUBLANES = 8
LANES = 128
HALO = 32
POOL_WINDOWS = (2, 4, 8, 16)
NG = 4
GW = D // NG
RMS_EPS = 1e-6
LN_EPS = 1e-5
LR, B1, B2, ADAM_EPS, WD, STEP = 0.001, 0.9, 0.999, 1e-08, 0.01, 10
NCHIP = 4
VMEM_LIMIT = 60 * 1024 * 1024
MESH = pl.DeviceIdType.MESH
SIBLING_COLLECTIVE_ID = 0
GATHER_COLLECTIVE_ID = 1
CHIPS_COLLECTIVE_ID = 2
TM = 256
TM_IN = 512

ANY = pl.BlockSpec(memory_space=pl.ANY)
VMEM_SPEC = pl.BlockSpec(memory_space=pltpu.VMEM)
SDS = jax.ShapeDtypeStruct


def _cp(**kw):
    return pltpu.CompilerParams(vmem_limit_bytes=VMEM_LIMIT, **kw)


def _mm(a, b):
    return jnp.dot(a, b, preferred_element_type=F32)


def _mm_nt(a, b):
    return lax.dot_general(a, b, (((1,), (1,)), ((), ())), preferred_element_type=F32)


def _mm_tn(a, b):
    return lax.dot_general(a, b, (((0,), (0,)), ((), ())), preferred_element_type=F32)


def _sigmoid(x):
    return 1.0 / (1.0 + jnp.exp(-x))


def _rowsum(x):
    return jnp.sum(x, axis=0, keepdims=True)


def _full(shape):
    return pl.BlockSpec(shape, lambda i: (0,) * len(shape))


def _tile(tm, cols):
    return pl.BlockSpec((tm, cols), lambda i: (i, 0))


def _prev_halo(tm):
    return pl.BlockSpec((HALO, D), lambda i: (jnp.maximum(i * (tm // HALO) - 1, 0), 0))


def _next_halo(tm, nt):
    return pl.BlockSpec((HALO, D), lambda i: (jnp.minimum((i + 1) * (tm // HALO), nt * (tm // HALO) - 1), 0))


MATS = ("w_in", "w_conv_out", "pool_w", "w_pool_out", "w_o", "w_ff1", "w_ff2")
CIN = NPROJ * D // NCHIP
CFF = FF // NCHIP
_ROWS = lambda k, h: (2 * k + h, 0)
_CHIP_MAJOR = lambda k, h: (k, h, 0)
GEOM = dict(
    w_in=((NCHIP, D, CIN), (None, D // 2, CIN), _CHIP_MAJOR),
    w_conv_out=((D, D), (D // (2 * NCHIP), D), _ROWS),
    pool_w=((NG, GW, GW), (NG // 2, GW // NCHIP, GW), lambda k, h: (h, k, 0)),
    w_pool_out=((D, D), (D // (2 * NCHIP), D), _ROWS),
    w_o=((D, D), (D // (2 * NCHIP), D), _ROWS),
    w_ff1=((NCHIP, D, CFF), (None, D // 2, CFF), _CHIP_MAJOR),
    w_ff2=((FF, D), (FF // (2 * NCHIP), D), _ROWS),
)
DSH = D // NCHIP


def _half_shape(name):
    return tuple(b for b in GEOM[name][1] if b is not None)


def _window(name, k, h):
    _, blk, idx = GEOM[name]
    return tuple(i if b is None else pl.ds(i * b, b) for i, b in zip(idx(k, h), blk))


def _shard_half(name, h):
    n0 = _half_shape(name)[0]
    return (pl.ds(h * n0, n0),) + (slice(None),) * (len(_half_shape(name)) - 1)


def _place():
    x, y, c = lax.axis_index("x"), lax.axis_index("y"), lax.axis_index("c")
    chips = [(1 - x, y), (x, 1 - y), (1 - x, 1 - y)]
    return x, y, c, 2 * x + y, chips, [2 * px + py for px, py in chips]


def _remote(src, dst, send_sem, recv_sem, to):
    return pltpu.make_async_remote_copy(src_ref=src, dst_ref=dst, send_sem=send_sem, recv_sem=recv_sem,
                                        device_id=to, device_id_type=MESH)


class GatherWeights:
    has_mid = True
    sibling_only = False

    def __init__(self, names, fulls, taps=None):
        self.names = names
        self.ins = list(fulls) + ([taps] if taps is not None else [])
        self.has_taps = taps is not None
        self.out_shapes = [SDS(a.shape, a.dtype) for a in self.ins]
        self.aliases = [(i, i) for i in range(len(self.ins))]
        n = len(self.ins)
        self.sems = [pltpu.SemaphoreType.DMA((n, 6)), pltpu.SemaphoreType.DMA((n, 6))]

    def _copies(self, ins, outs, sems):
        send_sems, recv_sems = sems
        x, y, c, me, chips, chip_ids = _place()
        sibling = (x, y, 1 - c)
        ici, ici_recv, d2d, d2d_recv = [], [], [], []
        for w, name in enumerate(self.names):
            for j, chip in enumerate(chips):
                ici.append(_remote(ins[w].at[_window(name, me, c)], outs[w].at[_window(name, me, c)],
                                   send_sems.at[w, j], recv_sems.at[w, j], (*chip, c)))
                got = outs[w].at[_window(name, chip_ids[j], c)]
                ici_recv.append(_remote(got, got, send_sems.at[w, j], recv_sems.at[w, j], sibling))
                d2d.append(_remote(got, got, send_sems.at[w, 3 + j], recv_sems.at[w, 3 + j], sibling))
                got = outs[w].at[_window(name, chip_ids[j], 1 - c)]
                d2d_recv.append(_remote(got, got, send_sems.at[w, 3 + j], recv_sems.at[w, 3 + j], sibling))
        if self.has_taps:
            w = len(self.names)
            for j, chip in enumerate(chips):
                ici.append(_remote(ins[w].at[:, pl.ds(me * DSH, DSH)], outs[w].at[:, pl.ds(me * DSH, DSH)],
                                   send_sems.at[w, j], recv_sems.at[w, j], (*chip, c)))
                got = outs[w].at[:, pl.ds(chip_ids[j] * DSH, DSH)]
                d2d_recv.append(_remote(got, got, send_sems.at[w, j], recv_sems.at[w, j], sibling))
        return ici, ici_recv, d2d, d2d_recv

    def start(self, ins, outs, sems):
        for cp in self._copies(ins, outs, sems)[0]:
            cp.start()

    def mid(self, ins, outs, sems):
        _, ici_recv, d2d, _ = self._copies(ins, outs, sems)
        for got, fwd in zip(ici_recv, d2d):
            got.wait_recv()
            fwd.start()

    def finish(self, ins, outs, sems):
        ici, _, d2d, d2d_recv = self._copies(ins, outs, sems)
        for cp in d2d_recv:
            cp.wait_recv()
        for cp in ici + d2d:
            cp.wait_send()


class ExchangePair:
    has_mid = False
    sibling_only = True
    aliases = ()

    def __init__(self, names, grads):
        self.names, self.ins = names, list(grads)
        self.out_shapes = [SDS((NCHIP, *_half_shape(n)), F32) for n in names]
        self.sems = [pltpu.SemaphoreType.DMA((len(names),)), pltpu.SemaphoreType.DMA((len(names),))]

    def start(self, ins, outs, sems):
        send_sems, recv_sems = sems
        x, y, c, me, chips, chip_ids = _place()
        for w, name in enumerate(self.names):
            for k in range(NCHIP):
                _remote(ins[w].at[_window(name, k, 1 - c)], outs[w].at[k], send_sems.at[w], recv_sems.at[w],
                        (x, y, 1 - c)).start()

    def finish(self, ins, outs, sems):
        send_sems, recv_sems = sems
        x, y, c, me, chips, chip_ids = _place()
        for w in range(len(self.names)):
            _remote(outs[w], outs[w], send_sems.at[w], recv_sems.at[w], (x, y, 1 - c)).wait()


class ExchangeChips:
    has_mid = False
    sibling_only = False
    aliases = ()

    def __init__(self, names, wires):
        self.names, self.ins = names, list(wires)
        self.out_shapes = [SDS((NCHIP - 1, *_half_shape(n)), BF16) for n in names]
        self.sems = [pltpu.SemaphoreType.DMA((len(names), NCHIP - 1)), pltpu.SemaphoreType.DMA((len(names), NCHIP - 1))]

    def _copies(self, ins, outs, sems):
        send_sems, recv_sems = sems
        x, y, c, me, chips, chip_ids = _place()
        return [_remote(ins[w].at[chip_ids[j]], outs[w].at[j], send_sems.at[w, j], recv_sems.at[w, j], (*chip, c))
                for w in range(len(self.names)) for j, chip in enumerate(chips)]

    def start(self, ins, outs, sems):
        for cp in self._copies(ins, outs, sems):
            cp.start()

    def finish(self, ins, outs, sems):
        for cp in self._copies(ins, outs, sems):
            cp.wait()


def _call(body, name, grid, in_specs, out_specs, out_shape, scratch, args, exchanges=(), after=()):
    n_in, n_out, n_scr = len(in_specs), len(out_specs), len(scratch)
    x_in = [a for e in exchanges for a in e.ins]
    x_out = [s for e in exchanges for s in e.out_shapes]
    x_sem = [s for e in exchanges for s in e.sems]
    nsteps = math.prod(grid)
    sibling_only = bool(exchanges) and all(e.sibling_only for e in exchanges)
    gather_only = bool(exchanges) and all(isinstance(e, GatherWeights) for e in exchanges)

    def wrapped(*refs):
        ins, rest = refs[:n_in], refs[n_in:]
        xin, rest = rest[:len(x_in)], rest[len(x_in) + len(after):]
        outs, rest = rest[:n_out], rest[n_out:]
        xout, rest = rest[:len(x_out)], rest[len(x_out):]
        scr, xsem = rest[:n_scr], rest[n_scr:]
        parts = []
        for e in exchanges:
            parts.append((xin[:len(e.ins)], xout[:len(e.out_shapes)], xsem[:len(e.sems)]))
            xin, xout, xsem = xin[len(e.ins):], xout[len(e.out_shapes):], xsem[len(e.sems):]
        def start_all():
            if sibling_only or gather_only:
                barrier = pltpu.get_barrier_semaphore()
                x, y, c = lax.axis_index("x"), lax.axis_index("y"), lax.axis_index("c")
                peers = [(x, y, 1 - c)] + ([(1 - x, y, c), (x, 1 - y, c), (1 - x, 1 - y, c)] if gather_only else [])
                for peer in peers:
                    pl.semaphore_signal(barrier, inc=1, device_id=peer, device_id_type=MESH)
                pl.semaphore_wait(barrier, len(peers))
            for e, p in zip(exchanges, parts):
                e.start(*p)

        if not grid:
            start_all()
            body(*ins, *outs, *scr)
            for e, p in zip(exchanges, parts):
                if e.has_mid:
                    e.mid(*p)
            for e, p in zip(exchanges, parts):
                e.finish(*p)
            return
        step = 0
        for axis, extent in enumerate(grid):
            step = step * extent + pl.program_id(axis)
        if exchanges:
            pl.when(step == 0)(start_all)

        body(*ins, *outs, *scr)
        if any(e.has_mid for e in exchanges):
            @pl.when(step == max(nsteps - 2, 0))
            def _():
                for e, p in zip(exchanges, parts):
                    if e.has_mid:
                        e.mid(*p)

        if exchanges:
            @pl.when(step == nsteps - 1)
            def _():
                for e, p in zip(exchanges, parts):
                    e.finish(*p)

    barrier = (dict(collective_id=SIBLING_COLLECTIVE_ID) if sibling_only else
               dict(collective_id=GATHER_COLLECTIVE_ID) if gather_only else {})
    kw = dict(grid=grid, compiler_params=_cp(dimension_semantics=("arbitrary",) * len(grid), **barrier)) if grid else dict(
        compiler_params=_cp(**barrier))
    aliases, i0, o0 = {}, n_in, n_out
    for e in exchanges:
        aliases.update({i0 + i: o0 + o for i, o in e.aliases})
        i0, o0 = i0 + len(e.ins), o0 + len(e.out_shapes)
    res = pl.pallas_call(
        wrapped, name=name, in_specs=list(in_specs) + [ANY] * (len(x_in) + len(after)),
        out_specs=list(out_specs) + [ANY] * len(x_out),
        out_shape=list(out_shape) + x_out, scratch_shapes=list(scratch) + x_sem, input_output_aliases=aliases, **kw,
    )(*args, *x_in, *after)
    outs, rest = res[:n_out], res[n_out:]
    xouts = []
    for e in exchanges:
        xouts.append(rest[:len(e.out_shapes)])
        rest = rest[len(e.out_shapes):]
    return outs, xouts


def exchange(name, ex):
    return _call(lambda: None, name, (), [], [], [], [], [], [ex])[1][0]


def _taps_scratch(tm):
    return pltpu.VMEM((SUBLANES, tm + HALO, LANES), F32)


def _taps(src_ref, k_ref, dst_ref, sh_ref, tm, off0, reverse):
    rc, cw = 64, LANES

    def col_chunk(cc, carry):
        cols = pl.ds(pl.multiple_of(cc * cw, cw), cw)
        for q in range(SUBLANES):
            n = tm + SUBLANES * (len(range(q, KW, SUBLANES)) - 1)
            sh_ref[q, 0:n, :] = src_ref[pl.ds(off0 + q, n), cols]
        for r in range(tm // rc):
            acc = jnp.zeros((rc, cw), F32)
            for q in range(SUBLANES):
                for a, j in enumerate(range(q, KW, SUBLANES)):
                    kj = KW - 1 - j if reverse else j
                    acc = acc + k_ref[kj:kj + 1, cols] * sh_ref[q, pl.ds(r * rc + SUBLANES * a, rc), :]
            dst_ref[pl.ds(r * rc, rc), cols] = acc
        return carry

    lax.fori_loop(0, D // cw, col_chunk, 0)


def _proj_pieces(g):
    lo, hi, pieces = g * D, (g + 1) * D, []
    while lo < hi:
        k = lo // CIN
        b = min(hi - k * CIN, CIN)
        pieces.append((k, lo - k * CIN, b))
        lo = k * CIN + b
    return pieces


def fwd_in(x, g1, w_in, tm, exchanges=()):
    t = x.shape[0]

    def body(x_ref, g_ref, w_ref, u_ref, glu_ref, ag_ref, p_ref, gt_ref):
        xf = x_ref[...]
        r = lax.rsqrt(jnp.mean(xf * xf, axis=-1, keepdims=True) + RMS_EPS)
        u = (xf * r * g_ref[...]).astype(BF16)
        u_ref[...] = u
        proj = lambda g: jnp.concatenate([_mm(u, w_ref[k, :, lo:hi]) for k, lo, hi in _proj_pieces(g)], axis=1)
        a = proj(0)
        gate = proj(1)
        glu_ref[...] = a * _sigmoid(gate)
        ag_ref[:, 0:D] = a.astype(BF16)
        ag_ref[:, D:2 * D] = gate.astype(BF16)
        p_ref[...] = proj(2)
        gt_ref[:, 0:D] = proj(3).astype(BF16)
        gt_ref[:, D:2 * D] = proj(4).astype(BF16)

    return _call(
        body, "fwd_in", (t // tm,),
        [_tile(tm, D), _full((1, D)), _full((NCHIP, D, CIN))],
        [_tile(tm, D), _tile(tm, D), _tile(tm, 2 * D), _tile(tm, D), _tile(tm, 2 * D)],
        [SDS((t, D), BF16), SDS((t, D), F32), SDS((t, 2 * D), BF16), SDS((t, D), F32), SDS((t, 2 * D), BF16)],
        [], [x, g1, w_in], exchanges)


def _pool_inv_count(i, tm, w):
    pos = i * tm + lax.broadcasted_iota(jnp.int32, (tm, 1), 0) + 1
    return 1.0 / jnp.minimum(pos, w).astype(F32)


def _window_sum(src_ref, tmp_ref, cols, tm, w, causal):
    lo, hi = 0, tm + HALO
    cur, span = None, 1
    while span < w:
        new_lo, new_hi = (lo + SUBLANES, hi) if causal else (lo, hi - SUBLANES)
        far = new_lo - span if causal else new_lo + span
        n = new_hi - new_lo
        if cur is None:
            near_v, far_v = src_ref[pl.ds(new_lo, n), cols], src_ref[pl.ds(far, n), cols]
        else:
            near_v = cur[new_lo - lo:new_lo - lo + n]
            if span % SUBLANES == 0:
                far_v = cur[far - lo:far - lo + n]
            else:
                tmp_ref[pl.ds(lo, hi - lo), :] = cur
                far_v = tmp_ref[pl.ds(far, n), :]
        cur, lo, hi, span = near_v + far_v, new_lo, new_hi, 2 * span
    off = HALO if causal else 0
    return cur[off - lo:off - lo + tm]


def fwd_mix(x, glu, p, gt, dwk, dwb, lng, lnb, w_co, pool_w, ps, w_po, w_o, g2, tm, exchanges=()):
    t = x.shape[0]

    def body(x_ref, glu_ref, gluh_ref, p_ref, ph_ref, gt_ref, k_ref, b_ref, lg_ref, lb_ref, wco_ref, pw_ref,
             ps_ref, wpo_ref, wo_ref, g2_ref,
             cv_ref, sw_ref, z_ref, zl_ref, zs_ref, yc_ref, yp_ref, mg_ref, mo_ref, h1_ref, ext_ref, win_ref, sh_ref):
        i = pl.program_id(0)
        keep = (i > 0).astype(F32)
        ext_ref[0:HALO, :] = gluh_ref[...] * keep
        ext_ref[HALO:HALO + tm, :] = glu_ref[...]
        _taps(ext_ref, k_ref, cv_ref, sh_ref, tm, HALO - (KW - 1), False)
        cv = cv_ref[...] + b_ref[...]
        cv_ref[...] = cv
        mu = jnp.mean(cv, axis=-1, keepdims=True)
        cen = cv - mu
        rstd = lax.rsqrt(jnp.mean(cen * cen, axis=-1, keepdims=True) + LN_EPS)
        ln = cen * rstd * lg_ref[...] + lb_ref[...]
        sw = (ln * _sigmoid(ln)).astype(BF16)
        sw_ref[...] = sw
        yc = _mm(sw, wco_ref[...])
        yc_ref[...] = yc.astype(BF16)
        ext_ref[0:HALO, :] = ph_ref[...] * keep
        ext_ref[HALO:HALO + tm, :] = p_ref[...]
        for g, w in enumerate(POOL_WINDOWS):
            cols = pl.ds(g * GW, GW)
            acc = _window_sum(ext_ref, win_ref, cols, tm, w, True)
            zg = (acc * _pool_inv_count(i, tm, w) - p_ref[:, cols]).astype(BF16)
            z_ref[:, cols] = zg
            zl = _mm(zg, pw_ref[g])
            zl_ref[:, cols] = zl.astype(BF16)
            zs_ref[:, cols] = (zl * ps_ref[:, cols]).astype(BF16)
        yp = _mm(zs_ref[...], wpo_ref[...])
        yp_ref[...] = yp.astype(BF16)
        gc = _sigmoid(gt_ref[:, 0:D].astype(F32))
        gp = _sigmoid(gt_ref[:, D:2 * D].astype(F32))
        mg = (gc * yc + gp * yp).astype(BF16)
        mg_ref[...] = mg
        mo = _mm(mg, wo_ref[...])
        mo_ref[...] = mo
        r2 = lax.rsqrt(jnp.mean(mo * mo, axis=-1, keepdims=True) + RMS_EPS)
        h1_ref[...] = x_ref[...] + mo * r2 * g2_ref[...]

    vec = _full((1, D))
    act = lambda dt: SDS((t, D), dt)
    return _call(
        body, "fwd_mix", (t // tm,),
        [_tile(tm, D), _tile(tm, D), _prev_halo(tm), _tile(tm, D), _prev_halo(tm), _tile(tm, 2 * D),
         _full((KW_PAD, D)), vec, vec, vec, _full((D, D)), _full((NG, GW, GW)), vec, _full((D, D)), _full((D, D)), vec],
        [_tile(tm, D)] * 10,
        [act(F32), act(BF16), act(BF16), act(BF16), act(BF16), act(BF16), act(BF16), act(BF16), act(F32), act(F32)],
        [pltpu.VMEM((tm + HALO, D), F32), pltpu.VMEM((tm + HALO, GW), F32), _taps_scratch(tm)],
        [x, glu, glu, p, p, gt, dwk, dwb, lng, lnb, w_co, pool_w, ps, w_po, w_o, g2], exchanges)


def mlp_fwd_bwd(h1, tgt, g3, g4, w1, w2, tm, exchanges=()):
    t = h1.shape[0]
    fc = CFF

    def body(h1_ref, tgt_ref, g3_ref, g4_ref, w1_ref, w2_ref,
             v_ref, a2_ref, df2_ref, df1_ref, dh1_ref, vec_ref, f1_ref):
        i = pl.program_id(0)

        @pl.when(i == 0)
        def _():
            vec_ref[...] = jnp.zeros_like(vec_ref)

        h1v = h1_ref[...]
        r3 = lax.rsqrt(jnp.mean(h1v * h1v, axis=-1, keepdims=True) + RMS_EPS)
        n3 = h1v * r3
        v = (n3 * g3_ref[...]).astype(BF16)
        v_ref[...] = v
        f2 = jnp.zeros((tm, D), F32)
        for c in range(FF // fc):
            cols = pl.ds(c * fc, fc)
            f1 = jnp.maximum(_mm(v, w1_ref[c]), 0.0)
            f1_ref[:, cols] = f1
            a2 = (f1 * f1).astype(BF16)
            a2_ref[:, cols] = a2
            f2 = f2 + _mm(a2, w2_ref[cols, :])
        r4 = lax.rsqrt(jnp.mean(f2 * f2, axis=-1, keepdims=True) + RMS_EPS)
        n4 = f2 * r4
        err = h1v + n4 * g4_ref[...] - tgt_ref[...]
        vec_ref[2:3, :] += _rowsum(err * err) * (0.5 / D)
        dh2 = err * (1.0 / D)
        vec_ref[1:2, :] += _rowsum(dh2 * n4)
        dn4 = dh2 * g4_ref[...]
        df2 = (r4 * (dn4 - n4 * jnp.mean(dn4 * n4, axis=-1, keepdims=True))).astype(BF16)
        df2_ref[...] = df2
        dv = jnp.zeros((tm, D), F32)
        for c in range(FF // fc):
            cols = pl.ds(c * fc, fc)
            da2 = _mm_nt(df2, w2_ref[cols, :])
            df1 = (da2 * (2.0 * f1_ref[:, cols])).astype(BF16)
            df1_ref[:, cols] = df1
            dv = dv + _mm_nt(df1, w1_ref[c])
        vec_ref[0:1, :] += _rowsum(dv * n3)
        dn3 = dv * g3_ref[...]
        dh1_ref[...] = dh2 + r3 * (dn3 - n3 * jnp.mean(dn3 * n3, axis=-1, keepdims=True))

    vec = _full((1, D))
    return _call(
        body, "mlp_fwd_bwd", (t // tm,),
        [_tile(tm, D), _tile(tm, D), vec, vec, _full((NCHIP, D, CFF)), _full((FF, D))],
        [_tile(tm, D), _tile(tm, FF), _tile(tm, D), _tile(tm, FF), _tile(tm, D), _full((8, D))],
        [SDS((t, D), BF16), SDS((t, FF), BF16), SDS((t, D), BF16), SDS((t, FF), BF16), SDS((t, D), F32),
         SDS((8, D), F32)],
        [pltpu.VMEM((tm, FF), F32)], [h1, tgt, g3, g4, w1, w2], exchanges)


def bwd_mix(dh1, mo, cv, zl, yc, yp, gt, lng, lnb, ps, g2, w_co, pool_w, w_po, w_o, tm, exchanges=()):
    t = dh1.shape[0]

    def body(dh1_ref, mo_ref, cv_ref, zl_ref, yc_ref, yp_ref, gt_ref, lg_ref, lb_ref, ps_ref, g2_ref,
             wco_ref, pw_ref, wpo_ref, wo_ref,
             dmo_ref, dgt_ref, dyc_ref, dyp_ref, dcv_ref, dzl_ref, dz_ref, vec_ref):
        i = pl.program_id(0)

        @pl.when(i == 0)
        def _():
            vec_ref[...] = jnp.zeros_like(vec_ref)

        dh1v = dh1_ref[...]
        mo = mo_ref[...]
        r2 = lax.rsqrt(jnp.mean(mo * mo, axis=-1, keepdims=True) + RMS_EPS)
        n2 = mo * r2
        vec_ref[0:1, :] += _rowsum(dh1v * n2)
        dn2 = dh1v * g2_ref[...]
        dmo = (r2 * (dn2 - n2 * jnp.mean(dn2 * n2, axis=-1, keepdims=True))).astype(BF16)
        dmo_ref[...] = dmo
        dmg = _mm_nt(dmo, wo_ref[...])
        gc = _sigmoid(gt_ref[:, 0:D].astype(F32))
        gp = _sigmoid(gt_ref[:, D:2 * D].astype(F32))
        dgt_ref[:, 0:D] = (dmg * yc_ref[...].astype(F32) * gc * (1.0 - gc)).astype(BF16)
        dgt_ref[:, D:2 * D] = (dmg * yp_ref[...].astype(F32) * gp * (1.0 - gp)).astype(BF16)
        dyc = (dmg * gc).astype(BF16)
        dyp = (dmg * gp).astype(BF16)
        dyc_ref[...] = dyc
        dyp_ref[...] = dyp
        dsw = _mm_nt(dyc, wco_ref[...])
        cv = cv_ref[...]
        mu = jnp.mean(cv, axis=-1, keepdims=True)
        cen = cv - mu
        rstd = lax.rsqrt(jnp.mean(cen * cen, axis=-1, keepdims=True) + LN_EPS)
        y = cen * rstd
        ln = y * lg_ref[...] + lb_ref[...]
        sg = _sigmoid(ln)
        dln = dsw * (sg * (1.0 + ln * (1.0 - sg)))
        vec_ref[1:2, :] += _rowsum(dln * y)
        vec_ref[2:3, :] += _rowsum(dln)
        dy = dln * lg_ref[...]
        dcv = rstd * (dy - jnp.mean(dy, axis=-1, keepdims=True) - y * jnp.mean(dy * y, axis=-1, keepdims=True))
        dcv_ref[...] = dcv.astype(BF16)
        vec_ref[3:4, :] += _rowsum(dcv)
        dzs = _mm_nt(dyp, wpo_ref[...])
        vec_ref[4:5, :] += _rowsum(dzs * zl_ref[...].astype(F32))
        dzl = (dzs * ps_ref[...]).astype(BF16)
        dzl_ref[...] = dzl
        for g in range(NG):
            cols = pl.ds(g * GW, GW)
            dz_ref[:, cols] = _mm_nt(dzl_ref[:, cols], pw_ref[g]).astype(BF16)

    vec = _full((1, D))
    act = lambda dt: SDS((t, D), dt)
    return _call(
        body, "bwd_mix", (t // tm,),
        [_tile(tm, D)] * 6 + [_tile(tm, 2 * D), vec, vec, vec, vec, _full((D, D)), _full((NG, GW, GW)), _full((D, D)),
                              _full((D, D))],
        [_tile(tm, D), _tile(tm, 2 * D)] + [_tile(tm, D)] * 5 + [_full((8, D))],
        [act(BF16), SDS((t, 2 * D), BF16), act(BF16), act(BF16), act(BF16), act(BF16), act(BF16), SDS((8, D), F32)],
        [], [dh1, mo, cv, zl, yc, yp, gt, lng, lnb, ps, g2, w_co, pool_w, w_po, w_o], exchanges)


def bwd_in(x, dh1, dcv, dz, glu, ag, dgt, g1, dwk, w_in, tm, after=()):
    t = x.shape[0]
    nt = t // tm

    def body(x_ref, dh1_ref, dcv_ref, dcvh_ref, dz_ref, dzh_ref, glu_ref, gluh_ref, ag_ref, dgt_ref, g1_ref,
             k_ref, w_ref, dx_ref, dproj_ref, vec_ref, dk_ref, ext_ref, tmp_ref, win_ref, sh_ref, gext_ref, gsh_ref):
        i = pl.program_id(0)

        @pl.when(i == 0)
        def _():
            vec_ref[...] = jnp.zeros_like(vec_ref)
            dk_ref[...] = jnp.zeros_like(dk_ref)

        first = (i > 0).astype(F32)
        last = (i < nt - 1).astype(F32)
        gext_ref[0:HALO, :] = gluh_ref[...] * first
        gext_ref[HALO:HALO + tm, :] = glu_ref[...]
        rc, cw = 32, LANES

        def dk_chunk(cc):
            cols = pl.ds(cc * cw, cw)
            for q in range(SUBLANES):
                taps = range(q, KW, SUBLANES)
                n = tm + SUBLANES * (len(taps) - 1)
                gsh_ref[q, 0:n, :] = gext_ref[pl.ds(HALO - (KW - 1) + q, n), cols]
                accs = [jnp.zeros((SUBLANES, cw), F32) for _ in taps]
                for r in range(tm // rc):
                    dchunk = dcv_ref[pl.ds(r * rc, rc), cols].astype(F32)
                    for a in range(len(taps)):
                        prod = dchunk * gsh_ref[q, pl.ds(r * rc + SUBLANES * a, rc), :]
                        accs[a] = accs[a] + jnp.sum(prod.reshape(rc // SUBLANES, SUBLANES, cw), axis=0)
                for a, j in enumerate(taps):
                    dk_ref[j:j + 1, cols] += _rowsum(accs[a])

        ext_ref[0:tm, :] = dcv_ref[...].astype(F32)
        ext_ref[tm:tm + HALO, :] = dcvh_ref[...].astype(F32) * last
        _taps(ext_ref, k_ref, tmp_ref, sh_ref, tm, 0, True)
        dglu = tmp_ref[...]
        a = ag_ref[:, 0:D].astype(F32)
        sg = _sigmoid(ag_ref[:, D:2 * D].astype(F32))
        dproj_ref[:, 0:D] = (dglu * sg).astype(BF16)
        dproj_ref[:, D:2 * D] = (dglu * a * sg * (1.0 - sg)).astype(BF16)
        for g, w in enumerate(POOL_WINDOWS):
            cols = pl.ds(g * GW, GW)
            pos = i * tm + lax.broadcasted_iota(jnp.int32, (tm + HALO, 1), 0) + 1
            inv = 1.0 / jnp.minimum(pos, w).astype(F32)
            dzg = dz_ref[:, cols].astype(F32)
            ext_ref[0:tm, cols] = dzg * inv[0:tm]
            ext_ref[tm:tm + HALO, cols] = dzh_ref[:, cols].astype(F32) * inv[tm:tm + HALO] * last
            acc = _window_sum(ext_ref, win_ref, cols, tm, w, False)
            dproj_ref[:, pl.ds(2 * D + g * GW, GW)] = (acc - dzg).astype(BF16)
        dproj_ref[:, 3 * D:5 * D] = dgt_ref[...]
        chunks_per_matmul = D // cw // NCHIP
        for k in range(NCHIP):
            part = _mm_nt(dproj_ref[:, k * CIN:(k + 1) * CIN], w_ref[k])
            if k == 0:
                tmp_ref[...] = part
            else:
                tmp_ref[...] += part
            for cc in range(k * chunks_per_matmul, (k + 1) * chunks_per_matmul):
                dk_chunk(cc)
        du = tmp_ref[...]
        xf = x_ref[...]
        r1 = lax.rsqrt(jnp.mean(xf * xf, axis=-1, keepdims=True) + RMS_EPS)
        n1 = xf * r1
        vec_ref[0:1, :] += _rowsum(du * n1)
        dn1 = du * g1_ref[...]
        dx_ref[...] = dh1_ref[...] + r1 * (dn1 - n1 * jnp.mean(dn1 * n1, axis=-1, keepdims=True))

    return _call(
        body, "bwd_in", (nt,),
        [_tile(tm, D), _tile(tm, D), _tile(tm, D), _next_halo(tm, nt), _tile(tm, D), _next_halo(tm, nt),
         _tile(tm, D), _prev_halo(tm), _tile(tm, 2 * D), _tile(tm, 2 * D), _full((1, D)),
         _full((KW_PAD, D)), _full((NCHIP, D, CIN))],
        [_tile(tm, D), _tile(tm, NPROJ * D), _full((8, D)), _full((KW_PAD, D))],
        [SDS((t, D), F32), SDS((t, NPROJ * D), BF16), SDS((8, D), F32), SDS((KW_PAD, D), F32)],
        [pltpu.VMEM((tm + HALO, D), F32), pltpu.VMEM((tm, D), F32), pltpu.VMEM((tm + HALO, GW), F32),
         _taps_scratch(tm), pltpu.VMEM((tm + HALO, D), F32), _taps_scratch(tm)],
        [x, dh1, dcv, dcv, dz, dz, glu, glu, ag, dgt, g1, dwk, w_in], after=after)[0]


def wgrad(a, b, name, bm=1024, bn=1024, bt=2048, chip_major=False, exchanges=()):
    t, m = a.shape
    n = b.shape[1]
    bm, bn, bt = min(bm, m), min(bn, n), min(bt, t)
    assert m % bm == 0 and n % bn == 0 and t % bt == 0, (a.shape, b.shape, bm, bn, bt)

    def body(a_ref, b_ref, o_ref):
        k = pl.program_id(2)

        @pl.when(k == 0)
        def _():
            o_ref[...] = jnp.zeros_like(o_ref)

        o_ref[...] += _mm_tn(a_ref[...], b_ref[...])

    if chip_major:
        out_spec, out_shape = pl.BlockSpec((None, bm, bn), lambda i, j, k: (j, i, 0)), SDS((n // bn, m, bn), F32)
    else:
        out_spec, out_shape = pl.BlockSpec((bm, bn), lambda i, j, k: (i, j)), SDS((m, n), F32)
    outs, xouts = _call(
        body, name, (m // bm, n // bn, t // bt),
        [pl.BlockSpec((bt, bm), lambda i, j, k: (k, i)), pl.BlockSpec((bt, bn), lambda i, j, k: (k, j))],
        [out_spec], [out_shape], [], [a, b], exchanges)
    return outs[0], xouts


def wgrad_pool(z, dzl, bt=4096):
    t = z.shape[0]
    bt = min(bt, t)
    assert t % bt == 0

    def body(a_ref, b_ref, o_ref):
        k = pl.program_id(1)

        @pl.when(k == 0)
        def _():
            o_ref[...] = jnp.zeros_like(o_ref)

        o_ref[0] += _mm_tn(a_ref[...], b_ref[...])

    return _call(
        body, "wgrad_pool", (NG, t // bt),
        [pl.BlockSpec((bt, GW), lambda g, k: (k, g)), pl.BlockSpec((bt, GW), lambda g, k: (k, g))],
        [pl.BlockSpec((1, GW, GW), lambda g, k: (g, 0, 0))], [SDS((NG, GW, GW), F32)], [], [z, dzl])[0][0]


def cast_shards(name, names, shards, exchanges=()):
    n = len(shards)

    def body(*refs):
        srcs, dsts, bufs, sems = refs[:n], refs[n:2 * n], refs[2 * n:3 * n], refs[3 * n]
        me = 2 * lax.axis_index("x") + lax.axis_index("y")
        copies = []
        for w, mat in enumerate(names):
            bufs[w][...] = srcs[w][...].astype(BF16)
            for h in range(2):
                cp = pltpu.make_async_copy(bufs[w].at[_shard_half(mat, h)], dsts[w].at[_window(mat, me, h)], sems.at[w, h])
                cp.start()
                copies.append(cp)
        for cp in copies:
            cp.wait()

    return _call(body, name, (), [VMEM_SPEC] * n, [ANY] * n, [SDS(GEOM[mat][0], BF16) for mat in names],
                 [pltpu.VMEM(s.shape, BF16) for s in shards] + [pltpu.SemaphoreType.DMA((n, 2))], list(shards), exchanges)


VEC_ROWS = 24
ROW = dict(mlp_pre_g=0, mlp_post_g=1, loss=2, mix_post_g=8, conv_ln_g=9, conv_ln_b=10, dw_bias=11, pool_scale=12,
           mix_pre_g=16)


NDEV = 8


class GatherSmall:
    has_mid = False
    sibling_only = False
    aliases = ()

    def __init__(self, part):
        self.ins = [part]
        self.out_shapes = [SDS((NDEV, *part.shape), F32)]
        self.sems = [pltpu.SemaphoreType.DMA((NDEV,)), pltpu.SemaphoreType.DMA((NDEV,))]

    def _copies(self, ins, outs, sems):
        send_sems, recv_sems = sems
        x, y, c = lax.axis_index("x"), lax.axis_index("y"), lax.axis_index("c")
        me = 4 * x + 2 * y + c
        own = pltpu.make_async_copy(ins[0], outs[0].at[me], send_sems.at[0])
        sends = [_remote(ins[0], outs[0].at[me], send_sems.at[m], recv_sems.at[m],
                         (x ^ (m >> 2), y ^ ((m >> 1) & 1), c ^ (m & 1))) for m in range(1, NDEV)]
        recvs = [_remote(outs[0].at[me ^ m], outs[0].at[me ^ m], send_sems.at[m], recv_sems.at[m], (x, y, c))
                 for m in range(1, NDEV)]
        return own, sends, recvs

    def start(self, ins, outs, sems):
        own, sends, _ = self._copies(ins, outs, sems)
        for cp in sends + [own]:
            cp.start()

    def finish(self, ins, outs, sems):
        own, sends, recvs = self._copies(ins, outs, sems)
        for cp in recvs:
            cp.wait_recv()
        for cp in sends:
            cp.wait_send()
        own.wait()


def sum_small(parts, after=()):
    def body(parts_ref, sum_ref, loss_ref):
        total = parts_ref[0]
        for d in range(1, NDEV):
            total = total + parts_ref[d]
        sum_ref[...] = total
        r = ROW["loss"]
        loss_ref[...] = jnp.zeros_like(loss_ref) + jnp.sum(total[r:r + 1, :])

    return _call(body, "sum_small", (), [VMEM_SPEC], [VMEM_SPEC, VMEM_SPEC],
                 [SDS(parts.shape[1:], F32), SDS((8, 128), F32)], [], [parts], after=after)[0]


def pair_add(tag, names, grads, from_sibling, exchanges=()):
    n = len(names)
    in_specs, wire_specs, own_specs, wire_shapes, own_shapes = [], [], [], [], []
    for name in names:
        _, gblk, idx = GEOM[name]
        blk = _half_shape(name)
        zeros = (0,) * len(blk)
        in_specs.append(pl.BlockSpec(gblk, lambda k, idx=idx: idx(k, lax.axis_index("c"))))
        wire_specs.append(pl.BlockSpec((1, *blk), lambda k, zeros=zeros: (k, *zeros)))
        own_specs.append(pl.BlockSpec(blk, lambda k, zeros=zeros: zeros))
        wire_shapes.append(SDS((NCHIP, *blk), BF16))
        own_shapes.append(SDS(blk, F32))

    def body(*refs):
        g, s, wire, own = refs[:n], refs[n:2 * n], refs[2 * n:3 * n], refs[3 * n:]
        mine = pl.program_id(0) == 2 * lax.axis_index("x") + lax.axis_index("y")
        for w in range(n):
            total = g[w][...] + s[w][0]
            wire[w][0] = total.astype(BF16)

            @pl.when(mine)
            def _(w=w, total=total):
                own[w][...] = total

    outs, xouts = _call(body, "pair_add_" + tag, (NCHIP,), in_specs + wire_specs, wire_specs + own_specs,
                        wire_shapes + own_shapes, [], list(grads) + list(from_sibling), exchanges)
    return outs[:n], outs[n:], xouts


SUM_STEPS = 4


def sum_partials(tag, names, owns, from_chips, exchanges=()):
    n = len(names)
    own_specs, part_specs, out_specs, out_shapes, part_args, counts = [], [], [], [], [], []
    for name, parts in zip(names, from_chips):
        half = _half_shape(name)
        blk = half[:-2] + (half[-2] // SUM_STEPS, half[-1])
        lead = (0,) * (len(half) - 2)
        own_specs.append(pl.BlockSpec(blk, lambda i, lead=lead: (*lead, i, 0)))
        for p in parts:
            part_specs.append(pl.BlockSpec((p.shape[0], *blk), lambda i, lead=lead: (0, *lead, i, 0)))
            part_args.append(p)
        counts.append(len(parts))
        out_specs.append(pl.BlockSpec((2, *blk), lambda i, lead=lead: (0, *lead, i, 0)))
        out_shapes.append(SDS((2, *half), F32))

    def body(*refs):
        own, parts, out = refs[:n], list(refs[n:n + len(part_args)]), refs[n + len(part_args):]
        c = lax.axis_index("c")
        for w in range(n):
            total = own[w][...]
            for p in [parts.pop(0) for _ in range(counts[w])]:
                for j in range(p.shape[0]):
                    total = total + p[j].astype(F32)
            out[w][c] = total

    return _call(body, "sum_partials_" + tag, (SUM_STEPS,), own_specs + part_specs, out_specs, out_shapes, [],
                 list(owns) + part_args, exchanges)


class SwapHalves:
    has_mid = False
    sibling_only = True

    def __init__(self, halves):
        self.ins = list(halves)
        self.out_shapes = [SDS(h.shape, h.dtype) for h in halves]
        self.aliases = [(i, i) for i in range(len(halves))]
        self.sems = [pltpu.SemaphoreType.DMA((len(halves),)), pltpu.SemaphoreType.DMA((len(halves),))]

    def start(self, ins, outs, sems):
        send_sems, recv_sems = sems
        x, y, c = lax.axis_index("x"), lax.axis_index("y"), lax.axis_index("c")
        for w in range(len(self.ins)):
            _remote(ins[w].at[c], outs[w].at[c], send_sems.at[w], recv_sems.at[w], (x, y, 1 - c)).start()

    def finish(self, ins, outs, sems):
        send_sems, recv_sems = sems
        x, y, c = lax.axis_index("x"), lax.axis_index("y"), lax.axis_index("c")
        for w in range(len(self.ins)):
            _remote(ins[w].at[c], outs[w].at[c], send_sems.at[w], recv_sems.at[w], (x, y, 1 - c)).wait_send()
            _remote(ins[w].at[1 - c], outs[w].at[1 - c], send_sems.at[w], recv_sems.at[w], (x, y, 1 - c)).wait_recv()


SEM_SPEC = pl.BlockSpec(memory_space=pltpu.SEMAPHORE)
HBM_SPEC = pl.BlockSpec(memory_space=pltpu.HBM)
DATAFLOW = pltpu.SideEffectType.DATAFLOW_SIDE_EFFECTING


def plan_chips(names):
    def plan(refs):
        n = len(names)
        x, y, c, me, chips, chip_ids = _place()
        return [(refs[w].at[chip_ids[j]], refs[n + w].at[j], (*chip, c)) for w in range(n) for j, chip in enumerate(chips)]
    return plan


def split_start(tag, arrays, plan, ncopies):
    n = len(arrays)

    def body(*refs):
        sems, token = refs[n:n + 2 * ncopies], refs[-1]
        barrier = pltpu.get_barrier_semaphore()
        x, y, c, me, chips, chip_ids = _place()
        for chip in chips:
            pl.semaphore_signal(barrier, inc=1, device_id=(*chip, c), device_id_type=MESH)
        pl.semaphore_wait(barrier, len(chips))
        for s, (src, dst, to) in enumerate(plan(refs[:n])):
            _remote(src, dst, sems[2 * s], sems[2 * s + 1], to).start()
        token[...] = jnp.zeros_like(token)

    res = pl.pallas_call(
        body, name="start_" + tag, in_specs=[HBM_SPEC] * n,
        out_specs=[SEM_SPEC] * (2 * ncopies) + [HBM_SPEC] * n + [VMEM_SPEC],
        out_shape=[pltpu.SemaphoreType.DMA(())] * (2 * ncopies) + [pltpu.HBM(a.shape, a.dtype) for a in arrays]
        + [SDS((8, 128), F32)],
        input_output_aliases={i: 2 * ncopies + i for i in range(n)},
        compiler_params=pltpu.CompilerParams(has_side_effects=DATAFLOW, collective_id=CHIPS_COLLECTIVE_ID),
    )(*arrays)
    return (res[:2 * ncopies], res[2 * ncopies:-1]), res[-1]


def split_wait(tag, started, plan, after):
    sems, arrays = started
    n = len(arrays)

    def body(*refs):
        sem = refs[n:n + len(sems)]
        for s, (src, dst, to) in enumerate(plan(refs[:n])):
            cp = _remote(src, dst, sem[2 * s], sem[2 * s + 1], to)
            cp.wait_send()
            cp.wait_recv()

    return pl.pallas_call(
        body, name="wait_" + tag, in_specs=[HBM_SPEC] * n + [SEM_SPEC] * len(sems) + [ANY] * len(after),
        out_specs=[HBM_SPEC] * n, out_shape=[pltpu.HBM(a.shape, a.dtype) for a in arrays],
        input_output_aliases={i: i for i in range(n)},
        compiler_params=pltpu.CompilerParams(has_side_effects=DATAFLOW),
    )(*arrays, *sems, *after)


def adamw(tag, ws, gs, ms, vs, steps, after=()):
    n = len(ws)
    specs = [pl.BlockSpec((a.shape[0] // steps, a.shape[1]), lambda i: (i, 0)) for a in ws]
    assert all(a.shape[0] % (steps * SUBLANES) == 0 for a in ws), [a.shape for a in ws]

    def body(*refs):
        w_, g_, m_, v_ = refs[:n], refs[n:2 * n], refs[2 * n:3 * n], refs[3 * n:4 * n]
        d_, nm_, nv_, gout_ = refs[4 * n:5 * n], refs[5 * n:6 * n], refs[6 * n:7 * n], refs[7 * n:]
        for i in range(n):
            gv = g_[i][...]
            gout_[i][...] = gv
            mn = B1 * m_[i][...] + (1.0 - B1) * gv
            vn = B2 * v_[i][...] + (1.0 - B2) * (gv * gv)
            m_hat = mn / (1.0 - B1 ** STEP)
            v_hat = vn / (1.0 - B2 ** STEP)
            d_[i][...] = -LR * (m_hat / (jnp.sqrt(v_hat) + ADAM_EPS) + WD * w_[i][...])
            nm_[i][...] = mn
            nv_[i][...] = vn

    outs, _ = _call(body, "adamw_" + tag, (steps,), specs * 4, specs * 4, [SDS(a.shape, F32) for a in ws] * 4, [],
                    list(ws) + list(gs) + list(ms) + list(vs), after=after)
    return list(zip(outs[:n], outs[n:2 * n], outs[2 * n:3 * n], outs[3 * n:]))


VECS = ("mix_pre_g", "dw_bias", "conv_ln_g", "conv_ln_b", "pool_scale", "mix_post_g", "mlp_pre_g", "mlp_post_g")
WEIGHTS = ("mix_pre_g", "w_in", "dw_kernel", "dw_bias", "conv_ln_g", "conv_ln_b", "w_conv_out", "pool_w", "pool_scale",
           "w_pool_out", "w_o", "mix_post_g", "mlp_pre_g", "w_ff1", "w_ff2", "mlp_post_g")
MIX_MATS = ("w_conv_out", "pool_w", "w_pool_out", "w_o")
FF_MATS = ("w_ff1", "w_ff2")


def kernel(x, mix_pre_g, w_in, dw_kernel, dw_bias, conv_ln_g, conv_ln_b, w_conv_out, pool_w, pool_scale, w_pool_out, w_o, mix_post_g, mlp_pre_g, w_ff1, w_ff2, mlp_post_g, loss_target, m_mix_pre_g, m_w_in, m_dw_kernel, m_dw_bias, m_conv_ln_g, m_conv_ln_b, m_w_conv_out, m_pool_w, m_pool_scale, m_w_pool_out, m_w_o, m_mix_post_g, m_mlp_pre_g, m_w_ff1, m_w_ff2, m_mlp_post_g, v_mix_pre_g, v_w_in, v_dw_kernel, v_dw_bias, v_conv_ln_g, v_conv_ln_b, v_w_conv_out, v_pool_w, v_pool_scale, v_w_pool_out, v_w_o, v_mix_post_g, v_mlp_pre_g, v_w_ff1, v_w_ff2, v_mlp_post_g):
    w = dict(mix_pre_g=mix_pre_g, w_in=w_in, dw_kernel=dw_kernel, dw_bias=dw_bias, conv_ln_g=conv_ln_g,
             conv_ln_b=conv_ln_b, w_conv_out=w_conv_out, pool_w=pool_w, pool_scale=pool_scale, w_pool_out=w_pool_out,
             w_o=w_o, mix_post_g=mix_post_g, mlp_pre_g=mlp_pre_g, w_ff1=w_ff1, w_ff2=w_ff2, mlp_post_g=mlp_post_g)
    m = dict(mix_pre_g=m_mix_pre_g, w_in=m_w_in, dw_kernel=m_dw_kernel, dw_bias=m_dw_bias, conv_ln_g=m_conv_ln_g,
             conv_ln_b=m_conv_ln_b, w_conv_out=m_w_conv_out, pool_w=m_pool_w, pool_scale=m_pool_scale,
             w_pool_out=m_w_pool_out, w_o=m_w_o, mix_post_g=m_mix_post_g, mlp_pre_g=m_mlp_pre_g, w_ff1=m_w_ff1,
             w_ff2=m_w_ff2, mlp_post_g=m_mlp_post_g)
    v = dict(mix_pre_g=v_mix_pre_g, w_in=v_w_in, dw_kernel=v_dw_kernel, dw_bias=v_dw_bias, conv_ln_g=v_conv_ln_g,
             conv_ln_b=v_conv_ln_b, w_conv_out=v_w_conv_out, pool_w=v_pool_w, pool_scale=v_pool_scale,
             w_pool_out=v_w_pool_out, w_o=v_w_o, mix_post_g=v_mix_post_g, mlp_pre_g=v_mlp_pre_g, w_ff1=v_w_ff1,
             w_ff2=v_w_ff2, mlp_post_g=v_mlp_post_g)
    chip = 2 * lax.axis_index("x") + lax.axis_index("y")
    xs, tgt = x[0], loss_target[0]
    vecs = {name: w[name].reshape(1, D) for name in VECS}

    taps = lax.dynamic_update_slice(jnp.zeros((KW_PAD, D), F32), dw_kernel, (0, chip * DSH))
    mine, full = {}, {}
    (mine["w_in"],), _ = cast_shards("cast_w_in", ("w_in",), [w["w_in"]])
    rest = MATS[1:]
    cast, ((full["w_in"], dwk),) = cast_shards(
        "cast_rest", rest, [w[name] for name in rest], [GatherWeights(("w_in",), [mine["w_in"]], taps)])
    mine.update(zip(rest, cast))
    (u, glu, ag, p, gt), (got,) = fwd_in(
        xs, vecs["mix_pre_g"], full["w_in"], TM_IN, [GatherWeights(MIX_MATS, [mine[n] for n in MIX_MATS])])
    full.update(zip(MIX_MATS, got))
    (cv, sw, z, zl, zs, yc, yp, mg, mo, h1), (got,) = fwd_mix(
        xs, glu, p, gt, dwk, vecs["dw_bias"], vecs["conv_ln_g"], vecs["conv_ln_b"], full["w_conv_out"], full["pool_w"],
        vecs["pool_scale"], full["w_pool_out"], full["w_o"], vecs["mix_post_g"], TM,
        [GatherWeights(FF_MATS, [mine[n] for n in FF_MATS])])
    full.update(zip(FF_MATS, got))

    (v_, a2, df2, df1, dh1, vec_mlp), _ = mlp_fwd_bwd(
        h1, tgt, vecs["mlp_pre_g"], vecs["mlp_post_g"], full["w_ff1"], full["w_ff2"], TM)
    grads, g, delta, new_m, new_v = {}, {}, {}, {}, {}
    rest_mats = FF_MATS + MIX_MATS
    grads["w_ff1"], _ = wgrad(v_, df1, "wgrad_ff1", bn=CFF, chip_major=True)
    grads["w_ff2"], _ = wgrad(a2, df2, "wgrad_ff2")
    landing = lambda names, slots, dt: [lax.empty((slots, *_half_shape(n)), dt) for n in names]
    grads_ff = [grads[n] for n in FF_MATS]
    (dmo, dgt, dyc, dyp, dcv, dzl, dz, vec_mix), (from_sibling,) = bwd_mix(
        dh1, mo, cv, zl, yc, yp, gt, vecs["conv_ln_g"], vecs["conv_ln_b"], vecs["pool_scale"], vecs["mix_post_g"],
        full["w_conv_out"], full["pool_w"], full["w_pool_out"], full["w_o"], TM,
        exchanges=[ExchangePair(FF_MATS, grads_ff)])
    grads["w_conv_out"], _ = wgrad(sw, dyc, "wgrad_conv_out", bt=1024)
    grads["pool_w"] = wgrad_pool(z, dzl)
    grads["w_pool_out"], _ = wgrad(zs, dyp, "wgrad_pool_out", bt=1024)
    grads["w_o"], _ = wgrad(mg, dmo, "wgrad_o", bt=1024)
    wire_ff, own_ff, (from_sibling,) = pair_add(
        "ff", FF_MATS, grads_ff, from_sibling, exchanges=[ExchangePair(MIX_MATS, [grads[n] for n in MIX_MATS])])
    wire_mix, own_mix, _ = pair_add("mix", MIX_MATS, [grads[n] for n in MIX_MATS], from_sibling)
    chips_rest, token = split_start("chips_rest", list(wire_ff) + list(wire_mix) + landing(rest_mats, NCHIP - 1, BF16),
                                    plan_chips(rest_mats), len(rest_mats) * (NCHIP - 1))
    dx, dproj, vec_in, dk = bwd_in(
        xs, dh1, dcv, dz, glu, ag, dgt, vecs["mix_pre_g"], dwk, full["w_in"], TM, after=[token])
    small_part = jnp.concatenate([vec_mlp, vec_mix, vec_in, dk], axis=0)
    grads["w_in"], ((small_parts,),) = wgrad(
        u, dproj, "wgrad_in", bn=CIN, chip_major=True, exchanges=[GatherSmall(small_part)])
    got = split_wait("chips_rest", chips_rest, plan_chips(rest_mats), after=[grads["w_in"]])[len(rest_mats):]
    chips_ff, chips_mix = got[:len(FF_MATS)], got[len(FF_MATS):]
    halves_rest, (from_sibling,) = sum_partials(
        "rest", rest_mats, list(own_ff) + list(own_mix), [[a] for a in list(chips_ff) + list(chips_mix)],
        [ExchangePair(("w_in",), [grads["w_in"]])])
    wire_in, own_in, (reduced_rest,) = pair_add(
        "in", ("w_in",), [grads["w_in"]], from_sibling, exchanges=[SwapHalves(halves_rest)])
    for n, red in zip(rest_mats, reduced_rest):
        g[n] = red.reshape(w[n].shape)

    def update(tag, names, steps, after=()):
        two_d = lambda a: a.reshape(-1, a.shape[-1])
        res = adamw(tag, [two_d(w[n]) for n in names], [two_d(g[n]) for n in names],
                    [two_d(m[n]) for n in names], [two_d(v[n]) for n in names], steps, after)
        for n, outs in zip(names, res):
            delta[n], new_m[n], new_v[n], g[n] = [a.reshape(w[n].shape) for a in outs]

    chips_in, token = split_start("chips_in", list(wire_in) + landing(("w_in",), NCHIP - 1, BF16),
                                  plan_chips(("w_in",)), NCHIP - 1)
    small, loss8 = sum_small(small_parts, after=[token])
    loss = loss8[0, 0]
    for name in VECS:
        g[name] = small[ROW[name]]
    g["dw_kernel"] = lax.dynamic_slice(small[VEC_ROWS:VEC_ROWS + KW_PAD], (0, chip * DSH), (KW_PAD, DSH))
    update("rest", rest_mats, 8, after=[token])
    stack = lambda d: jnp.concatenate([d[name].reshape(1, D) for name in VECS], axis=0)
    (res,) = adamw("vectors", [stack(w)], [stack(g)], [stack(m)], [stack(v)], 1, after=[token])
    for i, name in enumerate(VECS):
        delta[name], new_m[name], new_v[name] = [r[i] for r in res[:3]]
    padk = lambda a: jnp.pad(a, ((0, KW_PAD - KW), (0, 0)))
    (res,) = adamw("dw_kernel", [padk(w["dw_kernel"])], [g["dw_kernel"]], [padk(m["dw_kernel"])],
                   [padk(v["dw_kernel"])], 1, after=[token])
    delta["dw_kernel"], new_m["dw_kernel"], new_v["dw_kernel"] = [r[:KW] for r in res[:3]]
    g["dw_kernel"] = g["dw_kernel"][:KW]
    chips_in = split_wait("chips_in", chips_in, plan_chips(("w_in",)),
                          after=[delta[rest_mats[-1]], delta["dw_kernel"], delta[VECS[0]]])[1:]
    halves_in, _ = sum_partials("in", ("w_in",), own_in, [chips_in])
    (reduced_in,) = exchange("swap_w_in", SwapHalves(halves_in))
    g["w_in"] = reduced_in.reshape(w["w_in"].shape)
    update("in", ("w_in",), 8)

    return (loss, dx[None], *[g[n] for n in WEIGHTS], *[delta[n] for n in WEIGHTS], *[new_m[n] for n in WEIGHTS],
            *[new_v[n] for n in WEIGHTS])
```
